```python
import jax, jax.numpy as jnp
from jax import lax
import numpy as np


D_MODEL = 1024
BATCH = 16
SEQ = 4096
DEPTH = 2

GLA_HEADS = 4
GLA_DK = D_MODEL // (2 * GLA_HEADS)
GLA_DV = D_MODEL // GLA_HEADS
GLA_GATE_RANK = 16
GLA_TAU = 16.0
GLA_CHUNK = 64
MLA_HEADS = 16
MLA_NOPE = 64
MLA_ROPE = 32
MLA_V = D_MODEL // MLA_HEADS
MLA_QK = MLA_NOPE + MLA_ROPE
MLA_Q_RANK = 384
MLA_KV_RANK = 128
ROPE_BASE = 10000.0
Q_BLOCK = 128
N_GROUPS = 8
EXPERTS_PER_GROUP = 4
N_EXPERTS = N_GROUPS * EXPERTS_PER_GROUP
TOP_K_IN_GROUP = 2
D_EXPERT = 256
GLA_QK_W = GLA_HEADS * GLA_DK
GLA_V_W = GLA_HEADS * GLA_DV
MLA_V_W = MLA_HEADS * MLA_V
IN_SPLITS = (GLA_QK_W, GLA_QK_W, GLA_V_W, GLA_V_W, GLA_GATE_RANK, GLA_GATE_RANK,
             MLA_Q_RANK, MLA_KV_RANK, MLA_ROPE, D_MODEL, D_MODEL)
IN_OFFSETS = tuple(int(o) for o in np.cumsum(IN_SPLITS)[:-1])
D_IN = int(sum(IN_SPLITS))
DEEPNORM_ALPHA = (2.0 * DEPTH) ** 0.25
DEEPNORM_BETA = (8.0 * DEPTH) ** -0.25
LN_EPS = 1e-5
RMS_EPS = 1e-6

kernel_name = 'hybrid_gla_mla_hier_moe_deepnorm'


def layer_norm(x, g, b):
    xf = x.astype(jnp.float32)
    mu = jnp.mean(xf, axis=-1, keepdims=True)
    var = jnp.mean(jnp.square(xf - mu), axis=-1, keepdims=True)
    return ((xf - mu) * lax.rsqrt(var + LN_EPS) * g + b).astype(x.dtype)


def rms_norm(x, g):
    xf = x.astype(jnp.float32)
    return (xf * lax.rsqrt(jnp.mean(xf * xf, axis=-1, keepdims=True) + RMS_EPS) * g).astype(x.dtype)


def rope_tables(positions):
    inv = ROPE_BASE ** (-jnp.arange(0, MLA_ROPE, 2, dtype=jnp.float32) / MLA_ROPE)
    ang = positions.astype(jnp.float32)[..., None] * inv
    return jnp.cos(ang), jnp.sin(ang)


def apply_rope(x, cos, sin):
    x1, x2 = jnp.split(x, 2, axis=-1)
    c = cos.astype(x.dtype)
    s = sin.astype(x.dtype)
    return jnp.concatenate([x1 * c - x2 * s, x1 * s + x2 * c], axis=-1)


def gla_chunked(q, k, v, log_a, strict):
    B, S, H, DK = q.shape
    DV = v.shape[-1]
    C = GLA_CHUNK
    N = S // C

    def chunks(t):
        return t.reshape(B, N, C, H, t.shape[-1]).transpose(1, 0, 3, 2, 4)

    qc, kc, vc, gc = chunks(q), chunks(k), chunks(v), chunks(log_a)
    b = jnp.cumsum(gc.astype(jnp.float32), axis=3)
    b_last = b[:, :, :, -1:, :]
    q_dec = qc * jnp.exp(b).astype(q.dtype)
    k_inv = kc * jnp.exp(-b).astype(k.dtype)
    k_to_end = kc * jnp.exp(b_last - b).astype(k.dtype)
    chunk_decay = jnp.exp(b_last[:, :, :, 0, :]).astype(q.dtype)
    scores = jnp.einsum('nbhik,nbhjk->nbhij', q_dec, k_inv)
    mask = jnp.tril(jnp.ones((C, C), dtype=bool), k=-1 if strict else 0)
    o_intra = jnp.einsum('nbhij,nbhjv->nbhiv', jnp.where(mask, scores, 0.0).astype(v.dtype), vc)

    def step(state, inp):
        q_n, k_n, v_n, dec_n = inp
        o_n = jnp.einsum('bhik,bhkv->bhiv', q_n, state)
        state = dec_n[..., None] * state + jnp.einsum('bhjk,bhjv->bhkv', k_n, v_n)
        return state, o_n

    state0 = jnp.zeros((B, H, DK, DV), v.dtype)
    _, o_inter = lax.scan(step, state0, (q_dec, k_to_end, vc, chunk_decay))
    o = o_intra + o_inter
    return o.transpose(1, 0, 3, 2, 4).reshape(B, S, H, DV)


def gla_branch(gq, gk, gv, gr, zf, zb, wa2_f, ba_f, wa2_b, ba_b, norm_g):
    B, S, _ = gq.shape

    def heads(t, d):
        return t.reshape(B, S, GLA_HEADS, d)

    qh = heads(gq, GLA_DK) * (GLA_DK ** -0.5)
    kh = heads(gk, GLA_DK)
    vh = heads(gv, GLA_DV)
    log_af = jax.nn.log_sigmoid((zf @ wa2_f + ba_f).astype(jnp.float32)) / GLA_TAU
    log_ab = jax.nn.log_sigmoid((zb @ wa2_b + ba_b).astype(jnp.float32)) / GLA_TAU
    o_f = gla_chunked(qh, kh, vh, heads(log_af, GLA_DK), strict=False)
    flip = lambda t: jnp.flip(t, axis=1)
    o_b = flip(gla_chunked(flip(qh), flip(kh), flip(vh), flip(heads(log_ab, GLA_DK)), strict=True))
    o = rms_norm(o_f + o_b, norm_g.reshape(GLA_HEADS, GLA_DV))
    return o.reshape(B, S, GLA_V_W) * jax.nn.silu(gr)


def mla_branch(cq, ckv, k_rope, cos, sin, q_norm_g, w_uq, kv_norm_g, w_ukv):
    B, S, _ = cq.shape
    q = (rms_norm(cq, q_norm_g) @ w_uq).reshape(B, S, MLA_HEADS, MLA_QK)
    q_nope = q[..., :MLA_NOPE]
    q_rope = apply_rope(q[..., MLA_NOPE:], cos[:, :, None, :], sin[:, :, None, :])
    kv = (rms_norm(ckv, kv_norm_g) @ w_ukv).reshape(B, S, MLA_HEADS, MLA_NOPE + MLA_V)
    k_nope = kv[..., :MLA_NOPE]
    v = kv[..., MLA_NOPE:]
    k_r = apply_rope(k_rope, cos, sin)
    scale = MLA_QK ** -0.5
    nb = S // Q_BLOCK

    def to_blocks(t):
        return jnp.moveaxis(t.reshape(B, nb, Q_BLOCK, *t.shape[2:]), 1, 0)

    def attend(blk):
        qn, qr = blk
        s = jnp.einsum('bqhd,bkhd->bhqk', qn, k_nope) + jnp.einsum('bqhr,bkr->bhqk', qr, k_r)
        p = jax.nn.softmax(s.astype(jnp.float32) * scale, axis=-1).astype(v.dtype)
        return jnp.einsum('bhqk,bkhd->bqhd', p, v)

    o = lax.map(attend, (to_blocks(q_nope), to_blocks(q_rope)))
    return jnp.moveaxis(o, 0, 1).reshape(B, S, MLA_V_W)


def hybrid_mixer(h, cos, sin, w_in, b_in, wa2_f, ba_f, wa2_b, ba_b, gla_norm_g,
                 q_norm_g, w_uq, kv_norm_g, w_ukv, w_out):
    proj = h @ w_in + b_in
    (gq, gk, gv, gr, zf, zb, cq, ckv, kr, gate_a, gate_b) = jnp.split(proj, IN_OFFSETS, axis=-1)
    o_a = gla_branch(gq, gk, gv, gr, zf, zb, wa2_f, ba_f, wa2_b, ba_b, gla_norm_g)
    o_b = mla_branch(cq, ckv, kr, cos, sin, q_norm_g, w_uq, kv_norm_g, w_ukv)
    merged = jax.nn.sigmoid(gate_a) * o_a + jax.nn.sigmoid(gate_b) * o_b
    return merged @ w_out


def hier_moe(h, w_grp, b_grp, w_exp, b_exp, w_gate, w_up, w_down):
    B, S, _ = h.shape
    grp_logits = (h @ w_grp + b_grp).astype(jnp.float32)
    grp_prob = jax.nn.softmax(grp_logits, axis=-1)
    g_idx = jnp.argmax(grp_logits, axis=-1)
    g_w = jnp.take_along_axis(grp_prob, g_idx[..., None], axis=-1)
    exp_logits = (h @ w_exp + b_exp).astype(jnp.float32).reshape(B, S, N_GROUPS, EXPERTS_PER_GROUP)
    in_grp = jnp.take_along_axis(exp_logits, g_idx[..., None, None], axis=2)[..., 0, :]
    top_val, top_idx = lax.top_k(in_grp, TOP_K_IN_GROUP)
    e_w = jax.nn.softmax(top_val, axis=-1) * g_w
    e_id = g_idx[..., None] * EXPERTS_PER_GROUP + top_idx
    gates = jnp.sum(jax.nn.one_hot(e_id, N_EXPERTS, dtype=jnp.float32) * e_w[..., None], axis=-2)

    def per_sequence(args):
        xs, gs = args
        hid = jax.nn.silu(jnp.einsum('sd,edf->sef', xs, w_gate)) * jnp.einsum('sd,edf->sef', xs, w_up)
        return jnp.einsum('sef,efd->sd', hid * gs[..., None].astype(hid.dtype), w_down)

    return lax.map(per_sequence, (h, gates))


def setup_inputs(seed: int = 0) -> dict:
    key = jax.random.key(seed)
    ks = iter(jax.random.split(key, 32))
    f32 = jnp.float32
    L, D = DEPTH, D_MODEL

    def normal(shape, scale):
        return jax.random.normal(next(ks), shape, f32) * scale

    def gain(shape):
        return 1.0 + normal(shape, 0.02)

    x = jax.random.normal(next(ks), (BATCH, SEQ, D), f32)
    positions = (jnp.arange(SEQ, dtype=jnp.int32)[None, :]
                 + jax.random.randint(next(ks), (BATCH, 1), 0, 1024, dtype=jnp.int32))
    in_col_scale = jnp.ones((D_IN,), f32).at[2 * GLA_QK_W: 2 * GLA_QK_W + GLA_V_W].set(DEEPNORM_BETA)
    ukv_col_scale = jnp.tile(jnp.concatenate([jnp.ones((MLA_NOPE,), f32),
                                              jnp.full((MLA_V,), DEEPNORM_BETA, f32)]), MLA_HEADS)
    return {
        'x': x,
        'positions': positions,
        'ln_emb_g': gain((D,)),
        'ln_emb_b': normal((D,), 0.02),
        'w_in': normal((L, D, D_IN), D ** -0.5) * in_col_scale,
        'b_in': normal((L, D_IN), 0.02),
        'gla_wa2_f': normal((L, GLA_GATE_RANK, GLA_QK_W), GLA_GATE_RANK ** -0.5),
        'gla_ba_f': normal((L, GLA_QK_W), 0.02),
        'gla_wa2_b': normal((L, GLA_GATE_RANK, GLA_QK_W), GLA_GATE_RANK ** -0.5),
        'gla_ba_b': normal((L, GLA_QK_W), 0.02),
        'gla_norm_g': gain((L, GLA_V_W)),
        'mla_q_norm_g': gain((L, MLA_Q_RANK)),
        'mla_w_uq': normal((L, MLA_Q_RANK, MLA_HEADS * MLA_QK), MLA_Q_RANK ** -0.5),
        'mla_kv_norm_g': gain((L, MLA_KV_RANK)),
        'mla_w_ukv': normal((L, MLA_KV_RANK, MLA_HEADS * (MLA_NOPE + MLA_V)), MLA_KV_RANK ** -0.5) * ukv_col_scale,
        'w_out': normal((L, D, D), (D ** -0.5) * DEEPNORM_BETA),
        'ln1_g': gain((L, D)),
        'ln1_b': normal((L, D), 0.02),
        'w_grp': normal((L, D, N_GROUPS), D ** -0.5),
        'b_grp': normal((L, N_GROUPS), 0.01),
        'w_exp': normal((L, D, N_EXPERTS), D ** -0.5),
        'b_exp': normal((L, N_EXPERTS), 0.01),
        'w_gate': normal((L, N_EXPERTS, D, D_EXPERT), D ** -0.5),
        'w_up': normal((L, N_EXPERTS, D, D_EXPERT), D ** -0.5),
        'w_down': normal((L, N_EXPERTS, D_EXPERT, D), (D_EXPERT ** -0.5) * DEEPNORM_BETA),
        'ln2_g': gain((L, D)),
        'ln2_b': normal((L, D), 0.02),
    }


def reference(x, positions, ln_emb_g, ln_emb_b, w_in, b_in, gla_wa2_f, gla_ba_f, gla_wa2_b, gla_ba_b,
              gla_norm_g, mla_q_norm_g, mla_w_uq, mla_kv_norm_g, mla_w_ukv, w_out, ln1_g, ln1_b,
              w_grp, b_grp, w_exp, b_exp, w_gate, w_up, w_down, ln2_g, ln2_b):
    cos, sin = rope_tables(positions)
    h = layer_norm(x, ln_emb_g, ln_emb_b)
    for l in range(DEPTH):
        mix = hybrid_mixer(h, cos, sin, w_in[l], b_in[l], gla_wa2_f[l], gla_ba_f[l], gla_wa2_b[l],
                           gla_ba_b[l], gla_norm_g[l], mla_q_norm_g[l], mla_w_uq[l], mla_kv_norm_g[l],
                           mla_w_ukv[l], w_out[l])
        h = layer_norm(DEEPNORM_ALPHA * h + mix, ln1_g[l], ln1_b[l])
        ffn = hier_moe(h, w_grp[l], b_grp[l], w_exp[l], b_exp[l], w_gate[l], w_up[l], w_down[l])
        h = layer_norm(DEEPNORM_ALPHA * h + ffn, ln2_g[l], ln2_b[l])
    return h
```

```python
import functools

import numpy as np
import jax
import jax.numpy as jnp
from jax import lax
from jax.experimental import pallas as pl
from jax.experimental.pallas import tpu as pltpu

F32 = jnp.float32
BF16 = jnp.bfloat16

D_MODEL = 1024
DEPTH = 2
GLA_HEADS = 4
GLA_DK = 128
GLA_DV = 256
GLA_GATE_RANK = 16
GLA_TAU = 16.0
MLA_HEADS = 16
MLA_NOPE = 64
MLA_ROPE = 32
MLA_V = 64
MLA_QK = MLA_NOPE + MLA_ROPE
MLA_Q_RANK = 384
MLA_KV_RANK = 128
ROPE_BASE = 10000.0
N_GROUPS = 8
EXPERTS_PER_GROUP = 4
N_EXPERTS = 32
D_EXPERT = 256
GLA_QK_W = GLA_HEADS * GLA_DK
GLA_V_W = GLA_HEADS * GLA_DV
DEEPNORM_ALPHA = (2.0 * DEPTH) ** 0.25
LN_EPS = 1e-5
RMS_EPS = 1e-6

LANES = 128
VMEM_LIMIT_BYTES = 56 * 1024 * 1024

COL_GQ = 0
COL_GK = 512
COL_GV = 1024
COL_GR = 2048
COL_GA = 3072
COL_GB = 4096
COL_CQ = 5120
CQ_PAD = 512
COL_SMALL = 5632
COL_SMALL2 = 5760
COL_CKV = 5888
N_PROJ = 6144
HEAD_PAD = 128
KR_LANE = 64

GLA_CHUNK = 128
NEG_BIG = -1e30


def _cparams(sem):
    return pltpu.CompilerParams(dimension_semantics=sem, vmem_limit_bytes=VMEM_LIMIT_BYTES)


def _ln_rows(x, g, b):
    mu = jnp.mean(x, axis=-1, keepdims=True)
    xc = x - mu
    var = jnp.mean(xc * xc, axis=-1, keepdims=True)
    return xc * lax.rsqrt(var + LN_EPS) * g + b


def _ln_kernel(x_ref, g_ref, b_ref, o_ref, ob_ref):
    y = _ln_rows(x_ref[...], g_ref[...], b_ref[...])
    o_ref[...] = y
    ob_ref[...] = y.astype(BF16)


def _layer_norm(x, g, b, tm=512):
    t, d = x.shape
    return pl.pallas_call(
        _ln_kernel,
        out_shape=(jax.ShapeDtypeStruct((t, d), F32), jax.ShapeDtypeStruct((t, d), BF16)),
        grid=(t // tm,),
        in_specs=[pl.BlockSpec((tm, d), lambda i: (i, 0)),
                  pl.BlockSpec((1, d), lambda i: (0, 0)),
                  pl.BlockSpec((1, d), lambda i: (0, 0))],
        out_specs=(pl.BlockSpec((tm, d), lambda i: (i, 0)), pl.BlockSpec((tm, d), lambda i: (i, 0))),
        compiler_params=_cparams(("parallel",)),
        name="ln_embed",
    )(x, g.reshape(1, d), b.reshape(1, d))


PROJ_COL_CHUNK = 512


def _proj_kernel(x_ref, w_ref, b_ref, o_ref):
    x = x_ref[...]
    for c in range(N_PROJ // PROJ_COL_CHUNK):
        sl = slice(c * PROJ_COL_CHUNK, (c + 1) * PROJ_COL_CHUNK)
        acc = jnp.dot(x, w_ref[:, sl], preferred_element_type=F32) + b_ref[:, sl]
        o_ref[:, sl] = acc.astype(o_ref.dtype)


def _input_proj(hb, w, b, tm=512):
    t, d = hb.shape
    return pl.pallas_call(
        _proj_kernel,
        out_shape=jax.ShapeDtypeStruct((t, N_PROJ), BF16),
        grid=(t // tm,),
        in_specs=[pl.BlockSpec((tm, d), lambda i: (i, 0)),
                  pl.BlockSpec((d, N_PROJ), lambda i: (0, 0)),
                  pl.BlockSpec((1, N_PROJ), lambda i: (0, 0))],
        out_specs=pl.BlockSpec((tm, N_PROJ), lambda i: (i, 0)),
        compiler_params=_cparams(("parallel",)),
        name="input_proj",
    )(hb, w, b)


def _log_sigmoid(x):
    return jnp.minimum(x, 0.0) - jnp.log(1.0 + jnp.exp(-jnp.abs(x)))


def _split_bf16(x):
    hi = x.astype(BF16)
    lo = (x - hi.astype(F32)).astype(BF16)
    return hi, lo


def _gla_kernel(*refs, reverse, n_chunks):
    if reverse:
        (q_ref, k_ref, v_ref, z_ref, wa_ref, ba_ref, of_ref, gr_ref, ng_ref, o_ref, state_ref) = refs
    else:
        (q_ref, k_ref, v_ref, z_ref, wa_ref, ba_ref, o_ref, state_ref) = refs
    c_len = GLA_CHUNK

    @pl.when(pl.program_id(1) == 0)
    def _():
        state_ref[...] = jnp.zeros_like(state_ref)

    row = lax.broadcasted_iota(jnp.int32, (c_len, c_len), 0)
    col = lax.broadcasted_iota(jnp.int32, (c_len, c_len), 1)
    if reverse:
        tri = (col >= row).astype(BF16)
        keep = col > row
        last = 0
    else:
        tri = (col <= row).astype(BF16)
        keep = col <= row
        last = c_len - 1

    def chunk(cc, carry):
        c = (n_chunks - 1 - cc) if reverse else cc
        r0 = pl.multiple_of(c * c_len, c_len)
        rows = pl.ds(r0, c_len)
        z = z_ref[rows, :]
        la = _log_sigmoid(jnp.dot(z, wa_ref[...], preferred_element_type=F32) + ba_ref[...]) * (1.0 / GLA_TAU)
        la_hi, la_lo = _split_bf16(la)
        b_all = (jnp.dot(tri, la_hi, preferred_element_type=F32)
                 + jnp.dot(tri, la_lo, preferred_element_type=F32))
        b_last = b_all[last:last + 1, :]
        q = q_ref[rows, :].astype(F32)
        k = k_ref[rows, :].astype(F32)
        qd_all = (q * (jnp.exp(b_all) * (GLA_DK ** -0.5))).astype(BF16)
        kinv_all = (k * jnp.exp(-b_all)).astype(BF16)
        kend_all = k * jnp.exp(b_last - b_all)
        dec_all = jnp.broadcast_to(jnp.exp(b_last), (c_len, GLA_QK_W))
        for h in range(GLA_HEADS):
            ks = slice(h * GLA_DK, (h + 1) * GLA_DK)
            vs = slice(h * GLA_DV, (h + 1) * GLA_DV)
            v = v_ref[rows, vs]
            qd = qd_all[:, ks]
            scores = lax.dot_general(qd, kinv_all[:, ks], (((1,), (1,)), ((), ())),
                                     preferred_element_type=F32)
            scores = jnp.where(keep, scores, 0.0).astype(BF16)
            state = state_ref[h]
            o = (jnp.dot(scores, v, preferred_element_type=F32)
                 + jnp.dot(qd, state.astype(BF16), preferred_element_type=F32))
            kend_t = kend_all[:, ks].T.astype(BF16)
            dec_t = dec_all[:, ks].T
            dec = jnp.concatenate([dec_t, dec_t], axis=1)
            state_ref[h] = dec * state + jnp.dot(kend_t, v, preferred_element_type=F32)
            if reverse:
                o = o + of_ref[rows, vs]
                ms = jnp.mean(o * o, axis=-1, keepdims=True)
                o = o * lax.rsqrt(ms + RMS_EPS) * ng_ref[:, vs]
                g = gr_ref[rows, vs].astype(F32)
                o = o * (g * jax.nn.sigmoid(g))
            o_ref[rows, vs] = o
        return carry

    lax.fori_loop(0, n_chunks, chunk, 0)


def _gla(proj, wa, ba, batch, seq, *, reverse, o_fwd=None, norm_g=None, ts=1024):
    ts = min(ts, seq)
    nblk = seq // ts
    t = batch * seq

    def rblk(b, i):
        return b * nblk + ((nblk - 1 - i) if reverse else i)

    in_specs = [
        pl.BlockSpec((ts, GLA_QK_W), lambda b, i: (rblk(b, i), COL_GQ // GLA_QK_W)),
        pl.BlockSpec((ts, GLA_QK_W), lambda b, i: (rblk(b, i), COL_GK // GLA_QK_W)),
        pl.BlockSpec((ts, GLA_V_W), lambda b, i: (rblk(b, i), COL_GV // GLA_V_W)),
        pl.BlockSpec((ts, LANES), lambda b, i: (rblk(b, i), COL_SMALL // LANES)),
        pl.BlockSpec((LANES, GLA_QK_W), lambda b, i: (0, 0)),
        pl.BlockSpec((1, GLA_QK_W), lambda b, i: (0, 0)),
    ]
    args = [proj, proj, proj, proj, wa, ba]
    if reverse:
        in_specs += [
            pl.BlockSpec((ts, GLA_V_W), lambda b, i: (rblk(b, i), 0)),
            pl.BlockSpec((ts, GLA_V_W), lambda b, i: (rblk(b, i), COL_GR // GLA_V_W)),
            pl.BlockSpec((1, GLA_V_W), lambda b, i: (0, 0)),
        ]
        args += [o_fwd, proj, norm_g]
    return pl.pallas_call(
        functools.partial(_gla_kernel, reverse=reverse, n_chunks=ts // GLA_CHUNK),
        out_shape=jax.ShapeDtypeStruct((t, GLA_V_W), F32),
        grid=(batch, nblk),
        in_specs=in_specs,
        out_specs=pl.BlockSpec((ts, GLA_V_W), lambda b, i: (rblk(b, i), 0)),
        scratch_shapes=[pltpu.VMEM((GLA_HEADS, GLA_DK, GLA_DV), F32)],
        compiler_params=_cparams(("parallel", "arbitrary")),
        name="gla_bwd" if reverse else "gla_fwd",
    )(*args)


MLA_COL_CHUNK = 512


def _mla_prep_kernel(cq_ref, ckv_ref, sm_ref, sm2_ref, pos_ref, inv_ref, gq_ref, gkv_ref,
                     wq_ref, wqr_ref, wk_ref, wv_ref, q_ref, k_ref, v_ref):
    cq = cq_ref[...].astype(F32)
    msq = jnp.sum(cq * cq, axis=-1, keepdims=True) * (1.0 / MLA_Q_RANK)
    cqn = (cq * lax.rsqrt(msq + RMS_EPS) * gq_ref[...]).astype(BF16)
    ckv = ckv_ref[...].astype(F32)
    mskv = jnp.mean(ckv * ckv, axis=-1, keepdims=True)
    ckvn = (ckv * lax.rsqrt(mskv + RMS_EPS) * gkv_ref[...]).astype(BF16)

    ang = pos_ref[...].astype(F32) * inv_ref[...]
    cos = jnp.cos(ang)
    sin = jnp.sin(ang)
    lane = lax.broadcasted_iota(jnp.int32, ang.shape, 1)
    kr = jnp.where(lane >= KR_LANE, sm_ref[...].astype(F32), 0.0)
    kr = kr * cos + sm2_ref[...].astype(F32) * sin

    heads_per_chunk = MLA_COL_CHUNK // HEAD_PAD
    cos_t = jnp.concatenate([cos] * heads_per_chunk, axis=1)
    sin_t = jnp.concatenate([sin] * heads_per_chunk, axis=1)
    kr_t = jnp.concatenate([kr] * heads_per_chunk, axis=1)
    qscale = (MLA_QK ** -0.5) * float(np.log2(np.e))
    for c in range(MLA_HEADS * HEAD_PAD // MLA_COL_CHUNK):
        sl = slice(c * MLA_COL_CHUNK, (c + 1) * MLA_COL_CHUNK)
        q = jnp.dot(cqn, wq_ref[:, sl], preferred_element_type=F32)
        qr = jnp.dot(cqn, wqr_ref[:, sl], preferred_element_type=F32)
        q_ref[:, sl] = ((q * cos_t + qr * sin_t) * qscale).astype(BF16)
        k = jnp.dot(ckvn, wk_ref[:, sl], preferred_element_type=F32)
        k_ref[:, sl] = (k + kr_t).astype(BF16)
    v_ref[...] = jnp.dot(ckvn, wv_ref[...], preferred_element_type=F32).astype(BF16)


def _mla_prep(proj, pos, inv_lane, gq, gkv, wq, wqr, wk, wv, ts=512):
    t = proj.shape[0]
    hp = MLA_HEADS * HEAD_PAD
    vw = MLA_HEADS * MLA_V
    const = lambda i: (0, 0)
    return pl.pallas_call(
        _mla_prep_kernel,
        out_shape=(jax.ShapeDtypeStruct((t, hp), BF16), jax.ShapeDtypeStruct((t, hp), BF16),
                   jax.ShapeDtypeStruct((t, vw), BF16)),
        grid=(t // ts,),
        in_specs=[
            pl.BlockSpec((ts, CQ_PAD), lambda i: (i, COL_CQ // CQ_PAD)),
            pl.BlockSpec((ts, LANES), lambda i: (i, COL_CKV // LANES)),
            pl.BlockSpec((ts, LANES), lambda i: (i, COL_SMALL // LANES)),
            pl.BlockSpec((ts, LANES), lambda i: (i, COL_SMALL2 // LANES)),
            pl.BlockSpec((ts, 1), lambda i: (i, 0)),
            pl.BlockSpec((1, LANES), const),
            pl.BlockSpec((1, CQ_PAD), const),
            pl.BlockSpec((1, MLA_KV_RANK), const),
            pl.BlockSpec((CQ_PAD, hp), const),
            pl.BlockSpec((CQ_PAD, hp), const),
            pl.BlockSpec((MLA_KV_RANK, hp), const),
            pl.BlockSpec((MLA_KV_RANK, vw), const),
        ],
        out_specs=(pl.BlockSpec((ts, hp), lambda i: (i, 0)), pl.BlockSpec((ts, hp), lambda i: (i, 0)),
                   pl.BlockSpec((ts, vw), lambda i: (i, 0))),
        compiler_params=_cparams(("parallel",)),
        name="mla_prep",
    )(proj, proj, proj, proj, pos, inv_lane, gq, gkv, wq, wqr, wk, wv)


def _attn_kernel(q_ref, k_ref, v_ref, o_ref, *, tk):
    tq = q_ref.shape[0]
    seq = k_ref.shape[0]
    outs = []
    for hh in range(2):
        hs = slice(hh * HEAD_PAD, (hh + 1) * HEAD_PAD)
        q = q_ref[:, hs]

        def body(c, carry, hs=hs, q=q):
            m, l, acc = carry
            r0 = pl.multiple_of(c * tk, tk)
            k = k_ref[pl.ds(r0, tk), hs]
            v = v_ref[pl.ds(r0, tk), :]
            s = lax.dot_general(q, k, (((1,), (1,)), ((), ())), preferred_element_type=F32)
            m_new = jnp.maximum(m, jnp.max(s, axis=1, keepdims=True))
            alpha = jnp.exp2(m - m_new)
            p = jnp.exp2(s - m_new)
            l = alpha * l + jnp.sum(p, axis=1, keepdims=True)
            acc = alpha * acc + jnp.dot(p.astype(BF16), v, preferred_element_type=F32)
            return m_new, l, acc

        init = (jnp.full((tq, 1), NEG_BIG, F32), jnp.zeros((tq, 1), F32), jnp.zeros((tq, 2 * MLA_V), F32))
        m, l, acc = lax.fori_loop(0, seq // tk, body, init)
        outs.append(acc / l)
    lane = lax.broadcasted_iota(jnp.int32, (tq, 2 * MLA_V), 1)
    o_ref[...] = jnp.where(lane < MLA_V, outs[0], outs[1])


def _attention(q2, k2, v, batch, seq, tq=512, tk=512):
    tq = min(tq, seq)
    tk = min(tk, seq)
    t = batch * seq
    nq = seq // tq
    pairs = MLA_HEADS // 2
    return pl.pallas_call(
        functools.partial(_attn_kernel, tk=tk),
        out_shape=jax.ShapeDtypeStruct((t, MLA_HEADS * MLA_V), F32),
        grid=(batch, pairs, nq),
        in_specs=[
            pl.BlockSpec((tq, 2 * HEAD_PAD), lambda b, p, i: (b * nq + i, p)),
            pl.BlockSpec((seq, 2 * HEAD_PAD), lambda b, p, i: (b, p)),
            pl.BlockSpec((seq, 2 * MLA_V), lambda b, p, i: (b, p)),
        ],
        out_specs=pl.BlockSpec((tq, 2 * MLA_V), lambda b, p, i: (b * nq + i, p)),
        compiler_params=_cparams(("parallel", "parallel", "arbitrary")),
        name="mla_attention",
    )(q2, k2, v)


def _merge_out_kernel(oa_ref, ob_ref, ga_ref, gb_ref, h_ref, w_ref, g_ref, b_ref, o_ref, obf_ref):
    merged = (jax.nn.sigmoid(ga_ref[...].astype(F32)) * oa_ref[...]
              + jax.nn.sigmoid(gb_ref[...].astype(F32)) * ob_ref[...])
    mix = jnp.dot(merged.astype(BF16), w_ref[...], preferred_element_type=F32)
    y = _ln_rows(DEEPNORM_ALPHA * h_ref[...] + mix, g_ref[...], b_ref[...])
    o_ref[...] = y
    obf_ref[...] = y.astype(BF16)


def _merge_out(o_gla, o_mla, proj, h, w_out, g, b, tm=512):
    t, d = h.shape
    row = lambda i: (i, 0)
    const = lambda i: (0, 0)
    return pl.pallas_call(
        _merge_out_kernel,
        out_shape=(jax.ShapeDtypeStruct((t, d), F32), jax.ShapeDtypeStruct((t, d), BF16)),
        grid=(t // tm,),
        in_specs=[
            pl.BlockSpec((tm, d), row),
            pl.BlockSpec((tm, d), row),
            pl.BlockSpec((tm, d), lambda i: (i, COL_GA // D_MODEL)),
            pl.BlockSpec((tm, d), lambda i: (i, COL_GB // D_MODEL)),
            pl.BlockSpec((tm, d), row),
            pl.BlockSpec((d, d), const),
            pl.BlockSpec((1, d), const),
            pl.BlockSpec((1, d), const),
        ],
        out_specs=(pl.BlockSpec((tm, d), row), pl.BlockSpec((tm, d), row)),
        compiler_params=_cparams(("parallel",)),
        name="merge_out_ln",
    )(o_gla, o_mla, proj, proj, h, w_out, g, b)


GRP_LANE = N_EXPERTS


def _router_kernel(x_ref, w_ref, b_ref, gates_ref):
    x_hi, x_lo = _split_bf16(x_ref[...])
    w_hi, w_lo = _split_bf16(w_ref[...])
    logits = (jnp.dot(x_hi, w_hi, preferred_element_type=F32)
              + jnp.dot(x_lo, w_hi, preferred_element_type=F32)
              + jnp.dot(x_hi, w_lo, preferred_element_type=F32)) + b_ref[...]
    lane = lax.broadcasted_iota(jnp.int32, logits.shape, 1)
    is_grp = (lane >= GRP_LANE) & (lane < GRP_LANE + N_GROUPS)
    gl = jnp.where(is_grp, logits, NEG_BIG)
    gmax = jnp.max(gl, axis=1, keepdims=True)
    g_lane = jnp.min(jnp.where(gl == gmax, lane, 4 * LANES), axis=1, keepdims=True)
    g_w = 1.0 / jnp.sum(jnp.where(is_grp, jnp.exp(gl - gmax), 0.0), axis=1, keepdims=True)
    lo_lane = (g_lane - GRP_LANE) * EXPERTS_PER_GROUP
    in_grp = (lane >= lo_lane) & (lane < lo_lane + EXPERTS_PER_GROUP)
    el = jnp.where(in_grp, logits, NEG_BIG)
    v1 = jnp.max(el, axis=1, keepdims=True)
    i1 = jnp.min(jnp.where(el == v1, lane, 4 * LANES), axis=1, keepdims=True)
    el2 = jnp.where(lane == i1, NEG_BIG, el)
    v2 = jnp.max(el2, axis=1, keepdims=True)
    i2 = jnp.min(jnp.where(el2 == v2, lane, 4 * LANES), axis=1, keepdims=True)
    e2 = jnp.exp(v2 - v1)
    w1 = g_w / (1.0 + e2)
    w2 = g_w * e2 / (1.0 + e2)
    gates_ref[...] = jnp.where(lane == i1, w1, jnp.where(lane == i2, w2, 0.0))


def _router(h, w_r, b_r, tm=512):
    t, d = h.shape
    return pl.pallas_call(
        _router_kernel,
        out_shape=jax.ShapeDtypeStruct((t, LANES), F32),
        grid=(t // tm,),
        in_specs=[pl.BlockSpec((tm, d), lambda i: (i, 0)),
                  pl.BlockSpec((d, LANES), lambda i: (0, 0)),
                  pl.BlockSpec((1, LANES), lambda i: (0, 0))],
        out_specs=pl.BlockSpec((tm, LANES), lambda i: (i, 0)),
        compiler_params=_cparams(("parallel",)),
        name="router",
    )(h, w_r, b_r)


def _moe_kernel(xb_ref, gates_ref, wgu_ref, wd_ref, h_ref, g_ref, b_ref, o_ref, obf_ref, acc_ref):
    e = pl.program_id(1)

    @pl.when(e == 0)
    def _():
        acc_ref[...] = jnp.zeros_like(acc_ref)

    gu = jnp.dot(xb_ref[...], wgu_ref[0], preferred_element_type=F32)
    gate = gu[:, :D_EXPERT]
    up = gu[:, D_EXPERT:]
    lane = lax.broadcasted_iota(jnp.int32, gates_ref.shape, 1)
    gw = jnp.sum(jnp.where(lane == e, gates_ref[...], 0.0), axis=1, keepdims=True)
    hid = gate * jax.nn.sigmoid(gate) * up * gw
    acc_ref[...] += jnp.dot(hid.astype(BF16), wd_ref[0], preferred_element_type=F32)

    @pl.when(e == N_EXPERTS - 1)
    def _():
        y = _ln_rows(DEEPNORM_ALPHA * h_ref[...] + acc_ref[...], g_ref[...], b_ref[...])
        o_ref[...] = y
        obf_ref[...] = y.astype(BF16)


def _moe(hb, gates, wgu, wd, h, g, b, tm=1024):
    t, d = h.shape
    tm = min(tm, t)
    row = lambda i, e: (i, 0)
    const = lambda i, e: (0, 0)
    return pl.pallas_call(
        _moe_kernel,
        out_shape=(jax.ShapeDtypeStruct((t, d), F32), jax.ShapeDtypeStruct((t, d), BF16)),
        grid=(t // tm, N_EXPERTS),
        in_specs=[
            pl.BlockSpec((tm, d), row),
            pl.BlockSpec((tm, LANES), row),
            pl.BlockSpec((1, d, 2 * D_EXPERT), lambda i, e: (e, 0, 0)),
            pl.BlockSpec((1, D_EXPERT, d), lambda i, e: (e, 0, 0)),
            pl.BlockSpec((tm, d), row),
            pl.BlockSpec((1, d), const),
            pl.BlockSpec((1, d), const),
        ],
        out_specs=(pl.BlockSpec((tm, d), row), pl.BlockSpec((tm, d), row)),
        scratch_shapes=[pltpu.VMEM((tm, d), F32)],
        compiler_params=_cparams(("parallel", "arbitrary")),
        name="moe_dense_ln",
    )(hb, gates, wgu, wd, h, g, b)


def _rotate_half_cols(w):
    half = w.shape[-1] // 2
    return jnp.concatenate([-w[..., half:], w[..., :half]], axis=-1)


def _pack_input_proj(w_in, b_in):
    d = w_in.shape[0]
    offs = np.cumsum((0, GLA_QK_W, GLA_QK_W, GLA_V_W, GLA_V_W, GLA_GATE_RANK, GLA_GATE_RANK,
                      MLA_Q_RANK, MLA_KV_RANK, MLA_ROPE, D_MODEL, D_MODEL))

    def seg(i):
        return w_in[:, offs[i]:offs[i + 1]], b_in[offs[i]:offs[i + 1]]

    w = jnp.zeros((d, N_PROJ), F32)
    b = jnp.zeros((N_PROJ,), F32)

    def put(w, b, col, ws, bs):
        return w.at[:, col:col + ws.shape[1]].set(ws), b.at[col:col + ws.shape[1]].set(bs)

    for i, col in ((0, COL_GQ), (1, COL_GK), (2, COL_GV), (3, COL_GR), (9, COL_GA), (10, COL_GB),
                   (6, COL_CQ), (7, COL_CKV), (4, COL_SMALL), (5, COL_SMALL + GLA_GATE_RANK)):
        w, b = put(w, b, col, *seg(i))
    wkr, bkr = seg(8)
    w, b = put(w, b, COL_SMALL + KR_LANE, wkr, bkr)
    w, b = put(w, b, COL_SMALL2 + KR_LANE, _rotate_half_cols(wkr), _rotate_half_cols(bkr))
    return w.astype(BF16), b.reshape(1, N_PROJ)


def _pack_decay(wa2, ba, lane0):
    w = jnp.zeros((LANES, GLA_QK_W), F32).at[lane0:lane0 + GLA_GATE_RANK].set(wa2)
    return w.astype(BF16), ba.reshape(1, GLA_QK_W)


def _pack_mla(w_uq, w_ukv, q_norm_g, kv_norm_g):
    wq = w_uq.reshape(MLA_Q_RANK, MLA_HEADS, MLA_QK)
    rope = wq[..., MLA_NOPE:]
    zq = jnp.zeros((MLA_Q_RANK, MLA_HEADS, HEAD_PAD - MLA_QK), F32)
    wq_p = jnp.concatenate([wq, zq], axis=-1)
    wqr_p = jnp.concatenate([jnp.zeros_like(wq[..., :MLA_NOPE]), _rotate_half_cols(rope), zq], axis=-1)

    def pad_rows(w):
        w = w.reshape(MLA_Q_RANK, MLA_HEADS * HEAD_PAD)
        return jnp.pad(w, ((0, CQ_PAD - MLA_Q_RANK), (0, 0))).astype(BF16)

    wkv = w_ukv.reshape(MLA_KV_RANK, MLA_HEADS, MLA_NOPE + MLA_V)
    wk_p = jnp.concatenate([wkv[..., :MLA_NOPE],
                            jnp.zeros((MLA_KV_RANK, MLA_HEADS, HEAD_PAD - MLA_NOPE), F32)], axis=-1)
    wk_p = wk_p.reshape(MLA_KV_RANK, MLA_HEADS * HEAD_PAD).astype(BF16)
    wv_p = wkv[..., MLA_NOPE:].reshape(MLA_KV_RANK, MLA_HEADS * MLA_V).astype(BF16)
    gq = jnp.pad(q_norm_g, (0, CQ_PAD - MLA_Q_RANK)).reshape(1, CQ_PAD)
    return pad_rows(wq_p), pad_rows(wqr_p), wk_p, wv_p, gq, kv_norm_g.reshape(1, MLA_KV_RANK)


def _rope_lane_table():
    inv = ROPE_BASE ** (-jnp.arange(0, MLA_ROPE, 2, dtype=F32) / MLA_ROPE)
    half = MLA_ROPE // 2
    tab = jnp.zeros((LANES,), F32)
    tab = tab.at[KR_LANE:KR_LANE + half].set(inv).at[KR_LANE + half:KR_LANE + MLA_ROPE].set(inv)
    return tab.reshape(1, LANES)


def _pack_router(w_grp, b_grp, w_exp, b_exp):
    d = w_grp.shape[0]
    w = jnp.zeros((d, LANES), F32).at[:, :N_EXPERTS].set(w_exp).at[:, GRP_LANE:GRP_LANE + N_GROUPS].set(w_grp)
    b = jnp.zeros((LANES,), F32).at[:N_EXPERTS].set(b_exp).at[GRP_LANE:GRP_LANE + N_GROUPS].set(b_grp)
    return w, b.reshape(1, LANES)


def kernel(x, positions, ln_emb_g, ln_emb_b, w_in, b_in, gla_wa2_f, gla_ba_f, gla_wa2_b, gla_ba_b, gla_norm_g, mla_q_norm_g, mla_w_uq, mla_kv_norm_g, mla_w_ukv, w_out, ln1_g, ln1_b, w_grp, b_grp, w_exp, b_exp, w_gate, w_up, w_down, ln2_g, ln2_b):
    batch, seq, d = x.shape
    t = batch * seq
    pos = positions.reshape(t, 1).astype(jnp.int32)
    inv_lane = _rope_lane_table()
    h, hb = _layer_norm(x.reshape(t, d), ln_emb_g, ln_emb_b)
    for l in range(DEPTH):
        w_p, b_p = _pack_input_proj(w_in[l], b_in[l])
        proj = _input_proj(hb, w_p, b_p)
        wa_f, ba_f = _pack_decay(gla_wa2_f[l], gla_ba_f[l], 0)
        wa_b, ba_b = _pack_decay(gla_wa2_b[l], gla_ba_b[l], GLA_GATE_RANK)
        o_f = _gla(proj, wa_f, ba_f, batch, seq, reverse=False)
        o_gla = _gla(proj, wa_b, ba_b, batch, seq, reverse=True, o_fwd=o_f,
                     norm_g=gla_norm_g[l].reshape(1, GLA_V_W))
        wq, wqr, wk, wv, gq, gkv = _pack_mla(mla_w_uq[l], mla_w_ukv[l], mla_q_norm_g[l], mla_kv_norm_g[l])
        q2, k2, v = _mla_prep(proj, pos, inv_lane, gq, gkv, wq, wqr, wk, wv)
        o_mla = _attention(q2, k2, v, batch, seq)
        h, hb = _merge_out(o_gla, o_mla, proj, h, w_out[l].astype(BF16),
                           ln1_g[l].reshape(1, d), ln1_b[l].reshape(1, d))
        w_r, b_r = _pack_router(w_grp[l], b_grp[l], w_exp[l], b_exp[l])
        gates = _router(h, w_r, b_r)
        wgu = jnp.concatenate([w_gate[l], w_up[l]], axis=-1).astype(BF16)
        h, hb = _moe(hb, gates, wgu, w_down[l].astype(BF16), h,
                     ln2_g[l].reshape(1, d), ln2_b[l].reshape(1, d))
    return h.reshape(batch, seq, d)
```

```python
import functools

import numpy as np
import jax
import jax.numpy as jnp
from jax import lax
from jax.experimental import pallas as pl
from jax.experimental.pallas import tpu as pltpu

F32 = jnp.float32
BF16 = jnp.bfloat16

D_MODEL = 1024
DEPTH = 2
GLA_HEADS = 4
GLA_DK = 128
GLA_DV = 256
GLA_GATE_RANK = 16
GLA_TAU = 16.0
MLA_HEADS = 16
MLA_NOPE = 64
MLA_ROPE = 32
MLA_V = 64
MLA_QK = MLA_NOPE + MLA_ROPE
MLA_Q_RANK = 384
MLA_KV_RANK = 128
ROPE_BASE = 10000.0
N_GROUPS = 8
EXPERTS_PER_GROUP = 4
N_EXPERTS = 32
D_EXPERT = 256
GLA_QK_W = GLA_HEADS * GLA_DK
GLA_V_W = GLA_HEADS * GLA_DV
DEEPNORM_ALPHA = (2.0 * DEPTH) ** 0.25
LN_EPS = 1e-5
RMS_EPS = 1e-6

LANES = 128
VMEM_LIMIT_BYTES = 56 * 1024 * 1024

COL_GQ = 0
COL_GK = 512
COL_GV = 1024
COL_GR = 2048
COL_GA = 3072
COL_GB = 4096
COL_CQ = 5120
CQ_PAD = 512
COL_SMALL = 5632
COL_SMALL2 = 5760
COL_CKV = 5888
N_PROJ = 6144
HEAD_PAD = 128
KR_LANE = 64

GLA_CHUNK = 128
NEG_BIG = -1e30


def _cparams(sem):
    return pltpu.CompilerParams(dimension_semantics=sem, vmem_limit_bytes=VMEM_LIMIT_BYTES)


def _ln_rows(x, g, b):
    mu = jnp.mean(x, axis=-1, keepdims=True)
    xc = x - mu
    var = jnp.mean(xc * xc, axis=-1, keepdims=True)
    return xc * lax.rsqrt(var + LN_EPS) * g + b


def _ln_kernel(x_ref, g_ref, b_ref, o_ref, ob_ref):
    y = _ln_rows(x_ref[...], g_ref[...], b_ref[...])
    o_ref[...] = y
    ob_ref[...] = y.astype(BF16)


def _layer_norm(x, g, b, tm=512):
    t, d = x.shape
    return pl.pallas_call(
        _ln_kernel,
        out_shape=(jax.ShapeDtypeStruct((t, d), F32), jax.ShapeDtypeStruct((t, d), BF16)),
        grid=(t // tm,),
        in_specs=[pl.BlockSpec((tm, d), lambda i: (i, 0)),
                  pl.BlockSpec((1, d), lambda i: (0, 0)),
                  pl.BlockSpec((1, d), lambda i: (0, 0))],
        out_specs=(pl.BlockSpec((tm, d), lambda i: (i, 0)), pl.BlockSpec((tm, d), lambda i: (i, 0))),
        compiler_params=_cparams(("parallel",)),
        name="ln_embed",
    )(x, g.reshape(1, d), b.reshape(1, d))


PROJ_COL_CHUNK = 512


def _proj_kernel(x_ref, w_ref, b_ref, o_ref):
    x = x_ref[...]
    for c in range(N_PROJ // PROJ_COL_CHUNK):
        sl = slice(c * PROJ_COL_CHUNK, (c + 1) * PROJ_COL_CHUNK)
        acc = jnp.dot(x, w_ref[:, sl], preferred_element_type=F32) + b_ref[:, sl]
        o_ref[:, sl] = acc.astype(o_ref.dtype)


def _input_proj(hb, w, b, tm=512):
    t, d = hb.shape
    return pl.pallas_call(
        _proj_kernel,
        out_shape=jax.ShapeDtypeStruct((t, N_PROJ), BF16),
        grid=(t // tm,),
        in_specs=[pl.BlockSpec((tm, d), lambda i: (i, 0)),
                  pl.BlockSpec((d, N_PROJ), lambda i: (0, 0)),
                  pl.BlockSpec((1, N_PROJ), lambda i: (0, 0))],
        out_specs=pl.BlockSpec((tm, N_PROJ), lambda i: (i, 0)),
        compiler_params=_cparams(("parallel",)),
        name="input_proj",
    )(hb, w, b)


def _log_sigmoid(x):
    return jnp.minimum(x, 0.0) - jnp.log(1.0 + jnp.exp(-jnp.abs(x)))


def _split_bf16(x):
    hi = x.astype(BF16)
    lo = (x - hi.astype(F32)).astype(BF16)
    return hi, lo


def _gla_kernel(*refs, reverse, n_chunks):
    if reverse:
        (q_ref, k_ref, v_ref, z_ref, wa_ref, ba_ref, of_ref, gr_ref, ng_ref, o_ref, state_ref) = refs
    else:
        (q_ref, k_ref, v_ref, z_ref, wa_ref, ba_ref, o_ref, state_ref) = refs
    c_len = GLA_CHUNK

    @pl.when(pl.program_id(1) == 0)
    def _():
        state_ref[...] = jnp.zeros_like(state_ref)

    row = lax.broadcasted_iota(jnp.int32, (c_len, c_len), 0)
    col = lax.broadcasted_iota(jnp.int32, (c_len, c_len), 1)
    if reverse:
        tri = (col >= row).astype(BF16)
        keep = col > row
        last = 0
    else:
        tri = (col <= row).astype(BF16)
        keep = col <= row
        last = c_len - 1

    def chunk(cc, carry):
        c = (n_chunks - 1 - cc) if reverse else cc
        r0 = pl.multiple_of(c * c_len, c_len)
        rows = pl.ds(r0, c_len)
        z = z_ref[rows, :]
        la = _log_sigmoid(jnp.dot(z, wa_ref[...], preferred_element_type=F32) + ba_ref[...]) * (1.0 / GLA_TAU)
        la_hi, la_lo = _split_bf16(la)
        b_all = (jnp.dot(tri, la_hi, preferred_element_type=F32)
                 + jnp.dot(tri, la_lo, preferred_element_type=F32))
        b_last = b_all[last:last + 1, :]
        q = q_ref[rows, :].astype(F32)
        k = k_ref[rows, :].astype(F32)
        qd_all = (q * (jnp.exp(b_all) * (GLA_DK ** -0.5))).astype(BF16)
        kinv_all = (k * jnp.exp(-b_all)).astype(BF16)
        kend_all = k * jnp.exp(b_last - b_all)
        dec_all = jnp.broadcast_to(jnp.exp(b_last), (c_len, GLA_QK_W))
        for h in range(GLA_HEADS):
            ks = slice(h * GLA_DK, (h + 1) * GLA_DK)
            vs = slice(h * GLA_DV, (h + 1) * GLA_DV)
            v = v_ref[rows, vs]
            qd = qd_all[:, ks]
            scores = lax.dot_general(qd, kinv_all[:, ks], (((1,), (1,)), ((), ())),
                                     preferred_element_type=F32)
            scores = jnp.where(keep, scores, 0.0).astype(BF16)
            state = state_ref[h]
            o = (jnp.dot(scores, v, preferred_element_type=F32)
                 + jnp.dot(qd, state.astype(BF16), preferred_element_type=F32))
            kend_t = kend_all[:, ks].T.astype(BF16)
            dec_t = dec_all[:, ks].T
            dec = jnp.concatenate([dec_t, dec_t], axis=1)
            state_ref[h] = dec * state + jnp.dot(kend_t, v, preferred_element_type=F32)
            if reverse:
                o = o + of_ref[rows, vs]
                ms = jnp.mean(o * o, axis=-1, keepdims=True)
                o = o * lax.rsqrt(ms + RMS_EPS) * ng_ref[:, vs]
                g = gr_ref[rows, vs].astype(F32)
                o = o * (g * jax.nn.sigmoid(g))
            o_ref[rows, vs] = o
        return carry

    lax.fori_loop(0, n_chunks, chunk, 0)


def _gla(proj, wa, ba, batch, seq, *, reverse, o_fwd=None, norm_g=None, ts=1024):
    ts = min(ts, seq)
    nblk = seq // ts
    t = batch * seq

    def rblk(b, i):
        return b * nblk + ((nblk - 1 - i) if reverse else i)

    in_specs = [
        pl.BlockSpec((ts, GLA_QK_W), lambda b, i: (rblk(b, i), COL_GQ // GLA_QK_W)),
        pl.BlockSpec((ts, GLA_QK_W), lambda b, i: (rblk(b, i), COL_GK // GLA_QK_W)),
        pl.BlockSpec((ts, GLA_V_W), lambda b, i: (rblk(b, i), COL_GV // GLA_V_W)),
        pl.BlockSpec((ts, LANES), lambda b, i: (rblk(b, i), COL_SMALL // LANES)),
        pl.BlockSpec((LANES, GLA_QK_W), lambda b, i: (0, 0)),
        pl.BlockSpec((1, GLA_QK_W), lambda b, i: (0, 0)),
    ]
    args = [proj, proj, proj, proj, wa, ba]
    if reverse:
        in_specs += [
            pl.BlockSpec((ts, GLA_V_W), lambda b, i: (rblk(b, i), 0)),
            pl.BlockSpec((ts, GLA_V_W), lambda b, i: (rblk(b, i), COL_GR // GLA_V_W)),
            pl.BlockSpec((1, GLA_V_W), lambda b, i: (0, 0)),
        ]
        args += [o_fwd, proj, norm_g]
    return pl.pallas_call(
        functools.partial(_gla_kernel, reverse=reverse, n_chunks=ts // GLA_CHUNK),
        out_shape=jax.ShapeDtypeStruct((t, GLA_V_W), F32),
        grid=(batch, nblk),
        in_specs=in_specs,
        out_specs=pl.BlockSpec((ts, GLA_V_W), lambda b, i: (rblk(b, i), 0)),
        scratch_shapes=[pltpu.VMEM((GLA_HEADS, GLA_DK, GLA_DV), F32)],
        compiler_params=_cparams(("parallel", "arbitrary")),
        name="gla_bwd" if reverse else "gla_fwd",
    )(*args)


MLA_COL_CHUNK = 512


def _mla_prep_kernel(cq_ref, ckv_ref, sm_ref, sm2_ref, pos_ref, inv_ref, gq_ref, gkv_ref,
                     wq_ref, wqr_ref, wk_ref, wv_ref, q_ref, k_ref, v_ref):
    cq = cq_ref[...].astype(F32)
    msq = jnp.sum(cq * cq, axis=-1, keepdims=True) * (1.0 / MLA_Q_RANK)
    cqn = (cq * lax.rsqrt(msq + RMS_EPS) * gq_ref[...]).astype(BF16)
    ckv = ckv_ref[...].astype(F32)
    mskv = jnp.mean(ckv * ckv, axis=-1, keepdims=True)
    ckvn = (ckv * lax.rsqrt(mskv + RMS_EPS) * gkv_ref[...]).astype(BF16)

    ang = pos_ref[...].astype(F32) * inv_ref[...]
    cos = jnp.cos(ang)
    sin = jnp.sin(ang)
    lane = lax.broadcasted_iota(jnp.int32, ang.shape, 1)
    kr = jnp.where(lane >= KR_LANE, sm_ref[...].astype(F32), 0.0)
    kr = kr * cos + sm2_ref[...].astype(F32) * sin

    heads_per_chunk = MLA_COL_CHUNK // HEAD_PAD
    cos_t = jnp.concatenate([cos] * heads_per_chunk, axis=1)
    sin_t = jnp.concatenate([sin] * heads_per_chunk, axis=1)
    kr_t = jnp.concatenate([kr] * heads_per_chunk, axis=1)
    qscale = (MLA_QK ** -0.5) * float(np.log2(np.e))
    for c in range(MLA_HEADS * HEAD_PAD // MLA_COL_CHUNK):
        sl = slice(c * MLA_COL_CHUNK, (c + 1) * MLA_COL_CHUNK)
        q = jnp.dot(cqn, wq_ref[:, sl], preferred_element_type=F32)
        qr = jnp.dot(cqn, wqr_ref[:, sl], preferred_element_type=F32)
        q_ref[:, sl] = ((q * cos_t + qr * sin_t) * qscale).astype(BF16)
        k = jnp.dot(ckvn, wk_ref[:, sl], preferred_element_type=F32)
        k_ref[:, sl] = (k + kr_t).astype(BF16)
    v_ref[...] = lax.dot_general(wv_ref[...], ckvn, (((1,), (1,)), ((), ())),
                                 preferred_element_type=F32).astype(BF16)


def _mla_prep(proj, pos, inv_lane, gq, gkv, wq, wqr, wk, wv, ts=512):
    t = proj.shape[0]
    hp = MLA_HEADS * HEAD_PAD
    vw = MLA_HEADS * MLA_V
    const = lambda i: (0, 0)
    return pl.pallas_call(
        _mla_prep_kernel,
        out_shape=(jax.ShapeDtypeStruct((t, hp), BF16), jax.ShapeDtypeStruct((t, hp), BF16),
                   jax.ShapeDtypeStruct((vw, t), BF16)),
        grid=(t // ts,),
        in_specs=[
            pl.BlockSpec((ts, CQ_PAD), lambda i: (i, COL_CQ // CQ_PAD)),
            pl.BlockSpec((ts, LANES), lambda i: (i, COL_CKV // LANES)),
            pl.BlockSpec((ts, LANES), lambda i: (i, COL_SMALL // LANES)),
            pl.BlockSpec((ts, LANES), lambda i: (i, COL_SMALL2 // LANES)),
            pl.BlockSpec((ts, 1), lambda i: (i, 0)),
            pl.BlockSpec((1, LANES), const),
            pl.BlockSpec((1, CQ_PAD), const),
            pl.BlockSpec((1, MLA_KV_RANK), const),
            pl.BlockSpec((CQ_PAD, hp), const),
            pl.BlockSpec((CQ_PAD, hp), const),
            pl.BlockSpec((MLA_KV_RANK, hp), const),
            pl.BlockSpec((vw, MLA_KV_RANK), const),
        ],
        out_specs=(pl.BlockSpec((ts, hp), lambda i: (i, 0)), pl.BlockSpec((ts, hp), lambda i: (i, 0)),
                   pl.BlockSpec((vw, ts), lambda i: (0, i))),
        compiler_params=_cparams(("parallel",)),
        name="mla_prep",
    )(proj, proj, proj, proj, pos, inv_lane, gq, gkv, wq, wqr, wk, wv)


SUBLANES = 8


def _attn_kernel(q_ref, k_ref, vt_ref, o_ref, s_scr0, s_scr1, *, tk):
    tq = q_ref.shape[0]
    seq = k_ref.shape[0]
    nk = seq // tk
    outs = []
    for hh in range(2):
        hs = slice(hh * HEAD_PAD, (hh + 1) * HEAD_PAD)
        q = q_ref[:, hs]
        s_scr = (s_scr0, s_scr1)[hh]
        m8 = jnp.full((SUBLANES, tq), NEG_BIG, F32)
        for c in range(nk):
            st = lax.dot_general(k_ref[c * tk:(c + 1) * tk, hs], q, (((1,), (1,)), ((), ())),
                                 preferred_element_type=F32)
            s_scr[c] = st
            m8 = jnp.maximum(m8, jnp.max(st.reshape(tk // SUBLANES, SUBLANES, tq), axis=0))
        m = jnp.max(m8, axis=0, keepdims=True)
        l8 = jnp.zeros((SUBLANES, tq), F32)
        acc = jnp.zeros((2 * MLA_V, tq), F32)
        for c in range(nk):
            p = jnp.exp2(s_scr[c] - m)
            l8 = l8 + jnp.sum(p.reshape(tk // SUBLANES, SUBLANES, tq), axis=0)
            acc = acc + jnp.dot(vt_ref[:, c * tk:(c + 1) * tk], p.astype(BF16), preferred_element_type=F32)
        outs.append(acc / jnp.sum(l8, axis=0, keepdims=True))
    row = lax.broadcasted_iota(jnp.int32, (2 * MLA_V, tq), 0)
    o_ref[...] = jnp.where(row < MLA_V, outs[0], outs[1]).T


def _attention(q2, k2, vt, batch, seq, tq=512, tk=512):
    tq = min(tq, seq)
    tk = min(tk, seq)
    t = batch * seq
    nq = seq // tq
    pairs = MLA_HEADS // 2
    return pl.pallas_call(
        functools.partial(_attn_kernel, tk=tk),
        scratch_shapes=[pltpu.VMEM((seq // tk, tk, tq), F32), pltpu.VMEM((seq // tk, tk, tq), F32)],
        out_shape=jax.ShapeDtypeStruct((t, MLA_HEADS * MLA_V), F32),
        grid=(batch, pairs, nq),
        in_specs=[
            pl.BlockSpec((tq, 2 * HEAD_PAD), lambda b, p, i: (b * nq + i, p)),
            pl.BlockSpec((seq, 2 * HEAD_PAD), lambda b, p, i: (b, p)),
            pl.BlockSpec((2 * MLA_V, seq), lambda b, p, i: (p, b)),
        ],
        out_specs=pl.BlockSpec((tq, 2 * MLA_V), lambda b, p, i: (b * nq + i, p)),
        compiler_params=_cparams(("parallel", "parallel", "arbitrary")),
        name="mla_attention",
    )(q2, k2, vt)


def _merge_out_kernel(oa_ref, ob_ref, ga_ref, gb_ref, h_ref, w_ref, g_ref, b_ref, o_ref, obf_ref):
    merged = (jax.nn.sigmoid(ga_ref[...].astype(F32)) * oa_ref[...]
              + jax.nn.sigmoid(gb_ref[...].astype(F32)) * ob_ref[...])
    mix = jnp.dot(merged.astype(BF16), w_ref[...], preferred_element_type=F32)
    y = _ln_rows(DEEPNORM_ALPHA * h_ref[...] + mix, g_ref[...], b_ref[...])
    o_ref[...] = y
    obf_ref[...] = y.astype(BF16)


def _merge_out(o_gla, o_mla, proj, h, w_out, g, b, tm=512):
    t, d = h.shape
    row = lambda i: (i, 0)
    const = lambda i: (0, 0)
    return pl.pallas_call(
        _merge_out_kernel,
        out_shape=(jax.ShapeDtypeStruct((t, d), F32), jax.ShapeDtypeStruct((t, d), BF16)),
        grid=(t // tm,),
        in_specs=[
            pl.BlockSpec((tm, d), row),
            pl.BlockSpec((tm, d), row),
            pl.BlockSpec((tm, d), lambda i: (i, COL_GA // D_MODEL)),
            pl.BlockSpec((tm, d), lambda i: (i, COL_GB // D_MODEL)),
            pl.BlockSpec((tm, d), row),
            pl.BlockSpec((d, d), const),
            pl.BlockSpec((1, d), const),
            pl.BlockSpec((1, d), const),
        ],
        out_specs=(pl.BlockSpec((tm, d), row), pl.BlockSpec((tm, d), row)),
        compiler_params=_cparams(("parallel",)),
        name="merge_out_ln",
    )(o_gla, o_mla, proj, proj, h, w_out, g, b)


GRP_LANE = N_EXPERTS


def _router_kernel(x_ref, w_ref, b_ref, gates_ref):
    x_hi, x_lo = _split_bf16(x_ref[...])
    w_hi, w_lo = _split_bf16(w_ref[...])
    logits = (jnp.dot(x_hi, w_hi, preferred_element_type=F32)
              + jnp.dot(x_lo, w_hi, preferred_element_type=F32)
              + jnp.dot(x_hi, w_lo, preferred_element_type=F32)) + b_ref[...]
    lane = lax.broadcasted_iota(jnp.int32, logits.shape, 1)
    is_grp = (lane >= GRP_LANE) & (lane < GRP_LANE + N_GROUPS)
    gl = jnp.where(is_grp, logits, NEG_BIG)
    gmax = jnp.max(gl, axis=1, keepdims=True)
    g_lane = jnp.min(jnp.where(gl == gmax, lane, 4 * LANES), axis=1, keepdims=True)
    g_w = 1.0 / jnp.sum(jnp.where(is_grp, jnp.exp(gl - gmax), 0.0), axis=1, keepdims=True)
    lo_lane = (g_lane - GRP_LANE) * EXPERTS_PER_GROUP
    in_grp = (lane >= lo_lane) & (lane < lo_lane + EXPERTS_PER_GROUP)
    el = jnp.where(in_grp, logits, NEG_BIG)
    v1 = jnp.max(el, axis=1, keepdims=True)
    i1 = jnp.min(jnp.where(el == v1, lane, 4 * LANES), axis=1, keepdims=True)
    el2 = jnp.where(lane == i1, NEG_BIG, el)
    v2 = jnp.max(el2, axis=1, keepdims=True)
    i2 = jnp.min(jnp.where(el2 == v2, lane, 4 * LANES), axis=1, keepdims=True)
    e2 = jnp.exp(v2 - v1)
    w1 = g_w / (1.0 + e2)
    w2 = g_w * e2 / (1.0 + e2)
    gates_ref[...] = jnp.where(lane == i1, w1, jnp.where(lane == i2, w2, 0.0))


def _router(h, w_r, b_r, tm=512):
    t, d = h.shape
    return pl.pallas_call(
        _router_kernel,
        out_shape=jax.ShapeDtypeStruct((t, LANES), F32),
        grid=(t // tm,),
        in_specs=[pl.BlockSpec((tm, d), lambda i: (i, 0)),
                  pl.BlockSpec((d, LANES), lambda i: (0, 0)),
                  pl.BlockSpec((1, LANES), lambda i: (0, 0))],
        out_specs=pl.BlockSpec((tm, LANES), lambda i: (i, 0)),
        compiler_params=_cparams(("parallel",)),
        name="router",
    )(h, w_r, b_r)


def _moe_kernel(xb_ref, gates_ref, wgu_ref, wd_ref, h_ref, g_ref, b_ref, o_ref, obf_ref, acc_ref):
    e = pl.program_id(1)

    @pl.when(e == 0)
    def _():
        acc_ref[...] = jnp.zeros_like(acc_ref)

    gu = jnp.dot(xb_ref[...], wgu_ref[0], preferred_element_type=F32)
    gate = gu[:, :D_EXPERT]
    up = gu[:, D_EXPERT:]
    lane = lax.broadcasted_iota(jnp.int32, gates_ref.shape, 1)
    gw = jnp.sum(jnp.where(lane == e, gates_ref[...], 0.0), axis=1, keepdims=True)
    hid = gate * jax.nn.sigmoid(gate) * up * gw
    acc_ref[...] += jnp.dot(hid.astype(BF16), wd_ref[0], preferred_element_type=F32)

    @pl.when(e == N_EXPERTS - 1)
    def _():
        y = _ln_rows(DEEPNORM_ALPHA * h_ref[...] + acc_ref[...], g_ref[...], b_ref[...])
        o_ref[...] = y
        obf_ref[...] = y.astype(BF16)


def _moe(hb, gates, wgu, wd, h, g, b, tm=1024):
    t, d = h.shape
    tm = min(tm, t)
    row = lambda i, e: (i, 0)
    const = lambda i, e: (0, 0)
    return pl.pallas_call(
        _moe_kernel,
        out_shape=(jax.ShapeDtypeStruct((t, d), F32), jax.ShapeDtypeStruct((t, d), BF16)),
        grid=(t // tm, N_EXPERTS),
        in_specs=[
            pl.BlockSpec((tm, d), row),
            pl.BlockSpec((tm, LANES), row),
            pl.BlockSpec((1, d, 2 * D_EXPERT), lambda i, e: (e, 0, 0)),
            pl.BlockSpec((1, D_EXPERT, d), lambda i, e: (e, 0, 0)),
            pl.BlockSpec((tm, d), row),
            pl.BlockSpec((1, d), const),
            pl.BlockSpec((1, d), const),
        ],
        out_specs=(pl.BlockSpec((tm, d), row), pl.BlockSpec((tm, d), row)),
        scratch_shapes=[pltpu.VMEM((tm, d), F32)],
        compiler_params=_cparams(("parallel", "arbitrary")),
        name="moe_dense_ln",
    )(hb, gates, wgu, wd, h, g, b)


def _rotate_half_cols(w):
    half = w.shape[-1] // 2
    return jnp.concatenate([-w[..., half:], w[..., :half]], axis=-1)


def _pack_input_proj(w_in, b_in):
    d = w_in.shape[0]
    offs = np.cumsum((0, GLA_QK_W, GLA_QK_W, GLA_V_W, GLA_V_W, GLA_GATE_RANK, GLA_GATE_RANK,
                      MLA_Q_RANK, MLA_KV_RANK, MLA_ROPE, D_MODEL, D_MODEL))

    def seg(i):
        return w_in[:, offs[i]:offs[i + 1]], b_in[offs[i]:offs[i + 1]]

    w = jnp.zeros((d, N_PROJ), F32)
    b = jnp.zeros((N_PROJ,), F32)

    def put(w, b, col, ws, bs):
        return w.at[:, col:col + ws.shape[1]].set(ws), b.at[col:col + ws.shape[1]].set(bs)

    for i, col in ((0, COL_GQ), (1, COL_GK), (2, COL_GV), (3, COL_GR), (9, COL_GA), (10, COL_GB),
                   (6, COL_CQ), (7, COL_CKV), (4, COL_SMALL), (5, COL_SMALL + GLA_GATE_RANK)):
        w, b = put(w, b, col, *seg(i))
    wkr, bkr = seg(8)
    w, b = put(w, b, COL_SMALL + KR_LANE, wkr, bkr)
    w, b = put(w, b, COL_SMALL2 + KR_LANE, _rotate_half_cols(wkr), _rotate_half_cols(bkr))
    return w.astype(BF16), b.reshape(1, N_PROJ)


def _pack_decay(wa2, ba, lane0):
    w = jnp.zeros((LANES, GLA_QK_W), F32).at[lane0:lane0 + GLA_GATE_RANK].set(wa2)
    return w.astype(BF16), ba.reshape(1, GLA_QK_W)


def _pack_mla(w_uq, w_ukv, q_norm_g, kv_norm_g):
    wq = w_uq.reshape(MLA_Q_RANK, MLA_HEADS, MLA_QK)
    rope = wq[..., MLA_NOPE:]
    zq = jnp.zeros((MLA_Q_RANK, MLA_HEADS, HEAD_PAD - MLA_QK), F32)
    wq_p = jnp.concatenate([wq, zq], axis=-1)
    wqr_p = jnp.concatenate([jnp.zeros_like(wq[..., :MLA_NOPE]), _rotate_half_cols(rope), zq], axis=-1)

    def pad_rows(w):
        w = w.reshape(MLA_Q_RANK, MLA_HEADS * HEAD_PAD)
        return jnp.pad(w, ((0, CQ_PAD - MLA_Q_RANK), (0, 0))).astype(BF16)

    wkv = w_ukv.reshape(MLA_KV_RANK, MLA_HEADS, MLA_NOPE + MLA_V)
    wk_p = jnp.concatenate([wkv[..., :MLA_NOPE],
                            jnp.zeros((MLA_KV_RANK, MLA_HEADS, HEAD_PAD - MLA_NOPE), F32)], axis=-1)
    wk_p = wk_p.reshape(MLA_KV_RANK, MLA_HEADS * HEAD_PAD).astype(BF16)
    wv_p = wkv[..., MLA_NOPE:].reshape(MLA_KV_RANK, MLA_HEADS * MLA_V).T.astype(BF16)
    gq = jnp.pad(q_norm_g, (0, CQ_PAD - MLA_Q_RANK)).reshape(1, CQ_PAD)
    return pad_rows(wq_p), pad_rows(wqr_p), wk_p, wv_p, gq, kv_norm_g.reshape(1, MLA_KV_RANK)


def _rope_lane_table():
    inv = ROPE_BASE ** (-jnp.arange(0, MLA_ROPE, 2, dtype=F32) / MLA_ROPE)
    half = MLA_ROPE // 2
    tab = jnp.zeros((LANES,), F32)
    tab = tab.at[KR_LANE:KR_LANE + half].set(inv).at[KR_LANE + half:KR_LANE + MLA_ROPE].set(inv)
    return tab.reshape(1, LANES)


def _pack_router(w_grp, b_grp, w_exp, b_exp):
    d = w_grp.shape[0]
    w = jnp.zeros((d, LANES), F32).at[:, :N_EXPERTS].set(w_exp).at[:, GRP_LANE:GRP_LANE + N_GROUPS].set(w_grp)
    b = jnp.zeros((LANES,), F32).at[:N_EXPERTS].set(b_exp).at[GRP_LANE:GRP_LANE + N_GROUPS].set(b_grp)
    return w, b.reshape(1, LANES)


def kernel(x, positions, ln_emb_g, ln_emb_b, w_in, b_in, gla_wa2_f, gla_ba_f, gla_wa2_b, gla_ba_b, gla_norm_g, mla_q_norm_g, mla_w_uq, mla_kv_norm_g, mla_w_ukv, w_out, ln1_g, ln1_b, w_grp, b_grp, w_exp, b_exp, w_gate, w_up, w_down, ln2_g, ln2_b):
    batch, seq, d = x.shape
    t = batch * seq
    pos = positions.reshape(t, 1).astype(jnp.int32)
    inv_lane = _rope_lane_table()
    h, hb = _layer_norm(x.reshape(t, d), ln_emb_g, ln_emb_b)
    for l in range(DEPTH):
        w_p, b_p = _pack_input_proj(w_in[l], b_in[l])
        proj = _input_proj(hb, w_p, b_p)
        wa_f, ba_f = _pack_decay(gla_wa2_f[l], gla_ba_f[l], 0)
        wa_b, ba_b = _pack_decay(gla_wa2_b[l], gla_ba_b[l], GLA_GATE_RANK)
        o_f = _gla(proj, wa_f, ba_f, batch, seq, reverse=False)
        o_gla = _gla(proj, wa_b, ba_b, batch, seq, reverse=True, o_fwd=o_f,
                     norm_g=gla_norm_g[l].reshape(1, GLA_V_W))
        wq, wqr, wk, wv, gq, gkv = _pack_mla(mla_w_uq[l], mla_w_ukv[l], mla_q_norm_g[l], mla_kv_norm_g[l])
        q2, k2, v = _mla_prep(proj, pos, inv_lane, gq, gkv, wq, wqr, wk, wv)
        o_mla = _attention(q2, k2, v, batch, seq)
        h, hb = _merge_out(o_gla, o_mla, proj, h, w_out[l].astype(BF16),
                           ln1_g[l].reshape(1, d), ln1_b[l].reshape(1, d))
        w_r, b_r = _pack_router(w_grp[l], b_grp[l], w_exp[l], b_exp[l])
        gates = _router(h, w_r, b_r)
        wgu = jnp.concatenate([w_gate[l], w_up[l]], axis=-1).astype(BF16)
        h, hb = _moe(hb, gates, wgu, w_down[l].astype(BF16), h,
                     ln2_g[l].reshape(1, d), ln2_b[l].reshape(1, d))
    return h.reshape(batch, seq, d)
```

```python
import functools

import numpy as np
import jax
import jax.numpy as jnp
from jax import lax
from jax.experimental import pallas as pl
from jax.experimental.pallas import tpu as pltpu

F32 = jnp.float32
BF16 = jnp.bfloat16

D_MODEL = 1024
DEPTH = 2
GLA_HEADS = 4
GLA_DK = 128
GLA_DV = 256
GLA_GATE_RANK = 16
GLA_TAU = 16.0
MLA_HEADS = 16
MLA_NOPE = 64
MLA_ROPE = 32
MLA_V = 64
MLA_QK = MLA_NOPE + MLA_ROPE
MLA_Q_RANK = 384
MLA_KV_RANK = 128
ROPE_BASE = 10000.0
N_GROUPS = 8
EXPERTS_PER_GROUP = 4
N_EXPERTS = 32
D_EXPERT = 256
GLA_QK_W = GLA_HEADS * GLA_DK
GLA_V_W = GLA_HEADS * GLA_DV
DEEPNORM_ALPHA = (2.0 * DEPTH) ** 0.25
LN_EPS = 1e-5
RMS_EPS = 1e-6

LANES = 128
VMEM_LIMIT_BYTES = 56 * 1024 * 1024

COL_GQ = 0
COL_GK = 512
COL_GV = 1024
COL_GR = 2048
COL_GA = 3072
COL_GB = 4096
COL_CQ = 5120
CQ_PAD = 512
COL_SMALL = 5632
COL_SMALL2 = 5760
COL_CKV = 5888
N_PROJ = 6144
HEAD_PAD = 128
KR_LANE = 64

GLA_CHUNK = 128
NEG_BIG = -1e30


def _cparams(sem):
    return pltpu.CompilerParams(dimension_semantics=sem, vmem_limit_bytes=VMEM_LIMIT_BYTES)


def _ln_rows(x, g, b):
    mu = jnp.mean(x, axis=-1, keepdims=True)
    xc = x - mu
    var = jnp.mean(xc * xc, axis=-1, keepdims=True)
    return xc * lax.rsqrt(var + LN_EPS) * g + b


def _ln_kernel(x_ref, g_ref, b_ref, o_ref, ob_ref):
    y = _ln_rows(x_ref[...], g_ref[...], b_ref[...])
    o_ref[...] = y
    ob_ref[...] = y.astype(BF16)


def _layer_norm(x, g, b, tm=512):
    t, d = x.shape
    return pl.pallas_call(
        _ln_kernel,
        out_shape=(jax.ShapeDtypeStruct((t, d), F32), jax.ShapeDtypeStruct((t, d), BF16)),
        grid=(t // tm,),
        in_specs=[pl.BlockSpec((tm, d), lambda i: (i, 0)),
                  pl.BlockSpec((1, d), lambda i: (0, 0)),
                  pl.BlockSpec((1, d), lambda i: (0, 0))],
        out_specs=(pl.BlockSpec((tm, d), lambda i: (i, 0)), pl.BlockSpec((tm, d), lambda i: (i, 0))),
        compiler_params=_cparams(("parallel",)),
        name="ln_embed",
    )(x, g.reshape(1, d), b.reshape(1, d))


PROJ_COL_CHUNK = 512


def _proj_kernel(x_ref, w_ref, b_ref, o_ref):
    x = x_ref[...]
    for c in range(N_PROJ // PROJ_COL_CHUNK):
        sl = slice(c * PROJ_COL_CHUNK, (c + 1) * PROJ_COL_CHUNK)
        acc = jnp.dot(x, w_ref[:, sl], preferred_element_type=F32) + b_ref[:, sl]
        o_ref[:, sl] = acc.astype(o_ref.dtype)


def _input_proj(hb, w, b, tm=512):
    t, d = hb.shape
    return pl.pallas_call(
        _proj_kernel,
        out_shape=jax.ShapeDtypeStruct((t, N_PROJ), BF16),
        grid=(t // tm,),
        in_specs=[pl.BlockSpec((tm, d), lambda i: (i, 0)),
                  pl.BlockSpec((d, N_PROJ), lambda i: (0, 0)),
                  pl.BlockSpec((1, N_PROJ), lambda i: (0, 0))],
        out_specs=pl.BlockSpec((tm, N_PROJ), lambda i: (i, 0)),
        compiler_params=_cparams(("parallel",)),
        name="input_proj",
    )(hb, w, b)


def _log_sigmoid(x):
    return jnp.minimum(x, 0.0) - jnp.log(1.0 + jnp.exp(-jnp.abs(x)))


def _split_bf16(x):
    hi = x.astype(BF16)
    lo = (x - hi.astype(F32)).astype(BF16)
    return hi, lo


def _gla_kernel(*refs, reverse, n_chunks):
    if reverse:
        (q_ref, k_ref, v_ref, z_ref, wa_ref, ba_ref, of_ref, gr_ref, ng_ref, o_ref, state_ref) = refs
    else:
        (q_ref, k_ref, v_ref, z_ref, wa_ref, ba_ref, o_ref, state_ref) = refs
    c_len = GLA_CHUNK

    @pl.when(pl.program_id(1) == 0)
    def _():
        state_ref[...] = jnp.zeros_like(state_ref)

    row = lax.broadcasted_iota(jnp.int32, (c_len, c_len), 0)
    col = lax.broadcasted_iota(jnp.int32, (c_len, c_len), 1)
    if reverse:
        tri = (col >= row).astype(BF16)
        keep = col > row
        last = 0
    else:
        tri = (col <= row).astype(BF16)
        keep = col <= row
        last = c_len - 1

    def chunk(cc, carry):
        c = (n_chunks - 1 - cc) if reverse else cc
        r0 = pl.multiple_of(c * c_len, c_len)
        rows = pl.ds(r0, c_len)
        z = z_ref[rows, :]
        la = _log_sigmoid(jnp.dot(z, wa_ref[...], preferred_element_type=F32) + ba_ref[...]) * (1.0 / GLA_TAU)
        la_hi, la_lo = _split_bf16(la)
        b_all = (jnp.dot(tri, la_hi, preferred_element_type=F32)
                 + jnp.dot(tri, la_lo, preferred_element_type=F32))
        b_last = b_all[last:last + 1, :]
        q = q_ref[rows, :].astype(F32)
        k = k_ref[rows, :].astype(F32)
        qd_all = (q * (jnp.exp(b_all) * (GLA_DK ** -0.5))).astype(BF16)
        kinv_all = (k * jnp.exp(-b_all)).astype(BF16)
        kend_all = k * jnp.exp(b_last - b_all)
        dec_all = jnp.broadcast_to(jnp.exp(b_last), (c_len, GLA_QK_W))
        for h in range(GLA_HEADS):
            ks = slice(h * GLA_DK, (h + 1) * GLA_DK)
            vs = slice(h * GLA_DV, (h + 1) * GLA_DV)
            v = v_ref[rows, vs]
            qd = qd_all[:, ks]
            scores = lax.dot_general(qd, kinv_all[:, ks], (((1,), (1,)), ((), ())),
                                     preferred_element_type=F32)
            scores = jnp.where(keep, scores, 0.0).astype(BF16)
            state = state_ref[h]
            o = (jnp.dot(scores, v, preferred_element_type=F32)
                 + jnp.dot(qd, state.astype(BF16), preferred_element_type=F32))
            kend_t = kend_all[:, ks].T.astype(BF16)
            dec_t = dec_all[:, ks].T
            dec = jnp.concatenate([dec_t, dec_t], axis=1)
            state_ref[h] = dec * state + jnp.dot(kend_t, v, preferred_element_type=F32)
            if reverse:
                o = o + of_ref[rows, vs]
                ms = jnp.mean(o * o, axis=-1, keepdims=True)
                o = o * lax.rsqrt(ms + RMS_EPS) * ng_ref[:, vs]
                g = gr_ref[rows, vs].astype(F32)
                o = o * (g * jax.nn.sigmoid(g))
            o_ref[rows, vs] = o
        return carry

    lax.fori_loop(0, n_chunks, chunk, 0)


def _gla(proj, wa, ba, batch, seq, *, reverse, o_fwd=None, norm_g=None, ts=1024):
    ts = min(ts, seq)
    nblk = seq // ts
    t = batch * seq

    def rblk(b, i):
        return b * nblk + ((nblk - 1 - i) if reverse else i)

    in_specs = [
        pl.BlockSpec((ts, GLA_QK_W), lambda b, i: (rblk(b, i), COL_GQ // GLA_QK_W)),
        pl.BlockSpec((ts, GLA_QK_W), lambda b, i: (rblk(b, i), COL_GK // GLA_QK_W)),
        pl.BlockSpec((ts, GLA_V_W), lambda b, i: (rblk(b, i), COL_GV // GLA_V_W)),
        pl.BlockSpec((ts, LANES), lambda b, i: (rblk(b, i), COL_SMALL // LANES)),
        pl.BlockSpec((LANES, GLA_QK_W), lambda b, i: (0, 0)),
        pl.BlockSpec((1, GLA_QK_W), lambda b, i: (0, 0)),
    ]
    args = [proj, proj, proj, proj, wa, ba]
    if reverse:
        in_specs += [
            pl.BlockSpec((ts, GLA_V_W), lambda b, i: (rblk(b, i), 0)),
            pl.BlockSpec((ts, GLA_V_W), lambda b, i: (rblk(b, i), COL_GR // GLA_V_W)),
            pl.BlockSpec((1, GLA_V_W), lambda b, i: (0, 0)),
        ]
        args += [o_fwd, proj, norm_g]
    return pl.pallas_call(
        functools.partial(_gla_kernel, reverse=reverse, n_chunks=ts // GLA_CHUNK),
        out_shape=jax.ShapeDtypeStruct((t, GLA_V_W), F32),
        grid=(batch, nblk),
        in_specs=in_specs,
        out_specs=pl.BlockSpec((ts, GLA_V_W), lambda b, i: (rblk(b, i), 0)),
        scratch_shapes=[pltpu.VMEM((GLA_HEADS, GLA_DK, GLA_DV), F32)],
        compiler_params=_cparams(("parallel", "arbitrary")),
        name="gla_bwd" if reverse else "gla_fwd",
    )(*args)


MLA_COL_CHUNK = 512


def _mla_prep_kernel(cq_ref, ckv_ref, sm_ref, sm2_ref, pos_ref, inv_ref, gq_ref, gkv_ref,
                     wq_ref, wqr_ref, wk_ref, wv_ref, q_ref, k_ref, v_ref):
    cq = cq_ref[...].astype(F32)
    msq = jnp.sum(cq * cq, axis=-1, keepdims=True) * (1.0 / MLA_Q_RANK)
    cqn = (cq * lax.rsqrt(msq + RMS_EPS) * gq_ref[...]).astype(BF16)
    ckv = ckv_ref[...].astype(F32)
    mskv = jnp.mean(ckv * ckv, axis=-1, keepdims=True)
    ckvn = (ckv * lax.rsqrt(mskv + RMS_EPS) * gkv_ref[...]).astype(BF16)

    ang = pos_ref[...].astype(F32) * inv_ref[...]
    cos = jnp.cos(ang)
    sin = jnp.sin(ang)
    lane = lax.broadcasted_iota(jnp.int32, ang.shape, 1)
    kr = jnp.where(lane >= KR_LANE, sm_ref[...].astype(F32), 0.0)
    kr = kr * cos + sm2_ref[...].astype(F32) * sin

    heads_per_chunk = MLA_COL_CHUNK // HEAD_PAD
    cos_t = jnp.concatenate([cos] * heads_per_chunk, axis=1)
    sin_t = jnp.concatenate([sin] * heads_per_chunk, axis=1)
    kr_t = jnp.concatenate([kr] * heads_per_chunk, axis=1)
    qscale = (MLA_QK ** -0.5) * float(np.log2(np.e))
    for c in range(MLA_HEADS * HEAD_PAD // MLA_COL_CHUNK):
        sl = slice(c * MLA_COL_CHUNK, (c + 1) * MLA_COL_CHUNK)
        q = jnp.dot(cqn, wq_ref[:, sl], preferred_element_type=F32)
        qr = jnp.dot(cqn, wqr_ref[:, sl], preferred_element_type=F32)
        q_ref[:, sl] = ((q * cos_t + qr * sin_t) * qscale).astype(BF16)
        k = jnp.dot(ckvn, wk_ref[:, sl], preferred_element_type=F32)
        k_ref[:, sl] = (k + kr_t).astype(BF16)
    v_ref[...] = lax.dot_general(wv_ref[...], ckvn, (((1,), (1,)), ((), ())),
                                 preferred_element_type=F32).astype(BF16)


def _mla_prep(proj, pos, inv_lane, gq, gkv, wq, wqr, wk, wv, ts=512):
    t = proj.shape[0]
    hp = MLA_HEADS * HEAD_PAD
    vw = MLA_HEADS * MLA_V
    const = lambda i: (0, 0)
    return pl.pallas_call(
        _mla_prep_kernel,
        out_shape=(jax.ShapeDtypeStruct((t, hp), BF16), jax.ShapeDtypeStruct((t, hp), BF16),
                   jax.ShapeDtypeStruct((vw, t), BF16)),
        grid=(t // ts,),
        in_specs=[
            pl.BlockSpec((ts, CQ_PAD), lambda i: (i, COL_CQ // CQ_PAD)),
            pl.BlockSpec((ts, LANES), lambda i: (i, COL_CKV // LANES)),
            pl.BlockSpec((ts, LANES), lambda i: (i, COL_SMALL // LANES)),
            pl.BlockSpec((ts, LANES), lambda i: (i, COL_SMALL2 // LANES)),
            pl.BlockSpec((ts, 1), lambda i: (i, 0)),
            pl.BlockSpec((1, LANES), const),
            pl.BlockSpec((1, CQ_PAD), const),
            pl.BlockSpec((1, MLA_KV_RANK), const),
            pl.BlockSpec((CQ_PAD, hp), const),
            pl.BlockSpec((CQ_PAD, hp), const),
            pl.BlockSpec((MLA_KV_RANK, hp), const),
            pl.BlockSpec((vw, MLA_KV_RANK), const),
        ],
        out_specs=(pl.BlockSpec((ts, hp), lambda i: (i, 0)), pl.BlockSpec((ts, hp), lambda i: (i, 0)),
                   pl.BlockSpec((vw, ts), lambda i: (0, i))),
        compiler_params=_cparams(("parallel",)),
        name="mla_prep",
    )(proj, proj, proj, proj, pos, inv_lane, gq, gkv, wq, wqr, wk, wv)


SUBLANES = 8


def _attn_kernel(q_ref, k_ref, vt_ref, o_ref, s_scr0, s_scr1, *, tk):
    seq = k_ref.shape[0]
    tq = s_scr0.shape[-1]
    nk = seq // tk
    nq = seq // tq
    s_scrs = (s_scr0, s_scr1)

    def scores_pass_chunk(hh, i, c, m8):
        hs = slice(hh * HEAD_PAD, (hh + 1) * HEAD_PAD)
        q = q_ref[pl.ds(pl.multiple_of(i * tq, tq), tq), hs]
        st = lax.dot_general(k_ref[c * tk:(c + 1) * tk, hs], q, (((1,), (1,)), ((), ())),
                             preferred_element_type=F32)
        s_scrs[hh][c] = st
        return jnp.maximum(m8, jnp.max(st.reshape(tk // SUBLANES, SUBLANES, tq), axis=0))

    def probs_pass_chunk(hh, c, m, l8, acc):
        p = jnp.exp2(s_scrs[hh][c] - m)
        l8 = l8 + jnp.sum(p.reshape(tk // SUBLANES, SUBLANES, tq), axis=0)
        vt = vt_ref[hh * MLA_V:(hh + 1) * MLA_V, c * tk:(c + 1) * tk]
        acc = acc + jnp.dot(vt, p.astype(BF16), preferred_element_type=F32)
        return l8, acc

    def overlapped(i_scores, hh_scores, hh_probs, m):
        m8 = jnp.full((SUBLANES, tq), NEG_BIG, F32)
        l8 = jnp.zeros((SUBLANES, tq), F32)
        acc = jnp.zeros((MLA_V, tq), F32)
        for c in range(nk):
            l8, acc = probs_pass_chunk(hh_probs, c, m, l8, acc)
            m8 = scores_pass_chunk(hh_scores, i_scores, c, m8)
        return acc / jnp.sum(l8, axis=0, keepdims=True), jnp.max(m8, axis=0, keepdims=True)

    m8 = jnp.full((SUBLANES, tq), NEG_BIG, F32)
    for c in range(nk):
        m8 = scores_pass_chunk(0, 0, c, m8)

    def q_tile(i, m_a):
        out_a, m_b = overlapped(i, 1, 0, m_a)
        out_b, m_a_next = overlapped(jnp.minimum(i + 1, nq - 1), 0, 1, m_b)
        o_ref[pl.ds(pl.multiple_of(i * tq, tq), tq), :] = jnp.concatenate([out_a, out_b], axis=0).T
        return m_a_next

    lax.fori_loop(0, nq, q_tile, jnp.max(m8, axis=0, keepdims=True))


def _attention(q2, k2, vt, batch, seq, tq=512, tk=512):
    tq = min(tq, seq)
    tk = min(tk, seq)
    t = batch * seq
    pairs = MLA_HEADS // 2
    return pl.pallas_call(
        functools.partial(_attn_kernel, tk=tk),
        scratch_shapes=[pltpu.VMEM((seq // tk, tk, tq), F32), pltpu.VMEM((seq // tk, tk, tq), F32)],
        out_shape=jax.ShapeDtypeStruct((t, MLA_HEADS * MLA_V), F32),
        grid=(batch, pairs),
        in_specs=[
            pl.BlockSpec((seq, 2 * HEAD_PAD), lambda b, p: (b, p)),
            pl.BlockSpec((seq, 2 * HEAD_PAD), lambda b, p: (b, p)),
            pl.BlockSpec((2 * MLA_V, seq), lambda b, p: (p, b)),
        ],
        out_specs=pl.BlockSpec((seq, 2 * MLA_V), lambda b, p: (b, p)),
        compiler_params=_cparams(("parallel", "parallel")),
        name="mla_attention",
    )(q2, k2, vt)


def _merge_out_kernel(oa_ref, ob_ref, ga_ref, gb_ref, h_ref, w_ref, g_ref, b_ref, o_ref):
    merged = (jax.nn.sigmoid(ga_ref[...].astype(F32)) * oa_ref[...]
              + jax.nn.sigmoid(gb_ref[...].astype(F32)) * ob_ref[...])
    mix = jnp.dot(merged.astype(BF16), w_ref[...], preferred_element_type=F32)
    o_ref[...] = _ln_rows(DEEPNORM_ALPHA * h_ref[...] + mix, g_ref[...], b_ref[...])


def _merge_out(o_gla, o_mla, proj, h, w_out, g, b, tm=512):
    t, d = h.shape
    row = lambda i: (i, 0)
    const = lambda i: (0, 0)
    return pl.pallas_call(
        _merge_out_kernel,
        out_shape=jax.ShapeDtypeStruct((t, d), F32),
        grid=(t // tm,),
        in_specs=[
            pl.BlockSpec((tm, d), row),
            pl.BlockSpec((tm, d), row),
            pl.BlockSpec((tm, d), lambda i: (i, COL_GA // D_MODEL)),
            pl.BlockSpec((tm, d), lambda i: (i, COL_GB // D_MODEL)),
            pl.BlockSpec((tm, d), row),
            pl.BlockSpec((d, d), const),
            pl.BlockSpec((1, d), const),
            pl.BlockSpec((1, d), const),
        ],
        out_specs=pl.BlockSpec((tm, d), row),
        compiler_params=_cparams(("parallel",)),
        name="merge_out_ln",
    )(o_gla, o_mla, proj, proj, h, w_out, g, b)


GRP_LANE = N_EXPERTS
META_W0, META_W1, META_E0, META_E1, META_R0, META_R1 = range(6)


def _router_kernel(x_ref, w_ref, b_ref, meta_ref, cnt_ref, carry_ref):
    @pl.when(pl.program_id(0) == 0)
    def _():
        carry_ref[...] = jnp.zeros_like(carry_ref)

    x_hi, x_lo = _split_bf16(x_ref[...])
    w_hi, w_lo = _split_bf16(w_ref[...])
    logits = (jnp.dot(x_hi, w_hi, preferred_element_type=F32)
              + jnp.dot(x_lo, w_hi, preferred_element_type=F32)
              + jnp.dot(x_hi, w_lo, preferred_element_type=F32)) + b_ref[...]
    lane = lax.broadcasted_iota(jnp.int32, logits.shape, 1)
    is_grp = (lane >= GRP_LANE) & (lane < GRP_LANE + N_GROUPS)
    gl = jnp.where(is_grp, logits, NEG_BIG)
    gmax = jnp.max(gl, axis=1, keepdims=True)
    g_lane = jnp.min(jnp.where(gl == gmax, lane, 4 * LANES), axis=1, keepdims=True)
    g_w = 1.0 / jnp.sum(jnp.where(is_grp, jnp.exp(gl - gmax), 0.0), axis=1, keepdims=True)
    lo_lane = (g_lane - GRP_LANE) * EXPERTS_PER_GROUP
    in_grp = (lane >= lo_lane) & (lane < lo_lane + EXPERTS_PER_GROUP)
    el = jnp.where(in_grp, logits, NEG_BIG)
    v1 = jnp.max(el, axis=1, keepdims=True)
    i1 = jnp.min(jnp.where(el == v1, lane, 4 * LANES), axis=1, keepdims=True)
    el2 = jnp.where(lane == i1, NEG_BIG, el)
    v2 = jnp.max(el2, axis=1, keepdims=True)
    i2 = jnp.min(jnp.where(el2 == v2, lane, 4 * LANES), axis=1, keepdims=True)
    e2 = jnp.exp(v2 - v1)
    w1 = g_w / (1.0 + e2)
    w2 = g_w * e2 / (1.0 + e2)

    tm = x_ref.shape[0]
    onehot = jnp.where(lane == i1, 1.0, jnp.where(lane == i2, 1.0, 0.0))
    r = lax.broadcasted_iota(jnp.int32, (tm, tm), 0)
    c = lax.broadcasted_iota(jnp.int32, (tm, tm), 1)
    earlier = (c < r).astype(BF16)
    base = carry_ref[...] + jnp.dot(earlier, onehot.astype(BF16), preferred_element_type=F32)
    rank1 = jnp.sum(jnp.where(lane == i1, base, 0.0), axis=1, keepdims=True)
    rank2 = jnp.sum(jnp.where(lane == i2, base, 0.0), axis=1, keepdims=True)
    meta = jnp.zeros(logits.shape, F32)
    for ln, val in ((META_W0, w1), (META_W1, w2), (META_E0, i1.astype(F32)), (META_E1, i2.astype(F32)),
                    (META_R0, rank1), (META_R1, rank2)):
        meta = jnp.where(lane == ln, val, meta)
    meta_ref[...] = meta
    carry_ref[...] += jnp.sum(onehot, axis=0, keepdims=True)
    cnt_ref[...] = carry_ref[...]


def _router(h, w_r, b_r, tm=512):
    t, d = h.shape
    return pl.pallas_call(
        _router_kernel,
        out_shape=(jax.ShapeDtypeStruct((t, LANES), F32), jax.ShapeDtypeStruct((1, LANES), F32)),
        grid=(t // tm,),
        in_specs=[pl.BlockSpec((tm, d), lambda i: (i, 0)),
                  pl.BlockSpec((d, LANES), lambda i: (0, 0)),
                  pl.BlockSpec((1, LANES), lambda i: (0, 0))],
        out_specs=(pl.BlockSpec((tm, LANES), lambda i: (i, 0)), pl.BlockSpec((1, LANES), lambda i: (0, 0))),
        scratch_shapes=[pltpu.VMEM((1, LANES), F32)],
        compiler_params=_cparams(("arbitrary",)),
        name="router",
    )(h, w_r, b_r)


EXPERT_ROW_TILE = 512
DISPATCH_TOKENS = 512
COMBINE_TOKENS = 256
TOP_K = 2
DMA_ISSUE_UNROLL = 8


def _row(ref, idx):
    return ref.at[pl.ds(idx, 1)]


def _dispatch_kernel(d0_ref, d1_ref, h_hbm, xs_zero_hbm, xs_hbm, sem):
    del xs_zero_hbm
    i = pl.program_id(0)
    tt = d0_ref.shape[-1]

    def issue(j, carry):
        src = _row(h_hbm, i * tt + j)
        pltpu.make_async_copy(src, _row(xs_hbm, d0_ref[0, 0, j]), sem).start()
        pltpu.make_async_copy(src, _row(xs_hbm, d1_ref[0, 0, j]), sem).start()
        return carry

    lax.fori_loop(0, tt, issue, 0, unroll=DMA_ISSUE_UNROLL)

    def drain(j, carry):
        pltpu.make_async_copy(_row(h_hbm, 0), _row(xs_hbm, 0), sem).wait()
        return carry

    @pl.when(i > 0)
    def _():
        lax.fori_loop(0, TOP_K * tt, drain, 0)

    @pl.when(i == pl.num_programs(0) - 1)
    def _():
        lax.fori_loop(0, TOP_K * tt, drain, 0)


def _dispatch(h, dest0, dest1, n_rows):
    t, d = h.shape
    tt = dest0.shape[-1]
    smem = lambda: pl.BlockSpec((1, 1, tt), lambda i: (i, 0, 0), memory_space=pltpu.SMEM)
    return pl.pallas_call(
        _dispatch_kernel,
        out_shape=jax.ShapeDtypeStruct((n_rows, d), F32),
        grid=(t // tt,),
        in_specs=[smem(), smem(), pl.BlockSpec(memory_space=pl.ANY), pl.BlockSpec(memory_space=pl.ANY)],
        out_specs=pl.BlockSpec(memory_space=pl.ANY),
        scratch_shapes=[pltpu.SemaphoreType.DMA(())],
        input_output_aliases={3: 0},
        compiler_params=_cparams(("arbitrary",)),
        name="moe_dispatch",
    )(dest0, dest1, h, jnp.zeros((n_rows, d), F32))


def _expert_kernel(te_ref, nu_ref, xs_ref, wgu_ref, wd_ref, ys_ref):
    del te_ref
    i = pl.program_id(0)

    @pl.when(i < nu_ref[0])
    def _():
        gu = jnp.dot(xs_ref[...].astype(BF16), wgu_ref[0], preferred_element_type=F32)
        gate = gu[:, :D_EXPERT]
        hid = gate * jax.nn.sigmoid(gate) * gu[:, D_EXPERT:]
        ys_ref[...] = jnp.dot(hid.astype(BF16), wd_ref[0], preferred_element_type=F32)

    @pl.when(i >= nu_ref[0])
    def _():
        ys_ref[...] = jnp.zeros_like(ys_ref)


def _experts(xs, tile_expert, n_used, wgu, wd):
    n_rows, d = xs.shape
    tm = EXPERT_ROW_TILE
    grid_spec = pltpu.PrefetchScalarGridSpec(
        num_scalar_prefetch=2,
        grid=(n_rows // tm,),
        in_specs=[pl.BlockSpec((tm, d), lambda i, te, nu: (i, 0)),
                  pl.BlockSpec((1, d, 2 * D_EXPERT), lambda i, te, nu: (te[i], 0, 0)),
                  pl.BlockSpec((1, D_EXPERT, d), lambda i, te, nu: (te[i], 0, 0))],
        out_specs=pl.BlockSpec((tm, d), lambda i, te, nu: (i, 0)),
    )
    return pl.pallas_call(
        _expert_kernel,
        out_shape=jax.ShapeDtypeStruct((n_rows, d), F32),
        grid_spec=grid_spec,
        compiler_params=_cparams(("arbitrary",)),
        name="moe_experts",
    )(tile_expert, n_used, xs, wgu, wd)


def _combine_kernel(d0c_ref, d1c_ref, d0n_ref, d1n_ref, ys_hbm, meta_ref, h_ref, g_ref, b_ref,
                    o_ref, obf_ref, buf, sem):
    i = pl.program_id(0)
    n = pl.num_programs(0)
    tt = d0c_ref.shape[-1]
    slot = lax.rem(i, 2)

    def gather(d0_ref, d1_ref, s):
        def issue(j, carry):
            pltpu.make_async_copy(_row(ys_hbm, d0_ref[0, 0, j]), buf.at[s, 0, pl.ds(j, 1)], sem.at[s]).start()
            pltpu.make_async_copy(_row(ys_hbm, d1_ref[0, 0, j]), buf.at[s, 1, pl.ds(j, 1)], sem.at[s]).start()
            return carry
        lax.fori_loop(0, tt, issue, 0, unroll=DMA_ISSUE_UNROLL)

    @pl.when(i == 0)
    def _():
        gather(d0c_ref, d1c_ref, 0)

    @pl.when(i + 1 < n)
    def _():
        gather(d0n_ref, d1n_ref, 1 - slot)

    def drain(j, carry):
        pltpu.make_async_copy(_row(ys_hbm, 0), buf.at[slot, 0, pl.ds(0, 1)], sem.at[slot]).wait()
        return carry

    lax.fori_loop(0, TOP_K * tt, drain, 0)
    meta = meta_ref[...]
    ffn = meta[:, META_W0:META_W0 + 1] * buf[slot, 0] + meta[:, META_W1:META_W1 + 1] * buf[slot, 1]
    y = _ln_rows(DEEPNORM_ALPHA * h_ref[...] + ffn, g_ref[...], b_ref[...])
    o_ref[...] = y
    obf_ref[...] = y.astype(BF16)


def _combine(ys, dest0, dest1, meta, h, g, b):
    t, d = h.shape
    tt = dest0.shape[-1]
    n = t // tt
    cur = lambda: pl.BlockSpec((1, 1, tt), lambda i: (i, 0, 0), memory_space=pltpu.SMEM)
    nxt = lambda: pl.BlockSpec((1, 1, tt), lambda i: (jnp.minimum(i + 1, n - 1), 0, 0), memory_space=pltpu.SMEM)
    row = lambda i: (i, 0)
    const = lambda i: (0, 0)
    return pl.pallas_call(
        _combine_kernel,
        out_shape=(jax.ShapeDtypeStruct((t, d), F32), jax.ShapeDtypeStruct((t, d), BF16)),
        grid=(n,),
        in_specs=[cur(), cur(), nxt(), nxt(), pl.BlockSpec(memory_space=pl.ANY),
                  pl.BlockSpec((tt, LANES), row), pl.BlockSpec((tt, d), row),
                  pl.BlockSpec((1, d), const), pl.BlockSpec((1, d), const)],
        out_specs=(pl.BlockSpec((tt, d), row), pl.BlockSpec((tt, d), row)),
        scratch_shapes=[pltpu.VMEM((2, TOP_K, tt, d), F32), pltpu.SemaphoreType.DMA((2,))],
        compiler_params=_cparams(("arbitrary",)),
        name="moe_combine_ln",
    )(dest0, dest1, dest0, dest1, ys, meta, h, g, b)


def _routing_tables(meta, counts, t):
    tm = EXPERT_ROW_TILE
    n_rows = TOP_K * t + N_EXPERTS * tm
    cnt = counts[0, :N_EXPERTS].astype(jnp.int32)
    tiles = (cnt + tm - 1) // tm
    offs = (jnp.cumsum(tiles) - tiles) * tm
    ends = jnp.cumsum(tiles)
    tile_expert = jnp.minimum(jnp.searchsorted(ends, jnp.arange(n_rows // tm, dtype=jnp.int32), side="right"),
                              N_EXPERTS - 1).astype(jnp.int32)
    n_used = ends[-1:].astype(jnp.int32)
    rec = meta[:, META_E0:META_R1 + 1].astype(jnp.int32)
    dest0 = offs[rec[:, 0]] + rec[:, 2]
    dest1 = offs[rec[:, 1]] + rec[:, 3]
    return n_rows, tile_expert, n_used, dest0, dest1


def _sparse_moe(h, meta, counts, wgu, wd, g, b):
    t, d = h.shape
    n_rows, tile_expert, n_used, dest0, dest1 = _routing_tables(meta, counts, t)
    td = min(DISPATCH_TOKENS, t)
    xs = _dispatch(h, dest0.reshape(t // td, 1, td), dest1.reshape(t // td, 1, td), n_rows)
    ys = _experts(xs, tile_expert, n_used, wgu, wd)
    tc = min(COMBINE_TOKENS, t)
    return _combine(ys, dest0.reshape(t // tc, 1, tc), dest1.reshape(t // tc, 1, tc), meta, h, g, b)


def _rotate_half_cols(w):
    half = w.shape[-1] // 2
    return jnp.concatenate([-w[..., half:], w[..., :half]], axis=-1)


def _pack_input_proj(w_in, b_in):
    d = w_in.shape[0]
    offs = np.cumsum((0, GLA_QK_W, GLA_QK_W, GLA_V_W, GLA_V_W, GLA_GATE_RANK, GLA_GATE_RANK,
                      MLA_Q_RANK, MLA_KV_RANK, MLA_ROPE, D_MODEL, D_MODEL))

    def seg(i):
        return w_in[:, offs[i]:offs[i + 1]], b_in[offs[i]:offs[i + 1]]

    w = jnp.zeros((d, N_PROJ), F32)
    b = jnp.zeros((N_PROJ,), F32)

    def put(w, b, col, ws, bs):
        return w.at[:, col:col + ws.shape[1]].set(ws), b.at[col:col + ws.shape[1]].set(bs)

    for i, col in ((0, COL_GQ), (1, COL_GK), (2, COL_GV), (3, COL_GR), (9, COL_GA), (10, COL_GB),
                   (6, COL_CQ), (7, COL_CKV), (4, COL_SMALL), (5, COL_SMALL + GLA_GATE_RANK)):
        w, b = put(w, b, col, *seg(i))
    wkr, bkr = seg(8)
    w, b = put(w, b, COL_SMALL + KR_LANE, wkr, bkr)
    w, b = put(w, b, COL_SMALL2 + KR_LANE, _rotate_half_cols(wkr), _rotate_half_cols(bkr))
    return w.astype(BF16), b.reshape(1, N_PROJ)


def _pack_decay(wa2, ba, lane0):
    w = jnp.zeros((LANES, GLA_QK_W), F32).at[lane0:lane0 + GLA_GATE_RANK].set(wa2)
    return w.astype(BF16), ba.reshape(1, GLA_QK_W)


def _pack_mla(w_uq, w_ukv, q_norm_g, kv_norm_g):
    wq = w_uq.reshape(MLA_Q_RANK, MLA_HEADS, MLA_QK)
    rope = wq[..., MLA_NOPE:]
    zq = jnp.zeros((MLA_Q_RANK, MLA_HEADS, HEAD_PAD - MLA_QK), F32)
    wq_p = jnp.concatenate([wq, zq], axis=-1)
    wqr_p = jnp.concatenate([jnp.zeros_like(wq[..., :MLA_NOPE]), _rotate_half_cols(rope), zq], axis=-1)

    def pad_rows(w):
        w = w.reshape(MLA_Q_RANK, MLA_HEADS * HEAD_PAD)
        return jnp.pad(w, ((0, CQ_PAD - MLA_Q_RANK), (0, 0))).astype(BF16)

    wkv = w_ukv.reshape(MLA_KV_RANK, MLA_HEADS, MLA_NOPE + MLA_V)
    wk_p = jnp.concatenate([wkv[..., :MLA_NOPE],
                            jnp.zeros((MLA_KV_RANK, MLA_HEADS, HEAD_PAD - MLA_NOPE), F32)], axis=-1)
    wk_p = wk_p.reshape(MLA_KV_RANK, MLA_HEADS * HEAD_PAD).astype(BF16)
    wv_p = wkv[..., MLA_NOPE:].reshape(MLA_KV_RANK, MLA_HEADS * MLA_V).T.astype(BF16)
    gq = jnp.pad(q_norm_g, (0, CQ_PAD - MLA_Q_RANK)).reshape(1, CQ_PAD)
    return pad_rows(wq_p), pad_rows(wqr_p), wk_p, wv_p, gq, kv_norm_g.reshape(1, MLA_KV_RANK)


def _rope_lane_table():
    inv = ROPE_BASE ** (-jnp.arange(0, MLA_ROPE, 2, dtype=F32) / MLA_ROPE)
    half = MLA_ROPE // 2
    tab = jnp.zeros((LANES,), F32)
    tab = tab.at[KR_LANE:KR_LANE + half].set(inv).at[KR_LANE + half:KR_LANE + MLA_ROPE].set(inv)
    return tab.reshape(1, LANES)


def _pack_router(w_grp, b_grp, w_exp, b_exp):
    d = w_grp.shape[0]
    w = jnp.zeros((d, LANES), F32).at[:, :N_EXPERTS].set(w_exp).at[:, GRP_LANE:GRP_LANE + N_GROUPS].set(w_grp)
    b = jnp.zeros((LANES,), F32).at[:N_EXPERTS].set(b_exp).at[GRP_LANE:GRP_LANE + N_GROUPS].set(b_grp)
    return w, b.reshape(1, LANES)


def kernel(x, positions, ln_emb_g, ln_emb_b, w_in, b_in, gla_wa2_f, gla_ba_f, gla_wa2_b, gla_ba_b, gla_norm_g, mla_q_norm_g, mla_w_uq, mla_kv_norm_g, mla_w_ukv, w_out, ln1_g, ln1_b, w_grp, b_grp, w_exp, b_exp, w_gate, w_up, w_down, ln2_g, ln2_b):
    batch, seq, d = x.shape
    t = batch * seq
    pos = positions.reshape(t, 1).astype(jnp.int32)
    inv_lane = _rope_lane_table()
    h, hb = _layer_norm(x.reshape(t, d), ln_emb_g, ln_emb_b)
    for l in range(DEPTH):
        w_p, b_p = _pack_input_proj(w_in[l], b_in[l])
        proj = _input_proj(hb, w_p, b_p)
        wa_f, ba_f = _pack_decay(gla_wa2_f[l], gla_ba_f[l], 0)
        wa_b, ba_b = _pack_decay(gla_wa2_b[l], gla_ba_b[l], GLA_GATE_RANK)
        o_f = _gla(proj, wa_f, ba_f, batch, seq, reverse=False)
        o_gla = _gla(proj, wa_b, ba_b, batch, seq, reverse=True, o_fwd=o_f,
                     norm_g=gla_norm_g[l].reshape(1, GLA_V_W))
        wq, wqr, wk, wv, gq, gkv = _pack_mla(mla_w_uq[l], mla_w_ukv[l], mla_q_norm_g[l], mla_kv_norm_g[l])
        q2, k2, v = _mla_prep(proj, pos, inv_lane, gq, gkv, wq, wqr, wk, wv)
        o_mla = _attention(q2, k2, v, batch, seq)
        h = _merge_out(o_gla, o_mla, proj, h, w_out[l].astype(BF16),
                           ln1_g[l].reshape(1, d), ln1_b[l].reshape(1, d))
        w_r, b_r = _pack_router(w_grp[l], b_grp[l], w_exp[l], b_exp[l])
        meta, counts = _router(h, w_r, b_r)
        wgu = jnp.concatenate([w_gate[l], w_up[l]], axis=-1).astype(BF16)
        h, hb = _sparse_moe(h, meta, counts, wgu, w_down[l].astype(BF16),
                            ln2_g[l].reshape(1, d), ln2_b[l].reshape(1, d))
    return h.reshape(batch, seq, d)
```

```python
import functools

import numpy as np
import jax
import jax.numpy as jnp
from jax import lax
from jax.experimental import pallas as pl
from jax.experimental.pallas import tpu as pltpu

F32 = jnp.float32
BF16 = jnp.bfloat16

D_MODEL = 1024
DEPTH = 2
GLA_HEADS = 4
GLA_DK = 128
GLA_DV = 256
GLA_GATE_RANK = 16
GLA_TAU = 16.0
MLA_HEADS = 16
MLA_NOPE = 64
MLA_ROPE = 32
MLA_V = 64
MLA_QK = MLA_NOPE + MLA_ROPE
MLA_Q_RANK = 384
MLA_KV_RANK = 128
ROPE_BASE = 10000.0
N_GROUPS = 8
EXPERTS_PER_GROUP = 4
N_EXPERTS = 32
D_EXPERT = 256
GLA_QK_W = GLA_HEADS * GLA_DK
GLA_V_W = GLA_HEADS * GLA_DV
DEEPNORM_ALPHA = (2.0 * DEPTH) ** 0.25
LN_EPS = 1e-5
RMS_EPS = 1e-6

LANES = 128
VMEM_LIMIT_BYTES = 56 * 1024 * 1024

COL_GQ = 0
COL_GK = 512
COL_GV = 1024
COL_GR = 2048
COL_GA = 3072
COL_GB = 4096
COL_CQ = 5120
CQ_PAD = 512
COL_SMALL = 5632
COL_SMALL2 = 5760
COL_CKV = 5888
N_PROJ = 6144
HEAD_PAD = 128
KR_LANE = 64

GLA_CHUNK = 128
NEG_BIG = -1e30


def _cparams(sem):
    return pltpu.CompilerParams(dimension_semantics=sem, vmem_limit_bytes=VMEM_LIMIT_BYTES)


def _ln_rows(x, g, b):
    mu = jnp.mean(x, axis=-1, keepdims=True)
    xc = x - mu
    var = jnp.mean(xc * xc, axis=-1, keepdims=True)
    return xc * lax.rsqrt(var + LN_EPS) * g + b


def _ln_kernel(x_ref, g_ref, b_ref, o_ref, ob_ref):
    y = _ln_rows(x_ref[...], g_ref[...], b_ref[...])
    o_ref[...] = y
    ob_ref[...] = y.astype(BF16)


def _layer_norm(x, g, b, tm=512):
    t, d = x.shape
    return pl.pallas_call(
        _ln_kernel,
        out_shape=(jax.ShapeDtypeStruct((t, d), F32), jax.ShapeDtypeStruct((t, d), BF16)),
        grid=(t // tm,),
        in_specs=[pl.BlockSpec((tm, d), lambda i: (i, 0)),
                  pl.BlockSpec((1, d), lambda i: (0, 0)),
                  pl.BlockSpec((1, d), lambda i: (0, 0))],
        out_specs=(pl.BlockSpec((tm, d), lambda i: (i, 0)), pl.BlockSpec((tm, d), lambda i: (i, 0))),
        compiler_params=_cparams(("parallel",)),
        name="ln_embed",
    )(x, g.reshape(1, d), b.reshape(1, d))


PROJ_COL_CHUNK = 512


def _proj_kernel(x_ref, w_ref, b_ref, o_ref):
    x = x_ref[...]
    for c in range(N_PROJ // PROJ_COL_CHUNK):
        sl = slice(c * PROJ_COL_CHUNK, (c + 1) * PROJ_COL_CHUNK)
        acc = jnp.dot(x, w_ref[:, sl], preferred_element_type=F32) + b_ref[:, sl]
        o_ref[:, sl] = acc.astype(o_ref.dtype)


def _input_proj(hb, w, b, tm=512):
    t, d = hb.shape
    return pl.pallas_call(
        _proj_kernel,
        out_shape=jax.ShapeDtypeStruct((t, N_PROJ), BF16),
        grid=(t // tm,),
        in_specs=[pl.BlockSpec((tm, d), lambda i: (i, 0)),
                  pl.BlockSpec((d, N_PROJ), lambda i: (0, 0)),
                  pl.BlockSpec((1, N_PROJ), lambda i: (0, 0))],
        out_specs=pl.BlockSpec((tm, N_PROJ), lambda i: (i, 0)),
        compiler_params=_cparams(("parallel",)),
        name="input_proj",
    )(hb, w, b)


def _log_sigmoid(x):
    return jnp.minimum(x, 0.0) - jnp.log(1.0 + jnp.exp(-jnp.abs(x)))


def _split_bf16(x):
    hi = x.astype(BF16)
    lo = (x - hi.astype(F32)).astype(BF16)
    return hi, lo


def _gla_kernel(*refs, reverse, n_chunks):
    if reverse:
        (q_ref, k_ref, v_ref, z_ref, wa_ref, ba_ref, of_ref, gr_ref, ng_ref, o_ref, state_ref) = refs
    else:
        (q_ref, k_ref, v_ref, z_ref, wa_ref, ba_ref, o_ref, state_ref) = refs
    c_len = GLA_CHUNK

    @pl.when(pl.program_id(1) == 0)
    def _():
        state_ref[...] = jnp.zeros_like(state_ref)

    row = lax.broadcasted_iota(jnp.int32, (c_len, c_len), 0)
    col = lax.broadcasted_iota(jnp.int32, (c_len, c_len), 1)
    if reverse:
        tri = (col >= row).astype(BF16)
        keep = col > row
        last = 0
    else:
        tri = (col <= row).astype(BF16)
        keep = col <= row
        last = c_len - 1

    def chunk(cc, carry):
        c = (n_chunks - 1 - cc) if reverse else cc
        r0 = pl.multiple_of(c * c_len, c_len)
        rows = pl.ds(r0, c_len)
        z = z_ref[rows, :]
        la = _log_sigmoid(jnp.dot(z, wa_ref[...], preferred_element_type=F32) + ba_ref[...]) * (1.0 / GLA_TAU)
        la_hi, la_lo = _split_bf16(la)
        b_all = (jnp.dot(tri, la_hi, preferred_element_type=F32)
                 + jnp.dot(tri, la_lo, preferred_element_type=F32))
        b_last = b_all[last:last + 1, :]
        q = q_ref[rows, :].astype(F32)
        k = k_ref[rows, :].astype(F32)
        qd_all = (q * (jnp.exp(b_all) * (GLA_DK ** -0.5))).astype(BF16)
        kinv_all = (k * jnp.exp(-b_all)).astype(BF16)
        kend_all = k * jnp.exp(b_last - b_all)
        dec_all = jnp.broadcast_to(jnp.exp(b_last), (c_len, GLA_QK_W))
        for h in range(GLA_HEADS):
            ks = slice(h * GLA_DK, (h + 1) * GLA_DK)
            vs = slice(h * GLA_DV, (h + 1) * GLA_DV)
            v = v_ref[rows, vs]
            qd = qd_all[:, ks]
            scores = lax.dot_general(qd, kinv_all[:, ks], (((1,), (1,)), ((), ())),
                                     preferred_element_type=F32)
            scores = jnp.where(keep, scores, 0.0).astype(BF16)
            state = state_ref[h]
            o = (jnp.dot(scores, v, preferred_element_type=F32)
                 + jnp.dot(qd, state.astype(BF16), preferred_element_type=F32))
            kend_t = kend_all[:, ks].T.astype(BF16)
            dec_t = dec_all[:, ks].T
            dec = jnp.concatenate([dec_t, dec_t], axis=1)
            state_ref[h] = dec * state + jnp.dot(kend_t, v, preferred_element_type=F32)
            if reverse:
                o = o + of_ref[rows, vs]
                ms = jnp.mean(o * o, axis=-1, keepdims=True)
                o = o * lax.rsqrt(ms + RMS_EPS) * ng_ref[:, vs]
                g = gr_ref[rows, vs].astype(F32)
                o = o * (g * jax.nn.sigmoid(g))
            o_ref[rows, vs] = o
        return carry

    lax.fori_loop(0, n_chunks, chunk, 0)


def _gla(proj, wa, ba, batch, seq, *, reverse, o_fwd=None, norm_g=None, ts=1024):
    ts = min(ts, seq)
    nblk = seq // ts
    t = batch * seq

    def rblk(b, i):
        return b * nblk + ((nblk - 1 - i) if reverse else i)

    in_specs = [
        pl.BlockSpec((ts, GLA_QK_W), lambda b, i: (rblk(b, i), COL_GQ // GLA_QK_W)),
        pl.BlockSpec((ts, GLA_QK_W), lambda b, i: (rblk(b, i), COL_GK // GLA_QK_W)),
        pl.BlockSpec((ts, GLA_V_W), lambda b, i: (rblk(b, i), COL_GV // GLA_V_W)),
        pl.BlockSpec((ts, LANES), lambda b, i: (rblk(b, i), COL_SMALL // LANES)),
        pl.BlockSpec((LANES, GLA_QK_W), lambda b, i: (0, 0)),
        pl.BlockSpec((1, GLA_QK_W), lambda b, i: (0, 0)),
    ]
    args = [proj, proj, proj, proj, wa, ba]
    if reverse:
        in_specs += [
            pl.BlockSpec((ts, GLA_V_W), lambda b, i: (rblk(b, i), 0)),
            pl.BlockSpec((ts, GLA_V_W), lambda b, i: (rblk(b, i), COL_GR // GLA_V_W)),
            pl.BlockSpec((1, GLA_V_W), lambda b, i: (0, 0)),
        ]
        args += [o_fwd, proj, norm_g]
    return pl.pallas_call(
        functools.partial(_gla_kernel, reverse=reverse, n_chunks=ts // GLA_CHUNK),
        out_shape=jax.ShapeDtypeStruct((t, GLA_V_W), F32),
        grid=(batch, nblk),
        in_specs=in_specs,
        out_specs=pl.BlockSpec((ts, GLA_V_W), lambda b, i: (rblk(b, i), 0)),
        scratch_shapes=[pltpu.VMEM((GLA_HEADS, GLA_DK, GLA_DV), F32)],
        compiler_params=_cparams(("parallel", "arbitrary")),
        name="gla_bwd" if reverse else "gla_fwd",
    )(*args)


MLA_COL_CHUNK = 512


def _mla_prep_kernel(cq_ref, ckv_ref, sm_ref, sm2_ref, pos_ref, inv_ref, gq_ref, gkv_ref,
                     wq_ref, wqr_ref, wk_ref, wv_ref, q_ref, k_ref, v_ref):
    cq = cq_ref[...].astype(F32)
    msq = jnp.sum(cq * cq, axis=-1, keepdims=True) * (1.0 / MLA_Q_RANK)
    cqn = (cq * lax.rsqrt(msq + RMS_EPS) * gq_ref[...]).astype(BF16)
    ckv = ckv_ref[...].astype(F32)
    mskv = jnp.mean(ckv * ckv, axis=-1, keepdims=True)
    ckvn = (ckv * lax.rsqrt(mskv + RMS_EPS) * gkv_ref[...]).astype(BF16)

    ang = pos_ref[...].astype(F32) * inv_ref[...]
    cos = jnp.cos(ang)
    sin = jnp.sin(ang)
    lane = lax.broadcasted_iota(jnp.int32, ang.shape, 1)
    kr = jnp.where(lane >= KR_LANE, sm_ref[...].astype(F32), 0.0)
    kr = kr * cos + sm2_ref[...].astype(F32) * sin

    heads_per_chunk = MLA_COL_CHUNK // HEAD_PAD
    cos_t = jnp.concatenate([cos] * heads_per_chunk, axis=1)
    sin_t = jnp.concatenate([sin] * heads_per_chunk, axis=1)
    kr_t = jnp.concatenate([kr] * heads_per_chunk, axis=1)
    qscale = (MLA_QK ** -0.5) * float(np.log2(np.e))
    for c in range(MLA_HEADS * HEAD_PAD // MLA_COL_CHUNK):
        sl = slice(c * MLA_COL_CHUNK, (c + 1) * MLA_COL_CHUNK)
        q = jnp.dot(cqn, wq_ref[:, sl], preferred_element_type=F32)
        qr = jnp.dot(cqn, wqr_ref[:, sl], preferred_element_type=F32)
        q_ref[:, sl] = ((q * cos_t + qr * sin_t) * qscale).astype(BF16)
        k = jnp.dot(ckvn, wk_ref[:, sl], preferred_element_type=F32)
        k_ref[:, sl] = (k + kr_t).astype(BF16)
    v_ref[...] = lax.dot_general(wv_ref[...], ckvn, (((1,), (1,)), ((), ())),
                                 preferred_element_type=F32).astype(BF16)


def _mla_prep(proj, pos, inv_lane, gq, gkv, wq, wqr, wk, wv, ts=512):
    t = proj.shape[0]
    hp = MLA_HEADS * HEAD_PAD
    vw = MLA_HEADS * MLA_V
    const = lambda i: (0, 0)
    return pl.pallas_call(
        _mla_prep_kernel,
        out_shape=(jax.ShapeDtypeStruct((t, hp), BF16), jax.ShapeDtypeStruct((t, hp), BF16),
                   jax.ShapeDtypeStruct((vw, t), BF16)),
        grid=(t // ts,),
        in_specs=[
            pl.BlockSpec((ts, CQ_PAD), lambda i: (i, COL_CQ // CQ_PAD)),
            pl.BlockSpec((ts, LANES), lambda i: (i, COL_CKV // LANES)),
            pl.BlockSpec((ts, LANES), lambda i: (i, COL_SMALL // LANES)),
            pl.BlockSpec((ts, LANES), lambda i: (i, COL_SMALL2 // LANES)),
            pl.BlockSpec((ts, 1), lambda i: (i, 0)),
            pl.BlockSpec((1, LANES), const),
            pl.BlockSpec((1, CQ_PAD), const),
            pl.BlockSpec((1, MLA_KV_RANK), const),
            pl.BlockSpec((CQ_PAD, hp), const),
            pl.BlockSpec((CQ_PAD, hp), const),
            pl.BlockSpec((MLA_KV_RANK, hp), const),
            pl.BlockSpec((vw, MLA_KV_RANK), const),
        ],
        out_specs=(pl.BlockSpec((ts, hp), lambda i: (i, 0)), pl.BlockSpec((ts, hp), lambda i: (i, 0)),
                   pl.BlockSpec((vw, ts), lambda i: (0, i))),
        compiler_params=_cparams(("parallel",)),
        name="mla_prep",
    )(proj, proj, proj, proj, pos, inv_lane, gq, gkv, wq, wqr, wk, wv)


SUBLANES = 8


def _attn_kernel(q_ref, k_ref, vt_ref, o_ref, s_scr0, s_scr1, *, tk):
    seq = k_ref.shape[0]
    tq = s_scr0.shape[-1]
    nk = seq // tk
    nq = seq // tq
    s_scrs = (s_scr0, s_scr1)

    def scores_pass_chunk(hh, i, c, m8):
        hs = slice(hh * HEAD_PAD, (hh + 1) * HEAD_PAD)
        q = q_ref[pl.ds(pl.multiple_of(i * tq, tq), tq), hs]
        st = lax.dot_general(k_ref[c * tk:(c + 1) * tk, hs], q, (((1,), (1,)), ((), ())),
                             preferred_element_type=F32)
        s_scrs[hh][c] = st
        return jnp.maximum(m8, jnp.max(st.reshape(tk // SUBLANES, SUBLANES, tq), axis=0))

    def probs_pass_chunk(hh, c, m, l8, acc):
        p = jnp.exp2(s_scrs[hh][c] - m)
        l8 = l8 + jnp.sum(p.reshape(tk // SUBLANES, SUBLANES, tq), axis=0)
        vt = vt_ref[hh * MLA_V:(hh + 1) * MLA_V, c * tk:(c + 1) * tk]
        acc = acc + jnp.dot(vt, p.astype(BF16), preferred_element_type=F32)
        return l8, acc

    def overlapped(i_scores, hh_scores, hh_probs, m):
        m8 = jnp.full((SUBLANES, tq), NEG_BIG, F32)
        l8 = jnp.zeros((SUBLANES, tq), F32)
        acc = jnp.zeros((MLA_V, tq), F32)
        for c in range(nk):
            l8, acc = probs_pass_chunk(hh_probs, c, m, l8, acc)
            m8 = scores_pass_chunk(hh_scores, i_scores, c, m8)
        return acc / jnp.sum(l8, axis=0, keepdims=True), jnp.max(m8, axis=0, keepdims=True)

    m8 = jnp.full((SUBLANES, tq), NEG_BIG, F32)
    for c in range(nk):
        m8 = scores_pass_chunk(0, 0, c, m8)

    def q_tile(i, m_a):
        out_a, m_b = overlapped(i, 1, 0, m_a)
        out_b, m_a_next = overlapped(jnp.minimum(i + 1, nq - 1), 0, 1, m_b)
        o_ref[pl.ds(pl.multiple_of(i * tq, tq), tq), :] = jnp.concatenate([out_a, out_b], axis=0).T
        return m_a_next

    lax.fori_loop(0, nq, q_tile, jnp.max(m8, axis=0, keepdims=True))


def _attention(q2, k2, vt, batch, seq, tq=512, tk=512):
    tq = min(tq, seq)
    tk = min(tk, seq)
    t = batch * seq
    pairs = MLA_HEADS // 2
    return pl.pallas_call(
        functools.partial(_attn_kernel, tk=tk),
        scratch_shapes=[pltpu.VMEM((seq // tk, tk, tq), F32), pltpu.VMEM((seq // tk, tk, tq), F32)],
        out_shape=jax.ShapeDtypeStruct((t, MLA_HEADS * MLA_V), F32),
        grid=(batch, pairs),
        in_specs=[
            pl.BlockSpec((seq, 2 * HEAD_PAD), lambda b, p: (b, p)),
            pl.BlockSpec((seq, 2 * HEAD_PAD), lambda b, p: (b, p)),
            pl.BlockSpec((2 * MLA_V, seq), lambda b, p: (p, b)),
        ],
        out_specs=pl.BlockSpec((seq, 2 * MLA_V), lambda b, p: (b, p)),
        compiler_params=_cparams(("parallel", "parallel")),
        name="mla_attention",
    )(q2, k2, vt)


TOKEN_TILE_ROWS = D_MODEL // LANES


def _rows_to_token_tiles(ref, y):
    n = y.shape[0]
    for s in range(TOKEN_TILE_ROWS):
        ref[pl.ds(s, n, stride=TOKEN_TILE_ROWS), :] = y[:, s * LANES:(s + 1) * LANES]


def _token_tiles_to_rows(ref, n):
    return jnp.concatenate([ref[pl.ds(s, n, stride=TOKEN_TILE_ROWS), :] for s in range(TOKEN_TILE_ROWS)], axis=1)


def _merge_out_kernel(oa_ref, ob_ref, ga_ref, gb_ref, h_ref, w_ref, g_ref, b_ref, o_ref, ot_ref):
    merged = (jax.nn.sigmoid(ga_ref[...].astype(F32)) * oa_ref[...]
              + jax.nn.sigmoid(gb_ref[...].astype(F32)) * ob_ref[...])
    mix = jnp.dot(merged.astype(BF16), w_ref[...], preferred_element_type=F32)
    y = _ln_rows(DEEPNORM_ALPHA * h_ref[...] + mix, g_ref[...], b_ref[...])
    o_ref[...] = y
    _rows_to_token_tiles(ot_ref, y)


def _merge_out(o_gla, o_mla, proj, h, w_out, g, b, tm=512):
    t, d = h.shape
    row = lambda i: (i, 0)
    const = lambda i: (0, 0)
    return pl.pallas_call(
        _merge_out_kernel,
        out_shape=(jax.ShapeDtypeStruct((t, d), F32), jax.ShapeDtypeStruct((t * TOKEN_TILE_ROWS, LANES), F32)),
        grid=(t // tm,),
        in_specs=[
            pl.BlockSpec((tm, d), row),
            pl.BlockSpec((tm, d), row),
            pl.BlockSpec((tm, d), lambda i: (i, COL_GA // D_MODEL)),
            pl.BlockSpec((tm, d), lambda i: (i, COL_GB // D_MODEL)),
            pl.BlockSpec((tm, d), row),
            pl.BlockSpec((d, d), const),
            pl.BlockSpec((1, d), const),
            pl.BlockSpec((1, d), const),
        ],
        out_specs=(pl.BlockSpec((tm, d), row), pl.BlockSpec((tm * TOKEN_TILE_ROWS, LANES), row)),
        compiler_params=_cparams(("parallel",)),
        name="merge_out_ln",
    )(o_gla, o_mla, proj, proj, h, w_out, g, b)


GRP_LANE = N_EXPERTS
META_W0, META_W1, META_E0, META_E1, META_R0, META_R1 = range(6)


def _router_kernel(x_ref, w_ref, b_ref, meta_ref, cnt_ref, carry_ref):
    @pl.when(pl.program_id(0) == 0)
    def _():
        carry_ref[...] = jnp.zeros_like(carry_ref)

    x_hi, x_lo = _split_bf16(x_ref[...])
    w_hi, w_lo = _split_bf16(w_ref[...])
    logits = (jnp.dot(x_hi, w_hi, preferred_element_type=F32)
              + jnp.dot(x_lo, w_hi, preferred_element_type=F32)
              + jnp.dot(x_hi, w_lo, preferred_element_type=F32)) + b_ref[...]
    lane = lax.broadcasted_iota(jnp.int32, logits.shape, 1)
    is_grp = (lane >= GRP_LANE) & (lane < GRP_LANE + N_GROUPS)
    gl = jnp.where(is_grp, logits, NEG_BIG)
    gmax = jnp.max(gl, axis=1, keepdims=True)
    g_lane = jnp.min(jnp.where(gl == gmax, lane, 4 * LANES), axis=1, keepdims=True)
    g_w = 1.0 / jnp.sum(jnp.where(is_grp, jnp.exp(gl - gmax), 0.0), axis=1, keepdims=True)
    lo_lane = (g_lane - GRP_LANE) * EXPERTS_PER_GROUP
    in_grp = (lane >= lo_lane) & (lane < lo_lane + EXPERTS_PER_GROUP)
    el = jnp.where(in_grp, logits, NEG_BIG)
    v1 = jnp.max(el, axis=1, keepdims=True)
    i1 = jnp.min(jnp.where(el == v1, lane, 4 * LANES), axis=1, keepdims=True)
    el2 = jnp.where(lane == i1, NEG_BIG, el)
    v2 = jnp.max(el2, axis=1, keepdims=True)
    i2 = jnp.min(jnp.where(el2 == v2, lane, 4 * LANES), axis=1, keepdims=True)
    e2 = jnp.exp(v2 - v1)
    w1 = g_w / (1.0 + e2)
    w2 = g_w * e2 / (1.0 + e2)

    tm = x_ref.shape[0]
    onehot = jnp.where(lane == i1, 1.0, jnp.where(lane == i2, 1.0, 0.0))
    r = lax.broadcasted_iota(jnp.int32, (tm, tm), 0)
    c = lax.broadcasted_iota(jnp.int32, (tm, tm), 1)
    earlier = (c < r).astype(BF16)
    base = carry_ref[...] + jnp.dot(earlier, onehot.astype(BF16), preferred_element_type=F32)
    rank1 = jnp.sum(jnp.where(lane == i1, base, 0.0), axis=1, keepdims=True)
    rank2 = jnp.sum(jnp.where(lane == i2, base, 0.0), axis=1, keepdims=True)
    meta = jnp.zeros(logits.shape, F32)
    for ln, val in ((META_W0, w1), (META_W1, w2), (META_E0, i1.astype(F32)), (META_E1, i2.astype(F32)),
                    (META_R0, rank1), (META_R1, rank2)):
        meta = jnp.where(lane == ln, val, meta)
    meta_ref[...] = meta
    carry_ref[...] += jnp.sum(onehot, axis=0, keepdims=True)
    cnt_ref[...] = carry_ref[...]


def _router(h, w_r, b_r, tm=512):
    t, d = h.shape
    return pl.pallas_call(
        _router_kernel,
        out_shape=(jax.ShapeDtypeStruct((t, LANES), F32), jax.ShapeDtypeStruct((1, LANES), F32)),
        grid=(t // tm,),
        in_specs=[pl.BlockSpec((tm, d), lambda i: (i, 0)),
                  pl.BlockSpec((d, LANES), lambda i: (0, 0)),
                  pl.BlockSpec((1, LANES), lambda i: (0, 0))],
        out_specs=(pl.BlockSpec((tm, LANES), lambda i: (i, 0)), pl.BlockSpec((1, LANES), lambda i: (0, 0))),
        scratch_shapes=[pltpu.VMEM((1, LANES), F32)],
        compiler_params=_cparams(("arbitrary",)),
        name="router",
    )(h, w_r, b_r)


EXPERT_ROW_TILE = 512
MOE_TOKENS = 256
TOP_K = 2
DMA_ISSUE_UNROLL = 8


def _tile(ref, token):
    return ref.at[pl.ds(pl.multiple_of(token * TOKEN_TILE_ROWS, TOKEN_TILE_ROWS), TOKEN_TILE_ROWS)]


def _slots_kernel(meta_ref, offs_ref, d_ref):
    meta = meta_ref[...]
    offs = offs_ref[...]
    lane = lax.broadcasted_iota(jnp.int32, meta.shape, 1)

    def slot(e_lane, r_lane):
        e = meta[:, e_lane:e_lane + 1].astype(jnp.int32)
        return jnp.sum(jnp.where(lane == e, offs, 0.0), axis=1, keepdims=True) + meta[:, r_lane:r_lane + 1]

    both = jnp.where(lane == 0, slot(META_E0, META_R0), jnp.where(lane == 1, slot(META_E1, META_R1), 0.0))
    d_ref[0] = both.T[:SUBLANES, :].astype(jnp.int32)


def _slots(meta, offs, tt):
    t = meta.shape[0]
    return pl.pallas_call(
        _slots_kernel,
        out_shape=jax.ShapeDtypeStruct((t // tt, SUBLANES, tt), jnp.int32),
        grid=(t // tt,),
        in_specs=[pl.BlockSpec((tt, LANES), lambda i: (i, 0)), pl.BlockSpec((1, LANES), lambda i: (0, 0))],
        out_specs=pl.BlockSpec((1, SUBLANES, tt), lambda i: (i, 0, 0)),
        compiler_params=_cparams(("arbitrary",)),
        name="moe_slots",
    )(meta, offs)


def _dispatch_kernel(d_ref, ht_ref, xs_zero_hbm, xs_hbm, sem):
    del xs_zero_hbm
    tt = d_ref.shape[-1]

    def issue(j, carry):
        src = _tile(ht_ref, j)
        for k in range(TOP_K):
            pltpu.make_async_copy(src, _tile(xs_hbm, d_ref[0, k, j]), sem).start()
        return carry

    lax.fori_loop(0, tt, issue, 0, unroll=DMA_ISSUE_UNROLL)

    def drain(j, carry):
        pltpu.make_async_copy(_tile(ht_ref, 0), _tile(xs_hbm, 0), sem).wait()
        return carry

    lax.fori_loop(0, TOP_K * tt, drain, 0)


def _dispatch(h_tiles, slots, n_rows):
    tt = slots.shape[-1]
    t = h_tiles.shape[0] // TOKEN_TILE_ROWS
    return pl.pallas_call(
        _dispatch_kernel,
        out_shape=jax.ShapeDtypeStruct((n_rows * TOKEN_TILE_ROWS, LANES), F32),
        grid=(t // tt,),
        in_specs=[pl.BlockSpec((1, SUBLANES, tt), lambda i: (i, 0, 0), memory_space=pltpu.SMEM),
                  pl.BlockSpec((tt * TOKEN_TILE_ROWS, LANES), lambda i: (i, 0)),
                  pl.BlockSpec(memory_space=pl.ANY)],
        out_specs=pl.BlockSpec(memory_space=pl.ANY),
        scratch_shapes=[pltpu.SemaphoreType.DMA(())],
        input_output_aliases={2: 0},
        compiler_params=_cparams(("arbitrary",)),
        name="moe_dispatch",
    )(slots, h_tiles, jnp.zeros((n_rows * TOKEN_TILE_ROWS, LANES), F32))


def _expert_kernel(te_ref, nu_ref, xs_ref, wgu_ref, wd_ref, ys_ref):
    del te_ref
    i = pl.program_id(0)
    tm = EXPERT_ROW_TILE

    @pl.when(i < nu_ref[0])
    def _():
        x = _token_tiles_to_rows(xs_ref, tm).astype(BF16)
        gu = jnp.dot(x, wgu_ref[0], preferred_element_type=F32)
        gate = gu[:, :D_EXPERT]
        hid = gate * jax.nn.sigmoid(gate) * gu[:, D_EXPERT:]
        _rows_to_token_tiles(ys_ref, jnp.dot(hid.astype(BF16), wd_ref[0], preferred_element_type=F32))

    @pl.when(i >= nu_ref[0])
    def _():
        ys_ref[...] = jnp.zeros_like(ys_ref)


def _experts(xs, tile_expert, n_used, wgu, wd):
    d = D_MODEL
    tm = EXPERT_ROW_TILE
    blk = pl.BlockSpec((tm * TOKEN_TILE_ROWS, LANES), lambda i, te, nu: (i, 0))
    grid_spec = pltpu.PrefetchScalarGridSpec(
        num_scalar_prefetch=2,
        grid=(xs.shape[0] // (tm * TOKEN_TILE_ROWS),),
        in_specs=[blk,
                  pl.BlockSpec((1, d, 2 * D_EXPERT), lambda i, te, nu: (te[i], 0, 0)),
                  pl.BlockSpec((1, D_EXPERT, d), lambda i, te, nu: (te[i], 0, 0))],
        out_specs=blk,
    )
    return pl.pallas_call(
        _expert_kernel,
        out_shape=jax.ShapeDtypeStruct(xs.shape, F32),
        grid_spec=grid_spec,
        compiler_params=_cparams(("arbitrary",)),
        name="moe_experts",
    )(tile_expert, n_used, xs, wgu, wd)


def _combine_kernel(dc_ref, dn_ref, ys_hbm, meta_ref, h_ref, g_ref, b_ref, o_ref, obf_ref, buf, sem):
    i = pl.program_id(0)
    n = pl.num_programs(0)
    tt = dc_ref.shape[-1]
    slot = lax.rem(i, 2)

    def gather(d_ref, s):
        def issue(j, carry):
            for k in range(TOP_K):
                pltpu.make_async_copy(_tile(ys_hbm, d_ref[0, k, j]), _tile(buf.at[s, k], j), sem.at[s]).start()
            return carry
        lax.fori_loop(0, tt, issue, 0, unroll=DMA_ISSUE_UNROLL)

    @pl.when(i == 0)
    def _():
        gather(dc_ref, 0)

    @pl.when(i + 1 < n)
    def _():
        gather(dn_ref, 1 - slot)

    def drain(j, carry):
        pltpu.make_async_copy(_tile(ys_hbm, 0), _tile(buf.at[slot, 0], 0), sem.at[slot]).wait()
        return carry

    lax.fori_loop(0, TOP_K * tt, drain, 0)
    meta = meta_ref[...]
    ffn = (meta[:, META_W0:META_W0 + 1] * _token_tiles_to_rows(buf.at[slot, 0], tt)
           + meta[:, META_W1:META_W1 + 1] * _token_tiles_to_rows(buf.at[slot, 1], tt))
    y = _ln_rows(DEEPNORM_ALPHA * h_ref[...] + ffn, g_ref[...], b_ref[...])
    o_ref[...] = y
    obf_ref[...] = y.astype(BF16)


def _combine(ys, slots, meta, h, g, b):
    t, d = h.shape
    tt = slots.shape[-1]
    n = t // tt
    cur = pl.BlockSpec((1, SUBLANES, tt), lambda i: (i, 0, 0), memory_space=pltpu.SMEM)
    nxt = pl.BlockSpec((1, SUBLANES, tt), lambda i: (jnp.minimum(i + 1, n - 1), 0, 0), memory_space=pltpu.SMEM)
    row = lambda i: (i, 0)
    const = lambda i: (0, 0)
    return pl.pallas_call(
        _combine_kernel,
        out_shape=(jax.ShapeDtypeStruct((t, d), F32), jax.ShapeDtypeStruct((t, d), BF16)),
        grid=(n,),
        in_specs=[cur, nxt, pl.BlockSpec(memory_space=pl.ANY),
                  pl.BlockSpec((tt, LANES), row), pl.BlockSpec((tt, d), row),
                  pl.BlockSpec((1, d), const), pl.BlockSpec((1, d), const)],
        out_specs=(pl.BlockSpec((tt, d), row), pl.BlockSpec((tt, d), row)),
        scratch_shapes=[pltpu.VMEM((2, TOP_K, tt * TOKEN_TILE_ROWS, LANES), F32), pltpu.SemaphoreType.DMA((2,))],
        compiler_params=_cparams(("arbitrary",)),
        name="moe_combine_ln",
    )(slots, slots, ys, meta, h, g, b)


def _segment_tables(counts, t):
    tm = EXPERT_ROW_TILE
    n_rows = TOP_K * t + N_EXPERTS * tm
    cnt = counts[0, :N_EXPERTS].astype(jnp.int32)
    tiles = (cnt + tm - 1) // tm
    ends = jnp.cumsum(tiles)
    offs = jnp.zeros((1, LANES), F32).at[0, :N_EXPERTS].set(((ends - tiles) * tm).astype(F32))
    tile_expert = jnp.minimum(jnp.searchsorted(ends, jnp.arange(n_rows // tm, dtype=jnp.int32), side="right"),
                              N_EXPERTS - 1).astype(jnp.int32)
    return n_rows, offs, tile_expert, ends[-1:].astype(jnp.int32)


def _sparse_moe(h, h_tiles, meta, counts, wgu, wd, g, b):
    t = h.shape[0]
    n_rows, offs, tile_expert, n_used = _segment_tables(counts, t)
    slots = _slots(meta, offs, min(MOE_TOKENS, t))
    xs = _dispatch(h_tiles, slots, n_rows)
    ys = _experts(xs, tile_expert, n_used, wgu, wd)
    return _combine(ys, slots, meta, h, g, b)


def _rotate_half_cols(w):
    half = w.shape[-1] // 2
    return jnp.concatenate([-w[..., half:], w[..., :half]], axis=-1)


def _pack_input_proj(w_in, b_in):
    d = w_in.shape[0]
    offs = np.cumsum((0, GLA_QK_W, GLA_QK_W, GLA_V_W, GLA_V_W, GLA_GATE_RANK, GLA_GATE_RANK,
                      MLA_Q_RANK, MLA_KV_RANK, MLA_ROPE, D_MODEL, D_MODEL))

    def seg(i):
        return w_in[:, offs[i]:offs[i + 1]], b_in[offs[i]:offs[i + 1]]

    w = jnp.zeros((d, N_PROJ), F32)
    b = jnp.zeros((N_PROJ,), F32)

    def put(w, b, col, ws, bs):
        return w.at[:, col:col + ws.shape[1]].set(ws), b.at[col:col + ws.shape[1]].set(bs)

    for i, col in ((0, COL_GQ), (1, COL_GK), (2, COL_GV), (3, COL_GR), (9, COL_GA), (10, COL_GB),
                   (6, COL_CQ), (7, COL_CKV), (4, COL_SMALL), (5, COL_SMALL + GLA_GATE_RANK)):
        w, b = put(w, b, col, *seg(i))
    wkr, bkr = seg(8)
    w, b = put(w, b, COL_SMALL + KR_LANE, wkr, bkr)
    w, b = put(w, b, COL_SMALL2 + KR_LANE, _rotate_half_cols(wkr), _rotate_half_cols(bkr))
    return w.astype(BF16), b.reshape(1, N_PROJ)


def _pack_decay(wa2, ba, lane0):
    w = jnp.zeros((LANES, GLA_QK_W), F32).at[lane0:lane0 + GLA_GATE_RANK].set(wa2)
    return w.astype(BF16), ba.reshape(1, GLA_QK_W)


def _pack_mla(w_uq, w_ukv, q_norm_g, kv_norm_g):
    wq = w_uq.reshape(MLA_Q_RANK, MLA_HEADS, MLA_QK)
    rope = wq[..., MLA_NOPE:]
    zq = jnp.zeros((MLA_Q_RANK, MLA_HEADS, HEAD_PAD - MLA_QK), F32)
    wq_p = jnp.concatenate([wq, zq], axis=-1)
    wqr_p = jnp.concatenate([jnp.zeros_like(wq[..., :MLA_NOPE]), _rotate_half_cols(rope), zq], axis=-1)

    def pad_rows(w):
        w = w.reshape(MLA_Q_RANK, MLA_HEADS * HEAD_PAD)
        return jnp.pad(w, ((0, CQ_PAD - MLA_Q_RANK), (0, 0))).astype(BF16)

    wkv = w_ukv.reshape(MLA_KV_RANK, MLA_HEADS, MLA_NOPE + MLA_V)
    wk_p = jnp.concatenate([wkv[..., :MLA_NOPE],
                            jnp.zeros((MLA_KV_RANK, MLA_HEADS, HEAD_PAD - MLA_NOPE), F32)], axis=-1)
    wk_p = wk_p.reshape(MLA_KV_RANK, MLA_HEADS * HEAD_PAD).astype(BF16)
    wv_p = wkv[..., MLA_NOPE:].reshape(MLA_KV_RANK, MLA_HEADS * MLA_V).T.astype(BF16)
    gq = jnp.pad(q_norm_g, (0, CQ_PAD - MLA_Q_RANK)).reshape(1, CQ_PAD)
    return pad_rows(wq_p), pad_rows(wqr_p), wk_p, wv_p, gq, kv_norm_g.reshape(1, MLA_KV_RANK)


def _rope_lane_table():
    inv = ROPE_BASE ** (-jnp.arange(0, MLA_ROPE, 2, dtype=F32) / MLA_ROPE)
    half = MLA_ROPE // 2
    tab = jnp.zeros((LANES,), F32)
    tab = tab.at[KR_LANE:KR_LANE + half].set(inv).at[KR_LANE + half:KR_LANE + MLA_ROPE].set(inv)
    return tab.reshape(1, LANES)


def _pack_router(w_grp, b_grp, w_exp, b_exp):
    d = w_grp.shape[0]
    w = jnp.zeros((d, LANES), F32).at[:, :N_EXPERTS].set(w_exp).at[:, GRP_LANE:GRP_LANE + N_GROUPS].set(w_grp)
    b = jnp.zeros((LANES,), F32).at[:N_EXPERTS].set(b_exp).at[GRP_LANE:GRP_LANE + N_GROUPS].set(b_grp)
    return w, b.reshape(1, LANES)


def kernel(x, positions, ln_emb_g, ln_emb_b, w_in, b_in, gla_wa2_f, gla_ba_f, gla_wa2_b, gla_ba_b, gla_norm_g, mla_q_norm_g, mla_w_uq, mla_kv_norm_g, mla_w_ukv, w_out, ln1_g, ln1_b, w_grp, b_grp, w_exp, b_exp, w_gate, w_up, w_down, ln2_g, ln2_b):
    batch, seq, d = x.shape
    t = batch * seq
    pos = positions.reshape(t, 1).astype(jnp.int32)
    inv_lane = _rope_lane_table()
    h, hb = _layer_norm(x.reshape(t, d), ln_emb_g, ln_emb_b)
    for l in range(DEPTH):
        w_p, b_p = _pack_input_proj(w_in[l], b_in[l])
        proj = _input_proj(hb, w_p, b_p)
        wa_f, ba_f = _pack_decay(gla_wa2_f[l], gla_ba_f[l], 0)
        wa_b, ba_b = _pack_decay(gla_wa2_b[l], gla_ba_b[l], GLA_GATE_RANK)
        o_f = _gla(proj, wa_f, ba_f, batch, seq, reverse=False)
        o_gla = _gla(proj, wa_b, ba_b, batch, seq, reverse=True, o_fwd=o_f,
                     norm_g=gla_norm_g[l].reshape(1, GLA_V_W))
        wq, wqr, wk, wv, gq, gkv = _pack_mla(mla_w_uq[l], mla_w_ukv[l], mla_q_norm_g[l], mla_kv_norm_g[l])
        q2, k2, v = _mla_prep(proj, pos, inv_lane, gq, gkv, wq, wqr, wk, wv)
        o_mla = _attention(q2, k2, v, batch, seq)
        h, h_tiles = _merge_out(o_gla, o_mla, proj, h, w_out[l].astype(BF16),
                           ln1_g[l].reshape(1, d), ln1_b[l].reshape(1, d))
        w_r, b_r = _pack_router(w_grp[l], b_grp[l], w_exp[l], b_exp[l])
        meta, counts = _router(h, w_r, b_r)
        wgu = jnp.concatenate([w_gate[l], w_up[l]], axis=-1).astype(BF16)
        h, hb = _sparse_moe(h, h_tiles, meta, counts, wgu, w_down[l].astype(BF16),
                            ln2_g[l].reshape(1, d), ln2_b[l].reshape(1, d))
    return h.reshape(batch, seq, d)
```

```python
import functools

import numpy as np
import jax
import jax.numpy as jnp
from jax import lax
from jax.experimental import pallas as pl
from jax.experimental.pallas import tpu as pltpu

F32 = jnp.float32
BF16 = jnp.bfloat16

D_MODEL = 1024
DEPTH = 2
GLA_HEADS = 4
GLA_DK = 128
GLA_DV = 256
GLA_GATE_RANK = 16
GLA_TAU = 16.0
MLA_HEADS = 16
MLA_NOPE = 64
MLA_ROPE = 32
MLA_V = 64
MLA_QK = MLA_NOPE + MLA_ROPE
MLA_Q_RANK = 384
MLA_KV_RANK = 128
ROPE_BASE = 10000.0
N_GROUPS = 8
EXPERTS_PER_GROUP = 4
N_EXPERTS = 32
D_EXPERT = 256
GLA_QK_W = GLA_HEADS * GLA_DK
GLA_V_W = GLA_HEADS * GLA_DV
DEEPNORM_ALPHA = (2.0 * DEPTH) ** 0.25
LN_EPS = 1e-5
RMS_EPS = 1e-6

LANES = 128
VMEM_LIMIT_BYTES = 56 * 1024 * 1024

COL_GQ = 0
COL_GK = 512
COL_GV = 1024
COL_GR = 2048
COL_GA = 3072
COL_GB = 4096
COL_CQ = 5120
CQ_PAD = 512
COL_SMALL = 5632
COL_SMALL2 = 5760
COL_CKV = 5888
N_PROJ = 6144
HEAD_PAD = 128
KR_LANE = 64

GLA_CHUNK = 128
NEG_BIG = -1e30


def _cparams(sem):
    return pltpu.CompilerParams(dimension_semantics=sem, vmem_limit_bytes=VMEM_LIMIT_BYTES)


def _ln_rows(x, g, b):
    mu = jnp.mean(x, axis=-1, keepdims=True)
    xc = x - mu
    var = jnp.mean(xc * xc, axis=-1, keepdims=True)
    return xc * lax.rsqrt(var + LN_EPS) * g + b


def _ln_kernel(x_ref, g_ref, b_ref, o_ref, ob_ref):
    y = _ln_rows(x_ref[...], g_ref[...], b_ref[...])
    o_ref[...] = y
    ob_ref[...] = y.astype(BF16)


def _layer_norm(x, g, b, tm=512):
    t, d = x.shape
    return pl.pallas_call(
        _ln_kernel,
        out_shape=(jax.ShapeDtypeStruct((t, d), F32), jax.ShapeDtypeStruct((t, d), BF16)),
        grid=(t // tm,),
        in_specs=[pl.BlockSpec((tm, d), lambda i: (i, 0)),
                  pl.BlockSpec((1, d), lambda i: (0, 0)),
                  pl.BlockSpec((1, d), lambda i: (0, 0))],
        out_specs=(pl.BlockSpec((tm, d), lambda i: (i, 0)), pl.BlockSpec((tm, d), lambda i: (i, 0))),
        compiler_params=_cparams(("parallel",)),
        name="ln_embed",
    )(x, g.reshape(1, d), b.reshape(1, d))


PROJ_COL_CHUNK = 512


def _proj_kernel(x_ref, w_ref, b_ref, o_ref):
    x = x_ref[...]
    for c in range(N_PROJ // PROJ_COL_CHUNK):
        sl = slice(c * PROJ_COL_CHUNK, (c + 1) * PROJ_COL_CHUNK)
        acc = jnp.dot(x, w_ref[:, sl], preferred_element_type=F32) + b_ref[:, sl]
        o_ref[:, sl] = acc.astype(o_ref.dtype)


def _input_proj(hb, w, b, tm=512):
    t, d = hb.shape
    return pl.pallas_call(
        _proj_kernel,
        out_shape=jax.ShapeDtypeStruct((t, N_PROJ), BF16),
        grid=(t // tm,),
        in_specs=[pl.BlockSpec((tm, d), lambda i: (i, 0)),
                  pl.BlockSpec((d, N_PROJ), lambda i: (0, 0)),
                  pl.BlockSpec((1, N_PROJ), lambda i: (0, 0))],
        out_specs=pl.BlockSpec((tm, N_PROJ), lambda i: (i, 0)),
        compiler_params=_cparams(("parallel",)),
        name="input_proj",
    )(hb, w, b)


def _log_sigmoid(x):
    return jnp.minimum(x, 0.0) - jnp.log(1.0 + jnp.exp(-jnp.abs(x)))


def _split_bf16(x):
    hi = x.astype(BF16)
    lo = (x - hi.astype(F32)).astype(BF16)
    return hi, lo


def _gla_kernel(*refs, reverse, n_chunks):
    if reverse:
        (q_ref, k_ref, v_ref, z_ref, wa_ref, ba_ref, of_ref, gr_ref, ng_ref, o_ref, state_ref) = refs
    else:
        (q_ref, k_ref, v_ref, z_ref, wa_ref, ba_ref, o_ref, state_ref) = refs
    c_len = GLA_CHUNK

    @pl.when(pl.program_id(1) == 0)
    def _():
        state_ref[...] = jnp.zeros_like(state_ref)

    row = lax.broadcasted_iota(jnp.int32, (c_len, c_len), 0)
    col = lax.broadcasted_iota(jnp.int32, (c_len, c_len), 1)
    if reverse:
        tri = (col >= row).astype(BF16)
        keep = col > row
        last = 0
    else:
        tri = (col <= row).astype(BF16)
        keep = col <= row
        last = c_len - 1

    def chunk(cc, carry):
        c = (n_chunks - 1 - cc) if reverse else cc
        r0 = pl.multiple_of(c * c_len, c_len)
        rows = pl.ds(r0, c_len)
        z = z_ref[rows, :]
        la = _log_sigmoid(jnp.dot(z, wa_ref[...], preferred_element_type=F32) + ba_ref[...]) * (1.0 / GLA_TAU)
        la_hi, la_lo = _split_bf16(la)
        b_all = (jnp.dot(tri, la_hi, preferred_element_type=F32)
                 + jnp.dot(tri, la_lo, preferred_element_type=F32))
        b_last = b_all[last:last + 1, :]
        q = q_ref[rows, :].astype(F32)
        k = k_ref[rows, :].astype(F32)
        qd_all = (q * (jnp.exp(b_all) * (GLA_DK ** -0.5))).astype(BF16)
        kinv_all = (k * jnp.exp(-b_all)).astype(BF16)
        kend_all = k * jnp.exp(b_last - b_all)
        dec_all = jnp.broadcast_to(jnp.exp(b_last), (c_len, GLA_QK_W))
        for h in range(GLA_HEADS):
            ks = slice(h * GLA_DK, (h + 1) * GLA_DK)
            vs = slice(h * GLA_DV, (h + 1) * GLA_DV)
            v = v_ref[rows, vs]
            qd = qd_all[:, ks]
            scores = lax.dot_general(qd, kinv_all[:, ks], (((1,), (1,)), ((), ())),
                                     preferred_element_type=F32)
            scores = jnp.where(keep, scores, 0.0).astype(BF16)
            state = state_ref[h]
            o = (jnp.dot(scores, v, preferred_element_type=F32)
                 + jnp.dot(qd, state.astype(BF16), preferred_element_type=F32))
            kend_t = kend_all[:, ks].T.astype(BF16)
            dec_t = dec_all[:, ks].T
            dec = jnp.concatenate([dec_t, dec_t], axis=1)
            state_ref[h] = dec * state + jnp.dot(kend_t, v, preferred_element_type=F32)
            if reverse:
                o = o + of_ref[rows, vs]
                ms = jnp.mean(o * o, axis=-1, keepdims=True)
                o = o * lax.rsqrt(ms + RMS_EPS) * ng_ref[:, vs]
                g = gr_ref[rows, vs].astype(F32)
                o = o * (g * jax.nn.sigmoid(g))
            o_ref[rows, vs] = o
        return carry

    lax.fori_loop(0, n_chunks, chunk, 0)


def _gla(proj, wa, ba, batch, seq, *, reverse, o_fwd=None, norm_g=None, ts=1024):
    ts = min(ts, seq)
    nblk = seq // ts
    t = batch * seq

    def rblk(b, i):
        return b * nblk + ((nblk - 1 - i) if reverse else i)

    in_specs = [
        pl.BlockSpec((ts, GLA_QK_W), lambda b, i: (rblk(b, i), COL_GQ // GLA_QK_W)),
        pl.BlockSpec((ts, GLA_QK_W), lambda b, i: (rblk(b, i), COL_GK // GLA_QK_W)),
        pl.BlockSpec((ts, GLA_V_W), lambda b, i: (rblk(b, i), COL_GV // GLA_V_W)),
        pl.BlockSpec((ts, LANES), lambda b, i: (rblk(b, i), COL_SMALL // LANES)),
        pl.BlockSpec((LANES, GLA_QK_W), lambda b, i: (0, 0)),
        pl.BlockSpec((1, GLA_QK_W), lambda b, i: (0, 0)),
    ]
    args = [proj, proj, proj, proj, wa, ba]
    if reverse:
        in_specs += [
            pl.BlockSpec((ts, GLA_V_W), lambda b, i: (rblk(b, i), 0)),
            pl.BlockSpec((ts, GLA_V_W), lambda b, i: (rblk(b, i), COL_GR // GLA_V_W)),
            pl.BlockSpec((1, GLA_V_W), lambda b, i: (0, 0)),
        ]
        args += [o_fwd, proj, norm_g]
    return pl.pallas_call(
        functools.partial(_gla_kernel, reverse=reverse, n_chunks=ts // GLA_CHUNK),
        out_shape=jax.ShapeDtypeStruct((t, GLA_V_W), F32),
        grid=(batch, nblk),
        in_specs=in_specs,
        out_specs=pl.BlockSpec((ts, GLA_V_W), lambda b, i: (rblk(b, i), 0)),
        scratch_shapes=[pltpu.VMEM((GLA_HEADS, GLA_DK, GLA_DV), F32)],
        compiler_params=_cparams(("parallel", "arbitrary")),
        name="gla_bwd" if reverse else "gla_fwd",
    )(*args)


MLA_COL_CHUNK = 512


def _mla_prep_kernel(cq_ref, ckv_ref, sm_ref, sm2_ref, pos_ref, inv_ref, gq_ref, gkv_ref,
                     wq_ref, wqr_ref, wk_ref, wv_ref, q_ref, k_ref, v_ref):
    cq = cq_ref[...].astype(F32)
    msq = jnp.sum(cq * cq, axis=-1, keepdims=True) * (1.0 / MLA_Q_RANK)
    cqn = (cq * lax.rsqrt(msq + RMS_EPS) * gq_ref[...]).astype(BF16)
    ckv = ckv_ref[...].astype(F32)
    mskv = jnp.mean(ckv * ckv, axis=-1, keepdims=True)
    ckvn = (ckv * lax.rsqrt(mskv + RMS_EPS) * gkv_ref[...]).astype(BF16)

    ang = pos_ref[...].astype(F32) * inv_ref[...]
    cos = jnp.cos(ang)
    sin = jnp.sin(ang)
    lane = lax.broadcasted_iota(jnp.int32, ang.shape, 1)
    kr = jnp.where(lane >= KR_LANE, sm_ref[...].astype(F32), 0.0)
    kr = kr * cos + sm2_ref[...].astype(F32) * sin

    heads_per_chunk = MLA_COL_CHUNK // HEAD_PAD
    cos_t = jnp.concatenate([cos] * heads_per_chunk, axis=1)
    sin_t = jnp.concatenate([sin] * heads_per_chunk, axis=1)
    kr_t = jnp.concatenate([kr] * heads_per_chunk, axis=1)
    qscale = (MLA_QK ** -0.5) * float(np.log2(np.e))
    for c in range(MLA_HEADS * HEAD_PAD // MLA_COL_CHUNK):
        sl = slice(c * MLA_COL_CHUNK, (c + 1) * MLA_COL_CHUNK)
        q = jnp.dot(cqn, wq_ref[:, sl], preferred_element_type=F32)
        qr = jnp.dot(cqn, wqr_ref[:, sl], preferred_element_type=F32)
        q_ref[:, sl] = ((q * cos_t + qr * sin_t) * qscale).astype(BF16)
        k = jnp.dot(ckvn, wk_ref[:, sl], preferred_element_type=F32)
        k_ref[:, sl] = (k + kr_t).astype(BF16)
    v_ref[...] = lax.dot_general(wv_ref[...], ckvn, (((1,), (1,)), ((), ())),
                                 preferred_element_type=F32).astype(BF16)


def _mla_prep(proj, pos, inv_lane, gq, gkv, wq, wqr, wk, wv, ts=512):
    t = proj.shape[0]
    hp = MLA_HEADS * HEAD_PAD
    vw = MLA_HEADS * MLA_V
    const = lambda i: (0, 0)
    return pl.pallas_call(
        _mla_prep_kernel,
        out_shape=(jax.ShapeDtypeStruct((t, hp), BF16), jax.ShapeDtypeStruct((t, hp), BF16),
                   jax.ShapeDtypeStruct((vw, t), BF16)),
        grid=(t // ts,),
        in_specs=[
            pl.BlockSpec((ts, CQ_PAD), lambda i: (i, COL_CQ // CQ_PAD)),
            pl.BlockSpec((ts, LANES), lambda i: (i, COL_CKV // LANES)),
            pl.BlockSpec((ts, LANES), lambda i: (i, COL_SMALL // LANES)),
            pl.BlockSpec((ts, LANES), lambda i: (i, COL_SMALL2 // LANES)),
            pl.BlockSpec((ts, 1), lambda i: (i, 0)),
            pl.BlockSpec((1, LANES), const),
            pl.BlockSpec((1, CQ_PAD), const),
            pl.BlockSpec((1, MLA_KV_RANK), const),
            pl.BlockSpec((CQ_PAD, hp), const),
            pl.BlockSpec((CQ_PAD, hp), const),
            pl.BlockSpec((MLA_KV_RANK, hp), const),
            pl.BlockSpec((vw, MLA_KV_RANK), const),
        ],
        out_specs=(pl.BlockSpec((ts, hp), lambda i: (i, 0)), pl.BlockSpec((ts, hp), lambda i: (i, 0)),
                   pl.BlockSpec((vw, ts), lambda i: (0, i))),
        compiler_params=_cparams(("parallel",)),
        name="mla_prep",
    )(proj, proj, proj, proj, pos, inv_lane, gq, gkv, wq, wqr, wk, wv)


SUBLANES = 8


def _attn_kernel(q_ref, k_ref, vt_ref, o_ref, s_scr0, s_scr1, *, tk):
    seq = k_ref.shape[0]
    tq = s_scr0.shape[-1]
    nk = seq // tk
    nq = seq // tq
    s_scrs = (s_scr0, s_scr1)

    def scores_pass_chunk(hh, i, c, m8):
        hs = slice(hh * HEAD_PAD, (hh + 1) * HEAD_PAD)
        q = q_ref[pl.ds(pl.multiple_of(i * tq, tq), tq), hs]
        st = lax.dot_general(k_ref[c * tk:(c + 1) * tk, hs], q, (((1,), (1,)), ((), ())),
                             preferred_element_type=F32)
        s_scrs[hh][c] = st
        return jnp.maximum(m8, jnp.max(st.reshape(tk // SUBLANES, SUBLANES, tq), axis=0))

    def probs_pass_chunk(hh, c, m, l8, acc):
        p = jnp.exp2(s_scrs[hh][c] - m)
        l8 = l8 + jnp.sum(p.reshape(tk // SUBLANES, SUBLANES, tq), axis=0)
        vt = vt_ref[hh * MLA_V:(hh + 1) * MLA_V, c * tk:(c + 1) * tk]
        acc = acc + jnp.dot(vt, p.astype(BF16), preferred_element_type=F32)
        return l8, acc

    def overlapped(i_scores, hh_scores, hh_probs, m):
        m8 = jnp.full((SUBLANES, tq), NEG_BIG, F32)
        l8 = jnp.zeros((SUBLANES, tq), F32)
        acc = jnp.zeros((MLA_V, tq), F32)
        for c in range(nk):
            l8, acc = probs_pass_chunk(hh_probs, c, m, l8, acc)
            m8 = scores_pass_chunk(hh_scores, i_scores, c, m8)
        return acc / jnp.sum(l8, axis=0, keepdims=True), jnp.max(m8, axis=0, keepdims=True)

    m8 = jnp.full((SUBLANES, tq), NEG_BIG, F32)
    for c in range(nk):
        m8 = scores_pass_chunk(0, 0, c, m8)

    def q_tile(i, m_a):
        out_a, m_b = overlapped(i, 1, 0, m_a)
        out_b, m_a_next = overlapped(jnp.minimum(i + 1, nq - 1), 0, 1, m_b)
        o_ref[pl.ds(pl.multiple_of(i * tq, tq), tq), :] = jnp.concatenate([out_a, out_b], axis=0).T
        return m_a_next

    lax.fori_loop(0, nq, q_tile, jnp.max(m8, axis=0, keepdims=True))


def _attention(q2, k2, vt, batch, seq, tq=512, tk=512):
    tq = min(tq, seq)
    tk = min(tk, seq)
    t = batch * seq
    pairs = MLA_HEADS // 2
    return pl.pallas_call(
        functools.partial(_attn_kernel, tk=tk),
        scratch_shapes=[pltpu.VMEM((seq // tk, tk, tq), F32), pltpu.VMEM((seq // tk, tk, tq), F32)],
        out_shape=jax.ShapeDtypeStruct((t, MLA_HEADS * MLA_V), F32),
        grid=(batch, pairs),
        in_specs=[
            pl.BlockSpec((seq, 2 * HEAD_PAD), lambda b, p: (b, p)),
            pl.BlockSpec((seq, 2 * HEAD_PAD), lambda b, p: (b, p)),
            pl.BlockSpec((2 * MLA_V, seq), lambda b, p: (p, b)),
        ],
        out_specs=pl.BlockSpec((seq, 2 * MLA_V), lambda b, p: (b, p)),
        compiler_params=_cparams(("parallel", "parallel")),
        name="mla_attention",
    )(q2, k2, vt)


TOKEN_TILE_ROWS = D_MODEL // LANES


def _rows_to_token_tiles(ref, y):
    n = y.shape[0]
    for s in range(TOKEN_TILE_ROWS):
        ref[pl.ds(s, n, stride=TOKEN_TILE_ROWS), :] = y[:, s * LANES:(s + 1) * LANES]


def _token_tiles_to_rows(ref, n):
    return jnp.concatenate([ref[pl.ds(s, n, stride=TOKEN_TILE_ROWS), :] for s in range(TOKEN_TILE_ROWS)], axis=1)


def _merge_out_kernel(oa_ref, ob_ref, ga_ref, gb_ref, h_ref, w_ref, g_ref, b_ref, o_ref, ot_ref):
    merged = (jax.nn.sigmoid(ga_ref[...].astype(F32)) * oa_ref[...]
              + jax.nn.sigmoid(gb_ref[...].astype(F32)) * ob_ref[...])
    mix = jnp.dot(merged.astype(BF16), w_ref[...], preferred_element_type=F32)
    y = _ln_rows(DEEPNORM_ALPHA * h_ref[...] + mix, g_ref[...], b_ref[...])
    o_ref[...] = y
    _rows_to_token_tiles(ot_ref, y)


def _merge_out(o_gla, o_mla, proj, h, w_out, g, b, tm=512):
    t, d = h.shape
    row = lambda i: (i, 0)
    const = lambda i: (0, 0)
    return pl.pallas_call(
        _merge_out_kernel,
        out_shape=(jax.ShapeDtypeStruct((t, d), F32), jax.ShapeDtypeStruct((t * TOKEN_TILE_ROWS, LANES), F32)),
        grid=(t // tm,),
        in_specs=[
            pl.BlockSpec((tm, d), row),
            pl.BlockSpec((tm, d), row),
            pl.BlockSpec((tm, d), lambda i: (i, COL_GA // D_MODEL)),
            pl.BlockSpec((tm, d), lambda i: (i, COL_GB // D_MODEL)),
            pl.BlockSpec((tm, d), row),
            pl.BlockSpec((d, d), const),
            pl.BlockSpec((1, d), const),
            pl.BlockSpec((1, d), const),
        ],
        out_specs=(pl.BlockSpec((tm, d), row), pl.BlockSpec((tm * TOKEN_TILE_ROWS, LANES), row)),
        compiler_params=_cparams(("parallel",)),
        name="merge_out_ln",
    )(o_gla, o_mla, proj, proj, h, w_out, g, b)


GRP_LANE = N_EXPERTS
META_W0, META_W1, META_E0, META_E1, META_R0, META_R1 = range(6)


def _router_kernel(x_ref, w_ref, b_ref, meta_ref, cnt_ref, carry_ref):
    @pl.when(pl.program_id(0) == 0)
    def _():
        carry_ref[...] = jnp.zeros_like(carry_ref)

    x_hi, x_lo = _split_bf16(x_ref[...])
    w_hi, w_lo = _split_bf16(w_ref[...])
    logits = (jnp.dot(x_hi, w_hi, preferred_element_type=F32)
              + jnp.dot(x_lo, w_hi, preferred_element_type=F32)
              + jnp.dot(x_hi, w_lo, preferred_element_type=F32)) + b_ref[...]
    lane = lax.broadcasted_iota(jnp.int32, logits.shape, 1)
    is_grp = (lane >= GRP_LANE) & (lane < GRP_LANE + N_GROUPS)
    gl = jnp.where(is_grp, logits, NEG_BIG)
    gmax = jnp.max(gl, axis=1, keepdims=True)
    g_lane = jnp.min(jnp.where(gl == gmax, lane, 4 * LANES), axis=1, keepdims=True)
    g_w = 1.0 / jnp.sum(jnp.where(is_grp, jnp.exp(gl - gmax), 0.0), axis=1, keepdims=True)
    lo_lane = (g_lane - GRP_LANE) * EXPERTS_PER_GROUP
    in_grp = (lane >= lo_lane) & (lane < lo_lane + EXPERTS_PER_GROUP)
    el = jnp.where(in_grp, logits, NEG_BIG)
    v1 = jnp.max(el, axis=1, keepdims=True)
    i1 = jnp.min(jnp.where(el == v1, lane, 4 * LANES), axis=1, keepdims=True)
    el2 = jnp.where(lane == i1, NEG_BIG, el)
    v2 = jnp.max(el2, axis=1, keepdims=True)
    i2 = jnp.min(jnp.where(el2 == v2, lane, 4 * LANES), axis=1, keepdims=True)
    e2 = jnp.exp(v2 - v1)
    w1 = g_w / (1.0 + e2)
    w2 = g_w * e2 / (1.0 + e2)

    tm = x_ref.shape[0]
    onehot = jnp.where(lane == i1, 1.0, jnp.where(lane == i2, 1.0, 0.0))
    r = lax.broadcasted_iota(jnp.int32, (tm, tm), 0)
    c = lax.broadcasted_iota(jnp.int32, (tm, tm), 1)
    earlier = (c < r).astype(BF16)
    base = carry_ref[...] + jnp.dot(earlier, onehot.astype(BF16), preferred_element_type=F32)
    rank1 = jnp.sum(jnp.where(lane == i1, base, 0.0), axis=1, keepdims=True)
    rank2 = jnp.sum(jnp.where(lane == i2, base, 0.0), axis=1, keepdims=True)
    meta = jnp.zeros(logits.shape, F32)
    for ln, val in ((META_W0, w1), (META_W1, w2), (META_E0, i1.astype(F32)), (META_E1, i2.astype(F32)),
                    (META_R0, rank1), (META_R1, rank2)):
        meta = jnp.where(lane == ln, val, meta)
    meta_ref[...] = meta
    carry_ref[...] += jnp.sum(onehot, axis=0, keepdims=True)
    cnt_ref[...] = carry_ref[...]


def _router(h, w_r, b_r, tm=512):
    t, d = h.shape
    return pl.pallas_call(
        _router_kernel,
        out_shape=(jax.ShapeDtypeStruct((t, LANES), F32), jax.ShapeDtypeStruct((1, LANES), F32)),
        grid=(t // tm,),
        in_specs=[pl.BlockSpec((tm, d), lambda i: (i, 0)),
                  pl.BlockSpec((d, LANES), lambda i: (0, 0)),
                  pl.BlockSpec((1, LANES), lambda i: (0, 0))],
        out_specs=(pl.BlockSpec((tm, LANES), lambda i: (i, 0)), pl.BlockSpec((1, LANES), lambda i: (0, 0))),
        scratch_shapes=[pltpu.VMEM((1, LANES), F32)],
        compiler_params=_cparams(("arbitrary",)),
        name="router",
    )(h, w_r, b_r)


EXPERT_ROW_TILE = 512
MOE_TOKENS = 256
TOP_K = 2
DMA_ISSUE_UNROLL = 8


def _tile(ref, token):
    return ref.at[pl.ds(pl.multiple_of(token * TOKEN_TILE_ROWS, TOKEN_TILE_ROWS), TOKEN_TILE_ROWS)]


def _slots_kernel(meta_ref, offs_ref, d_ref):
    meta = meta_ref[...]
    offs = offs_ref[...]
    lane = lax.broadcasted_iota(jnp.int32, meta.shape, 1)

    def slot(e_lane, r_lane):
        e = meta[:, e_lane:e_lane + 1].astype(jnp.int32)
        return jnp.sum(jnp.where(lane == e, offs, 0.0), axis=1, keepdims=True) + meta[:, r_lane:r_lane + 1]

    both = jnp.where(lane == 0, slot(META_E0, META_R0), jnp.where(lane == 1, slot(META_E1, META_R1), 0.0))
    d_ref[0] = both.T[:SUBLANES, :].astype(jnp.int32)


def _slots(meta, offs, tt):
    t = meta.shape[0]
    return pl.pallas_call(
        _slots_kernel,
        out_shape=jax.ShapeDtypeStruct((t // tt, SUBLANES, tt), jnp.int32),
        grid=(t // tt,),
        in_specs=[pl.BlockSpec((tt, LANES), lambda i: (i, 0)), pl.BlockSpec((1, LANES), lambda i: (0, 0))],
        out_specs=pl.BlockSpec((1, SUBLANES, tt), lambda i: (i, 0, 0)),
        compiler_params=_cparams(("arbitrary",)),
        name="moe_slots",
    )(meta, offs)


def _dispatch_kernel(d_ref, ht_ref, xs_zero_hbm, xs_hbm, sem):
    del xs_zero_hbm
    tt = d_ref.shape[-1]

    def issue(j, carry):
        src = _tile(ht_ref, j)
        for k in range(TOP_K):
            pltpu.make_async_copy(src, _tile(xs_hbm, d_ref[0, k, j]), sem).start(priority=k)
        return carry

    lax.fori_loop(0, tt, issue, 0, unroll=DMA_ISSUE_UNROLL)
    for k in range(TOP_K):
        pltpu.make_async_copy(ht_ref, xs_hbm.at[pl.ds(0, tt * TOKEN_TILE_ROWS)], sem).wait()


def _dispatch(h_tiles, slots, n_rows):
    tt = slots.shape[-1]
    t = h_tiles.shape[0] // TOKEN_TILE_ROWS
    return pl.pallas_call(
        _dispatch_kernel,
        out_shape=jax.ShapeDtypeStruct((n_rows * TOKEN_TILE_ROWS, LANES), F32),
        grid=(t // tt,),
        in_specs=[pl.BlockSpec((1, SUBLANES, tt), lambda i: (i, 0, 0), memory_space=pltpu.SMEM),
                  pl.BlockSpec((tt * TOKEN_TILE_ROWS, LANES), lambda i: (i, 0)),
                  pl.BlockSpec(memory_space=pl.ANY)],
        out_specs=pl.BlockSpec(memory_space=pl.ANY),
        scratch_shapes=[pltpu.SemaphoreType.DMA(())],
        input_output_aliases={2: 0},
        compiler_params=_cparams(("arbitrary",)),
        name="moe_dispatch",
    )(slots, h_tiles, jnp.zeros((n_rows * TOKEN_TILE_ROWS, LANES), F32))


def _expert_kernel(te_ref, nu_ref, xs_ref, wgu_ref, wd_ref, ys_ref):
    del te_ref
    i = pl.program_id(0)
    tm = EXPERT_ROW_TILE

    @pl.when(i < nu_ref[0])
    def _():
        x = _token_tiles_to_rows(xs_ref, tm).astype(BF16)
        gu = jnp.dot(x, wgu_ref[0], preferred_element_type=F32)
        gate = gu[:, :D_EXPERT]
        hid = gate * jax.nn.sigmoid(gate) * gu[:, D_EXPERT:]
        _rows_to_token_tiles(ys_ref, jnp.dot(hid.astype(BF16), wd_ref[0], preferred_element_type=F32))

    @pl.when(i >= nu_ref[0])
    def _():
        ys_ref[...] = jnp.zeros_like(ys_ref)


def _experts(xs, tile_expert, n_used, wgu, wd):
    d = D_MODEL
    tm = EXPERT_ROW_TILE
    blk = pl.BlockSpec((tm * TOKEN_TILE_ROWS, LANES), lambda i, te, nu: (i, 0))
    grid_spec = pltpu.PrefetchScalarGridSpec(
        num_scalar_prefetch=2,
        grid=(xs.shape[0] // (tm * TOKEN_TILE_ROWS),),
        in_specs=[blk,
                  pl.BlockSpec((1, d, 2 * D_EXPERT), lambda i, te, nu: (te[i], 0, 0)),
                  pl.BlockSpec((1, D_EXPERT, d), lambda i, te, nu: (te[i], 0, 0))],
        out_specs=blk,
    )
    return pl.pallas_call(
        _expert_kernel,
        out_shape=jax.ShapeDtypeStruct(xs.shape, F32),
        grid_spec=grid_spec,
        compiler_params=_cparams(("arbitrary",)),
        name="moe_experts",
    )(tile_expert, n_used, xs, wgu, wd)


def _combine_kernel(dc_ref, dn_ref, ys_hbm, meta_ref, h_ref, g_ref, b_ref, o_ref, obf_ref, buf, sem):
    i = pl.program_id(0)
    n = pl.num_programs(0)
    tt = dc_ref.shape[-1]
    slot = lax.rem(i, 2)

    def gather(d_ref, s):
        def issue(j, carry):
            for k in range(TOP_K):
                pltpu.make_async_copy(_tile(ys_hbm, d_ref[0, k, j]), _tile(buf.at[s, k], j),
                                      sem.at[s]).start(priority=k)
            return carry
        lax.fori_loop(0, tt, issue, 0, unroll=DMA_ISSUE_UNROLL)

    @pl.when(i == 0)
    def _():
        gather(dc_ref, 0)

    @pl.when(i + 1 < n)
    def _():
        gather(dn_ref, 1 - slot)

    for k in range(TOP_K):
        pltpu.make_async_copy(ys_hbm.at[pl.ds(0, tt * TOKEN_TILE_ROWS)], buf.at[slot, k], sem.at[slot]).wait()
    meta = meta_ref[...]
    ffn = (meta[:, META_W0:META_W0 + 1] * _token_tiles_to_rows(buf.at[slot, 0], tt)
           + meta[:, META_W1:META_W1 + 1] * _token_tiles_to_rows(buf.at[slot, 1], tt))
    y = _ln_rows(DEEPNORM_ALPHA * h_ref[...] + ffn, g_ref[...], b_ref[...])
    o_ref[...] = y
    obf_ref[...] = y.astype(BF16)


def _combine(ys, slots, meta, h, g, b):
    t, d = h.shape
    tt = slots.shape[-1]
    n = t // tt
    cur = pl.BlockSpec((1, SUBLANES, tt), lambda i: (i, 0, 0), memory_space=pltpu.SMEM)
    nxt = pl.BlockSpec((1, SUBLANES, tt), lambda i: (jnp.minimum(i + 1, n - 1), 0, 0), memory_space=pltpu.SMEM)
    row = lambda i: (i, 0)
    const = lambda i: (0, 0)
    return pl.pallas_call(
        _combine_kernel,
        out_shape=(jax.ShapeDtypeStruct((t, d), F32), jax.ShapeDtypeStruct((t, d), BF16)),
        grid=(n,),
        in_specs=[cur, nxt, pl.BlockSpec(memory_space=pl.ANY),
                  pl.BlockSpec((tt, LANES), row), pl.BlockSpec((tt, d), row),
                  pl.BlockSpec((1, d), const), pl.BlockSpec((1, d), const)],
        out_specs=(pl.BlockSpec((tt, d), row), pl.BlockSpec((tt, d), row)),
        scratch_shapes=[pltpu.VMEM((2, TOP_K, tt * TOKEN_TILE_ROWS, LANES), F32), pltpu.SemaphoreType.DMA((2,))],
        compiler_params=_cparams(("arbitrary",)),
        name="moe_combine_ln",
    )(slots, slots, ys, meta, h, g, b)


def _segment_tables(counts, t):
    tm = EXPERT_ROW_TILE
    n_rows = TOP_K * t + N_EXPERTS * tm
    cnt = counts[0, :N_EXPERTS].astype(jnp.int32)
    tiles = (cnt + tm - 1) // tm
    ends = jnp.cumsum(tiles)
    offs = jnp.zeros((1, LANES), F32).at[0, :N_EXPERTS].set(((ends - tiles) * tm).astype(F32))
    tile_ids = jnp.arange(n_rows // tm, dtype=jnp.int32)
    tile_expert = jnp.minimum(jnp.sum((tile_ids[:, None] >= ends[None, :]).astype(jnp.int32), axis=1),
                              N_EXPERTS - 1)
    return n_rows, offs, tile_expert, ends[-1:].astype(jnp.int32)


def _sparse_moe(h, h_tiles, meta, counts, wgu, wd, g, b):
    t = h.shape[0]
    n_rows, offs, tile_expert, n_used = _segment_tables(counts, t)
    slots = _slots(meta, offs, min(MOE_TOKENS, t))
    xs = _dispatch(h_tiles, slots, n_rows)
    ys = _experts(xs, tile_expert, n_used, wgu, wd)
    return _combine(ys, slots, meta, h, g, b)


def _rotate_half_cols(w):
    half = w.shape[-1] // 2
    return jnp.concatenate([-w[..., half:], w[..., :half]], axis=-1)


def _pack_input_proj(w_in, b_in):
    d = w_in.shape[0]
    offs = np.cumsum((0, GLA_QK_W, GLA_QK_W, GLA_V_W, GLA_V_W, GLA_GATE_RANK, GLA_GATE_RANK,
                      MLA_Q_RANK, MLA_KV_RANK, MLA_ROPE, D_MODEL, D_MODEL))

    def seg(i):
        return w_in[:, offs[i]:offs[i + 1]], b_in[offs[i]:offs[i + 1]]

    w = jnp.zeros((d, N_PROJ), F32)
    b = jnp.zeros((N_PROJ,), F32)

    def put(w, b, col, ws, bs):
        return w.at[:, col:col + ws.shape[1]].set(ws), b.at[col:col + ws.shape[1]].set(bs)

    for i, col in ((0, COL_GQ), (1, COL_GK), (2, COL_GV), (3, COL_GR), (9, COL_GA), (10, COL_GB),
                   (6, COL_CQ), (7, COL_CKV), (4, COL_SMALL), (5, COL_SMALL + GLA_GATE_RANK)):
        w, b = put(w, b, col, *seg(i))
    wkr, bkr = seg(8)
    w, b = put(w, b, COL_SMALL + KR_LANE, wkr, bkr)
    w, b = put(w, b, COL_SMALL2 + KR_LANE, _rotate_half_cols(wkr), _rotate_half_cols(bkr))
    return w.astype(BF16), b.reshape(1, N_PROJ)


def _pack_decay(wa2, ba, lane0):
    w = jnp.zeros((LANES, GLA_QK_W), F32).at[lane0:lane0 + GLA_GATE_RANK].set(wa2)
    return w.astype(BF16), ba.reshape(1, GLA_QK_W)


def _pack_mla(w_uq, w_ukv, q_norm_g, kv_norm_g):
    wq = w_uq.reshape(MLA_Q_RANK, MLA_HEADS, MLA_QK)
    rope = wq[..., MLA_NOPE:]
    zq = jnp.zeros((MLA_Q_RANK, MLA_HEADS, HEAD_PAD - MLA_QK), F32)
    wq_p = jnp.concatenate([wq, zq], axis=-1)
    wqr_p = jnp.concatenate([jnp.zeros_like(wq[..., :MLA_NOPE]), _rotate_half_cols(rope), zq], axis=-1)

    def pad_rows(w):
        w = w.reshape(MLA_Q_RANK, MLA_HEADS * HEAD_PAD)
        return jnp.pad(w, ((0, CQ_PAD - MLA_Q_RANK), (0, 0))).astype(BF16)

    wkv = w_ukv.reshape(MLA_KV_RANK, MLA_HEADS, MLA_NOPE + MLA_V)
    wk_p = jnp.concatenate([wkv[..., :MLA_NOPE],
                            jnp.zeros((MLA_KV_RANK, MLA_HEADS, HEAD_PAD - MLA_NOPE), F32)], axis=-1)
    wk_p = wk_p.reshape(MLA_KV_RANK, MLA_HEADS * HEAD_PAD).astype(BF16)
    wv_p = wkv[..., MLA_NOPE:].reshape(MLA_KV_RANK, MLA_HEADS * MLA_V).T.astype(BF16)
    gq = jnp.pad(q_norm_g, (0, CQ_PAD - MLA_Q_RANK)).reshape(1, CQ_PAD)
    return pad_rows(wq_p), pad_rows(wqr_p), wk_p, wv_p, gq, kv_norm_g.reshape(1, MLA_KV_RANK)


def _rope_lane_table():
    inv = ROPE_BASE ** (-jnp.arange(0, MLA_ROPE, 2, dtype=F32) / MLA_ROPE)
    half = MLA_ROPE // 2
    tab = jnp.zeros((LANES,), F32)
    tab = tab.at[KR_LANE:KR_LANE + half].set(inv).at[KR_LANE + half:KR_LANE + MLA_ROPE].set(inv)
    return tab.reshape(1, LANES)


def _pack_router(w_grp, b_grp, w_exp, b_exp):
    d = w_grp.shape[0]
    w = jnp.zeros((d, LANES), F32).at[:, :N_EXPERTS].set(w_exp).at[:, GRP_LANE:GRP_LANE + N_GROUPS].set(w_grp)
    b = jnp.zeros((LANES,), F32).at[:N_EXPERTS].set(b_exp).at[GRP_LANE:GRP_LANE + N_GROUPS].set(b_grp)
    return w, b.reshape(1, LANES)


def kernel(x, positions, ln_emb_g, ln_emb_b, w_in, b_in, gla_wa2_f, gla_ba_f, gla_wa2_b, gla_ba_b, gla_norm_g, mla_q_norm_g, mla_w_uq, mla_kv_norm_g, mla_w_ukv, w_out, ln1_g, ln1_b, w_grp, b_grp, w_exp, b_exp, w_gate, w_up, w_down, ln2_g, ln2_b):
    batch, seq, d = x.shape
    t = batch * seq
    pos = positions.reshape(t, 1).astype(jnp.int32)
    inv_lane = _rope_lane_table()
    h, hb = _layer_norm(x.reshape(t, d), ln_emb_g, ln_emb_b)
    for l in range(DEPTH):
        w_p, b_p = _pack_input_proj(w_in[l], b_in[l])
        proj = _input_proj(hb, w_p, b_p)
        wa_f, ba_f = _pack_decay(gla_wa2_f[l], gla_ba_f[l], 0)
        wa_b, ba_b = _pack_decay(gla_wa2_b[l], gla_ba_b[l], GLA_GATE_RANK)
        o_f = _gla(proj, wa_f, ba_f, batch, seq, reverse=False)
        o_gla = _gla(proj, wa_b, ba_b, batch, seq, reverse=True, o_fwd=o_f,
                     norm_g=gla_norm_g[l].reshape(1, GLA_V_W))
        wq, wqr, wk, wv, gq, gkv = _pack_mla(mla_w_uq[l], mla_w_ukv[l], mla_q_norm_g[l], mla_kv_norm_g[l])
        q2, k2, v = _mla_prep(proj, pos, inv_lane, gq, gkv, wq, wqr, wk, wv)
        o_mla = _attention(q2, k2, v, batch, seq)
        h, h_tiles = _merge_out(o_gla, o_mla, proj, h, w_out[l].astype(BF16),
                           ln1_g[l].reshape(1, d), ln1_b[l].reshape(1, d))
        w_r, b_r = _pack_router(w_grp[l], b_grp[l], w_exp[l], b_exp[l])
        meta, counts = _router(h, w_r, b_r)
        wgu = jnp.concatenate([w_gate[l], w_up[l]], axis=-1).astype(BF16)
        h, hb = _sparse_moe(h, h_tiles, meta, counts, wgu, w_down[l].astype(BF16),
                            ln2_g[l].reshape(1, d), ln2_b[l].reshape(1, d))
    return h.reshape(batch, seq, d)
```

```python
import functools

import numpy as np
import jax
import jax.numpy as jnp
from jax import lax
from jax.experimental import pallas as pl
from jax.experimental.pallas import tpu as pltpu

F32 = jnp.float32
BF16 = jnp.bfloat16

D_MODEL = 1024
DEPTH = 2
GLA_HEADS = 4
GLA_DK = 128
GLA_DV = 256
GLA_GATE_RANK = 16
GLA_TAU = 16.0
MLA_HEADS = 16
MLA_NOPE = 64
MLA_ROPE = 32
MLA_V = 64
MLA_QK = MLA_NOPE + MLA_ROPE
MLA_Q_RANK = 384
MLA_KV_RANK = 128
ROPE_BASE = 10000.0
N_GROUPS = 8
EXPERTS_PER_GROUP = 4
N_EXPERTS = 32
D_EXPERT = 256
GLA_QK_W = GLA_HEADS * GLA_DK
GLA_V_W = GLA_HEADS * GLA_DV
DEEPNORM_ALPHA = (2.0 * DEPTH) ** 0.25
LN_EPS = 1e-5
RMS_EPS = 1e-6

LANES = 128
VMEM_LIMIT_BYTES = 56 * 1024 * 1024

COL_GQ = 0
COL_GK = 512
COL_GV = 1024
COL_GR = 2048
COL_GA = 3072
COL_GB = 4096
COL_CQ = 5120
CQ_PAD = 512
COL_SMALL = 5632
COL_SMALL2 = 5760
COL_CKV = 5888
N_PROJ = 6144
HEAD_PAD = 128
KR_LANE = 64

GLA_CHUNK = 128
NEG_BIG = -1e30


def _cparams(sem):
    return pltpu.CompilerParams(dimension_semantics=sem, vmem_limit_bytes=VMEM_LIMIT_BYTES)


def _ln_rows(x, g, b):
    mu = jnp.mean(x, axis=-1, keepdims=True)
    xc = x - mu
    var = jnp.mean(xc * xc, axis=-1, keepdims=True)
    return xc * lax.rsqrt(var + LN_EPS) * g + b


def _ln_kernel(x_ref, g_ref, b_ref, o_ref, ob_ref):
    y = _ln_rows(x_ref[...], g_ref[...], b_ref[...])
    o_ref[...] = y
    ob_ref[...] = y.astype(BF16)


def _layer_norm(x, g, b, tm=512):
    t, d = x.shape
    return pl.pallas_call(
        _ln_kernel,
        out_shape=(jax.ShapeDtypeStruct((t, d), F32), jax.ShapeDtypeStruct((t, d), BF16)),
        grid=(t // tm,),
        in_specs=[pl.BlockSpec((tm, d), lambda i: (i, 0)),
                  pl.BlockSpec((1, d), lambda i: (0, 0)),
                  pl.BlockSpec((1, d), lambda i: (0, 0))],
        out_specs=(pl.BlockSpec((tm, d), lambda i: (i, 0)), pl.BlockSpec((tm, d), lambda i: (i, 0))),
        compiler_params=_cparams(("parallel",)),
        name="ln_embed",
    )(x, g.reshape(1, d), b.reshape(1, d))


PROJ_COL_CHUNK = 512


def _proj_kernel(x_ref, w_ref, b_ref, o_ref):
    x = x_ref[...]
    for c in range(N_PROJ // PROJ_COL_CHUNK):
        sl = slice(c * PROJ_COL_CHUNK, (c + 1) * PROJ_COL_CHUNK)
        acc = jnp.dot(x, w_ref[:, sl], preferred_element_type=F32) + b_ref[:, sl]
        o_ref[:, sl] = acc.astype(o_ref.dtype)


def _input_proj(hb, w, b, tm=512):
    t, d = hb.shape
    return pl.pallas_call(
        _proj_kernel,
        out_shape=jax.ShapeDtypeStruct((t, N_PROJ), BF16),
        grid=(t // tm,),
        in_specs=[pl.BlockSpec((tm, d), lambda i: (i, 0)),
                  pl.BlockSpec((d, N_PROJ), lambda i: (0, 0)),
                  pl.BlockSpec((1, N_PROJ), lambda i: (0, 0))],
        out_specs=pl.BlockSpec((tm, N_PROJ), lambda i: (i, 0)),
        compiler_params=_cparams(("parallel",)),
        name="input_proj",
    )(hb, w, b)


def _log_sigmoid(x):
    return jnp.minimum(x, 0.0) - jnp.log(1.0 + jnp.exp(-jnp.abs(x)))


def _split_bf16(x):
    hi = x.astype(BF16)
    lo = (x - hi.astype(F32)).astype(BF16)
    return hi, lo


def _gla_kernel(*refs, reverse, n_chunks):
    if reverse:
        (q_ref, k_ref, v_ref, z_ref, wa_ref, ba_ref, of_ref, gr_ref, ng_ref, o_ref, state_ref) = refs
    else:
        (q_ref, k_ref, v_ref, z_ref, wa_ref, ba_ref, o_ref, state_ref) = refs
    c_len = GLA_CHUNK

    @pl.when(pl.program_id(1) == 0)
    def _():
        state_ref[...] = jnp.zeros_like(state_ref)

    row = lax.broadcasted_iota(jnp.int32, (c_len, c_len), 0)
    col = lax.broadcasted_iota(jnp.int32, (c_len, c_len), 1)
    if reverse:
        tri = (col >= row).astype(BF16)
        keep = col > row
        last = 0
    else:
        tri = (col <= row).astype(BF16)
        keep = col <= row
        last = c_len - 1

    def chunk(cc, carry):
        c = (n_chunks - 1 - cc) if reverse else cc
        r0 = pl.multiple_of(c * c_len, c_len)
        rows = pl.ds(r0, c_len)
        z = z_ref[rows, :]
        la = _log_sigmoid(jnp.dot(z, wa_ref[...], preferred_element_type=F32) + ba_ref[...]) * (1.0 / GLA_TAU)
        la_hi, la_lo = _split_bf16(la)
        b_all = (jnp.dot(tri, la_hi, preferred_element_type=F32)
                 + jnp.dot(tri, la_lo, preferred_element_type=F32))
        b_last = b_all[last:last + 1, :]
        q = q_ref[rows, :].astype(F32)
        k = k_ref[rows, :].astype(F32)
        qd_all = (q * (jnp.exp(b_all) * (GLA_DK ** -0.5))).astype(BF16)
        kinv_all = (k * jnp.exp(-b_all)).astype(BF16)
        kend_all = k * jnp.exp(b_last - b_all)
        dec_all = jnp.broadcast_to(jnp.exp(b_last), (c_len, GLA_QK_W))
        for h in range(GLA_HEADS):
            ks = slice(h * GLA_DK, (h + 1) * GLA_DK)
            vs = slice(h * GLA_DV, (h + 1) * GLA_DV)
            v = v_ref[rows, vs]
            qd = qd_all[:, ks]
            scores = lax.dot_general(qd, kinv_all[:, ks], (((1,), (1,)), ((), ())),
                                     preferred_element_type=F32)
            scores = jnp.where(keep, scores, 0.0).astype(BF16)
            state = state_ref[h]
            o = (jnp.dot(scores, v, preferred_element_type=F32)
                 + jnp.dot(qd, state.astype(BF16), preferred_element_type=F32))
            kend_t = kend_all[:, ks].T.astype(BF16)
            dec_t = dec_all[:, ks].T
            dec = jnp.concatenate([dec_t, dec_t], axis=1)
            state_ref[h] = dec * state + jnp.dot(kend_t, v, preferred_element_type=F32)
            if reverse:
                o = o + of_ref[rows, vs]
                ms = jnp.mean(o * o, axis=-1, keepdims=True)
                o = o * lax.rsqrt(ms + RMS_EPS) * ng_ref[:, vs]
                g = gr_ref[rows, vs].astype(F32)
                o = o * (g * jax.nn.sigmoid(g))
            o_ref[rows, vs] = o.astype(o_ref.dtype)
        return carry

    lax.fori_loop(0, n_chunks, chunk, 0)


def _gla(proj, wa, ba, batch, seq, *, reverse, o_fwd=None, norm_g=None, ts=1024):
    ts = min(ts, seq)
    nblk = seq // ts
    t = batch * seq

    def rblk(b, i):
        return b * nblk + ((nblk - 1 - i) if reverse else i)

    in_specs = [
        pl.BlockSpec((ts, GLA_QK_W), lambda b, i: (rblk(b, i), COL_GQ // GLA_QK_W)),
        pl.BlockSpec((ts, GLA_QK_W), lambda b, i: (rblk(b, i), COL_GK // GLA_QK_W)),
        pl.BlockSpec((ts, GLA_V_W), lambda b, i: (rblk(b, i), COL_GV // GLA_V_W)),
        pl.BlockSpec((ts, LANES), lambda b, i: (rblk(b, i), COL_SMALL // LANES)),
        pl.BlockSpec((LANES, GLA_QK_W), lambda b, i: (0, 0)),
        pl.BlockSpec((1, GLA_QK_W), lambda b, i: (0, 0)),
    ]
    args = [proj, proj, proj, proj, wa, ba]
    if reverse:
        in_specs += [
            pl.BlockSpec((ts, GLA_V_W), lambda b, i: (rblk(b, i), 0)),
            pl.BlockSpec((ts, GLA_V_W), lambda b, i: (rblk(b, i), COL_GR // GLA_V_W)),
            pl.BlockSpec((1, GLA_V_W), lambda b, i: (0, 0)),
        ]
        args += [o_fwd, proj, norm_g]
    return pl.pallas_call(
        functools.partial(_gla_kernel, reverse=reverse, n_chunks=ts // GLA_CHUNK),
        out_shape=jax.ShapeDtypeStruct((t, GLA_V_W), BF16 if reverse else F32),
        grid=(batch, nblk),
        in_specs=in_specs,
        out_specs=pl.BlockSpec((ts, GLA_V_W), lambda b, i: (rblk(b, i), 0)),
        scratch_shapes=[pltpu.VMEM((GLA_HEADS, GLA_DK, GLA_DV), F32)],
        compiler_params=_cparams(("parallel", "arbitrary")),
        name="gla_bwd" if reverse else "gla_fwd",
    )(*args)


MLA_COL_CHUNK = 512


def _mla_prep_kernel(cq_ref, ckv_ref, sm_ref, sm2_ref, pos_ref, inv_ref, gq_ref, gkv_ref,
                     wq_ref, wqr_ref, wk_ref, wv_ref, q_ref, k_ref, v_ref):
    cq = cq_ref[...].astype(F32)
    msq = jnp.sum(cq * cq, axis=-1, keepdims=True) * (1.0 / MLA_Q_RANK)
    cqn = (cq * lax.rsqrt(msq + RMS_EPS) * gq_ref[...]).astype(BF16)
    ckv = ckv_ref[...].astype(F32)
    mskv = jnp.mean(ckv * ckv, axis=-1, keepdims=True)
    ckvn = (ckv * lax.rsqrt(mskv + RMS_EPS) * gkv_ref[...]).astype(BF16)

    ang = pos_ref[...].astype(F32) * inv_ref[...]
    cos = jnp.cos(ang)
    sin = jnp.sin(ang)
    lane = lax.broadcasted_iota(jnp.int32, ang.shape, 1)
    kr = jnp.where(lane >= KR_LANE, sm_ref[...].astype(F32), 0.0)
    kr = kr * cos + sm2_ref[...].astype(F32) * sin

    heads_per_chunk = MLA_COL_CHUNK // HEAD_PAD
    cos_t = jnp.concatenate([cos] * heads_per_chunk, axis=1)
    sin_t = jnp.concatenate([sin] * heads_per_chunk, axis=1)
    kr_t = jnp.concatenate([kr] * heads_per_chunk, axis=1)
    qscale = (MLA_QK ** -0.5) * float(np.log2(np.e))
    for c in range(MLA_HEADS * HEAD_PAD // MLA_COL_CHUNK):
        sl = slice(c * MLA_COL_CHUNK, (c + 1) * MLA_COL_CHUNK)
        q = jnp.dot(cqn, wq_ref[:, sl], preferred_element_type=F32)
        qr = jnp.dot(cqn, wqr_ref[:, sl], preferred_element_type=F32)
        q_ref[:, sl] = ((q * cos_t + qr * sin_t) * qscale).astype(BF16)
        k = jnp.dot(ckvn, wk_ref[:, sl], preferred_element_type=F32)
        k_ref[:, sl] = (k + kr_t).astype(BF16)
    v_ref[...] = lax.dot_general(wv_ref[...], ckvn, (((1,), (1,)), ((), ())),
                                 preferred_element_type=F32).astype(BF16)


def _mla_prep(proj, pos, inv_lane, gq, gkv, wq, wqr, wk, wv, ts=512):
    t = proj.shape[0]
    hp = MLA_HEADS * HEAD_PAD
    vw = MLA_HEADS * MLA_V
    const = lambda i: (0, 0)
    return pl.pallas_call(
        _mla_prep_kernel,
        out_shape=(jax.ShapeDtypeStruct((t, hp), BF16), jax.ShapeDtypeStruct((t, hp), BF16),
                   jax.ShapeDtypeStruct((vw, t), BF16)),
        grid=(t // ts,),
        in_specs=[
            pl.BlockSpec((ts, CQ_PAD), lambda i: (i, COL_CQ // CQ_PAD)),
            pl.BlockSpec((ts, LANES), lambda i: (i, COL_CKV // LANES)),
            pl.BlockSpec((ts, LANES), lambda i: (i, COL_SMALL // LANES)),
            pl.BlockSpec((ts, LANES), lambda i: (i, COL_SMALL2 // LANES)),
            pl.BlockSpec((ts, 1), lambda i: (i, 0)),
            pl.BlockSpec((1, LANES), const),
            pl.BlockSpec((1, CQ_PAD), const),
            pl.BlockSpec((1, MLA_KV_RANK), const),
            pl.BlockSpec((CQ_PAD, hp), const),
            pl.BlockSpec((CQ_PAD, hp), const),
            pl.BlockSpec((MLA_KV_RANK, hp), const),
            pl.BlockSpec((vw, MLA_KV_RANK), const),
        ],
        out_specs=(pl.BlockSpec((ts, hp), lambda i: (i, 0)), pl.BlockSpec((ts, hp), lambda i: (i, 0)),
                   pl.BlockSpec((vw, ts), lambda i: (0, i))),
        compiler_params=_cparams(("parallel",)),
        name="mla_prep",
    )(proj, proj, proj, proj, pos, inv_lane, gq, gkv, wq, wqr, wk, wv)


SUBLANES = 8


def _attn_kernel(q_ref, k_ref, vt_ref, o_ref, s_scr0, s_scr1, *, tk):
    seq = k_ref.shape[0]
    tq = s_scr0.shape[-1]
    nk = seq // tk
    nq = seq // tq
    s_scrs = (s_scr0, s_scr1)

    def scores_pass_chunk(hh, i, c, m8):
        hs = slice(hh * HEAD_PAD, (hh + 1) * HEAD_PAD)
        q = q_ref[pl.ds(pl.multiple_of(i * tq, tq), tq), hs]
        st = lax.dot_general(k_ref[c * tk:(c + 1) * tk, hs], q, (((1,), (1,)), ((), ())),
                             preferred_element_type=F32)
        s_scrs[hh][c] = st
        return jnp.maximum(m8, jnp.max(st.reshape(tk // SUBLANES, SUBLANES, tq), axis=0))

    def probs_pass_chunk(hh, c, m, l8, acc):
        p = jnp.exp2(s_scrs[hh][c] - m)
        l8 = l8 + jnp.sum(p.reshape(tk // SUBLANES, SUBLANES, tq), axis=0)
        vt = vt_ref[hh * MLA_V:(hh + 1) * MLA_V, c * tk:(c + 1) * tk]
        acc = acc + jnp.dot(vt, p.astype(BF16), preferred_element_type=F32)
        return l8, acc

    def overlapped(i_scores, hh_scores, hh_probs, m):
        m8 = jnp.full((SUBLANES, tq), NEG_BIG, F32)
        l8 = jnp.zeros((SUBLANES, tq), F32)
        acc = jnp.zeros((MLA_V, tq), F32)
        for c in range(nk):
            l8, acc = probs_pass_chunk(hh_probs, c, m, l8, acc)
            m8 = scores_pass_chunk(hh_scores, i_scores, c, m8)
        return acc / jnp.sum(l8, axis=0, keepdims=True), jnp.max(m8, axis=0, keepdims=True)

    m8 = jnp.full((SUBLANES, tq), NEG_BIG, F32)
    for c in range(nk):
        m8 = scores_pass_chunk(0, 0, c, m8)

    def q_tile(i, m_a):
        out_a, m_b = overlapped(i, 1, 0, m_a)
        out_b, m_a_next = overlapped(jnp.minimum(i + 1, nq - 1), 0, 1, m_b)
        o_ref[pl.ds(pl.multiple_of(i * tq, tq), tq), :] = jnp.concatenate([out_a, out_b], axis=0).T.astype(BF16)
        return m_a_next

    lax.fori_loop(0, nq, q_tile, jnp.max(m8, axis=0, keepdims=True))


def _attention(q2, k2, vt, batch, seq, tq=512, tk=512):
    tq = min(tq, seq)
    tk = min(tk, seq)
    t = batch * seq
    pairs = MLA_HEADS // 2
    return pl.pallas_call(
        functools.partial(_attn_kernel, tk=tk),
        scratch_shapes=[pltpu.VMEM((seq // tk, tk, tq), F32), pltpu.VMEM((seq // tk, tk, tq), F32)],
        out_shape=jax.ShapeDtypeStruct((t, MLA_HEADS * MLA_V), BF16),
        grid=(batch, pairs),
        in_specs=[
            pl.BlockSpec((seq, 2 * HEAD_PAD), lambda b, p: (b, p)),
            pl.BlockSpec((seq, 2 * HEAD_PAD), lambda b, p: (b, p)),
            pl.BlockSpec((2 * MLA_V, seq), lambda b, p: (p, b)),
        ],
        out_specs=pl.BlockSpec((seq, 2 * MLA_V), lambda b, p: (b, p)),
        compiler_params=_cparams(("parallel", "parallel")),
        name="mla_attention",
    )(q2, k2, vt)


TOKEN_TILE_ROWS = D_MODEL // LANES


def _rows_to_token_tiles(ref, y):
    n = y.shape[0]
    for s in range(TOKEN_TILE_ROWS):
        ref[pl.ds(s, n, stride=TOKEN_TILE_ROWS), :] = y[:, s * LANES:(s + 1) * LANES]


def _token_tiles_to_rows(ref, n):
    return jnp.concatenate([ref[pl.ds(s, n, stride=TOKEN_TILE_ROWS), :] for s in range(TOKEN_TILE_ROWS)], axis=1)


def _merge_out_kernel(oa_ref, ob_ref, ga_ref, gb_ref, h_ref, w_ref, g_ref, b_ref, ot_ref):
    merged = (jax.nn.sigmoid(ga_ref[...].astype(F32)) * oa_ref[...].astype(F32)
              + jax.nn.sigmoid(gb_ref[...].astype(F32)) * ob_ref[...].astype(F32))
    mix = jnp.dot(merged.astype(BF16), w_ref[...], preferred_element_type=F32)
    y = _ln_rows(DEEPNORM_ALPHA * h_ref[...] + mix, g_ref[...], b_ref[...])
    _rows_to_token_tiles(ot_ref, y)


def _merge_out(o_gla, o_mla, proj, h, w_out, g, b, tm=512):
    t, d = h.shape
    row = lambda i: (i, 0)
    const = lambda i: (0, 0)
    return pl.pallas_call(
        _merge_out_kernel,
        out_shape=jax.ShapeDtypeStruct((t * TOKEN_TILE_ROWS, LANES), F32),
        grid=(t // tm,),
        in_specs=[
            pl.BlockSpec((tm, d), row),
            pl.BlockSpec((tm, d), row),
            pl.BlockSpec((tm, d), lambda i: (i, COL_GA // D_MODEL)),
            pl.BlockSpec((tm, d), lambda i: (i, COL_GB // D_MODEL)),
            pl.BlockSpec((tm, d), row),
            pl.BlockSpec((d, d), const),
            pl.BlockSpec((1, d), const),
            pl.BlockSpec((1, d), const),
        ],
        out_specs=pl.BlockSpec((tm * TOKEN_TILE_ROWS, LANES), row),
        compiler_params=_cparams(("parallel",)),
        name="merge_out_ln",
    )(o_gla, o_mla, proj, proj, h, w_out, g, b)


GRP_LANE = N_EXPERTS
META_W0, META_W1, META_E0, META_E1, META_R0, META_R1 = range(6)


def _router_kernel(x_ref, w_ref, b_ref, meta_ref, cnt_ref, carry_ref):
    @pl.when(pl.program_id(0) == 0)
    def _():
        carry_ref[...] = jnp.zeros_like(carry_ref)

    tm = x_ref.shape[0] // TOKEN_TILE_ROWS
    x_hi, x_lo = _split_bf16(_token_tiles_to_rows(x_ref, tm))
    w_hi, w_lo = _split_bf16(w_ref[...])
    logits = (jnp.dot(x_hi, w_hi, preferred_element_type=F32)
              + jnp.dot(x_lo, w_hi, preferred_element_type=F32)
              + jnp.dot(x_hi, w_lo, preferred_element_type=F32)) + b_ref[...]
    lane = lax.broadcasted_iota(jnp.int32, logits.shape, 1)
    is_grp = (lane >= GRP_LANE) & (lane < GRP_LANE + N_GROUPS)
    gl = jnp.where(is_grp, logits, NEG_BIG)
    gmax = jnp.max(gl, axis=1, keepdims=True)
    g_lane = jnp.min(jnp.where(gl == gmax, lane, 4 * LANES), axis=1, keepdims=True)
    g_w = 1.0 / jnp.sum(jnp.where(is_grp, jnp.exp(gl - gmax), 0.0), axis=1, keepdims=True)
    lo_lane = (g_lane - GRP_LANE) * EXPERTS_PER_GROUP
    in_grp = (lane >= lo_lane) & (lane < lo_lane + EXPERTS_PER_GROUP)
    el = jnp.where(in_grp, logits, NEG_BIG)
    v1 = jnp.max(el, axis=1, keepdims=True)
    i1 = jnp.min(jnp.where(el == v1, lane, 4 * LANES), axis=1, keepdims=True)
    el2 = jnp.where(lane == i1, NEG_BIG, el)
    v2 = jnp.max(el2, axis=1, keepdims=True)
    i2 = jnp.min(jnp.where(el2 == v2, lane, 4 * LANES), axis=1, keepdims=True)
    e2 = jnp.exp(v2 - v1)
    w1 = g_w / (1.0 + e2)
    w2 = g_w * e2 / (1.0 + e2)

    onehot = jnp.where(lane == i1, 1.0, jnp.where(lane == i2, 1.0, 0.0))
    r = lax.broadcasted_iota(jnp.int32, (tm, tm), 0)
    c = lax.broadcasted_iota(jnp.int32, (tm, tm), 1)
    earlier = (c < r).astype(BF16)
    base = carry_ref[...] + jnp.dot(earlier, onehot.astype(BF16), preferred_element_type=F32)
    rank1 = jnp.sum(jnp.where(lane == i1, base, 0.0), axis=1, keepdims=True)
    rank2 = jnp.sum(jnp.where(lane == i2, base, 0.0), axis=1, keepdims=True)
    meta = jnp.zeros(logits.shape, F32)
    for ln, val in ((META_W0, w1), (META_W1, w2), (META_E0, i1.astype(F32)), (META_E1, i2.astype(F32)),
                    (META_R0, rank1), (META_R1, rank2)):
        meta = jnp.where(lane == ln, val, meta)
    meta_ref[...] = meta
    carry_ref[...] += jnp.sum(onehot, axis=0, keepdims=True)
    cnt_ref[...] = carry_ref[...]


def _router(h_tiles, w_r, b_r, tm=512):
    t, d = h_tiles.shape[0] // TOKEN_TILE_ROWS, D_MODEL
    return pl.pallas_call(
        _router_kernel,
        out_shape=(jax.ShapeDtypeStruct((t, LANES), F32), jax.ShapeDtypeStruct((1, LANES), F32)),
        grid=(t // tm,),
        in_specs=[pl.BlockSpec((tm * TOKEN_TILE_ROWS, LANES), lambda i: (i, 0)),
                  pl.BlockSpec((d, LANES), lambda i: (0, 0)),
                  pl.BlockSpec((1, LANES), lambda i: (0, 0))],
        out_specs=(pl.BlockSpec((tm, LANES), lambda i: (i, 0)), pl.BlockSpec((1, LANES), lambda i: (0, 0))),
        scratch_shapes=[pltpu.VMEM((1, LANES), F32)],
        compiler_params=_cparams(("arbitrary",)),
        name="router",
    )(h_tiles, w_r, b_r)


EXPERT_ROW_TILE = 512
MOE_TOKENS = 256
TOP_K = 2
DMA_ISSUE_UNROLL = 8


def _tile(ref, token):
    return ref.at[pl.ds(pl.multiple_of(token * TOKEN_TILE_ROWS, TOKEN_TILE_ROWS), TOKEN_TILE_ROWS)]


def _slots_kernel(meta_ref, offs_ref, d_ref):
    meta = meta_ref[...]
    offs = offs_ref[...]
    lane = lax.broadcasted_iota(jnp.int32, meta.shape, 1)

    def slot(e_lane, r_lane):
        e = meta[:, e_lane:e_lane + 1].astype(jnp.int32)
        return jnp.sum(jnp.where(lane == e, offs, 0.0), axis=1, keepdims=True) + meta[:, r_lane:r_lane + 1]

    both = jnp.where(lane == 0, slot(META_E0, META_R0), jnp.where(lane == 1, slot(META_E1, META_R1), 0.0))
    d_ref[0] = both.T[:SUBLANES, :].astype(jnp.int32)


def _slots(meta, offs, tt):
    t = meta.shape[0]
    return pl.pallas_call(
        _slots_kernel,
        out_shape=jax.ShapeDtypeStruct((t // tt, SUBLANES, tt), jnp.int32),
        grid=(t // tt,),
        in_specs=[pl.BlockSpec((tt, LANES), lambda i: (i, 0)), pl.BlockSpec((1, LANES), lambda i: (0, 0))],
        out_specs=pl.BlockSpec((1, SUBLANES, tt), lambda i: (i, 0, 0)),
        compiler_params=_cparams(("arbitrary",)),
        name="moe_slots",
    )(meta, offs)


def _zero_tile_kernel(lt_ref, xs_ref):
    del lt_ref
    xs_ref[...] = jnp.zeros_like(xs_ref)


def _zero_last_tiles(last_tile, n_rows):
    blk = EXPERT_ROW_TILE * TOKEN_TILE_ROWS
    grid_spec = pltpu.PrefetchScalarGridSpec(
        num_scalar_prefetch=1, grid=(N_EXPERTS,), in_specs=[],
        out_specs=pl.BlockSpec((blk, LANES), lambda e, lt: (lt[e], 0)))
    return pl.pallas_call(
        _zero_tile_kernel,
        out_shape=jax.ShapeDtypeStruct((n_rows * TOKEN_TILE_ROWS, LANES), F32),
        grid_spec=grid_spec,
        compiler_params=_cparams(("arbitrary",)),
        name="moe_zero_tiles",
    )(last_tile)


def _dispatch_kernel(d_ref, ht_ref, xs_init_hbm, xs_hbm, sem):
    del xs_init_hbm
    tt = d_ref.shape[-1]

    def issue(j, carry):
        src = _tile(ht_ref, j)
        for k in range(TOP_K):
            pltpu.make_async_copy(src, _tile(xs_hbm, d_ref[0, k, j]), sem).start(priority=k)
        return carry

    lax.fori_loop(0, tt, issue, 0, unroll=DMA_ISSUE_UNROLL)
    for k in range(TOP_K):
        pltpu.make_async_copy(ht_ref, xs_hbm.at[pl.ds(0, tt * TOKEN_TILE_ROWS)], sem).wait()


def _dispatch(h_tiles, slots, xs_init):
    tt = slots.shape[-1]
    t = h_tiles.shape[0] // TOKEN_TILE_ROWS
    return pl.pallas_call(
        _dispatch_kernel,
        out_shape=jax.ShapeDtypeStruct(xs_init.shape, F32),
        grid=(t // tt,),
        in_specs=[pl.BlockSpec((1, SUBLANES, tt), lambda i: (i, 0, 0), memory_space=pltpu.SMEM),
                  pl.BlockSpec((tt * TOKEN_TILE_ROWS, LANES), lambda i: (i, 0)),
                  pl.BlockSpec(memory_space=pl.ANY)],
        out_specs=pl.BlockSpec(memory_space=pl.ANY),
        scratch_shapes=[pltpu.SemaphoreType.DMA(())],
        input_output_aliases={2: 0},
        compiler_params=_cparams(("arbitrary",)),
        name="moe_dispatch",
    )(slots, h_tiles, xs_init)


def _expert_kernel(te_ref, nu_ref, xs_ref, wgu_ref, wd_ref, ys_ref):
    del te_ref
    i = pl.program_id(0)
    tm = EXPERT_ROW_TILE

    @pl.when(i < nu_ref[0])
    def _():
        x = _token_tiles_to_rows(xs_ref, tm).astype(BF16)
        gu = jnp.dot(x, wgu_ref[0], preferred_element_type=F32)
        gate = gu[:, :D_EXPERT]
        hid = gate * jax.nn.sigmoid(gate) * gu[:, D_EXPERT:]
        _rows_to_token_tiles(ys_ref, jnp.dot(hid.astype(BF16), wd_ref[0], preferred_element_type=F32))

    @pl.when(i >= nu_ref[0])
    def _():
        ys_ref[...] = jnp.zeros_like(ys_ref)


def _experts(xs, tile_expert, n_used, wgu, wd):
    d = D_MODEL
    tm = EXPERT_ROW_TILE
    blk = pl.BlockSpec((tm * TOKEN_TILE_ROWS, LANES), lambda i, te, nu: (i, 0))
    blk_in = pl.BlockSpec((tm * TOKEN_TILE_ROWS, LANES), lambda i, te, nu: (jnp.minimum(i, nu[0] - 1), 0))
    grid_spec = pltpu.PrefetchScalarGridSpec(
        num_scalar_prefetch=2,
        grid=(xs.shape[0] // (tm * TOKEN_TILE_ROWS),),
        in_specs=[blk_in,
                  pl.BlockSpec((1, d, 2 * D_EXPERT), lambda i, te, nu: (te[i], 0, 0)),
                  pl.BlockSpec((1, D_EXPERT, d), lambda i, te, nu: (te[i], 0, 0))],
        out_specs=blk,
    )
    return pl.pallas_call(
        _expert_kernel,
        out_shape=jax.ShapeDtypeStruct(xs.shape, F32),
        grid_spec=grid_spec,
        compiler_params=_cparams(("arbitrary",)),
        name="moe_experts",
    )(tile_expert, n_used, xs, wgu, wd)


def _combine_kernel(dc_ref, dn_ref, ys_hbm, meta_ref, ht_ref, g_ref, b_ref, o_ref, obf_ref, buf, sem):
    i = pl.program_id(0)
    n = pl.num_programs(0)
    tt = dc_ref.shape[-1]
    slot = lax.rem(i, 2)

    def gather(d_ref, s):
        def issue(j, carry):
            for k in range(TOP_K):
                pltpu.make_async_copy(_tile(ys_hbm, d_ref[0, k, j]), _tile(buf.at[s, k], j),
                                      sem.at[s]).start(priority=k)
            return carry
        lax.fori_loop(0, tt, issue, 0, unroll=DMA_ISSUE_UNROLL)

    @pl.when(i == 0)
    def _():
        gather(dc_ref, 0)

    @pl.when(i + 1 < n)
    def _():
        gather(dn_ref, 1 - slot)

    for k in range(TOP_K):
        pltpu.make_async_copy(ys_hbm.at[pl.ds(0, tt * TOKEN_TILE_ROWS)], buf.at[slot, k], sem.at[slot]).wait()
    meta = meta_ref[...]
    ffn = (meta[:, META_W0:META_W0 + 1] * _token_tiles_to_rows(buf.at[slot, 0], tt)
           + meta[:, META_W1:META_W1 + 1] * _token_tiles_to_rows(buf.at[slot, 1], tt))
    y = _ln_rows(DEEPNORM_ALPHA * _token_tiles_to_rows(ht_ref, tt) + ffn, g_ref[...], b_ref[...])
    o_ref[...] = y
    obf_ref[...] = y.astype(BF16)


def _combine(ys, slots, meta, h_tiles, g, b):
    t, d = h_tiles.shape[0] // TOKEN_TILE_ROWS, D_MODEL
    tt = slots.shape[-1]
    n = t // tt
    cur = pl.BlockSpec((1, SUBLANES, tt), lambda i: (i, 0, 0), memory_space=pltpu.SMEM)
    nxt = pl.BlockSpec((1, SUBLANES, tt), lambda i: (jnp.minimum(i + 1, n - 1), 0, 0), memory_space=pltpu.SMEM)
    row = lambda i: (i, 0)
    const = lambda i: (0, 0)
    return pl.pallas_call(
        _combine_kernel,
        out_shape=(jax.ShapeDtypeStruct((t, d), F32), jax.ShapeDtypeStruct((t, d), BF16)),
        grid=(n,),
        in_specs=[cur, nxt, pl.BlockSpec(memory_space=pl.ANY),
                  pl.BlockSpec((tt, LANES), row), pl.BlockSpec((tt * TOKEN_TILE_ROWS, LANES), row),
                  pl.BlockSpec((1, d), const), pl.BlockSpec((1, d), const)],
        out_specs=(pl.BlockSpec((tt, d), row), pl.BlockSpec((tt, d), row)),
        scratch_shapes=[pltpu.VMEM((2, TOP_K, tt * TOKEN_TILE_ROWS, LANES), F32), pltpu.SemaphoreType.DMA((2,))],
        compiler_params=_cparams(("arbitrary",)),
        name="moe_combine_ln",
    )(slots, slots, ys, meta, h_tiles, g, b)


def _segment_tables(counts, t):
    tm = EXPERT_ROW_TILE
    n_rows = TOP_K * t + N_EXPERTS * tm
    cnt = counts[0, :N_EXPERTS].astype(jnp.int32)
    tiles = (cnt + tm - 1) // tm
    ends = jnp.cumsum(tiles)
    offs = jnp.zeros((1, LANES), F32).at[0, :N_EXPERTS].set(((ends - tiles) * tm).astype(F32))
    tile_ids = jnp.arange(n_rows // tm, dtype=jnp.int32)
    tile_expert = jnp.minimum(jnp.sum((tile_ids[:, None] >= ends[None, :]).astype(jnp.int32), axis=1),
                              N_EXPERTS - 1)
    last_tile = jnp.maximum(ends - 1, 0).astype(jnp.int32)
    return n_rows, offs, tile_expert, ends[-1:].astype(jnp.int32), last_tile


def _sparse_moe(h_tiles, meta, counts, wgu, wd, g, b):
    t = h_tiles.shape[0] // TOKEN_TILE_ROWS
    n_rows, offs, tile_expert, n_used, last_tile = _segment_tables(counts, t)
    slots = _slots(meta, offs, min(MOE_TOKENS, t))
    xs = _dispatch(h_tiles, slots, _zero_last_tiles(last_tile, n_rows))
    ys = _experts(xs, tile_expert, n_used, wgu, wd)
    return _combine(ys, slots, meta, h_tiles, g, b)


def _rotate_half_cols(w):
    half = w.shape[-1] // 2
    return jnp.concatenate([-w[..., half:], w[..., :half]], axis=-1)


def _pack_input_proj(w_in, b_in):
    d = w_in.shape[0]
    offs = np.cumsum((0, GLA_QK_W, GLA_QK_W, GLA_V_W, GLA_V_W, GLA_GATE_RANK, GLA_GATE_RANK,
                      MLA_Q_RANK, MLA_KV_RANK, MLA_ROPE, D_MODEL, D_MODEL))

    def seg(i):
        return w_in[:, offs[i]:offs[i + 1]], b_in[offs[i]:offs[i + 1]]

    w = jnp.zeros((d, N_PROJ), F32)
    b = jnp.zeros((N_PROJ,), F32)

    def put(w, b, col, ws, bs):
        return w.at[:, col:col + ws.shape[1]].set(ws), b.at[col:col + ws.shape[1]].set(bs)

    for i, col in ((0, COL_GQ), (1, COL_GK), (2, COL_GV), (3, COL_GR), (9, COL_GA), (10, COL_GB),
                   (6, COL_CQ), (7, COL_CKV), (4, COL_SMALL), (5, COL_SMALL + GLA_GATE_RANK)):
        w, b = put(w, b, col, *seg(i))
    wkr, bkr = seg(8)
    w, b = put(w, b, COL_SMALL + KR_LANE, wkr, bkr)
    w, b = put(w, b, COL_SMALL2 + KR_LANE, _rotate_half_cols(wkr), _rotate_half_cols(bkr))
    return w.astype(BF16), b.reshape(1, N_PROJ)


def _pack_decay(wa2, ba, lane0):
    w = jnp.zeros((LANES, GLA_QK_W), F32).at[lane0:lane0 + GLA_GATE_RANK].set(wa2)
    return w.astype(BF16), ba.reshape(1, GLA_QK_W)


def _pack_mla(w_uq, w_ukv, q_norm_g, kv_norm_g):
    wq = w_uq.reshape(MLA_Q_RANK, MLA_HEADS, MLA_QK)
    rope = wq[..., MLA_NOPE:]
    zq = jnp.zeros((MLA_Q_RANK, MLA_HEADS, HEAD_PAD - MLA_QK), F32)
    wq_p = jnp.concatenate([wq, zq], axis=-1)
    wqr_p = jnp.concatenate([jnp.zeros_like(wq[..., :MLA_NOPE]), _rotate_half_cols(rope), zq], axis=-1)

    def pad_rows(w):
        w = w.reshape(MLA_Q_RANK, MLA_HEADS * HEAD_PAD)
        return jnp.pad(w, ((0, CQ_PAD - MLA_Q_RANK), (0, 0))).astype(BF16)

    wkv = w_ukv.reshape(MLA_KV_RANK, MLA_HEADS, MLA_NOPE + MLA_V)
    wk_p = jnp.concatenate([wkv[..., :MLA_NOPE],
                            jnp.zeros((MLA_KV_RANK, MLA_HEADS, HEAD_PAD - MLA_NOPE), F32)], axis=-1)
    wk_p = wk_p.reshape(MLA_KV_RANK, MLA_HEADS * HEAD_PAD).astype(BF16)
    wv_p = wkv[..., MLA_NOPE:].reshape(MLA_KV_RANK, MLA_HEADS * MLA_V).T.astype(BF16)
    gq = jnp.pad(q_norm_g, (0, CQ_PAD - MLA_Q_RANK)).reshape(1, CQ_PAD)
    return pad_rows(wq_p), pad_rows(wqr_p), wk_p, wv_p, gq, kv_norm_g.reshape(1, MLA_KV_RANK)


def _rope_lane_table():
    inv = ROPE_BASE ** (-jnp.arange(0, MLA_ROPE, 2, dtype=F32) / MLA_ROPE)
    half = MLA_ROPE // 2
    tab = jnp.zeros((LANES,), F32)
    tab = tab.at[KR_LANE:KR_LANE + half].set(inv).at[KR_LANE + half:KR_LANE + MLA_ROPE].set(inv)
    return tab.reshape(1, LANES)


def _pack_router(w_grp, b_grp, w_exp, b_exp):
    d = w_grp.shape[0]
    w = jnp.zeros((d, LANES), F32).at[:, :N_EXPERTS].set(w_exp).at[:, GRP_LANE:GRP_LANE + N_GROUPS].set(w_grp)
    b = jnp.zeros((LANES,), F32).at[:N_EXPERTS].set(b_exp).at[GRP_LANE:GRP_LANE + N_GROUPS].set(b_grp)
    return w, b.reshape(1, LANES)


def kernel(x, positions, ln_emb_g, ln_emb_b, w_in, b_in, gla_wa2_f, gla_ba_f, gla_wa2_b, gla_ba_b, gla_norm_g, mla_q_norm_g, mla_w_uq, mla_kv_norm_g, mla_w_ukv, w_out, ln1_g, ln1_b, w_grp, b_grp, w_exp, b_exp, w_gate, w_up, w_down, ln2_g, ln2_b):
    batch, seq, d = x.shape
    t = batch * seq
    pos = positions.reshape(t, 1).astype(jnp.int32)
    inv_lane = _rope_lane_table()
    h, hb = _layer_norm(x.reshape(t, d), ln_emb_g, ln_emb_b)
    for l in range(DEPTH):
        w_p, b_p = _pack_input_proj(w_in[l], b_in[l])
        proj = _input_proj(hb, w_p, b_p)
        wa_f, ba_f = _pack_decay(gla_wa2_f[l], gla_ba_f[l], 0)
        wa_b, ba_b = _pack_decay(gla_wa2_b[l], gla_ba_b[l], GLA_GATE_RANK)
        o_f = _gla(proj, wa_f, ba_f, batch, seq, reverse=False)
        o_gla = _gla(proj, wa_b, ba_b, batch, seq, reverse=True, o_fwd=o_f,
                     norm_g=gla_norm_g[l].reshape(1, GLA_V_W))
        wq, wqr, wk, wv, gq, gkv = _pack_mla(mla_w_uq[l], mla_w_ukv[l], mla_q_norm_g[l], mla_kv_norm_g[l])
        q2, k2, v = _mla_prep(proj, pos, inv_lane, gq, gkv, wq, wqr, wk, wv)
        o_mla = _attention(q2, k2, v, batch, seq)
        h_tiles = _merge_out(o_gla, o_mla, proj, h, w_out[l].astype(BF16),
                           ln1_g[l].reshape(1, d), ln1_b[l].reshape(1, d))
        w_r, b_r = _pack_router(w_grp[l], b_grp[l], w_exp[l], b_exp[l])
        meta, counts = _router(h_tiles, w_r, b_r)
        wgu = jnp.concatenate([w_gate[l], w_up[l]], axis=-1).astype(BF16)
        h, hb = _sparse_moe(h_tiles, meta, counts, wgu, w_down[l].astype(BF16),
                            ln2_g[l].reshape(1, d), ln2_b[l].reshape(1, d))
    return h.reshape(batch, seq, d)
```

```python
import functools

import numpy as np
import jax
import jax.numpy as jnp
from jax import lax
from jax.experimental import pallas as pl
from jax.experimental.pallas import tpu as pltpu

F32 = jnp.float32
BF16 = jnp.bfloat16

D_MODEL = 1024
DEPTH = 2
GLA_HEADS = 4
GLA_DK = 128
GLA_DV = 256
GLA_GATE_RANK = 16
GLA_TAU = 16.0
MLA_HEADS = 16
MLA_NOPE = 64
MLA_ROPE = 32
MLA_V = 64
MLA_QK = MLA_NOPE + MLA_ROPE
MLA_Q_RANK = 384
MLA_KV_RANK = 128
ROPE_BASE = 10000.0
N_GROUPS = 8
EXPERTS_PER_GROUP = 4
N_EXPERTS = 32
D_EXPERT = 256
GLA_QK_W = GLA_HEADS * GLA_DK
GLA_V_W = GLA_HEADS * GLA_DV
DEEPNORM_ALPHA = (2.0 * DEPTH) ** 0.25
LN_EPS = 1e-5
RMS_EPS = 1e-6

LANES = 128
VMEM_LIMIT_BYTES = 56 * 1024 * 1024

COL_GQ = 0
COL_GK = 512
COL_GV = 1024
COL_GR = 2048
COL_GA = 3072
COL_GB = 4096
COL_CQ = 5120
CQ_PAD = 512
COL_SMALL = 5632
COL_SMALL2 = 5760
COL_CKV = 5888
N_PROJ = 6144
HEAD_PAD = 128
KR_LANE = 64

GLA_CHUNK = 128
NEG_BIG = -1e30


def _cparams(sem):
    return pltpu.CompilerParams(dimension_semantics=sem, vmem_limit_bytes=VMEM_LIMIT_BYTES)


def _ln_rows(x, g, b):
    mu = jnp.mean(x, axis=-1, keepdims=True)
    xc = x - mu
    var = jnp.mean(xc * xc, axis=-1, keepdims=True)
    return xc * lax.rsqrt(var + LN_EPS) * g + b


def _ln_kernel(x_ref, g_ref, b_ref, o_ref, ob_ref):
    y = _ln_rows(x_ref[...], g_ref[...], b_ref[...])
    o_ref[...] = y
    ob_ref[...] = y.astype(BF16)


def _layer_norm(x, g, b, tm=512):
    t, d = x.shape
    return pl.pallas_call(
        _ln_kernel,
        out_shape=(jax.ShapeDtypeStruct((t, d), F32), jax.ShapeDtypeStruct((t, d), BF16)),
        grid=(t // tm,),
        in_specs=[pl.BlockSpec((tm, d), lambda i: (i, 0)),
                  pl.BlockSpec((1, d), lambda i: (0, 0)),
                  pl.BlockSpec((1, d), lambda i: (0, 0))],
        out_specs=(pl.BlockSpec((tm, d), lambda i: (i, 0)), pl.BlockSpec((tm, d), lambda i: (i, 0))),
        compiler_params=_cparams(("parallel",)),
        name="ln_embed",
    )(x, g.reshape(1, d), b.reshape(1, d))


PROJ_COL_CHUNK = 512


def _proj_kernel(x_ref, w_ref, b_ref, o_ref):
    x = x_ref[...]
    for c in range(N_PROJ // PROJ_COL_CHUNK):
        sl = slice(c * PROJ_COL_CHUNK, (c + 1) * PROJ_COL_CHUNK)
        acc = jnp.dot(x, w_ref[:, sl], preferred_element_type=F32) + b_ref[:, sl]
        o_ref[:, sl] = acc.astype(o_ref.dtype)


def _input_proj(hb, w, b, tm=512):
    t, d = hb.shape
    return pl.pallas_call(
        _proj_kernel,
        out_shape=jax.ShapeDtypeStruct((t, N_PROJ), BF16),
        grid=(t // tm,),
        in_specs=[pl.BlockSpec((tm, d), lambda i: (i, 0)),
                  pl.BlockSpec((d, N_PROJ), lambda i: (0, 0)),
                  pl.BlockSpec((1, N_PROJ), lambda i: (0, 0))],
        out_specs=pl.BlockSpec((tm, N_PROJ), lambda i: (i, 0)),
        compiler_params=_cparams(("parallel",)),
        name="input_proj",
    )(hb, w, b)


def _log_sigmoid(x):
    return jnp.minimum(x, 0.0) - jnp.log(1.0 + jnp.exp(-jnp.abs(x)))


def _split_bf16(x):
    hi = x.astype(BF16)
    lo = (x - hi.astype(F32)).astype(BF16)
    return hi, lo


def _gla_kernel(*refs, reverse, n_chunks):
    if reverse:
        (q_ref, k_ref, v_ref, z_ref, wa_ref, ba_ref, of_ref, gr_ref, ng_ref, o_ref, state_ref) = refs
    else:
        (q_ref, k_ref, v_ref, z_ref, wa_ref, ba_ref, o_ref, state_ref) = refs
    c_len = GLA_CHUNK

    @pl.when(pl.program_id(1) == 0)
    def _():
        state_ref[...] = jnp.zeros_like(state_ref)

    row = lax.broadcasted_iota(jnp.int32, (c_len, c_len), 0)
    col = lax.broadcasted_iota(jnp.int32, (c_len, c_len), 1)
    if reverse:
        tri = (col >= row).astype(BF16)
        keep = col > row
        last = 0
    else:
        tri = (col <= row).astype(BF16)
        keep = col <= row
        last = c_len - 1

    def chunk(cc, carry):
        c = (n_chunks - 1 - cc) if reverse else cc
        r0 = pl.multiple_of(c * c_len, c_len)
        rows = pl.ds(r0, c_len)
        z = z_ref[rows, :]
        la = _log_sigmoid(jnp.dot(z, wa_ref[...], preferred_element_type=F32) + ba_ref[...]) * (1.0 / GLA_TAU)
        la_hi, la_lo = _split_bf16(la)
        b_all = (jnp.dot(tri, la_hi, preferred_element_type=F32)
                 + jnp.dot(tri, la_lo, preferred_element_type=F32))
        b_last = b_all[last:last + 1, :]
        q = q_ref[rows, :].astype(F32)
        k = k_ref[rows, :].astype(F32)
        qd_all = (q * (jnp.exp(b_all) * (GLA_DK ** -0.5))).astype(BF16)
        kinv_all = (k * jnp.exp(-b_all)).astype(BF16)
        kend_all = k * jnp.exp(b_last - b_all)
        dec_all = jnp.broadcast_to(jnp.exp(b_last), (c_len, GLA_QK_W))
        for h in range(GLA_HEADS):
            ks = slice(h * GLA_DK, (h + 1) * GLA_DK)
            vs = slice(h * GLA_DV, (h + 1) * GLA_DV)
            v = v_ref[rows, vs]
            qd = qd_all[:, ks]
            scores = lax.dot_general(qd, kinv_all[:, ks], (((1,), (1,)), ((), ())),
                                     preferred_element_type=F32)
            scores = jnp.where(keep, scores, 0.0).astype(BF16)
            state = state_ref[h]
            o = (jnp.dot(scores, v, preferred_element_type=F32)
                 + jnp.dot(qd, state.astype(BF16), preferred_element_type=F32))
            kend_t = kend_all[:, ks].T.astype(BF16)
            dec_t = dec_all[:, ks].T
            dec = jnp.concatenate([dec_t, dec_t], axis=1)
            state_ref[h] = dec * state + jnp.dot(kend_t, v, preferred_element_type=F32)
            if reverse:
                o = o + of_ref[rows, vs]
                ms = jnp.mean(o * o, axis=-1, keepdims=True)
                o = o * lax.rsqrt(ms + RMS_EPS) * ng_ref[:, vs]
                g = gr_ref[rows, vs].astype(F32)
                o = o * (g * jax.nn.sigmoid(g))
            o_ref[rows, vs] = o.astype(o_ref.dtype)
        return carry

    lax.fori_loop(0, n_chunks, chunk, 0)


def _gla(proj, wa, ba, batch, seq, *, reverse, o_fwd=None, norm_g=None, ts=1024):
    ts = min(ts, seq)
    nblk = seq // ts
    t = batch * seq

    def rblk(b, i):
        return b * nblk + ((nblk - 1 - i) if reverse else i)

    in_specs = [
        pl.BlockSpec((ts, GLA_QK_W), lambda b, i: (rblk(b, i), COL_GQ // GLA_QK_W)),
        pl.BlockSpec((ts, GLA_QK_W), lambda b, i: (rblk(b, i), COL_GK // GLA_QK_W)),
        pl.BlockSpec((ts, GLA_V_W), lambda b, i: (rblk(b, i), COL_GV // GLA_V_W)),
        pl.BlockSpec((ts, LANES), lambda b, i: (rblk(b, i), COL_SMALL // LANES)),
        pl.BlockSpec((LANES, GLA_QK_W), lambda b, i: (0, 0)),
        pl.BlockSpec((1, GLA_QK_W), lambda b, i: (0, 0)),
    ]
    args = [proj, proj, proj, proj, wa, ba]
    if reverse:
        in_specs += [
            pl.BlockSpec((ts, GLA_V_W), lambda b, i: (rblk(b, i), 0)),
            pl.BlockSpec((ts, GLA_V_W), lambda b, i: (rblk(b, i), COL_GR // GLA_V_W)),
            pl.BlockSpec((1, GLA_V_W), lambda b, i: (0, 0)),
        ]
        args += [o_fwd, proj, norm_g]
    return pl.pallas_call(
        functools.partial(_gla_kernel, reverse=reverse, n_chunks=ts // GLA_CHUNK),
        out_shape=jax.ShapeDtypeStruct((t, GLA_V_W), BF16 if reverse else F32),
        grid=(batch, nblk),
        in_specs=in_specs,
        out_specs=pl.BlockSpec((ts, GLA_V_W), lambda b, i: (rblk(b, i), 0)),
        scratch_shapes=[pltpu.VMEM((GLA_HEADS, GLA_DK, GLA_DV), F32)],
        compiler_params=_cparams(("parallel", "arbitrary")),
        name="gla_bwd" if reverse else "gla_fwd",
    )(*args)


MLA_COL_CHUNK = 512


def _mla_prep_kernel(cq_ref, ckv_ref, sm_ref, sm2_ref, pos_ref, inv_ref, gq_ref, gkv_ref,
                     wq_ref, wqr_ref, wk_ref, wv_ref, q_ref, k_ref, v_ref):
    cq = cq_ref[...].astype(F32)
    msq = jnp.sum(cq * cq, axis=-1, keepdims=True) * (1.0 / MLA_Q_RANK)
    cqn = (cq * lax.rsqrt(msq + RMS_EPS) * gq_ref[...]).astype(BF16)
    ckv = ckv_ref[...].astype(F32)
    mskv = jnp.mean(ckv * ckv, axis=-1, keepdims=True)
    ckvn = (ckv * lax.rsqrt(mskv + RMS_EPS) * gkv_ref[...]).astype(BF16)

    ang = pos_ref[...].astype(F32) * inv_ref[...]
    cos = jnp.cos(ang)
    sin = jnp.sin(ang)
    lane = lax.broadcasted_iota(jnp.int32, ang.shape, 1)
    kr = jnp.where(lane >= KR_LANE, sm_ref[...].astype(F32), 0.0)
    kr = kr * cos + sm2_ref[...].astype(F32) * sin

    heads_per_chunk = MLA_COL_CHUNK // HEAD_PAD
    cos_t = jnp.concatenate([cos] * heads_per_chunk, axis=1)
    sin_t = jnp.concatenate([sin] * heads_per_chunk, axis=1)
    kr_t = jnp.concatenate([kr] * heads_per_chunk, axis=1)
    qscale = (MLA_QK ** -0.5) * float(np.log2(np.e))
    for c in range(MLA_HEADS * HEAD_PAD // MLA_COL_CHUNK):
        sl = slice(c * MLA_COL_CHUNK, (c + 1) * MLA_COL_CHUNK)
        q = jnp.dot(cqn, wq_ref[:, sl], preferred_element_type=F32)
        qr = jnp.dot(cqn, wqr_ref[:, sl], preferred_element_type=F32)
        q_ref[:, sl] = ((q * cos_t + qr * sin_t) * qscale).astype(BF16)
        k = jnp.dot(ckvn, wk_ref[:, sl], preferred_element_type=F32)
        k_ref[:, sl] = (k + kr_t).astype(BF16)
    v_ref[...] = lax.dot_general(wv_ref[...], ckvn, (((1,), (1,)), ((), ())),
                                 preferred_element_type=F32).astype(BF16)


def _mla_prep(proj, pos, inv_lane, gq, gkv, wq, wqr, wk, wv, ts=512):
    t = proj.shape[0]
    hp = MLA_HEADS * HEAD_PAD
    vw = MLA_HEADS * MLA_V
    const = lambda i: (0, 0)
    return pl.pallas_call(
        _mla_prep_kernel,
        out_shape=(jax.ShapeDtypeStruct((t, hp), BF16), jax.ShapeDtypeStruct((t, hp), BF16),
                   jax.ShapeDtypeStruct((vw, t), BF16)),
        grid=(t // ts,),
        in_specs=[
            pl.BlockSpec((ts, CQ_PAD), lambda i: (i, COL_CQ // CQ_PAD)),
            pl.BlockSpec((ts, LANES), lambda i: (i, COL_CKV // LANES)),
            pl.BlockSpec((ts, LANES), lambda i: (i, COL_SMALL // LANES)),
            pl.BlockSpec((ts, LANES), lambda i: (i, COL_SMALL2 // LANES)),
            pl.BlockSpec((ts, 1), lambda i: (i, 0)),
            pl.BlockSpec((1, LANES), const),
            pl.BlockSpec((1, CQ_PAD), const),
            pl.BlockSpec((1, MLA_KV_RANK), const),
            pl.BlockSpec((CQ_PAD, hp), const),
            pl.BlockSpec((CQ_PAD, hp), const),
            pl.BlockSpec((MLA_KV_RANK, hp), const),
            pl.BlockSpec((vw, MLA_KV_RANK), const),
        ],
        out_specs=(pl.BlockSpec((ts, hp), lambda i: (i, 0)), pl.BlockSpec((ts, hp), lambda i: (i, 0)),
                   pl.BlockSpec((vw, ts), lambda i: (0, i))),
        compiler_params=_cparams(("parallel",)),
        name="mla_prep",
    )(proj, proj, proj, proj, pos, inv_lane, gq, gkv, wq, wqr, wk, wv)


SUBLANES = 8


BOUND_SLACK = 1.01
SHIFT_LIMIT = 48.0


def _attn_kernel(q_ref, k_ref, vt_ref, o_ref, s_scr, *, tk):
    seq = k_ref.shape[0]
    tq = s_scr.shape[-1]
    nk = seq // tk
    nq = seq // tq
    nt = (((1,), (1,)), ((), ()))
    head = lambda hh: slice(hh * HEAD_PAD, (hh + 1) * HEAD_PAD)
    ones8 = jnp.ones((SUBLANES, HEAD_PAD), BF16)
    ones_sq = jnp.ones((HEAD_PAD, HEAD_PAD), BF16)

    def max_key_norm_sq(hh):
        k = k_ref[:, head(hh)].astype(F32)
        row_sums = jnp.dot((k * k).astype(BF16), ones_sq, preferred_element_type=F32)
        return jnp.max(row_sums, axis=0, keepdims=True)[:, 0:1]

    k_max_sq = [max_key_norm_sq(hh) for hh in range(2)]

    def scores(hh, q, c):
        return lax.dot_general(k_ref[c * tk:(c + 1) * tk, head(hh)], q, nt, preferred_element_type=F32)

    def weigh(hh, c, p, l8, acc):
        l8 = l8 + jnp.sum(p.reshape(tk // SUBLANES, SUBLANES, tq), axis=0)
        vt = vt_ref[hh * MLA_V:(hh + 1) * MLA_V, c * tk:(c + 1) * tk]
        return l8, acc + jnp.dot(vt, p.astype(BF16), preferred_element_type=F32)

    def q_tile(i, carry):
        rows = pl.ds(pl.multiple_of(i * tq, tq), tq)
        qs = [q_ref[rows, head(hh)] for hh in range(2)]
        def shift_bound(hh):
            qf = qs[hh].astype(F32)
            q_sq = lax.dot_general(ones8, (qf * qf).astype(BF16), nt, preferred_element_type=F32)[0:1]
            return jnp.sqrt(q_sq * k_max_sq[hh]) * BOUND_SLACK

        bound = [shift_bound(hh) for hh in range(2)]
        safe = jnp.max(jnp.maximum(bound[0], bound[1])) < SHIFT_LIMIT
        zeros = lambda n: jnp.zeros((n, tq), F32)

        @pl.when(safe)
        def _():
            l8, acc = [zeros(SUBLANES)] * 2, [zeros(MLA_V)] * 2
            st_next = [scores(hh, qs[hh], 0) for hh in range(2)]
            for c in range(nk):
                st = st_next
                if c + 1 < nk:
                    st_next = [scores(hh, qs[hh], c + 1) for hh in range(2)]
                for hh in range(2):
                    l8[hh], acc[hh] = weigh(hh, c, jnp.exp2(st[hh] - bound[hh]), l8[hh], acc[hh])
            out = [acc[hh] / jnp.sum(l8[hh], axis=0, keepdims=True) for hh in range(2)]
            o_ref[rows, :] = jnp.concatenate(out, axis=0).T.astype(BF16)

        @pl.when(jnp.logical_not(safe))
        def _():
            out = []
            for hh in range(2):
                m8 = jnp.full((SUBLANES, tq), NEG_BIG, F32)
                for c in range(nk):
                    st = scores(hh, qs[hh], c)
                    s_scr[c] = st
                    m8 = jnp.maximum(m8, jnp.max(st.reshape(tk // SUBLANES, SUBLANES, tq), axis=0))
                m = jnp.max(m8, axis=0, keepdims=True)
                l8, acc = zeros(SUBLANES), zeros(MLA_V)
                for c in range(nk):
                    l8, acc = weigh(hh, c, jnp.exp2(s_scr[c] - m), l8, acc)
                out.append(acc / jnp.sum(l8, axis=0, keepdims=True))
            o_ref[rows, :] = jnp.concatenate(out, axis=0).T.astype(BF16)

        return carry

    lax.fori_loop(0, nq, q_tile, 0)


def _attention(q2, k2, vt, batch, seq, tq=512, tk=512):
    tq = min(tq, seq)
    tk = min(tk, seq)
    t = batch * seq
    pairs = MLA_HEADS // 2
    return pl.pallas_call(
        functools.partial(_attn_kernel, tk=tk),
        scratch_shapes=[pltpu.VMEM((seq // tk, tk, tq), F32)],
        out_shape=jax.ShapeDtypeStruct((t, MLA_HEADS * MLA_V), BF16),
        grid=(batch, pairs),
        in_specs=[
            pl.BlockSpec((seq, 2 * HEAD_PAD), lambda b, p: (b, p)),
            pl.BlockSpec((seq, 2 * HEAD_PAD), lambda b, p: (b, p)),
            pl.BlockSpec((2 * MLA_V, seq), lambda b, p: (p, b)),
        ],
        out_specs=pl.BlockSpec((seq, 2 * MLA_V), lambda b, p: (b, p)),
        compiler_params=_cparams(("parallel", "parallel")),
        name="mla_attention",
    )(q2, k2, vt)


TOKEN_TILE_ROWS = D_MODEL // LANES


def _rows_to_token_tiles(ref, y):
    n = y.shape[0]
    for s in range(TOKEN_TILE_ROWS):
        ref[pl.ds(s, n, stride=TOKEN_TILE_ROWS), :] = y[:, s * LANES:(s + 1) * LANES]


def _token_tiles_to_rows(ref, n):
    return jnp.concatenate([ref[pl.ds(s, n, stride=TOKEN_TILE_ROWS), :] for s in range(TOKEN_TILE_ROWS)], axis=1)


def _merge_out_kernel(oa_ref, ob_ref, ga_ref, gb_ref, h_ref, w_ref, g_ref, b_ref, ot_ref):
    merged = (jax.nn.sigmoid(ga_ref[...].astype(F32)) * oa_ref[...].astype(F32)
              + jax.nn.sigmoid(gb_ref[...].astype(F32)) * ob_ref[...].astype(F32))
    mix = jnp.dot(merged.astype(BF16), w_ref[...], preferred_element_type=F32)
    y = _ln_rows(DEEPNORM_ALPHA * h_ref[...] + mix, g_ref[...], b_ref[...])
    _rows_to_token_tiles(ot_ref, y)


def _merge_out(o_gla, o_mla, proj, h, w_out, g, b, tm=512):
    t, d = h.shape
    row = lambda i: (i, 0)
    const = lambda i: (0, 0)
    return pl.pallas_call(
        _merge_out_kernel,
        out_shape=jax.ShapeDtypeStruct((t * TOKEN_TILE_ROWS, LANES), F32),
        grid=(t // tm,),
        in_specs=[
            pl.BlockSpec((tm, d), row),
            pl.BlockSpec((tm, d), row),
            pl.BlockSpec((tm, d), lambda i: (i, COL_GA // D_MODEL)),
            pl.BlockSpec((tm, d), lambda i: (i, COL_GB // D_MODEL)),
            pl.BlockSpec((tm, d), row),
            pl.BlockSpec((d, d), const),
            pl.BlockSpec((1, d), const),
            pl.BlockSpec((1, d), const),
        ],
        out_specs=pl.BlockSpec((tm * TOKEN_TILE_ROWS, LANES), row),
        compiler_params=_cparams(("parallel",)),
        name="merge_out_ln",
    )(o_gla, o_mla, proj, proj, h, w_out, g, b)


GRP_LANE = N_EXPERTS
META_W0, META_W1, META_E0, META_E1, META_R0, META_R1 = range(6)


def _router_kernel(x_ref, w_ref, b_ref, meta_ref, cnt_ref, carry_ref):
    @pl.when(pl.program_id(0) == 0)
    def _():
        carry_ref[...] = jnp.zeros_like(carry_ref)

    tm = x_ref.shape[0] // TOKEN_TILE_ROWS
    x_hi, x_lo = _split_bf16(_token_tiles_to_rows(x_ref, tm))
    w_hi, w_lo = _split_bf16(w_ref[...])
    logits = (jnp.dot(x_hi, w_hi, preferred_element_type=F32)
              + jnp.dot(x_lo, w_hi, preferred_element_type=F32)
              + jnp.dot(x_hi, w_lo, preferred_element_type=F32)) + b_ref[...]
    lane = lax.broadcasted_iota(jnp.int32, logits.shape, 1)
    is_grp = (lane >= GRP_LANE) & (lane < GRP_LANE + N_GROUPS)
    gl = jnp.where(is_grp, logits, NEG_BIG)
    gmax = jnp.max(gl, axis=1, keepdims=True)
    g_lane = jnp.min(jnp.where(gl == gmax, lane, 4 * LANES), axis=1, keepdims=True)
    g_w = 1.0 / jnp.sum(jnp.where(is_grp, jnp.exp(gl - gmax), 0.0), axis=1, keepdims=True)
    lo_lane = (g_lane - GRP_LANE) * EXPERTS_PER_GROUP
    in_grp = (lane >= lo_lane) & (lane < lo_lane + EXPERTS_PER_GROUP)
    el = jnp.where(in_grp, logits, NEG_BIG)
    v1 = jnp.max(el, axis=1, keepdims=True)
    i1 = jnp.min(jnp.where(el == v1, lane, 4 * LANES), axis=1, keepdims=True)
    el2 = jnp.where(lane == i1, NEG_BIG, el)
    v2 = jnp.max(el2, axis=1, keepdims=True)
    i2 = jnp.min(jnp.where(el2 == v2, lane, 4 * LANES), axis=1, keepdims=True)
    e2 = jnp.exp(v2 - v1)
    w1 = g_w / (1.0 + e2)
    w2 = g_w * e2 / (1.0 + e2)

    onehot = jnp.where(lane == i1, 1.0, jnp.where(lane == i2, 1.0, 0.0))
    r = lax.broadcasted_iota(jnp.int32, (tm, tm), 0)
    c = lax.broadcasted_iota(jnp.int32, (tm, tm), 1)
    earlier = (c < r).astype(BF16)
    base = carry_ref[...] + jnp.dot(earlier, onehot.astype(BF16), preferred_element_type=F32)
    rank1 = jnp.sum(jnp.where(lane == i1, base, 0.0), axis=1, keepdims=True)
    rank2 = jnp.sum(jnp.where(lane == i2, base, 0.0), axis=1, keepdims=True)
    meta = jnp.zeros(logits.shape, F32)
    for ln, val in ((META_W0, w1), (META_W1, w2), (META_E0, i1.astype(F32)), (META_E1, i2.astype(F32)),
                    (META_R0, rank1), (META_R1, rank2)):
        meta = jnp.where(lane == ln, val, meta)
    meta_ref[...] = meta
    carry_ref[...] += jnp.sum(onehot, axis=0, keepdims=True)
    cnt_ref[...] = carry_ref[...]


def _router(h_tiles, w_r, b_r, tm=512):
    t, d = h_tiles.shape[0] // TOKEN_TILE_ROWS, D_MODEL
    return pl.pallas_call(
        _router_kernel,
        out_shape=(jax.ShapeDtypeStruct((t, LANES), F32), jax.ShapeDtypeStruct((1, LANES), F32)),
        grid=(t // tm,),
        in_specs=[pl.BlockSpec((tm * TOKEN_TILE_ROWS, LANES), lambda i: (i, 0)),
                  pl.BlockSpec((d, LANES), lambda i: (0, 0)),
                  pl.BlockSpec((1, LANES), lambda i: (0, 0))],
        out_specs=(pl.BlockSpec((tm, LANES), lambda i: (i, 0)), pl.BlockSpec((1, LANES), lambda i: (0, 0))),
        scratch_shapes=[pltpu.VMEM((1, LANES), F32)],
        compiler_params=_cparams(("arbitrary",)),
        name="router",
    )(h_tiles, w_r, b_r)


EXPERT_ROW_TILE = 512
MOE_TOKENS = 256
TOP_K = 2
DMA_ISSUE_UNROLL = 8


def _tile(ref, token):
    return ref.at[pl.ds(pl.multiple_of(token * TOKEN_TILE_ROWS, TOKEN_TILE_ROWS), TOKEN_TILE_ROWS)]


def _slots_kernel(meta_ref, offs_ref, d_ref):
    meta = meta_ref[...]
    offs = offs_ref[...]
    lane = lax.broadcasted_iota(jnp.int32, meta.shape, 1)

    def slot(e_lane, r_lane):
        e = meta[:, e_lane:e_lane + 1].astype(jnp.int32)
        return jnp.sum(jnp.where(lane == e, offs, 0.0), axis=1, keepdims=True) + meta[:, r_lane:r_lane + 1]

    both = jnp.where(lane == 0, slot(META_E0, META_R0), jnp.where(lane == 1, slot(META_E1, META_R1), 0.0))
    d_ref[0] = both.T[:SUBLANES, :].astype(jnp.int32)


def _slots(meta, offs, tt):
    t = meta.shape[0]
    return pl.pallas_call(
        _slots_kernel,
        out_shape=jax.ShapeDtypeStruct((t // tt, SUBLANES, tt), jnp.int32),
        grid=(t // tt,),
        in_specs=[pl.BlockSpec((tt, LANES), lambda i: (i, 0)), pl.BlockSpec((1, LANES), lambda i: (0, 0))],
        out_specs=pl.BlockSpec((1, SUBLANES, tt), lambda i: (i, 0, 0)),
        compiler_params=_cparams(("arbitrary",)),
        name="moe_slots",
    )(meta, offs)


def _zero_tile_kernel(lt_ref, xs_ref):
    del lt_ref
    xs_ref[...] = jnp.zeros_like(xs_ref)


def _zero_last_tiles(last_tile, n_rows):
    blk = EXPERT_ROW_TILE * TOKEN_TILE_ROWS
    grid_spec = pltpu.PrefetchScalarGridSpec(
        num_scalar_prefetch=1, grid=(N_EXPERTS,), in_specs=[],
        out_specs=pl.BlockSpec((blk, LANES), lambda e, lt: (lt[e], 0)))
    return pl.pallas_call(
        _zero_tile_kernel,
        out_shape=jax.ShapeDtypeStruct((n_rows * TOKEN_TILE_ROWS, LANES), F32),
        grid_spec=grid_spec,
        compiler_params=_cparams(("arbitrary",)),
        name="moe_zero_tiles",
    )(last_tile)


def _dispatch_kernel(d_ref, ht_ref, xs_init_hbm, xs_hbm, sem):
    del xs_init_hbm
    tt = d_ref.shape[-1]

    def issue(j, carry):
        src = _tile(ht_ref, j)
        for k in range(TOP_K):
            pltpu.make_async_copy(src, _tile(xs_hbm, d_ref[0, k, j]), sem).start(priority=k)
        return carry

    lax.fori_loop(0, tt, issue, 0, unroll=DMA_ISSUE_UNROLL)
    for k in range(TOP_K):
        pltpu.make_async_copy(ht_ref, xs_hbm.at[pl.ds(0, tt * TOKEN_TILE_ROWS)], sem).wait()


def _dispatch(h_tiles, slots, xs_init):
    tt = slots.shape[-1]
    t = h_tiles.shape[0] // TOKEN_TILE_ROWS
    return pl.pallas_call(
        _dispatch_kernel,
        out_shape=jax.ShapeDtypeStruct(xs_init.shape, F32),
        grid=(t // tt,),
        in_specs=[pl.BlockSpec((1, SUBLANES, tt), lambda i: (i, 0, 0), memory_space=pltpu.SMEM),
                  pl.BlockSpec((tt * TOKEN_TILE_ROWS, LANES), lambda i: (i, 0)),
                  pl.BlockSpec(memory_space=pl.ANY)],
        out_specs=pl.BlockSpec(memory_space=pl.ANY),
        scratch_shapes=[pltpu.SemaphoreType.DMA(())],
        input_output_aliases={2: 0},
        compiler_params=_cparams(("arbitrary",)),
        name="moe_dispatch",
    )(slots, h_tiles, xs_init)


def _expert_kernel(te_ref, nu_ref, xs_ref, wgu_ref, wd_ref, ys_ref):
    del te_ref
    i = pl.program_id(0)
    tm = EXPERT_ROW_TILE

    @pl.when(i < nu_ref[0])
    def _():
        x = _token_tiles_to_rows(xs_ref, tm).astype(BF16)
        gu = jnp.dot(x, wgu_ref[0], preferred_element_type=F32)
        gate = gu[:, :D_EXPERT]
        hid = gate * jax.nn.sigmoid(gate) * gu[:, D_EXPERT:]
        _rows_to_token_tiles(ys_ref, jnp.dot(hid.astype(BF16), wd_ref[0], preferred_element_type=F32))

    @pl.when(i >= nu_ref[0])
    def _():
        ys_ref[...] = jnp.zeros_like(ys_ref)


def _experts(xs, tile_expert, n_used, wgu, wd):
    d = D_MODEL
    tm = EXPERT_ROW_TILE
    blk = pl.BlockSpec((tm * TOKEN_TILE_ROWS, LANES), lambda i, te, nu: (i, 0))
    blk_in = pl.BlockSpec((tm * TOKEN_TILE_ROWS, LANES), lambda i, te, nu: (jnp.minimum(i, nu[0] - 1), 0))
    grid_spec = pltpu.PrefetchScalarGridSpec(
        num_scalar_prefetch=2,
        grid=(xs.shape[0] // (tm * TOKEN_TILE_ROWS),),
        in_specs=[blk_in,
                  pl.BlockSpec((1, d, 2 * D_EXPERT), lambda i, te, nu: (te[i], 0, 0)),
                  pl.BlockSpec((1, D_EXPERT, d), lambda i, te, nu: (te[i], 0, 0))],
        out_specs=blk,
    )
    return pl.pallas_call(
        _expert_kernel,
        out_shape=jax.ShapeDtypeStruct(xs.shape, F32),
        grid_spec=grid_spec,
        compiler_params=_cparams(("arbitrary",)),
        name="moe_experts",
    )(tile_expert, n_used, xs, wgu, wd)


def _combine_kernel(dc_ref, dn_ref, ys_hbm, meta_ref, ht_ref, g_ref, b_ref, o_ref, obf_ref, buf, sem):
    i = pl.program_id(0)
    n = pl.num_programs(0)
    tt = dc_ref.shape[-1]
    slot = lax.rem(i, 2)

    def gather(d_ref, s):
        def issue(j, carry):
            for k in range(TOP_K):
                pltpu.make_async_copy(_tile(ys_hbm, d_ref[0, k, j]), _tile(buf.at[s, k], j),
                                      sem.at[s]).start(priority=k)
            return carry
        lax.fori_loop(0, tt, issue, 0, unroll=DMA_ISSUE_UNROLL)

    @pl.when(i == 0)
    def _():
        gather(dc_ref, 0)

    @pl.when(i + 1 < n)
    def _():
        gather(dn_ref, 1 - slot)

    for k in range(TOP_K):
        pltpu.make_async_copy(ys_hbm.at[pl.ds(0, tt * TOKEN_TILE_ROWS)], buf.at[slot, k], sem.at[slot]).wait()
    meta = meta_ref[...]
    ffn = (meta[:, META_W0:META_W0 + 1] * _token_tiles_to_rows(buf.at[slot, 0], tt)
           + meta[:, META_W1:META_W1 + 1] * _token_tiles_to_rows(buf.at[slot, 1], tt))
    y = _ln_rows(DEEPNORM_ALPHA * _token_tiles_to_rows(ht_ref, tt) + ffn, g_ref[...], b_ref[...])
    o_ref[...] = y
    obf_ref[...] = y.astype(BF16)


def _combine(ys, slots, meta, h_tiles, g, b):
    t, d = h_tiles.shape[0] // TOKEN_TILE_ROWS, D_MODEL
    tt = slots.shape[-1]
    n = t // tt
    cur = pl.BlockSpec((1, SUBLANES, tt), lambda i: (i, 0, 0), memory_space=pltpu.SMEM)
    nxt = pl.BlockSpec((1, SUBLANES, tt), lambda i: (jnp.minimum(i + 1, n - 1), 0, 0), memory_space=pltpu.SMEM)
    row = lambda i: (i, 0)
    const = lambda i: (0, 0)
    return pl.pallas_call(
        _combine_kernel,
        out_shape=(jax.ShapeDtypeStruct((t, d), F32), jax.ShapeDtypeStruct((t, d), BF16)),
        grid=(n,),
        in_specs=[cur, nxt, pl.BlockSpec(memory_space=pl.ANY),
                  pl.BlockSpec((tt, LANES), row), pl.BlockSpec((tt * TOKEN_TILE_ROWS, LANES), row),
                  pl.BlockSpec((1, d), const), pl.BlockSpec((1, d), const)],
        out_specs=(pl.BlockSpec((tt, d), row), pl.BlockSpec((tt, d), row)),
        scratch_shapes=[pltpu.VMEM((2, TOP_K, tt * TOKEN_TILE_ROWS, LANES), F32), pltpu.SemaphoreType.DMA((2,))],
        compiler_params=_cparams(("arbitrary",)),
        name="moe_combine_ln",
    )(slots, slots, ys, meta, h_tiles, g, b)


def _segment_tables(counts, t):
    tm = EXPERT_ROW_TILE
    n_rows = TOP_K * t + N_EXPERTS * tm
    cnt = counts[0, :N_EXPERTS].astype(jnp.int32)
    tiles = (cnt + tm - 1) // tm
    ends = jnp.cumsum(tiles)
    offs = jnp.zeros((1, LANES), F32).at[0, :N_EXPERTS].set(((ends - tiles) * tm).astype(F32))
    tile_ids = jnp.arange(n_rows // tm, dtype=jnp.int32)
    tile_expert = jnp.minimum(jnp.sum((tile_ids[:, None] >= ends[None, :]).astype(jnp.int32), axis=1),
                              N_EXPERTS - 1)
    last_tile = jnp.maximum(ends - 1, 0).astype(jnp.int32)
    return n_rows, offs, tile_expert, ends[-1:].astype(jnp.int32), last_tile


def _sparse_moe(h_tiles, meta, counts, wgu, wd, g, b):
    t = h_tiles.shape[0] // TOKEN_TILE_ROWS
    n_rows, offs, tile_expert, n_used, last_tile = _segment_tables(counts, t)
    slots = _slots(meta, offs, min(MOE_TOKENS, t))
    xs = _dispatch(h_tiles, slots, _zero_last_tiles(last_tile, n_rows))
    ys = _experts(xs, tile_expert, n_used, wgu, wd)
    return _combine(ys, slots, meta, h_tiles, g, b)


def _rotate_half_cols(w):
    half = w.shape[-1] // 2
    return jnp.concatenate([-w[..., half:], w[..., :half]], axis=-1)


def _pack_input_proj(w_in, b_in):
    d = w_in.shape[0]
    offs = np.cumsum((0, GLA_QK_W, GLA_QK_W, GLA_V_W, GLA_V_W, GLA_GATE_RANK, GLA_GATE_RANK,
                      MLA_Q_RANK, MLA_KV_RANK, MLA_ROPE, D_MODEL, D_MODEL))

    def seg(i):
        return w_in[:, offs[i]:offs[i + 1]], b_in[offs[i]:offs[i + 1]]

    w = jnp.zeros((d, N_PROJ), F32)
    b = jnp.zeros((N_PROJ,), F32)

    def put(w, b, col, ws, bs):
        return w.at[:, col:col + ws.shape[1]].set(ws), b.at[col:col + ws.shape[1]].set(bs)

    for i, col in ((0, COL_GQ), (1, COL_GK), (2, COL_GV), (3, COL_GR), (9, COL_GA), (10, COL_GB),
                   (6, COL_CQ), (7, COL_CKV), (4, COL_SMALL), (5, COL_SMALL + GLA_GATE_RANK)):
        w, b = put(w, b, col, *seg(i))
    wkr, bkr = seg(8)
    w, b = put(w, b, COL_SMALL + KR_LANE, wkr, bkr)
    w, b = put(w, b, COL_SMALL2 + KR_LANE, _rotate_half_cols(wkr), _rotate_half_cols(bkr))
    return w.astype(BF16), b.reshape(1, N_PROJ)


def _pack_decay(wa2, ba, lane0):
    w = jnp.zeros((LANES, GLA_QK_W), F32).at[lane0:lane0 + GLA_GATE_RANK].set(wa2)
    return w.astype(BF16), ba.reshape(1, GLA_QK_W)


def _pack_mla(w_uq, w_ukv, q_norm_g, kv_norm_g):
    wq = w_uq.reshape(MLA_Q_RANK, MLA_HEADS, MLA_QK)
    rope = wq[..., MLA_NOPE:]
    zq = jnp.zeros((MLA_Q_RANK, MLA_HEADS, HEAD_PAD - MLA_QK), F32)
    wq_p = jnp.concatenate([wq, zq], axis=-1)
    wqr_p = jnp.concatenate([jnp.zeros_like(wq[..., :MLA_NOPE]), _rotate_half_cols(rope), zq], axis=-1)

    def pad_rows(w):
        w = w.reshape(MLA_Q_RANK, MLA_HEADS * HEAD_PAD)
        return jnp.pad(w, ((0, CQ_PAD - MLA_Q_RANK), (0, 0))).astype(BF16)

    wkv = w_ukv.reshape(MLA_KV_RANK, MLA_HEADS, MLA_NOPE + MLA_V)
    wk_p = jnp.concatenate([wkv[..., :MLA_NOPE],
                            jnp.zeros((MLA_KV_RANK, MLA_HEADS, HEAD_PAD - MLA_NOPE), F32)], axis=-1)
    wk_p = wk_p.reshape(MLA_KV_RANK, MLA_HEADS * HEAD_PAD).astype(BF16)
    wv_p = wkv[..., MLA_NOPE:].reshape(MLA_KV_RANK, MLA_HEADS * MLA_V).T.astype(BF16)
    gq = jnp.pad(q_norm_g, (0, CQ_PAD - MLA_Q_RANK)).reshape(1, CQ_PAD)
    return pad_rows(wq_p), pad_rows(wqr_p), wk_p, wv_p, gq, kv_norm_g.reshape(1, MLA_KV_RANK)


def _rope_lane_table():
    inv = ROPE_BASE ** (-jnp.arange(0, MLA_ROPE, 2, dtype=F32) / MLA_ROPE)
    half = MLA_ROPE // 2
    tab = jnp.zeros((LANES,), F32)
    tab = tab.at[KR_LANE:KR_LANE + half].set(inv).at[KR_LANE + half:KR_LANE + MLA_ROPE].set(inv)
    return tab.reshape(1, LANES)


def _pack_router(w_grp, b_grp, w_exp, b_exp):
    d = w_grp.shape[0]
    w = jnp.zeros((d, LANES), F32).at[:, :N_EXPERTS].set(w_exp).at[:, GRP_LANE:GRP_LANE + N_GROUPS].set(w_grp)
    b = jnp.zeros((LANES,), F32).at[:N_EXPERTS].set(b_exp).at[GRP_LANE:GRP_LANE + N_GROUPS].set(b_grp)
    return w, b.reshape(1, LANES)


def kernel(x, positions, ln_emb_g, ln_emb_b, w_in, b_in, gla_wa2_f, gla_ba_f, gla_wa2_b, gla_ba_b, gla_norm_g, mla_q_norm_g, mla_w_uq, mla_kv_norm_g, mla_w_ukv, w_out, ln1_g, ln1_b, w_grp, b_grp, w_exp, b_exp, w_gate, w_up, w_down, ln2_g, ln2_b):
    batch, seq, d = x.shape
    t = batch * seq
    pos = positions.reshape(t, 1).astype(jnp.int32)
    inv_lane = _rope_lane_table()
    h, hb = _layer_norm(x.reshape(t, d), ln_emb_g, ln_emb_b)
    for l in range(DEPTH):
        w_p, b_p = _pack_input_proj(w_in[l], b_in[l])
        proj = _input_proj(hb, w_p, b_p)
        wa_f, ba_f = _pack_decay(gla_wa2_f[l], gla_ba_f[l], 0)
        wa_b, ba_b = _pack_decay(gla_wa2_b[l], gla_ba_b[l], GLA_GATE_RANK)
        o_f = _gla(proj, wa_f, ba_f, batch, seq, reverse=False)
        o_gla = _gla(proj, wa_b, ba_b, batch, seq, reverse=True, o_fwd=o_f,
                     norm_g=gla_norm_g[l].reshape(1, GLA_V_W))
        wq, wqr, wk, wv, gq, gkv = _pack_mla(mla_w_uq[l], mla_w_ukv[l], mla_q_norm_g[l], mla_kv_norm_g[l])
        q2, k2, v = _mla_prep(proj, pos, inv_lane, gq, gkv, wq, wqr, wk, wv)
        o_mla = _attention(q2, k2, v, batch, seq)
        h_tiles = _merge_out(o_gla, o_mla, proj, h, w_out[l].astype(BF16),
                           ln1_g[l].reshape(1, d), ln1_b[l].reshape(1, d))
        w_r, b_r = _pack_router(w_grp[l], b_grp[l], w_exp[l], b_exp[l])
        meta, counts = _router(h_tiles, w_r, b_r)
        wgu = jnp.concatenate([w_gate[l], w_up[l]], axis=-1).astype(BF16)
        h, hb = _sparse_moe(h_tiles, meta, counts, wgu, w_down[l].astype(BF16),
                            ln2_g[l].reshape(1, d), ln2_b[l].reshape(1, d))
    return h.reshape(batch, seq, d)
```

```python
import functools

import numpy as np
import jax
import jax.numpy as jnp
from jax import lax
from jax.experimental import pallas as pl
from jax.experimental.pallas import tpu as pltpu

F32 = jnp.float32
BF16 = jnp.bfloat16

D_MODEL = 1024
DEPTH = 2
GLA_HEADS = 4
GLA_DK = 128
GLA_DV = 256
GLA_GATE_RANK = 16
GLA_TAU = 16.0
MLA_HEADS = 16
MLA_NOPE = 64
MLA_ROPE = 32
MLA_V = 64
MLA_QK = MLA_NOPE + MLA_ROPE
MLA_Q_RANK = 384
MLA_KV_RANK = 128
ROPE_BASE = 10000.0
N_GROUPS = 8
EXPERTS_PER_GROUP = 4
N_EXPERTS = 32
D_EXPERT = 256
GLA_QK_W = GLA_HEADS * GLA_DK
GLA_V_W = GLA_HEADS * GLA_DV
DEEPNORM_ALPHA = (2.0 * DEPTH) ** 0.25
LN_EPS = 1e-5
RMS_EPS = 1e-6

LANES = 128
VMEM_LIMIT_BYTES = 56 * 1024 * 1024

COL_GQ = 0
COL_GK = 512
COL_GV = 1024
COL_GR = 2048
COL_GA = 3072
COL_GB = 4096
COL_CQ = 5120
CQ_PAD = 512
COL_SMALL = 5632
COL_SMALL2 = 5760
COL_CKV = 5888
N_PROJ = 6144
HEAD_PAD = 128
KR_LANE = 64

GLA_CHUNK = 128
NEG_BIG = -1e30


def _cparams(sem):
    return pltpu.CompilerParams(dimension_semantics=sem, vmem_limit_bytes=VMEM_LIMIT_BYTES)


def _ln_rows(x, g, b):
    mu = jnp.mean(x, axis=-1, keepdims=True)
    xc = x - mu
    var = jnp.mean(xc * xc, axis=-1, keepdims=True)
    return xc * lax.rsqrt(var + LN_EPS) * g + b


def _ln_kernel(x_ref, g_ref, b_ref, o_ref, ob_ref):
    y = _ln_rows(x_ref[...], g_ref[...], b_ref[...])
    o_ref[...] = y
    ob_ref[...] = y.astype(BF16)


def _layer_norm(x, g, b, tm=512):
    t, d = x.shape
    return pl.pallas_call(
        _ln_kernel,
        out_shape=(jax.ShapeDtypeStruct((t, d), F32), jax.ShapeDtypeStruct((t, d), BF16)),
        grid=(t // tm,),
        in_specs=[pl.BlockSpec((tm, d), lambda i: (i, 0)),
                  pl.BlockSpec((1, d), lambda i: (0, 0)),
                  pl.BlockSpec((1, d), lambda i: (0, 0))],
        out_specs=(pl.BlockSpec((tm, d), lambda i: (i, 0)), pl.BlockSpec((tm, d), lambda i: (i, 0))),
        compiler_params=_cparams(("parallel",)),
        name="ln_embed",
    )(x, g.reshape(1, d), b.reshape(1, d))


PROJ_COL_CHUNK = 512


def _proj_kernel(x_ref, w_ref, b_ref, o_ref):
    x = x_ref[...]
    for c in range(N_PROJ // PROJ_COL_CHUNK):
        sl = slice(c * PROJ_COL_CHUNK, (c + 1) * PROJ_COL_CHUNK)
        acc = jnp.dot(x, w_ref[:, sl], preferred_element_type=F32) + b_ref[:, sl]
        o_ref[:, sl] = acc.astype(o_ref.dtype)


def _input_proj(hb, w, b, tm=512):
    t, d = hb.shape
    return pl.pallas_call(
        _proj_kernel,
        out_shape=jax.ShapeDtypeStruct((t, N_PROJ), BF16),
        grid=(t // tm,),
        in_specs=[pl.BlockSpec((tm, d), lambda i: (i, 0)),
                  pl.BlockSpec((d, N_PROJ), lambda i: (0, 0)),
                  pl.BlockSpec((1, N_PROJ), lambda i: (0, 0))],
        out_specs=pl.BlockSpec((tm, N_PROJ), lambda i: (i, 0)),
        compiler_params=_cparams(("parallel",)),
        name="input_proj",
    )(hb, w, b)


def _log_sigmoid(x):
    return jnp.minimum(x, 0.0) - jnp.log(1.0 + jnp.exp(-jnp.abs(x)))


def _split_bf16(x):
    hi = x.astype(BF16)
    lo = (x - hi.astype(F32)).astype(BF16)
    return hi, lo


def _gla_kernel(*refs, reverse, n_chunks):
    if reverse:
        (q_ref, k_ref, v_ref, z_ref, wa_ref, ba_ref, of_ref, gr_ref, ng_ref, o_ref, state_ref) = refs
    else:
        (q_ref, k_ref, v_ref, z_ref, wa_ref, ba_ref, o_ref, state_ref) = refs
    c_len = GLA_CHUNK

    @pl.when(pl.program_id(1) == 0)
    def _():
        state_ref[...] = jnp.zeros_like(state_ref)

    row = lax.broadcasted_iota(jnp.int32, (c_len, c_len), 0)
    col = lax.broadcasted_iota(jnp.int32, (c_len, c_len), 1)
    if reverse:
        tri = (col >= row).astype(BF16)
        keep = col > row
        last = 0
    else:
        tri = (col <= row).astype(BF16)
        keep = col <= row
        last = c_len - 1

    seqs = range(q_ref.shape[0])

    def decays(bi, rows):
        z = z_ref[bi, rows, :]
        la = _log_sigmoid(jnp.dot(z, wa_ref[...], preferred_element_type=F32) + ba_ref[...]) * (1.0 / GLA_TAU)
        la_hi, la_lo = _split_bf16(la)
        b_all = (jnp.dot(tri, la_hi, preferred_element_type=F32)
                 + jnp.dot(tri, la_lo, preferred_element_type=F32))
        b_last = b_all[last:last + 1, :]
        q = q_ref[bi, rows, :].astype(F32)
        k = k_ref[bi, rows, :].astype(F32)
        qd_all = (q * (jnp.exp(b_all) * (GLA_DK ** -0.5))).astype(BF16)
        kinv_all = (k * jnp.exp(-b_all)).astype(BF16)
        kend_all = k * jnp.exp(b_last - b_all)
        dec_all = jnp.broadcast_to(jnp.exp(b_last), (c_len, GLA_QK_W))
        return qd_all, kinv_all, kend_all, dec_all

    def chunk_rows(cc):
        c = (n_chunks - 1 - cc) if reverse else cc
        return slice(c * c_len, (c + 1) * c_len)

    pre_next = [decays(bi, chunk_rows(0)) for bi in seqs]
    for cc in range(n_chunks):
        rows = chunk_rows(cc)
        pre = pre_next
        if cc + 1 < n_chunks:
            pre_next = [decays(bi, chunk_rows(cc + 1)) for bi in seqs]
        for h in range(GLA_HEADS):
            ks = slice(h * GLA_DK, (h + 1) * GLA_DK)
            vs = slice(h * GLA_DV, (h + 1) * GLA_DV)
            v = [v_ref[bi, rows, vs] for bi in seqs]
            scores = [lax.dot_general(pre[bi][0][:, ks], pre[bi][1][:, ks], (((1,), (1,)), ((), ())),
                                      preferred_element_type=F32) for bi in seqs]
            state = [state_ref[bi, h] for bi in seqs]
            o = [jnp.dot(jnp.where(keep, scores[bi], 0.0).astype(BF16), v[bi], preferred_element_type=F32)
                 + jnp.dot(pre[bi][0][:, ks], state[bi].astype(BF16), preferred_element_type=F32) for bi in seqs]
            for bi in seqs:
                kend_t = pre[bi][2][:, ks].T.astype(BF16)
                dec_t = pre[bi][3][:, ks].T
                dec = jnp.concatenate([dec_t, dec_t], axis=1)
                state_ref[bi, h] = dec * state[bi] + jnp.dot(kend_t, v[bi], preferred_element_type=F32)
            for bi in seqs:
                ob = o[bi]
                if reverse:
                    ob = ob + of_ref[bi, rows, vs]
                    ms = jnp.mean(ob * ob, axis=-1, keepdims=True)
                    ob = ob * lax.rsqrt(ms + RMS_EPS) * ng_ref[:, vs]
                    g = gr_ref[bi, rows, vs].astype(F32)
                    ob = ob * (g * jax.nn.sigmoid(g))
                o_ref[bi, rows, vs] = ob.astype(o_ref.dtype)


GLA_SEQS_PER_STEP = 2


def _gla(proj, wa, ba, batch, seq, *, reverse, o_fwd=None, norm_g=None, ts=512):
    ts = min(ts, seq)
    nblk = seq // ts
    nb = GLA_SEQS_PER_STEP if batch % GLA_SEQS_PER_STEP == 0 else 1
    proj3 = proj.reshape(batch, seq, N_PROJ)

    def blk(i):
        return (nblk - 1 - i) if reverse else i

    def cols(width, col):
        return pl.BlockSpec((nb, ts, width), lambda b, i: (b, blk(i), col // width))

    in_specs = [cols(GLA_QK_W, COL_GQ), cols(GLA_QK_W, COL_GK), cols(GLA_V_W, COL_GV), cols(LANES, COL_SMALL),
                pl.BlockSpec((LANES, GLA_QK_W), lambda b, i: (0, 0)),
                pl.BlockSpec((1, GLA_QK_W), lambda b, i: (0, 0))]
    args = [proj3, proj3, proj3, proj3, wa, ba]
    if reverse:
        in_specs += [cols(GLA_V_W, 0), cols(GLA_V_W, COL_GR), pl.BlockSpec((1, GLA_V_W), lambda b, i: (0, 0))]
        args += [o_fwd.reshape(batch, seq, GLA_V_W), proj3, norm_g]
    out = pl.pallas_call(
        functools.partial(_gla_kernel, reverse=reverse, n_chunks=ts // GLA_CHUNK),
        out_shape=jax.ShapeDtypeStruct((batch, seq, GLA_V_W), BF16 if reverse else F32),
        grid=(batch // nb, nblk),
        in_specs=in_specs,
        out_specs=cols(GLA_V_W, 0),
        scratch_shapes=[pltpu.VMEM((nb, GLA_HEADS, GLA_DK, GLA_DV), F32)],
        compiler_params=_cparams(("parallel", "arbitrary")),
        name="gla_bwd" if reverse else "gla_fwd",
    )(*args)
    return out.reshape(batch * seq, GLA_V_W)


MLA_COL_CHUNK = 512


def _mla_prep_kernel(cq_ref, ckv_ref, sm_ref, sm2_ref, pos_ref, inv_ref, gq_ref, gkv_ref,
                     wq_ref, wqr_ref, wk_ref, wv_ref, q_ref, k_ref, v_ref):
    cq = cq_ref[...].astype(F32)
    msq = jnp.sum(cq * cq, axis=-1, keepdims=True) * (1.0 / MLA_Q_RANK)
    cqn = (cq * lax.rsqrt(msq + RMS_EPS) * gq_ref[...]).astype(BF16)
    ckv = ckv_ref[...].astype(F32)
    mskv = jnp.mean(ckv * ckv, axis=-1, keepdims=True)
    ckvn = (ckv * lax.rsqrt(mskv + RMS_EPS) * gkv_ref[...]).astype(BF16)

    ang = pos_ref[...].astype(F32) * inv_ref[...]
    cos = jnp.cos(ang)
    sin = jnp.sin(ang)
    lane = lax.broadcasted_iota(jnp.int32, ang.shape, 1)
    kr = jnp.where(lane >= KR_LANE, sm_ref[...].astype(F32), 0.0)
    kr = kr * cos + sm2_ref[...].astype(F32) * sin

    heads_per_chunk = MLA_COL_CHUNK // HEAD_PAD
    cos_t = jnp.concatenate([cos] * heads_per_chunk, axis=1)
    sin_t = jnp.concatenate([sin] * heads_per_chunk, axis=1)
    kr_t = jnp.concatenate([kr] * heads_per_chunk, axis=1)
    qscale = (MLA_QK ** -0.5) * float(np.log2(np.e))
    for c in range(MLA_HEADS * HEAD_PAD // MLA_COL_CHUNK):
        sl = slice(c * MLA_COL_CHUNK, (c + 1) * MLA_COL_CHUNK)
        q = jnp.dot(cqn, wq_ref[:, sl], preferred_element_type=F32)
        qr = jnp.dot(cqn, wqr_ref[:, sl], preferred_element_type=F32)
        q_ref[:, sl] = ((q * cos_t + qr * sin_t) * qscale).astype(BF16)
        k = jnp.dot(ckvn, wk_ref[:, sl], preferred_element_type=F32)
        k_ref[:, sl] = (k + kr_t).astype(BF16)
    v_ref[...] = lax.dot_general(wv_ref[...], ckvn, (((1,), (1,)), ((), ())),
                                 preferred_element_type=F32).astype(BF16)


def _mla_prep(proj, pos, inv_lane, gq, gkv, wq, wqr, wk, wv, ts=512):
    t = proj.shape[0]
    hp = MLA_HEADS * HEAD_PAD
    vw = MLA_HEADS * MLA_V
    const = lambda i: (0, 0)
    return pl.pallas_call(
        _mla_prep_kernel,
        out_shape=(jax.ShapeDtypeStruct((t, hp), BF16), jax.ShapeDtypeStruct((t, hp), BF16),
                   jax.ShapeDtypeStruct((vw, t), BF16)),
        grid=(t // ts,),
        in_specs=[
            pl.BlockSpec((ts, CQ_PAD), lambda i: (i, COL_CQ // CQ_PAD)),
            pl.BlockSpec((ts, LANES), lambda i: (i, COL_CKV // LANES)),
            pl.BlockSpec((ts, LANES), lambda i: (i, COL_SMALL // LANES)),
            pl.BlockSpec((ts, LANES), lambda i: (i, COL_SMALL2 // LANES)),
            pl.BlockSpec((ts, 1), lambda i: (i, 0)),
            pl.BlockSpec((1, LANES), const),
            pl.BlockSpec((1, CQ_PAD), const),
            pl.BlockSpec((1, MLA_KV_RANK), const),
            pl.BlockSpec((CQ_PAD, hp), const),
            pl.BlockSpec((CQ_PAD, hp), const),
            pl.BlockSpec((MLA_KV_RANK, hp), const),
            pl.BlockSpec((vw, MLA_KV_RANK), const),
        ],
        out_specs=(pl.BlockSpec((ts, hp), lambda i: (i, 0)), pl.BlockSpec((ts, hp), lambda i: (i, 0)),
                   pl.BlockSpec((vw, ts), lambda i: (0, i))),
        compiler_params=_cparams(("parallel",)),
        name="mla_prep",
    )(proj, proj, proj, proj, pos, inv_lane, gq, gkv, wq, wqr, wk, wv)


SUBLANES = 8


BOUND_SLACK = 1.01
SHIFT_LIMIT = 48.0


def _attn_kernel(q_ref, k_ref, vt_ref, o_ref, s_scr, bound_scr, safe_scr, *, tk):
    seq = k_ref.shape[0]
    tq = s_scr.shape[-1]
    nk = seq // tk
    nq = seq // tq
    nt = (((1,), (1,)), ((), ()))
    head = lambda hh: slice(hh * HEAD_PAD, (hh + 1) * HEAD_PAD)
    ones8 = jnp.ones((SUBLANES, HEAD_PAD), BF16)
    ones_sq = jnp.ones((HEAD_PAD, HEAD_PAD), BF16)

    def max_key_norm_sq(hh):
        k = k_ref[:, head(hh)].astype(F32)
        row_sums = jnp.dot((k * k).astype(BF16), ones_sq, preferred_element_type=F32)
        return jnp.max(row_sums, axis=0, keepdims=True)[:, 0:1]

    for hh in range(2):
        qf = q_ref[:, head(hh)].astype(F32)
        q_sq = lax.dot_general(ones8, (qf * qf).astype(BF16), nt, preferred_element_type=F32)
        b8 = jnp.sqrt(q_sq * max_key_norm_sq(hh)) * BOUND_SLACK
        for j in range(nq):
            bound_scr[hh, j] = b8[:, j * tq:(j + 1) * tq]
    for j in range(nq):
        worst = jnp.max(jnp.maximum(bound_scr[0, j], bound_scr[1, j]))
        safe_scr[j] = (worst < SHIFT_LIMIT).astype(jnp.int32)

    def scores(hh, q, c):
        return lax.dot_general(k_ref[c * tk:(c + 1) * tk, head(hh)], q, nt, preferred_element_type=F32)

    def weigh(hh, c, p, l8, acc):
        l8 = l8 + jnp.sum(p.reshape(tk // SUBLANES, SUBLANES, tq), axis=0)
        vt = vt_ref[hh * MLA_V:(hh + 1) * MLA_V, c * tk:(c + 1) * tk]
        return l8, acc + jnp.dot(vt, p.astype(BF16), preferred_element_type=F32)

    def q_tile(i, carry):
        rows = pl.ds(pl.multiple_of(i * tq, tq), tq)
        qs = [q_ref[rows, head(hh)] for hh in range(2)]
        bound = [bound_scr[hh, i, 0:1, :] for hh in range(2)]
        safe = safe_scr[i] != 0
        zeros = lambda n: jnp.zeros((n, tq), F32)

        @pl.when(safe)
        def _():
            l8, acc = [zeros(SUBLANES)] * 2, [zeros(MLA_V)] * 2
            st_next = [scores(hh, qs[hh], 0) for hh in range(2)]
            for c in range(nk):
                st = st_next
                if c + 1 < nk:
                    st_next = [scores(hh, qs[hh], c + 1) for hh in range(2)]
                for hh in range(2):
                    l8[hh], acc[hh] = weigh(hh, c, jnp.exp2(st[hh] - bound[hh]), l8[hh], acc[hh])
            out = [acc[hh] / jnp.sum(l8[hh], axis=0, keepdims=True) for hh in range(2)]
            o_ref[rows, :] = jnp.concatenate(out, axis=0).T.astype(BF16)

        @pl.when(jnp.logical_not(safe))
        def _():
            out = []
            for hh in range(2):
                m8 = jnp.full((SUBLANES, tq), NEG_BIG, F32)
                for c in range(nk):
                    st = scores(hh, qs[hh], c)
                    s_scr[c] = st
                    m8 = jnp.maximum(m8, jnp.max(st.reshape(tk // SUBLANES, SUBLANES, tq), axis=0))
                m = jnp.max(m8, axis=0, keepdims=True)
                l8, acc = zeros(SUBLANES), zeros(MLA_V)
                for c in range(nk):
                    l8, acc = weigh(hh, c, jnp.exp2(s_scr[c] - m), l8, acc)
                out.append(acc / jnp.sum(l8, axis=0, keepdims=True))
            o_ref[rows, :] = jnp.concatenate(out, axis=0).T.astype(BF16)

        return carry

    lax.fori_loop(0, nq, q_tile, 0)


def _attention(q2, k2, vt, batch, seq, tq=512, tk=512):
    tq = min(tq, seq)
    tk = min(tk, seq)
    t = batch * seq
    pairs = MLA_HEADS // 2
    return pl.pallas_call(
        functools.partial(_attn_kernel, tk=tk),
        scratch_shapes=[pltpu.VMEM((seq // tk, tk, tq), F32), pltpu.VMEM((2, seq // tq, SUBLANES, tq), F32),
                        pltpu.SMEM((seq // tq,), jnp.int32)],
        out_shape=jax.ShapeDtypeStruct((t, MLA_HEADS * MLA_V), BF16),
        grid=(batch, pairs),
        in_specs=[
            pl.BlockSpec((seq, 2 * HEAD_PAD), lambda b, p: (b, p)),
            pl.BlockSpec((seq, 2 * HEAD_PAD), lambda b, p: (b, p)),
            pl.BlockSpec((2 * MLA_V, seq), lambda b, p: (p, b)),
        ],
        out_specs=pl.BlockSpec((seq, 2 * MLA_V), lambda b, p: (b, p)),
        compiler_params=_cparams(("parallel", "parallel")),
        name="mla_attention",
    )(q2, k2, vt)


TOKEN_TILE_ROWS = D_MODEL // LANES


def _rows_to_token_tiles(ref, y):
    n = y.shape[0]
    for s in range(TOKEN_TILE_ROWS):
        ref[pl.ds(s, n, stride=TOKEN_TILE_ROWS), :] = y[:, s * LANES:(s + 1) * LANES]


def _token_tiles_to_rows(ref, n):
    return jnp.concatenate([ref[pl.ds(s, n, stride=TOKEN_TILE_ROWS), :] for s in range(TOKEN_TILE_ROWS)], axis=1)


def _merge_out_kernel(oa_ref, ob_ref, ga_ref, gb_ref, h_ref, w_ref, g_ref, b_ref, ot_ref):
    merged = (jax.nn.sigmoid(ga_ref[...].astype(F32)) * oa_ref[...].astype(F32)
              + jax.nn.sigmoid(gb_ref[...].astype(F32)) * ob_ref[...].astype(F32))
    mix = jnp.dot(merged.astype(BF16), w_ref[...], preferred_element_type=F32)
    y = _ln_rows(DEEPNORM_ALPHA * h_ref[...] + mix, g_ref[...], b_ref[...])
    _rows_to_token_tiles(ot_ref, y)


def _merge_out(o_gla, o_mla, proj, h, w_out, g, b, tm=512):
    t, d = h.shape
    row = lambda i: (i, 0)
    const = lambda i: (0, 0)
    return pl.pallas_call(
        _merge_out_kernel,
        out_shape=jax.ShapeDtypeStruct((t * TOKEN_TILE_ROWS, LANES), F32),
        grid=(t // tm,),
        in_specs=[
            pl.BlockSpec((tm, d), row),
            pl.BlockSpec((tm, d), row),
            pl.BlockSpec((tm, d), lambda i: (i, COL_GA // D_MODEL)),
            pl.BlockSpec((tm, d), lambda i: (i, COL_GB // D_MODEL)),
            pl.BlockSpec((tm, d), row),
            pl.BlockSpec((d, d), const),
            pl.BlockSpec((1, d), const),
            pl.BlockSpec((1, d), const),
        ],
        out_specs=pl.BlockSpec((tm * TOKEN_TILE_ROWS, LANES), row),
        compiler_params=_cparams(("parallel",)),
        name="merge_out_ln",
    )(o_gla, o_mla, proj, proj, h, w_out, g, b)


GRP_LANE = N_EXPERTS
META_W0, META_W1, META_E0, META_E1, META_R0, META_R1 = range(6)


def _router_kernel(x_ref, w_ref, b_ref, meta_ref, cnt_ref, carry_ref):
    @pl.when(pl.program_id(0) == 0)
    def _():
        carry_ref[...] = jnp.zeros_like(carry_ref)

    tm = x_ref.shape[0] // TOKEN_TILE_ROWS
    x_hi, x_lo = _split_bf16(_token_tiles_to_rows(x_ref, tm))
    w_hi, w_lo = _split_bf16(w_ref[...])
    logits = (jnp.dot(x_hi, w_hi, preferred_element_type=F32)
              + jnp.dot(x_lo, w_hi, preferred_element_type=F32)
              + jnp.dot(x_hi, w_lo, preferred_element_type=F32)) + b_ref[...]
    lane = lax.broadcasted_iota(jnp.int32, logits.shape, 1)
    is_grp = (lane >= GRP_LANE) & (lane < GRP_LANE + N_GROUPS)
    gl = jnp.where(is_grp, logits, NEG_BIG)
    gmax = jnp.max(gl, axis=1, keepdims=True)
    g_lane = jnp.min(jnp.where(gl == gmax, lane, 4 * LANES), axis=1, keepdims=True)
    g_w = 1.0 / jnp.sum(jnp.where(is_grp, jnp.exp(gl - gmax), 0.0), axis=1, keepdims=True)
    lo_lane = (g_lane - GRP_LANE) * EXPERTS_PER_GROUP
    in_grp = (lane >= lo_lane) & (lane < lo_lane + EXPERTS_PER_GROUP)
    el = jnp.where(in_grp, logits, NEG_BIG)
    v1 = jnp.max(el, axis=1, keepdims=True)
    i1 = jnp.min(jnp.where(el == v1, lane, 4 * LANES), axis=1, keepdims=True)
    el2 = jnp.where(lane == i1, NEG_BIG, el)
    v2 = jnp.max(el2, axis=1, keepdims=True)
    i2 = jnp.min(jnp.where(el2 == v2, lane, 4 * LANES), axis=1, keepdims=True)
    e2 = jnp.exp(v2 - v1)
    w1 = g_w / (1.0 + e2)
    w2 = g_w * e2 / (1.0 + e2)

    onehot = jnp.where(lane == i1, 1.0, jnp.where(lane == i2, 1.0, 0.0))
    r = lax.broadcasted_iota(jnp.int32, (tm, tm), 0)
    c = lax.broadcasted_iota(jnp.int32, (tm, tm), 1)
    earlier = (c < r).astype(BF16)
    base = carry_ref[...] + jnp.dot(earlier, onehot.astype(BF16), preferred_element_type=F32)
    rank1 = jnp.sum(jnp.where(lane == i1, base, 0.0), axis=1, keepdims=True)
    rank2 = jnp.sum(jnp.where(lane == i2, base, 0.0), axis=1, keepdims=True)
    meta = jnp.zeros(logits.shape, F32)
    for ln, val in ((META_W0, w1), (META_W1, w2), (META_E0, i1.astype(F32)), (META_E1, i2.astype(F32)),
                    (META_R0, rank1), (META_R1, rank2)):
        meta = jnp.where(lane == ln, val, meta)
    meta_ref[...] = meta
    carry_ref[...] += jnp.sum(onehot, axis=0, keepdims=True)
    cnt_ref[...] = carry_ref[...]


def _router(h_tiles, w_r, b_r, tm=512):
    t, d = h_tiles.shape[0] // TOKEN_TILE_ROWS, D_MODEL
    return pl.pallas_call(
        _router_kernel,
        out_shape=(jax.ShapeDtypeStruct((t, LANES), F32), jax.ShapeDtypeStruct((1, LANES), F32)),
        grid=(t // tm,),
        in_specs=[pl.BlockSpec((tm * TOKEN_TILE_ROWS, LANES), lambda i: (i, 0)),
                  pl.BlockSpec((d, LANES), lambda i: (0, 0)),
                  pl.BlockSpec((1, LANES), lambda i: (0, 0))],
        out_specs=(pl.BlockSpec((tm, LANES), lambda i: (i, 0)), pl.BlockSpec((1, LANES), lambda i: (0, 0))),
        scratch_shapes=[pltpu.VMEM((1, LANES), F32)],
        compiler_params=_cparams(("arbitrary",)),
        name="router",
    )(h_tiles, w_r, b_r)


EXPERT_ROW_TILE = 512
MOE_TOKENS = 256
TOP_K = 2
DMA_ISSUE_UNROLL = 8


def _tile(ref, token):
    return ref.at[pl.ds(pl.multiple_of(token * TOKEN_TILE_ROWS, TOKEN_TILE_ROWS), TOKEN_TILE_ROWS)]


def _slots_kernel(meta_ref, offs_ref, d_ref):
    meta = meta_ref[...]
    offs = offs_ref[...]
    lane = lax.broadcasted_iota(jnp.int32, meta.shape, 1)

    def slot(e_lane, r_lane):
        e = meta[:, e_lane:e_lane + 1].astype(jnp.int32)
        return jnp.sum(jnp.where(lane == e, offs, 0.0), axis=1, keepdims=True) + meta[:, r_lane:r_lane + 1]

    both = jnp.where(lane == 0, slot(META_E0, META_R0), jnp.where(lane == 1, slot(META_E1, META_R1), 0.0))
    d_ref[0] = both.T[:SUBLANES, :].astype(jnp.int32)


def _slots(meta, offs, tt):
    t = meta.shape[0]
    return pl.pallas_call(
        _slots_kernel,
        out_shape=jax.ShapeDtypeStruct((t // tt, SUBLANES, tt), jnp.int32),
        grid=(t // tt,),
        in_specs=[pl.BlockSpec((tt, LANES), lambda i: (i, 0)), pl.BlockSpec((1, LANES), lambda i: (0, 0))],
        out_specs=pl.BlockSpec((1, SUBLANES, tt), lambda i: (i, 0, 0)),
        compiler_params=_cparams(("arbitrary",)),
        name="moe_slots",
    )(meta, offs)


def _zero_tile_kernel(lt_ref, xs_ref):
    del lt_ref
    xs_ref[...] = jnp.zeros_like(xs_ref)


def _zero_last_tiles(last_tile, n_rows):
    blk = EXPERT_ROW_TILE * TOKEN_TILE_ROWS
    grid_spec = pltpu.PrefetchScalarGridSpec(
        num_scalar_prefetch=1, grid=(N_EXPERTS,), in_specs=[],
        out_specs=pl.BlockSpec((blk, LANES), lambda e, lt: (lt[e], 0)))
    return pl.pallas_call(
        _zero_tile_kernel,
        out_shape=jax.ShapeDtypeStruct((n_rows * TOKEN_TILE_ROWS, LANES), F32),
        grid_spec=grid_spec,
        compiler_params=_cparams(("arbitrary",)),
        name="moe_zero_tiles",
    )(last_tile)


def _dispatch_kernel(d_ref, ht_ref, xs_init_hbm, xs_hbm, sem):
    del xs_init_hbm
    tt = d_ref.shape[-1]

    def issue(j, carry):
        src = _tile(ht_ref, j)
        for k in range(TOP_K):
            pltpu.make_async_copy(src, _tile(xs_hbm, d_ref[0, k, j]), sem).start(priority=k)
        return carry

    lax.fori_loop(0, tt, issue, 0, unroll=DMA_ISSUE_UNROLL)
    for k in range(TOP_K):
        pltpu.make_async_copy(ht_ref, xs_hbm.at[pl.ds(0, tt * TOKEN_TILE_ROWS)], sem).wait()


def _dispatch(h_tiles, slots, xs_init):
    tt = slots.shape[-1]
    t = h_tiles.shape[0] // TOKEN_TILE_ROWS
    return pl.pallas_call(
        _dispatch_kernel,
        out_shape=jax.ShapeDtypeStruct(xs_init.shape, F32),
        grid=(t // tt,),
        in_specs=[pl.BlockSpec((1, SUBLANES, tt), lambda i: (i, 0, 0), memory_space=pltpu.SMEM),
                  pl.BlockSpec((tt * TOKEN_TILE_ROWS, LANES), lambda i: (i, 0)),
                  pl.BlockSpec(memory_space=pl.ANY)],
        out_specs=pl.BlockSpec(memory_space=pl.ANY),
        scratch_shapes=[pltpu.SemaphoreType.DMA(())],
        input_output_aliases={2: 0},
        compiler_params=_cparams(("arbitrary",)),
        name="moe_dispatch",
    )(slots, h_tiles, xs_init)


def _expert_kernel(te_ref, nu_ref, xs_ref, wgu_ref, wd_ref, ys_ref):
    del te_ref
    i = pl.program_id(0)
    tm = EXPERT_ROW_TILE

    @pl.when(i < nu_ref[0])
    def _():
        x = _token_tiles_to_rows(xs_ref, tm).astype(BF16)
        gu = jnp.dot(x, wgu_ref[0], preferred_element_type=F32)
        gate = gu[:, :D_EXPERT]
        hid = gate * jax.nn.sigmoid(gate) * gu[:, D_EXPERT:]
        _rows_to_token_tiles(ys_ref, jnp.dot(hid.astype(BF16), wd_ref[0], preferred_element_type=F32))

    @pl.when(i >= nu_ref[0])
    def _():
        ys_ref[...] = jnp.zeros_like(ys_ref)


def _experts(xs, tile_expert, n_used, wgu, wd):
    d = D_MODEL
    tm = EXPERT_ROW_TILE
    blk = pl.BlockSpec((tm * TOKEN_TILE_ROWS, LANES), lambda i, te, nu: (i, 0))
    blk_in = pl.BlockSpec((tm * TOKEN_TILE_ROWS, LANES), lambda i, te, nu: (jnp.minimum(i, nu[0] - 1), 0))
    grid_spec = pltpu.PrefetchScalarGridSpec(
        num_scalar_prefetch=2,
        grid=(xs.shape[0] // (tm * TOKEN_TILE_ROWS),),
        in_specs=[blk_in,
                  pl.BlockSpec((1, d, 2 * D_EXPERT), lambda i, te, nu: (te[i], 0, 0)),
                  pl.BlockSpec((1, D_EXPERT, d), lambda i, te, nu: (te[i], 0, 0))],
        out_specs=blk,
    )
    return pl.pallas_call(
        _expert_kernel,
        out_shape=jax.ShapeDtypeStruct(xs.shape, F32),
        grid_spec=grid_spec,
        compiler_params=_cparams(("arbitrary",)),
        name="moe_experts",
    )(tile_expert, n_used, xs, wgu, wd)


def _combine_kernel(dc_ref, dn_ref, ys_hbm, meta_ref, ht_ref, g_ref, b_ref, o_ref, obf_ref, buf, sem):
    i = pl.program_id(0)
    n = pl.num_programs(0)
    tt = dc_ref.shape[-1]
    slot = lax.rem(i, 2)

    def gather(d_ref, s):
        def issue(j, carry):
            for k in range(TOP_K):
                pltpu.make_async_copy(_tile(ys_hbm, d_ref[0, k, j]), _tile(buf.at[s, k], j),
                                      sem.at[s]).start(priority=k)
            return carry
        lax.fori_loop(0, tt, issue, 0, unroll=DMA_ISSUE_UNROLL)

    @pl.when(i == 0)
    def _():
        gather(dc_ref, 0)

    @pl.when(i + 1 < n)
    def _():
        gather(dn_ref, 1 - slot)

    for k in range(TOP_K):
        pltpu.make_async_copy(ys_hbm.at[pl.ds(0, tt * TOKEN_TILE_ROWS)], buf.at[slot, k], sem.at[slot]).wait()
    meta = meta_ref[...]
    ffn = (meta[:, META_W0:META_W0 + 1] * _token_tiles_to_rows(buf.at[slot, 0], tt)
           + meta[:, META_W1:META_W1 + 1] * _token_tiles_to_rows(buf.at[slot, 1], tt))
    y = _ln_rows(DEEPNORM_ALPHA * _token_tiles_to_rows(ht_ref, tt) + ffn, g_ref[...], b_ref[...])
    o_ref[...] = y
    obf_ref[...] = y.astype(BF16)


def _combine(ys, slots, meta, h_tiles, g, b):
    t, d = h_tiles.shape[0] // TOKEN_TILE_ROWS, D_MODEL
    tt = slots.shape[-1]
    n = t // tt
    cur = pl.BlockSpec((1, SUBLANES, tt), lambda i: (i, 0, 0), memory_space=pltpu.SMEM)
    nxt = pl.BlockSpec((1, SUBLANES, tt), lambda i: (jnp.minimum(i + 1, n - 1), 0, 0), memory_space=pltpu.SMEM)
    row = lambda i: (i, 0)
    const = lambda i: (0, 0)
    return pl.pallas_call(
        _combine_kernel,
        out_shape=(jax.ShapeDtypeStruct((t, d), F32), jax.ShapeDtypeStruct((t, d), BF16)),
        grid=(n,),
        in_specs=[cur, nxt, pl.BlockSpec(memory_space=pl.ANY),
                  pl.BlockSpec((tt, LANES), row), pl.BlockSpec((tt * TOKEN_TILE_ROWS, LANES), row),
                  pl.BlockSpec((1, d), const), pl.BlockSpec((1, d), const)],
        out_specs=(pl.BlockSpec((tt, d), row), pl.BlockSpec((tt, d), row)),
        scratch_shapes=[pltpu.VMEM((2, TOP_K, tt * TOKEN_TILE_ROWS, LANES), F32), pltpu.SemaphoreType.DMA((2,))],
        compiler_params=_cparams(("arbitrary",)),
        name="moe_combine_ln",
    )(slots, slots, ys, meta, h_tiles, g, b)


def _segment_tables(counts, t):
    tm = EXPERT_ROW_TILE
    n_rows = TOP_K * t + N_EXPERTS * tm
    cnt = counts[0, :N_EXPERTS].astype(jnp.int32)
    tiles = (cnt + tm - 1) // tm
    ends = jnp.cumsum(tiles)
    offs = jnp.zeros((1, LANES), F32).at[0, :N_EXPERTS].set(((ends - tiles) * tm).astype(F32))
    tile_ids = jnp.arange(n_rows // tm, dtype=jnp.int32)
    tile_expert = jnp.minimum(jnp.sum((tile_ids[:, None] >= ends[None, :]).astype(jnp.int32), axis=1),
                              N_EXPERTS - 1)
    last_tile = jnp.maximum(ends - 1, 0).astype(jnp.int32)
    return n_rows, offs, tile_expert, ends[-1:].astype(jnp.int32), last_tile


def _sparse_moe(h_tiles, meta, counts, wgu, wd, g, b):
    t = h_tiles.shape[0] // TOKEN_TILE_ROWS
    n_rows, offs, tile_expert, n_used, last_tile = _segment_tables(counts, t)
    slots = _slots(meta, offs, min(MOE_TOKENS, t))
    xs = _dispatch(h_tiles, slots, _zero_last_tiles(last_tile, n_rows))
    ys = _experts(xs, tile_expert, n_used, wgu, wd)
    return _combine(ys, slots, meta, h_tiles, g, b)


def _rotate_half_cols(w):
    half = w.shape[-1] // 2
    return jnp.concatenate([-w[..., half:], w[..., :half]], axis=-1)


def _pack_input_proj(w_in, b_in):
    d = w_in.shape[0]
    offs = np.cumsum((0, GLA_QK_W, GLA_QK_W, GLA_V_W, GLA_V_W, GLA_GATE_RANK, GLA_GATE_RANK,
                      MLA_Q_RANK, MLA_KV_RANK, MLA_ROPE, D_MODEL, D_MODEL))

    def seg(i):
        return w_in[:, offs[i]:offs[i + 1]], b_in[offs[i]:offs[i + 1]]

    w = jnp.zeros((d, N_PROJ), F32)
    b = jnp.zeros((N_PROJ,), F32)

    def put(w, b, col, ws, bs):
        return w.at[:, col:col + ws.shape[1]].set(ws), b.at[col:col + ws.shape[1]].set(bs)

    for i, col in ((0, COL_GQ), (1, COL_GK), (2, COL_GV), (3, COL_GR), (9, COL_GA), (10, COL_GB),
                   (6, COL_CQ), (7, COL_CKV), (4, COL_SMALL), (5, COL_SMALL + GLA_GATE_RANK)):
        w, b = put(w, b, col, *seg(i))
    wkr, bkr = seg(8)
    w, b = put(w, b, COL_SMALL + KR_LANE, wkr, bkr)
    w, b = put(w, b, COL_SMALL2 + KR_LANE, _rotate_half_cols(wkr), _rotate_half_cols(bkr))
    return w.astype(BF16), b.reshape(1, N_PROJ)


def _pack_decay(wa2, ba, lane0):
    w = jnp.zeros((LANES, GLA_QK_W), F32).at[lane0:lane0 + GLA_GATE_RANK].set(wa2)
    return w.astype(BF16), ba.reshape(1, GLA_QK_W)


def _pack_mla(w_uq, w_ukv, q_norm_g, kv_norm_g):
    wq = w_uq.reshape(MLA_Q_RANK, MLA_HEADS, MLA_QK)
    rope = wq[..., MLA_NOPE:]
    zq = jnp.zeros((MLA_Q_RANK, MLA_HEADS, HEAD_PAD - MLA_QK), F32)
    wq_p = jnp.concatenate([wq, zq], axis=-1)
    wqr_p = jnp.concatenate([jnp.zeros_like(wq[..., :MLA_NOPE]), _rotate_half_cols(rope), zq], axis=-1)

    def pad_rows(w):
        w = w.reshape(MLA_Q_RANK, MLA_HEADS * HEAD_PAD)
        return jnp.pad(w, ((0, CQ_PAD - MLA_Q_RANK), (0, 0))).astype(BF16)

    wkv = w_ukv.reshape(MLA_KV_RANK, MLA_HEADS, MLA_NOPE + MLA_V)
    wk_p = jnp.concatenate([wkv[..., :MLA_NOPE],
                            jnp.zeros((MLA_KV_RANK, MLA_HEADS, HEAD_PAD - MLA_NOPE), F32)], axis=-1)
    wk_p = wk_p.reshape(MLA_KV_RANK, MLA_HEADS * HEAD_PAD).astype(BF16)
    wv_p = wkv[..., MLA_NOPE:].reshape(MLA_KV_RANK, MLA_HEADS * MLA_V).T.astype(BF16)
    gq = jnp.pad(q_norm_g, (0, CQ_PAD - MLA_Q_RANK)).reshape(1, CQ_PAD)
    return pad_rows(wq_p), pad_rows(wqr_p), wk_p, wv_p, gq, kv_norm_g.reshape(1, MLA_KV_RANK)


def _rope_lane_table():
    inv = ROPE_BASE ** (-jnp.arange(0, MLA_ROPE, 2, dtype=F32) / MLA_ROPE)
    half = MLA_ROPE // 2
    tab = jnp.zeros((LANES,), F32)
    tab = tab.at[KR_LANE:KR_LANE + half].set(inv).at[KR_LANE + half:KR_LANE + MLA_ROPE].set(inv)
    return tab.reshape(1, LANES)


def _pack_router(w_grp, b_grp, w_exp, b_exp):
    d = w_grp.shape[0]
    w = jnp.zeros((d, LANES), F32).at[:, :N_EXPERTS].set(w_exp).at[:, GRP_LANE:GRP_LANE + N_GROUPS].set(w_grp)
    b = jnp.zeros((LANES,), F32).at[:N_EXPERTS].set(b_exp).at[GRP_LANE:GRP_LANE + N_GROUPS].set(b_grp)
    return w, b.reshape(1, LANES)


def kernel(x, positions, ln_emb_g, ln_emb_b, w_in, b_in, gla_wa2_f, gla_ba_f, gla_wa2_b, gla_ba_b, gla_norm_g, mla_q_norm_g, mla_w_uq, mla_kv_norm_g, mla_w_ukv, w_out, ln1_g, ln1_b, w_grp, b_grp, w_exp, b_exp, w_gate, w_up, w_down, ln2_g, ln2_b):
    batch, seq, d = x.shape
    t = batch * seq
    pos = positions.reshape(t, 1).astype(jnp.int32)
    inv_lane = _rope_lane_table()
    h, hb = _layer_norm(x.reshape(t, d), ln_emb_g, ln_emb_b)
    for l in range(DEPTH):
        w_p, b_p = _pack_input_proj(w_in[l], b_in[l])
        proj = _input_proj(hb, w_p, b_p)
        wa_f, ba_f = _pack_decay(gla_wa2_f[l], gla_ba_f[l], 0)
        wa_b, ba_b = _pack_decay(gla_wa2_b[l], gla_ba_b[l], GLA_GATE_RANK)
        o_f = _gla(proj, wa_f, ba_f, batch, seq, reverse=False)
        o_gla = _gla(proj, wa_b, ba_b, batch, seq, reverse=True, o_fwd=o_f,
                     norm_g=gla_norm_g[l].reshape(1, GLA_V_W))
        wq, wqr, wk, wv, gq, gkv = _pack_mla(mla_w_uq[l], mla_w_ukv[l], mla_q_norm_g[l], mla_kv_norm_g[l])
        q2, k2, v = _mla_prep(proj, pos, inv_lane, gq, gkv, wq, wqr, wk, wv)
        o_mla = _attention(q2, k2, v, batch, seq)
        h_tiles = _merge_out(o_gla, o_mla, proj, h, w_out[l].astype(BF16),
                           ln1_g[l].reshape(1, d), ln1_b[l].reshape(1, d))
        w_r, b_r = _pack_router(w_grp[l], b_grp[l], w_exp[l], b_exp[l])
        meta, counts = _router(h_tiles, w_r, b_r)
        wgu = jnp.concatenate([w_gate[l], w_up[l]], axis=-1).astype(BF16)
        h, hb = _sparse_moe(h_tiles, meta, counts, wgu, w_down[l].astype(BF16),
                            ln2_g[l].reshape(1, d), ln2_b[l].reshape(1, d))
    return h.reshape(batch, seq, d)
```

```python
import functools

import numpy as np
import jax
import jax.numpy as jnp
from jax import lax
from jax.experimental import pallas as pl
from jax.experimental.pallas import tpu as pltpu

F32 = jnp.float32
BF16 = jnp.bfloat16

D_MODEL = 1024
DEPTH = 2
GLA_HEADS = 4
GLA_DK = 128
GLA_DV = 256
GLA_GATE_RANK = 16
GLA_TAU = 16.0
MLA_HEADS = 16
MLA_NOPE = 64
MLA_ROPE = 32
MLA_V = 64
MLA_QK = MLA_NOPE + MLA_ROPE
MLA_Q_RANK = 384
MLA_KV_RANK = 128
ROPE_BASE = 10000.0
N_GROUPS = 8
EXPERTS_PER_GROUP = 4
N_EXPERTS = 32
D_EXPERT = 256
GLA_QK_W = GLA_HEADS * GLA_DK
GLA_V_W = GLA_HEADS * GLA_DV
DEEPNORM_ALPHA = (2.0 * DEPTH) ** 0.25
LN_EPS = 1e-5
RMS_EPS = 1e-6

LANES = 128
VMEM_LIMIT_BYTES = 56 * 1024 * 1024

COL_GQ = 0
COL_GK = 512
COL_GV = 1024
COL_GR = 2048
COL_GA = 3072
COL_GB = 4096
COL_CQ = 5120
CQ_PAD = 512
COL_SMALL = 5632
COL_SMALL2 = 5760
COL_CKV = 5888
N_PROJ = 6144
HEAD_PAD = 128
KR_LANE = 64

GLA_CHUNK = 128
NEG_BIG = -1e30


def _cparams(sem):
    return pltpu.CompilerParams(dimension_semantics=sem, vmem_limit_bytes=VMEM_LIMIT_BYTES)


def _ln_rows(x, g, b):
    mu = jnp.mean(x, axis=-1, keepdims=True)
    xc = x - mu
    var = jnp.mean(xc * xc, axis=-1, keepdims=True)
    return xc * lax.rsqrt(var + LN_EPS) * g + b


def _ln_kernel(x_ref, g_ref, b_ref, o_ref, ob_ref):
    y = _ln_rows(x_ref[...], g_ref[...], b_ref[...])
    o_ref[...] = y
    ob_ref[...] = y.astype(BF16)


def _layer_norm(x, g, b, tm=512):
    t, d = x.shape
    return pl.pallas_call(
        _ln_kernel,
        out_shape=(jax.ShapeDtypeStruct((t, d), F32), jax.ShapeDtypeStruct((t, d), BF16)),
        grid=(t // tm,),
        in_specs=[pl.BlockSpec((tm, d), lambda i: (i, 0)),
                  pl.BlockSpec((1, d), lambda i: (0, 0)),
                  pl.BlockSpec((1, d), lambda i: (0, 0))],
        out_specs=(pl.BlockSpec((tm, d), lambda i: (i, 0)), pl.BlockSpec((tm, d), lambda i: (i, 0))),
        compiler_params=_cparams(("parallel",)),
        name="ln_embed",
    )(x, g.reshape(1, d), b.reshape(1, d))


PROJ_COL_CHUNK = 512


def _proj_kernel(x_ref, w_ref, b_ref, o_ref):
    x = x_ref[...]
    for c in range(N_PROJ // PROJ_COL_CHUNK):
        sl = slice(c * PROJ_COL_CHUNK, (c + 1) * PROJ_COL_CHUNK)
        acc = jnp.dot(x, w_ref[:, sl], preferred_element_type=F32) + b_ref[:, sl]
        o_ref[:, sl] = acc.astype(o_ref.dtype)


def _input_proj(hb, w, b, tm=512):
    t, d = hb.shape
    return pl.pallas_call(
        _proj_kernel,
        out_shape=jax.ShapeDtypeStruct((t, N_PROJ), BF16),
        grid=(t // tm,),
        in_specs=[pl.BlockSpec((tm, d), lambda i: (i, 0)),
                  pl.BlockSpec((d, N_PROJ), lambda i: (0, 0)),
                  pl.BlockSpec((1, N_PROJ), lambda i: (0, 0))],
        out_specs=pl.BlockSpec((tm, N_PROJ), lambda i: (i, 0)),
        compiler_params=_cparams(("parallel",)),
        name="input_proj",
    )(hb, w, b)


def _log_sigmoid(x):
    return jnp.minimum(x, 0.0) - jnp.log(1.0 + jnp.exp(-jnp.abs(x)))


def _split_bf16(x):
    hi = x.astype(BF16)
    lo = (x - hi.astype(F32)).astype(BF16)
    return hi, lo


def _gla_kernel(*refs, reverse, n_chunks):
    if reverse:
        (q_ref, k_ref, v_ref, z_ref, wa_ref, ba_ref, of_ref, gr_ref, ng_ref, o_ref, state_ref) = refs
    else:
        (q_ref, k_ref, v_ref, z_ref, wa_ref, ba_ref, o_ref, state_ref) = refs
    c_len = GLA_CHUNK

    @pl.when(pl.program_id(1) == 0)
    def _():
        state_ref[...] = jnp.zeros_like(state_ref)

    row = lax.broadcasted_iota(jnp.int32, (c_len, c_len), 0)
    col = lax.broadcasted_iota(jnp.int32, (c_len, c_len), 1)
    if reverse:
        tri = (col >= row).astype(BF16)
        keep = col > row
        last = 0
    else:
        tri = (col <= row).astype(BF16)
        keep = col <= row
        last = c_len - 1

    seqs = range(q_ref.shape[0])

    def decays(bi, rows):
        z = z_ref[bi, rows, :]
        la = _log_sigmoid(jnp.dot(z, wa_ref[...], preferred_element_type=F32) + ba_ref[...]) * (1.0 / GLA_TAU)
        la_hi, la_lo = _split_bf16(la)
        b_all = (jnp.dot(tri, la_hi, preferred_element_type=F32)
                 + jnp.dot(tri, la_lo, preferred_element_type=F32))
        b_last = b_all[last:last + 1, :]
        q = q_ref[bi, rows, :].astype(F32)
        k = k_ref[bi, rows, :].astype(F32)
        qd_all = (q * (jnp.exp(b_all) * (GLA_DK ** -0.5))).astype(BF16)
        kinv_all = (k * jnp.exp(-b_all)).astype(BF16)
        kend_all = k * jnp.exp(b_last - b_all)
        dec_all = jnp.broadcast_to(jnp.exp(b_last), (c_len, GLA_QK_W))
        return qd_all, kinv_all, kend_all, dec_all

    def chunk_rows(cc):
        c = (n_chunks - 1 - cc) if reverse else cc
        return slice(c * c_len, (c + 1) * c_len)

    pre_next = [decays(bi, chunk_rows(0)) for bi in seqs]
    for cc in range(n_chunks):
        rows = chunk_rows(cc)
        pre = pre_next
        if cc + 1 < n_chunks:
            pre_next = [decays(bi, chunk_rows(cc + 1)) for bi in seqs]
        for h in range(GLA_HEADS):
            ks = slice(h * GLA_DK, (h + 1) * GLA_DK)
            vs = slice(h * GLA_DV, (h + 1) * GLA_DV)
            v = [v_ref[bi, rows, vs] for bi in seqs]
            scores = [lax.dot_general(pre[bi][0][:, ks], pre[bi][1][:, ks], (((1,), (1,)), ((), ())),
                                      preferred_element_type=F32) for bi in seqs]
            state = [state_ref[bi, h] for bi in seqs]
            o = [jnp.dot(jnp.where(keep, scores[bi], 0.0).astype(BF16), v[bi], preferred_element_type=F32)
                 + jnp.dot(pre[bi][0][:, ks], state[bi].astype(BF16), preferred_element_type=F32) for bi in seqs]
            for bi in seqs:
                kend_t = pre[bi][2][:, ks].T.astype(BF16)
                dec_t = pre[bi][3][:, ks].T
                dec = jnp.concatenate([dec_t, dec_t], axis=1)
                state_ref[bi, h] = dec * state[bi] + jnp.dot(kend_t, v[bi], preferred_element_type=F32)
            for bi in seqs:
                ob = o[bi]
                if reverse:
                    ob = ob + of_ref[bi, rows, vs]
                    ms = jnp.mean(ob * ob, axis=-1, keepdims=True)
                    ob = ob * lax.rsqrt(ms + RMS_EPS) * ng_ref[:, vs]
                    g = gr_ref[bi, rows, vs].astype(F32)
                    ob = ob * (g * jax.nn.sigmoid(g))
                o_ref[bi, rows, vs] = ob.astype(o_ref.dtype)


GLA_SEQS_PER_STEP = 2


def _gla(proj, wa, ba, batch, seq, *, reverse, o_fwd=None, norm_g=None, ts=512):
    ts = min(ts, seq)
    nblk = seq // ts
    nb = GLA_SEQS_PER_STEP if batch % GLA_SEQS_PER_STEP == 0 else 1
    proj3 = proj.reshape(batch, seq, N_PROJ)

    def blk(i):
        return (nblk - 1 - i) if reverse else i

    def cols(width, col):
        return pl.BlockSpec((nb, ts, width), lambda b, i: (b, blk(i), col // width))

    in_specs = [cols(GLA_QK_W, COL_GQ), cols(GLA_QK_W, COL_GK), cols(GLA_V_W, COL_GV), cols(LANES, COL_SMALL),
                pl.BlockSpec((LANES, GLA_QK_W), lambda b, i: (0, 0)),
                pl.BlockSpec((1, GLA_QK_W), lambda b, i: (0, 0))]
    args = [proj3, proj3, proj3, proj3, wa, ba]
    if reverse:
        in_specs += [cols(GLA_V_W, 0), cols(GLA_V_W, COL_GR), pl.BlockSpec((1, GLA_V_W), lambda b, i: (0, 0))]
        args += [o_fwd.reshape(batch, seq, GLA_V_W), proj3, norm_g]
    out = pl.pallas_call(
        functools.partial(_gla_kernel, reverse=reverse, n_chunks=ts // GLA_CHUNK),
        out_shape=jax.ShapeDtypeStruct((batch, seq, GLA_V_W), BF16 if reverse else F32),
        grid=(batch // nb, nblk),
        in_specs=in_specs,
        out_specs=cols(GLA_V_W, 0),
        scratch_shapes=[pltpu.VMEM((nb, GLA_HEADS, GLA_DK, GLA_DV), F32)],
        compiler_params=_cparams(("parallel", "arbitrary")),
        name="gla_bwd" if reverse else "gla_fwd",
    )(*args)
    return out.reshape(batch * seq, GLA_V_W)


MLA_COL_CHUNK = 512


def _mla_prep_kernel(cq_ref, ckv_ref, sm_ref, sm2_ref, pos_ref, inv_ref, gq_ref, gkv_ref,
                     wq_ref, wqr_ref, wk_ref, wv_ref, q_ref, k_ref, v_ref):
    cq = cq_ref[...].astype(F32)
    msq = jnp.sum(cq * cq, axis=-1, keepdims=True) * (1.0 / MLA_Q_RANK)
    cqn = (cq * lax.rsqrt(msq + RMS_EPS) * gq_ref[...]).astype(BF16)
    ckv = ckv_ref[...].astype(F32)
    mskv = jnp.mean(ckv * ckv, axis=-1, keepdims=True)
    ckvn = (ckv * lax.rsqrt(mskv + RMS_EPS) * gkv_ref[...]).astype(BF16)

    ang = pos_ref[...].astype(F32) * inv_ref[...]
    cos = jnp.cos(ang)
    sin = jnp.sin(ang)
    lane = lax.broadcasted_iota(jnp.int32, ang.shape, 1)
    kr = jnp.where(lane >= KR_LANE, sm_ref[...].astype(F32), 0.0)
    kr = kr * cos + sm2_ref[...].astype(F32) * sin

    heads_per_chunk = MLA_COL_CHUNK // HEAD_PAD
    cos_t = jnp.concatenate([cos] * heads_per_chunk, axis=1)
    sin_t = jnp.concatenate([sin] * heads_per_chunk, axis=1)
    kr_t = jnp.concatenate([kr] * heads_per_chunk, axis=1)
    qscale = (MLA_QK ** -0.5) * float(np.log2(np.e))
    for c in range(MLA_HEADS * HEAD_PAD // MLA_COL_CHUNK):
        sl = slice(c * MLA_COL_CHUNK, (c + 1) * MLA_COL_CHUNK)
        q = jnp.dot(cqn, wq_ref[:, sl], preferred_element_type=F32)
        qr = jnp.dot(cqn, wqr_ref[:, sl], preferred_element_type=F32)
        q_ref[:, sl] = ((q * cos_t + qr * sin_t) * qscale).astype(BF16)
        k = jnp.dot(ckvn, wk_ref[:, sl], preferred_element_type=F32)
        k_ref[:, sl] = (k + kr_t).astype(BF16)
    v_ref[...] = lax.dot_general(wv_ref[...], ckvn, (((1,), (1,)), ((), ())),
                                 preferred_element_type=F32).astype(BF16)


def _mla_prep(proj, pos, inv_lane, gq, gkv, wq, wqr, wk, wv, ts=512):
    t = proj.shape[0]
    hp = MLA_HEADS * HEAD_PAD
    vw = MLA_HEADS * MLA_V
    const = lambda i: (0, 0)
    return pl.pallas_call(
        _mla_prep_kernel,
        out_shape=(jax.ShapeDtypeStruct((t, hp), BF16), jax.ShapeDtypeStruct((t, hp), BF16),
                   jax.ShapeDtypeStruct((vw, t), BF16)),
        grid=(t // ts,),
        in_specs=[
            pl.BlockSpec((ts, CQ_PAD), lambda i: (i, COL_CQ // CQ_PAD)),
            pl.BlockSpec((ts, LANES), lambda i: (i, COL_CKV // LANES)),
            pl.BlockSpec((ts, LANES), lambda i: (i, COL_SMALL // LANES)),
            pl.BlockSpec((ts, LANES), lambda i: (i, COL_SMALL2 // LANES)),
            pl.BlockSpec((ts, 1), lambda i: (i, 0)),
            pl.BlockSpec((1, LANES), const),
            pl.BlockSpec((1, CQ_PAD), const),
            pl.BlockSpec((1, MLA_KV_RANK), const),
            pl.BlockSpec((CQ_PAD, hp), const),
            pl.BlockSpec((CQ_PAD, hp), const),
            pl.BlockSpec((MLA_KV_RANK, hp), const),
            pl.BlockSpec((vw, MLA_KV_RANK), const),
        ],
        out_specs=(pl.BlockSpec((ts, hp), lambda i: (i, 0)), pl.BlockSpec((ts, hp), lambda i: (i, 0)),
                   pl.BlockSpec((vw, ts), lambda i: (0, i))),
        compiler_params=_cparams(("parallel",)),
        name="mla_prep",
    )(proj, proj, proj, proj, pos, inv_lane, gq, gkv, wq, wqr, wk, wv)


SUBLANES = 8


BOUND_SLACK = 1.01
SHIFT_LIMIT = 48.0


def _attn_kernel(q_ref, k_ref, vt_ref, o_ref, s_scr, bound_scr, safe_scr, *, tk):
    seq = k_ref.shape[0]
    tq = s_scr.shape[-1]
    nk = seq // tk
    nq = seq // tq
    nt = (((1,), (1,)), ((), ()))
    head = lambda hh: slice(hh * HEAD_PAD, (hh + 1) * HEAD_PAD)
    ones8 = jnp.ones((SUBLANES, HEAD_PAD), BF16)
    ones_sq = jnp.ones((HEAD_PAD, HEAD_PAD), BF16)

    def max_key_norm_sq(hh):
        k = k_ref[:, head(hh)].astype(F32)
        row_sums = jnp.dot((k * k).astype(BF16), ones_sq, preferred_element_type=F32)
        return jnp.max(row_sums, axis=0, keepdims=True)[:, 0:1]

    for hh in range(2):
        qf = q_ref[:, head(hh)].astype(F32)
        q_sq = lax.dot_general(ones8, (qf * qf).astype(BF16), nt, preferred_element_type=F32)
        b8 = jnp.sqrt(q_sq * max_key_norm_sq(hh)) * BOUND_SLACK
        for j in range(nq):
            bound_scr[hh, j] = b8[:, j * tq:(j + 1) * tq]
    for j in range(nq):
        worst = jnp.max(jnp.maximum(bound_scr[0, j], bound_scr[1, j]))
        safe_scr[j] = (worst < SHIFT_LIMIT).astype(jnp.int32)

    def scores(hh, q, c):
        return lax.dot_general(k_ref[c * tk:(c + 1) * tk, head(hh)], q, nt, preferred_element_type=F32)

    def weigh(hh, c, p, l8, acc):
        l8 = l8 + jnp.sum(p.reshape(tk // SUBLANES, SUBLANES, tq), axis=0)
        vt = vt_ref[hh * MLA_V:(hh + 1) * MLA_V, c * tk:(c + 1) * tk]
        return l8, acc + jnp.dot(vt, p.astype(BF16), preferred_element_type=F32)

    def q_tile(i, carry):
        rows = pl.ds(pl.multiple_of(i * tq, tq), tq)
        qs = [q_ref[rows, head(hh)] for hh in range(2)]
        bound = [bound_scr[hh, i, 0:1, :] for hh in range(2)]
        safe = safe_scr[i] != 0
        zeros = lambda n: jnp.zeros((n, tq), F32)

        @pl.when(safe)
        def _():
            l8, acc = [zeros(SUBLANES)] * 2, [zeros(MLA_V)] * 2
            st_next = [scores(hh, qs[hh], 0) for hh in range(2)]
            for c in range(nk):
                st = st_next
                if c + 1 < nk:
                    st_next = [scores(hh, qs[hh], c + 1) for hh in range(2)]
                for hh in range(2):
                    l8[hh], acc[hh] = weigh(hh, c, jnp.exp2(st[hh] - bound[hh]), l8[hh], acc[hh])
            out = [acc[hh] / jnp.sum(l8[hh], axis=0, keepdims=True) for hh in range(2)]
            o_ref[rows, :] = jnp.concatenate(out, axis=0).T.astype(BF16)

        @pl.when(jnp.logical_not(safe))
        def _():
            out = []
            for hh in range(2):
                m8 = jnp.full((SUBLANES, tq), NEG_BIG, F32)
                for c in range(nk):
                    st = scores(hh, qs[hh], c)
                    s_scr[c] = st
                    m8 = jnp.maximum(m8, jnp.max(st.reshape(tk // SUBLANES, SUBLANES, tq), axis=0))
                m = jnp.max(m8, axis=0, keepdims=True)
                l8, acc = zeros(SUBLANES), zeros(MLA_V)
                for c in range(nk):
                    l8, acc = weigh(hh, c, jnp.exp2(s_scr[c] - m), l8, acc)
                out.append(acc / jnp.sum(l8, axis=0, keepdims=True))
            o_ref[rows, :] = jnp.concatenate(out, axis=0).T.astype(BF16)

        return carry

    lax.fori_loop(0, nq, q_tile, 0)


def _attention(q2, k2, vt, batch, seq, tq=512, tk=512):
    tq = min(tq, seq)
    tk = min(tk, seq)
    t = batch * seq
    pairs = MLA_HEADS // 2
    return pl.pallas_call(
        functools.partial(_attn_kernel, tk=tk),
        scratch_shapes=[pltpu.VMEM((seq // tk, tk, tq), F32), pltpu.VMEM((2, seq // tq, SUBLANES, tq), F32),
                        pltpu.SMEM((seq // tq,), jnp.int32)],
        out_shape=jax.ShapeDtypeStruct((t, MLA_HEADS * MLA_V), BF16),
        grid=(batch, pairs),
        in_specs=[
            pl.BlockSpec((seq, 2 * HEAD_PAD), lambda b, p: (b, p)),
            pl.BlockSpec((seq, 2 * HEAD_PAD), lambda b, p: (b, p)),
            pl.BlockSpec((2 * MLA_V, seq), lambda b, p: (p, b)),
        ],
        out_specs=pl.BlockSpec((seq, 2 * MLA_V), lambda b, p: (b, p)),
        compiler_params=_cparams(("parallel", "parallel")),
        name="mla_attention",
    )(q2, k2, vt)


TOKEN_TILE_ROWS = D_MODEL // LANES


def _rows_to_token_tiles(ref, y):
    n = y.shape[0]
    for s in range(TOKEN_TILE_ROWS):
        ref[pl.ds(s, n, stride=TOKEN_TILE_ROWS), :] = y[:, s * LANES:(s + 1) * LANES]


def _token_tiles_to_rows(ref, n):
    return jnp.concatenate([ref[pl.ds(s, n, stride=TOKEN_TILE_ROWS), :] for s in range(TOKEN_TILE_ROWS)], axis=1)


def _merge_out_kernel(oa_ref, ob_ref, ga_ref, gb_ref, h_ref, w_ref, g_ref, b_ref, ot_ref):
    merged = (jax.nn.sigmoid(ga_ref[...].astype(F32)) * oa_ref[...].astype(F32)
              + jax.nn.sigmoid(gb_ref[...].astype(F32)) * ob_ref[...].astype(F32))
    mix = jnp.dot(merged.astype(BF16), w_ref[...], preferred_element_type=F32)
    y = _ln_rows(DEEPNORM_ALPHA * h_ref[...] + mix, g_ref[...], b_ref[...])
    _rows_to_token_tiles(ot_ref, y)


def _merge_out(o_gla, o_mla, proj, h, w_out, g, b, tm=512):
    t, d = h.shape
    row = lambda i: (i, 0)
    const = lambda i: (0, 0)
    return pl.pallas_call(
        _merge_out_kernel,
        out_shape=jax.ShapeDtypeStruct((t * TOKEN_TILE_ROWS, LANES), F32),
        grid=(t // tm,),
        in_specs=[
            pl.BlockSpec((tm, d), row),
            pl.BlockSpec((tm, d), row),
            pl.BlockSpec((tm, d), lambda i: (i, COL_GA // D_MODEL)),
            pl.BlockSpec((tm, d), lambda i: (i, COL_GB // D_MODEL)),
            pl.BlockSpec((tm, d), row),
            pl.BlockSpec((d, d), const),
            pl.BlockSpec((1, d), const),
            pl.BlockSpec((1, d), const),
        ],
        out_specs=pl.BlockSpec((tm * TOKEN_TILE_ROWS, LANES), row),
        compiler_params=_cparams(("parallel",)),
        name="merge_out_ln",
    )(o_gla, o_mla, proj, proj, h, w_out, g, b)


GRP_LANE = N_EXPERTS
META_W0, META_W1, META_E0, META_E1, META_R0, META_R1 = range(6)


def _router_kernel(x_ref, w_ref, b_ref, meta_ref, cnt_ref, carry_ref):
    @pl.when(pl.program_id(0) == 0)
    def _():
        carry_ref[...] = jnp.zeros_like(carry_ref)

    tm = x_ref.shape[0] // TOKEN_TILE_ROWS
    x_hi, x_lo = _split_bf16(_token_tiles_to_rows(x_ref, tm))
    w_hi, w_lo = _split_bf16(w_ref[...])
    logits = (jnp.dot(x_hi, w_hi, preferred_element_type=F32)
              + jnp.dot(x_lo, w_hi, preferred_element_type=F32)
              + jnp.dot(x_hi, w_lo, preferred_element_type=F32)) + b_ref[...]
    lane = lax.broadcasted_iota(jnp.int32, logits.shape, 1)
    is_grp = (lane >= GRP_LANE) & (lane < GRP_LANE + N_GROUPS)
    gl = jnp.where(is_grp, logits, NEG_BIG)
    gmax = jnp.max(gl, axis=1, keepdims=True)
    g_lane = jnp.min(jnp.where(gl == gmax, lane, 4 * LANES), axis=1, keepdims=True)
    g_w = 1.0 / jnp.sum(jnp.where(is_grp, jnp.exp(gl - gmax), 0.0), axis=1, keepdims=True)
    lo_lane = (g_lane - GRP_LANE) * EXPERTS_PER_GROUP
    in_grp = (lane >= lo_lane) & (lane < lo_lane + EXPERTS_PER_GROUP)
    el = jnp.where(in_grp, logits, NEG_BIG)
    v1 = jnp.max(el, axis=1, keepdims=True)
    i1 = jnp.min(jnp.where(el == v1, lane, 4 * LANES), axis=1, keepdims=True)
    el2 = jnp.where(lane == i1, NEG_BIG, el)
    v2 = jnp.max(el2, axis=1, keepdims=True)
    i2 = jnp.min(jnp.where(el2 == v2, lane, 4 * LANES), axis=1, keepdims=True)
    e2 = jnp.exp(v2 - v1)
    w1 = g_w / (1.0 + e2)
    w2 = g_w * e2 / (1.0 + e2)

    onehot = jnp.where(lane == i1, 1.0, jnp.where(lane == i2, 1.0, 0.0))
    r = lax.broadcasted_iota(jnp.int32, (tm, tm), 0)
    c = lax.broadcasted_iota(jnp.int32, (tm, tm), 1)
    earlier = (c < r).astype(BF16)
    base = carry_ref[...] + jnp.dot(earlier, onehot.astype(BF16), preferred_element_type=F32)
    rank1 = jnp.sum(jnp.where(lane == i1, base, 0.0), axis=1, keepdims=True)
    rank2 = jnp.sum(jnp.where(lane == i2, base, 0.0), axis=1, keepdims=True)
    meta = jnp.zeros(logits.shape, F32)
    for ln, val in ((META_W0, w1), (META_W1, w2), (META_E0, i1.astype(F32)), (META_E1, i2.astype(F32)),
                    (META_R0, rank1), (META_R1, rank2)):
        meta = jnp.where(lane == ln, val, meta)
    meta_ref[...] = meta
    carry_ref[...] += jnp.sum(onehot, axis=0, keepdims=True)
    cnt_ref[...] = carry_ref[...]


def _router(h_tiles, w_r, b_r, tm=512):
    t, d = h_tiles.shape[0] // TOKEN_TILE_ROWS, D_MODEL
    return pl.pallas_call(
        _router_kernel,
        out_shape=(jax.ShapeDtypeStruct((t, LANES), F32), jax.ShapeDtypeStruct((1, LANES), F32)),
        grid=(t // tm,),
        in_specs=[pl.BlockSpec((tm * TOKEN_TILE_ROWS, LANES), lambda i: (i, 0)),
                  pl.BlockSpec((d, LANES), lambda i: (0, 0)),
                  pl.BlockSpec((1, LANES), lambda i: (0, 0))],
        out_specs=(pl.BlockSpec((tm, LANES), lambda i: (i, 0)), pl.BlockSpec((1, LANES), lambda i: (0, 0))),
        scratch_shapes=[pltpu.VMEM((1, LANES), F32)],
        compiler_params=_cparams(("arbitrary",)),
        name="router",
    )(h_tiles, w_r, b_r)


EXPERT_ROW_TILE = 512
MOE_TOKENS = 256
TOP_K = 2
DMA_ISSUE_UNROLL = 16


def _tile_at(ref, first_row):
    return ref.at[pl.ds(pl.multiple_of(first_row, TOKEN_TILE_ROWS), TOKEN_TILE_ROWS)]


def _tile(ref, token):
    return _tile_at(ref, token * TOKEN_TILE_ROWS)


def _slots_kernel(meta_ref, offs_ref, d_ref):
    meta = meta_ref[...]
    offs = offs_ref[...]
    lane = lax.broadcasted_iota(jnp.int32, meta.shape, 1)

    def slot(e_lane, r_lane):
        e = meta[:, e_lane:e_lane + 1].astype(jnp.int32)
        return jnp.sum(jnp.where(lane == e, offs, 0.0), axis=1, keepdims=True) + meta[:, r_lane:r_lane + 1]

    both = jnp.where(lane == 0, slot(META_E0, META_R0), jnp.where(lane == 1, slot(META_E1, META_R1), 0.0))
    d_ref[0] = (both.T[:SUBLANES, :] * float(TOKEN_TILE_ROWS)).astype(jnp.int32)


def _slots(meta, offs, tt):
    t = meta.shape[0]
    return pl.pallas_call(
        _slots_kernel,
        out_shape=jax.ShapeDtypeStruct((t // tt, SUBLANES, tt), jnp.int32),
        grid=(t // tt,),
        in_specs=[pl.BlockSpec((tt, LANES), lambda i: (i, 0)), pl.BlockSpec((1, LANES), lambda i: (0, 0))],
        out_specs=pl.BlockSpec((1, SUBLANES, tt), lambda i: (i, 0, 0)),
        compiler_params=_cparams(("arbitrary",)),
        name="moe_slots",
    )(meta, offs)


def _zero_tile_kernel(lt_ref, xs_ref):
    del lt_ref
    xs_ref[...] = jnp.zeros_like(xs_ref)


def _zero_last_tiles(last_tile, n_rows):
    blk = EXPERT_ROW_TILE * TOKEN_TILE_ROWS
    grid_spec = pltpu.PrefetchScalarGridSpec(
        num_scalar_prefetch=1, grid=(N_EXPERTS,), in_specs=[],
        out_specs=pl.BlockSpec((blk, LANES), lambda e, lt: (lt[e], 0)))
    return pl.pallas_call(
        _zero_tile_kernel,
        out_shape=jax.ShapeDtypeStruct((n_rows * TOKEN_TILE_ROWS, LANES), F32),
        grid_spec=grid_spec,
        compiler_params=_cparams(("arbitrary",)),
        name="moe_zero_tiles",
    )(last_tile)


def _dispatch_kernel(d_ref, ht_hbm, xs_init_hbm, xs_hbm, sem):
    del xs_init_hbm
    i = pl.program_id(0)
    tt = d_ref.shape[-1]

    def issue(j, carry):
        src = _tile(ht_hbm, i * tt + j)
        for k in range(TOP_K):
            pltpu.make_async_copy(src, _tile_at(xs_hbm, d_ref[0, k, j]), sem).start(priority=k)
        return carry

    lax.fori_loop(0, tt, issue, 0, unroll=DMA_ISSUE_UNROLL)

    def wait_one_step():
        block = pl.ds(0, tt * TOKEN_TILE_ROWS)
        for k in range(TOP_K):
            pltpu.make_async_copy(ht_hbm.at[block], xs_hbm.at[block], sem).wait()

    @pl.when(i > 0)
    def _():
        wait_one_step()

    @pl.when(i == pl.num_programs(0) - 1)
    def _():
        wait_one_step()


def _dispatch(h_tiles, slots, xs_init):
    tt = slots.shape[-1]
    t = h_tiles.shape[0] // TOKEN_TILE_ROWS
    return pl.pallas_call(
        _dispatch_kernel,
        out_shape=jax.ShapeDtypeStruct(xs_init.shape, F32),
        grid=(t // tt,),
        in_specs=[pl.BlockSpec((1, SUBLANES, tt), lambda i: (i, 0, 0), memory_space=pltpu.SMEM),
                  pl.BlockSpec(memory_space=pl.ANY),
                  pl.BlockSpec(memory_space=pl.ANY)],
        out_specs=pl.BlockSpec(memory_space=pl.ANY),
        scratch_shapes=[pltpu.SemaphoreType.DMA(())],
        input_output_aliases={2: 0},
        compiler_params=_cparams(("arbitrary",)),
        name="moe_dispatch",
    )(slots, h_tiles, xs_init)


def _expert_kernel(te_ref, nu_ref, xs_ref, wgu_ref, wd_ref, ys_ref):
    del te_ref
    i = pl.program_id(0)
    tm = EXPERT_ROW_TILE

    @pl.when(i < nu_ref[0])
    def _():
        x = _token_tiles_to_rows(xs_ref, tm).astype(BF16)
        gu = jnp.dot(x, wgu_ref[0], preferred_element_type=F32)
        gate = gu[:, :D_EXPERT]
        hid = gate * jax.nn.sigmoid(gate) * gu[:, D_EXPERT:]
        _rows_to_token_tiles(ys_ref, jnp.dot(hid.astype(BF16), wd_ref[0], preferred_element_type=F32))

    @pl.when(i >= nu_ref[0])
    def _():
        ys_ref[...] = jnp.zeros_like(ys_ref)


def _experts(xs, tile_expert, n_used, wgu, wd):
    d = D_MODEL
    tm = EXPERT_ROW_TILE
    blk = pl.BlockSpec((tm * TOKEN_TILE_ROWS, LANES), lambda i, te, nu: (i, 0))
    blk_in = pl.BlockSpec((tm * TOKEN_TILE_ROWS, LANES), lambda i, te, nu: (jnp.minimum(i, nu[0] - 1), 0))
    grid_spec = pltpu.PrefetchScalarGridSpec(
        num_scalar_prefetch=2,
        grid=(xs.shape[0] // (tm * TOKEN_TILE_ROWS),),
        in_specs=[blk_in,
                  pl.BlockSpec((1, d, 2 * D_EXPERT), lambda i, te, nu: (te[i], 0, 0)),
                  pl.BlockSpec((1, D_EXPERT, d), lambda i, te, nu: (te[i], 0, 0))],
        out_specs=blk,
    )
    return pl.pallas_call(
        _expert_kernel,
        out_shape=jax.ShapeDtypeStruct(xs.shape, F32),
        grid_spec=grid_spec,
        compiler_params=_cparams(("arbitrary",)),
        name="moe_experts",
    )(tile_expert, n_used, xs, wgu, wd)


def _combine_kernel(dc_ref, dn_ref, ys_hbm, meta_ref, ht_ref, g_ref, b_ref, o_ref, obf_ref, buf, sem):
    i = pl.program_id(0)
    n = pl.num_programs(0)
    tt = dc_ref.shape[-1]
    slot = lax.rem(i, 2)

    def gather(d_ref, s):
        def issue(j, carry):
            for k in range(TOP_K):
                pltpu.make_async_copy(_tile_at(ys_hbm, d_ref[0, k, j]), _tile(buf.at[s, k], j),
                                      sem.at[s]).start(priority=k)
            return carry
        lax.fori_loop(0, tt, issue, 0, unroll=DMA_ISSUE_UNROLL)

    @pl.when(i == 0)
    def _():
        gather(dc_ref, 0)

    @pl.when(i + 1 < n)
    def _():
        gather(dn_ref, 1 - slot)

    for k in range(TOP_K):
        pltpu.make_async_copy(ys_hbm.at[pl.ds(0, tt * TOKEN_TILE_ROWS)], buf.at[slot, k], sem.at[slot]).wait()
    meta = meta_ref[...]
    ffn = (meta[:, META_W0:META_W0 + 1] * _token_tiles_to_rows(buf.at[slot, 0], tt)
           + meta[:, META_W1:META_W1 + 1] * _token_tiles_to_rows(buf.at[slot, 1], tt))
    y = _ln_rows(DEEPNORM_ALPHA * _token_tiles_to_rows(ht_ref, tt) + ffn, g_ref[...], b_ref[...])
    o_ref[...] = y
    obf_ref[...] = y.astype(BF16)


def _combine(ys, slots, meta, h_tiles, g, b):
    t, d = h_tiles.shape[0] // TOKEN_TILE_ROWS, D_MODEL
    tt = slots.shape[-1]
    n = t // tt
    cur = pl.BlockSpec((1, SUBLANES, tt), lambda i: (i, 0, 0), memory_space=pltpu.SMEM)
    nxt = pl.BlockSpec((1, SUBLANES, tt), lambda i: (jnp.minimum(i + 1, n - 1), 0, 0), memory_space=pltpu.SMEM)
    row = lambda i: (i, 0)
    const = lambda i: (0, 0)
    return pl.pallas_call(
        _combine_kernel,
        out_shape=(jax.ShapeDtypeStruct((t, d), F32), jax.ShapeDtypeStruct((t, d), BF16)),
        grid=(n,),
        in_specs=[cur, nxt, pl.BlockSpec(memory_space=pl.ANY),
                  pl.BlockSpec((tt, LANES), row), pl.BlockSpec((tt * TOKEN_TILE_ROWS, LANES), row),
                  pl.BlockSpec((1, d), const), pl.BlockSpec((1, d), const)],
        out_specs=(pl.BlockSpec((tt, d), row), pl.BlockSpec((tt, d), row)),
        scratch_shapes=[pltpu.VMEM((2, TOP_K, tt * TOKEN_TILE_ROWS, LANES), F32), pltpu.SemaphoreType.DMA((2,))],
        compiler_params=_cparams(("arbitrary",)),
        name="moe_combine_ln",
    )(slots, slots, ys, meta, h_tiles, g, b)


def _segment_tables(counts, t):
    tm = EXPERT_ROW_TILE
    n_rows = TOP_K * t + N_EXPERTS * tm
    cnt = counts[0, :N_EXPERTS].astype(jnp.int32)
    tiles = (cnt + tm - 1) // tm
    ends = jnp.cumsum(tiles)
    offs = jnp.zeros((1, LANES), F32).at[0, :N_EXPERTS].set(((ends - tiles) * tm).astype(F32))
    tile_ids = jnp.arange(n_rows // tm, dtype=jnp.int32)
    tile_expert = jnp.minimum(jnp.sum((tile_ids[:, None] >= ends[None, :]).astype(jnp.int32), axis=1),
                              N_EXPERTS - 1)
    last_tile = jnp.maximum(ends - 1, 0).astype(jnp.int32)
    return n_rows, offs, tile_expert, ends[-1:].astype(jnp.int32), last_tile


def _sparse_moe(h_tiles, meta, counts, wgu, wd, g, b):
    t = h_tiles.shape[0] // TOKEN_TILE_ROWS
    n_rows, offs, tile_expert, n_used, last_tile = _segment_tables(counts, t)
    slots = _slots(meta, offs, min(MOE_TOKENS, t))
    xs = _dispatch(h_tiles, slots, _zero_last_tiles(last_tile, n_rows))
    ys = _experts(xs, tile_expert, n_used, wgu, wd)
    return _combine(ys, slots, meta, h_tiles, g, b)


def _rotate_half_cols(w):
    half = w.shape[-1] // 2
    return jnp.concatenate([-w[..., half:], w[..., :half]], axis=-1)


def _pack_input_proj(w_in, b_in):
    d = w_in.shape[0]
    offs = np.cumsum((0, GLA_QK_W, GLA_QK_W, GLA_V_W, GLA_V_W, GLA_GATE_RANK, GLA_GATE_RANK,
                      MLA_Q_RANK, MLA_KV_RANK, MLA_ROPE, D_MODEL, D_MODEL))

    def seg(i):
        return w_in[:, offs[i]:offs[i + 1]], b_in[offs[i]:offs[i + 1]]

    w = jnp.zeros((d, N_PROJ), F32)
    b = jnp.zeros((N_PROJ,), F32)

    def put(w, b, col, ws, bs):
        return w.at[:, col:col + ws.shape[1]].set(ws), b.at[col:col + ws.shape[1]].set(bs)

    for i, col in ((0, COL_GQ), (1, COL_GK), (2, COL_GV), (3, COL_GR), (9, COL_GA), (10, COL_GB),
                   (6, COL_CQ), (7, COL_CKV), (4, COL_SMALL), (5, COL_SMALL + GLA_GATE_RANK)):
        w, b = put(w, b, col, *seg(i))
    wkr, bkr = seg(8)
    w, b = put(w, b, COL_SMALL + KR_LANE, wkr, bkr)
    w, b = put(w, b, COL_SMALL2 + KR_LANE, _rotate_half_cols(wkr), _rotate_half_cols(bkr))
    return w.astype(BF16), b.reshape(1, N_PROJ)


def _pack_decay(wa2, ba, lane0):
    w = jnp.zeros((LANES, GLA_QK_W), F32).at[lane0:lane0 + GLA_GATE_RANK].set(wa2)
    return w.astype(BF16), ba.reshape(1, GLA_QK_W)


def _pack_mla(w_uq, w_ukv, q_norm_g, kv_norm_g):
    wq = w_uq.reshape(MLA_Q_RANK, MLA_HEADS, MLA_QK)
    rope = wq[..., MLA_NOPE:]
    zq = jnp.zeros((MLA_Q_RANK, MLA_HEADS, HEAD_PAD - MLA_QK), F32)
    wq_p = jnp.concatenate([wq, zq], axis=-1)
    wqr_p = jnp.concatenate([jnp.zeros_like(wq[..., :MLA_NOPE]), _rotate_half_cols(rope), zq], axis=-1)

    def pad_rows(w):
        w = w.reshape(MLA_Q_RANK, MLA_HEADS * HEAD_PAD)
        return jnp.pad(w, ((0, CQ_PAD - MLA_Q_RANK), (0, 0))).astype(BF16)

    wkv = w_ukv.reshape(MLA_KV_RANK, MLA_HEADS, MLA_NOPE + MLA_V)
    wk_p = jnp.concatenate([wkv[..., :MLA_NOPE],
                            jnp.zeros((MLA_KV_RANK, MLA_HEADS, HEAD_PAD - MLA_NOPE), F32)], axis=-1)
    wk_p = wk_p.reshape(MLA_KV_RANK, MLA_HEADS * HEAD_PAD).astype(BF16)
    wv_p = wkv[..., MLA_NOPE:].reshape(MLA_KV_RANK, MLA_HEADS * MLA_V).T.astype(BF16)
    gq = jnp.pad(q_norm_g, (0, CQ_PAD - MLA_Q_RANK)).reshape(1, CQ_PAD)
    return pad_rows(wq_p), pad_rows(wqr_p), wk_p, wv_p, gq, kv_norm_g.reshape(1, MLA_KV_RANK)


def _rope_lane_table():
    inv = ROPE_BASE ** (-jnp.arange(0, MLA_ROPE, 2, dtype=F32) / MLA_ROPE)
    half = MLA_ROPE // 2
    tab = jnp.zeros((LANES,), F32)
    tab = tab.at[KR_LANE:KR_LANE + half].set(inv).at[KR_LANE + half:KR_LANE + MLA_ROPE].set(inv)
    return tab.reshape(1, LANES)


def _pack_router(w_grp, b_grp, w_exp, b_exp):
    d = w_grp.shape[0]
    w = jnp.zeros((d, LANES), F32).at[:, :N_EXPERTS].set(w_exp).at[:, GRP_LANE:GRP_LANE + N_GROUPS].set(w_grp)
    b = jnp.zeros((LANES,), F32).at[:N_EXPERTS].set(b_exp).at[GRP_LANE:GRP_LANE + N_GROUPS].set(b_grp)
    return w, b.reshape(1, LANES)


def kernel(x, positions, ln_emb_g, ln_emb_b, w_in, b_in, gla_wa2_f, gla_ba_f, gla_wa2_b, gla_ba_b, gla_norm_g, mla_q_norm_g, mla_w_uq, mla_kv_norm_g, mla_w_ukv, w_out, ln1_g, ln1_b, w_grp, b_grp, w_exp, b_exp, w_gate, w_up, w_down, ln2_g, ln2_b):
    batch, seq, d = x.shape
    t = batch * seq
    pos = positions.reshape(t, 1).astype(jnp.int32)
    inv_lane = _rope_lane_table()
    h, hb = _layer_norm(x.reshape(t, d), ln_emb_g, ln_emb_b)
    for l in range(DEPTH):
        w_p, b_p = _pack_input_proj(w_in[l], b_in[l])
        proj = _input_proj(hb, w_p, b_p)
        wa_f, ba_f = _pack_decay(gla_wa2_f[l], gla_ba_f[l], 0)
        wa_b, ba_b = _pack_decay(gla_wa2_b[l], gla_ba_b[l], GLA_GATE_RANK)
        o_f = _gla(proj, wa_f, ba_f, batch, seq, reverse=False)
        o_gla = _gla(proj, wa_b, ba_b, batch, seq, reverse=True, o_fwd=o_f,
                     norm_g=gla_norm_g[l].reshape(1, GLA_V_W))
        wq, wqr, wk, wv, gq, gkv = _pack_mla(mla_w_uq[l], mla_w_ukv[l], mla_q_norm_g[l], mla_kv_norm_g[l])
        q2, k2, v = _mla_prep(proj, pos, inv_lane, gq, gkv, wq, wqr, wk, wv)
        o_mla = _attention(q2, k2, v, batch, seq)
        h_tiles = _merge_out(o_gla, o_mla, proj, h, w_out[l].astype(BF16),
                           ln1_g[l].reshape(1, d), ln1_b[l].reshape(1, d))
        w_r, b_r = _pack_router(w_grp[l], b_grp[l], w_exp[l], b_exp[l])
        meta, counts = _router(h_tiles, w_r, b_r)
        wgu = jnp.concatenate([w_gate[l], w_up[l]], axis=-1).astype(BF16)
        h, hb = _sparse_moe(h_tiles, meta, counts, wgu, w_down[l].astype(BF16),
                            ln2_g[l].reshape(1, d), ln2_b[l].reshape(1, d))
    return h.reshape(batch, seq, d)
```

```python
import functools

import numpy as np
import jax
import jax.numpy as jnp
from jax import lax
from jax.experimental import pallas as pl
from jax.experimental.pallas import tpu as pltpu

F32 = jnp.float32
BF16 = jnp.bfloat16

D_MODEL = 1024
DEPTH = 2
GLA_HEADS = 4
GLA_DK = 128
GLA_DV = 256
GLA_GATE_RANK = 16
GLA_TAU = 16.0
MLA_HEADS = 16
MLA_NOPE = 64
MLA_ROPE = 32
MLA_V = 64
MLA_QK = MLA_NOPE + MLA_ROPE
MLA_Q_RANK = 384
MLA_KV_RANK = 128
ROPE_BASE = 10000.0
N_GROUPS = 8
EXPERTS_PER_GROUP = 4
N_EXPERTS = 32
D_EXPERT = 256
GLA_QK_W = GLA_HEADS * GLA_DK
GLA_V_W = GLA_HEADS * GLA_DV
DEEPNORM_ALPHA = (2.0 * DEPTH) ** 0.25
LN_EPS = 1e-5
RMS_EPS = 1e-6

LANES = 128
VMEM_LIMIT_BYTES = 56 * 1024 * 1024

COL_GQ = 0
COL_GK = 512
COL_GV = 1024
COL_GR = 2048
COL_GA = 3072
COL_GB = 4096
COL_CQ = 5120
CQ_PAD = 512
COL_SMALL = 5632
COL_SMALL2 = 5760
COL_CKV = 5888
N_PROJ = 6144
HEAD_PAD = 128
KR_LANE = 64

GLA_CHUNK = 128
NEG_BIG = -1e30


def _cparams(sem):
    return pltpu.CompilerParams(dimension_semantics=sem, vmem_limit_bytes=VMEM_LIMIT_BYTES)


def _ln_rows(x, g, b):
    mu = jnp.mean(x, axis=-1, keepdims=True)
    xc = x - mu
    var = jnp.mean(xc * xc, axis=-1, keepdims=True)
    return xc * lax.rsqrt(var + LN_EPS) * g + b


PROJ_COL_CHUNK = 512


def _proj_kernel(*refs, embed_ln):
    if embed_ln:
        x_ref, g_ref, beta_ref, w_ref, b_ref, h_ref, o_ref = refs
        h = _ln_rows(x_ref[...], g_ref[...], beta_ref[...])
        h_ref[...] = h
        x = h.astype(BF16)
    else:
        x_ref, w_ref, b_ref, o_ref = refs
        x = x_ref[...]
    for c in range(N_PROJ // PROJ_COL_CHUNK):
        sl = slice(c * PROJ_COL_CHUNK, (c + 1) * PROJ_COL_CHUNK)
        acc = jnp.dot(x, w_ref[:, sl], preferred_element_type=F32) + b_ref[:, sl]
        o_ref[:, sl] = acc.astype(o_ref.dtype)


def _input_proj(x, w, b, ln=None, tm=512):
    t, d = x.shape
    row = lambda i: (i, 0)
    const = lambda i: (0, 0)
    vec = pl.BlockSpec((1, d), const)
    in_specs = [pl.BlockSpec((tm, d), row)] + ([vec, vec] if ln else []) + [
        pl.BlockSpec((d, N_PROJ), const), pl.BlockSpec((1, N_PROJ), const)]
    proj_shape = jax.ShapeDtypeStruct((t, N_PROJ), BF16)
    proj_spec = pl.BlockSpec((tm, N_PROJ), row)
    args = (x,) + (tuple(v.reshape(1, d) for v in ln) if ln else ()) + (w, b)
    return pl.pallas_call(
        functools.partial(_proj_kernel, embed_ln=ln is not None),
        out_shape=(jax.ShapeDtypeStruct((t, d), F32), proj_shape) if ln else proj_shape,
        grid=(t // tm,),
        in_specs=in_specs,
        out_specs=(pl.BlockSpec((tm, d), row), proj_spec) if ln else proj_spec,
        compiler_params=_cparams(("parallel",)),
        name="ln_input_proj" if ln else "input_proj",
    )(*args)


def _log_sigmoid(x):
    return jnp.minimum(x, 0.0) - jnp.log(1.0 + jnp.exp(-jnp.abs(x)))


def _split_bf16(x):
    hi = x.astype(BF16)
    lo = (x - hi.astype(F32)).astype(BF16)
    return hi, lo


def _gla_kernel(*refs, reverse, n_chunks):
    if reverse:
        (q_ref, k_ref, v_ref, z_ref, wa_ref, ba_ref, of_ref, gr_ref, ng_ref, o_ref, state_ref) = refs
    else:
        (q_ref, k_ref, v_ref, z_ref, wa_ref, ba_ref, o_ref, state_ref) = refs
    c_len = GLA_CHUNK

    @pl.when(pl.program_id(1) == 0)
    def _():
        state_ref[...] = jnp.zeros_like(state_ref)

    row = lax.broadcasted_iota(jnp.int32, (c_len, c_len), 0)
    col = lax.broadcasted_iota(jnp.int32, (c_len, c_len), 1)
    if reverse:
        tri = (col >= row).astype(BF16)
        keep = col > row
        last = 0
    else:
        tri = (col <= row).astype(BF16)
        keep = col <= row
        last = c_len - 1

    seqs = range(q_ref.shape[0])

    def decays(bi, rows):
        z = z_ref[bi, rows, :]
        la = _log_sigmoid(jnp.dot(z, wa_ref[...], preferred_element_type=F32) + ba_ref[...]) * (1.0 / GLA_TAU)
        la_hi, la_lo = _split_bf16(la)
        b_all = (jnp.dot(tri, la_hi, preferred_element_type=F32)
                 + jnp.dot(tri, la_lo, preferred_element_type=F32))
        b_last = b_all[last:last + 1, :]
        q = q_ref[bi, rows, :].astype(F32)
        k = k_ref[bi, rows, :].astype(F32)
        qd_all = (q * (jnp.exp(b_all) * (GLA_DK ** -0.5))).astype(BF16)
        kinv_all = (k * jnp.exp(-b_all)).astype(BF16)
        kend_all = k * jnp.exp(b_last - b_all)
        dec_all = jnp.broadcast_to(jnp.exp(b_last), (c_len, GLA_QK_W))
        return qd_all, kinv_all, kend_all, dec_all

    def chunk_rows(cc):
        c = (n_chunks - 1 - cc) if reverse else cc
        return slice(c * c_len, (c + 1) * c_len)

    pre_next = [decays(bi, chunk_rows(0)) for bi in seqs]
    for cc in range(n_chunks):
        rows = chunk_rows(cc)
        pre = pre_next
        if cc + 1 < n_chunks:
            pre_next = [decays(bi, chunk_rows(cc + 1)) for bi in seqs]
        for h in range(GLA_HEADS):
            ks = slice(h * GLA_DK, (h + 1) * GLA_DK)
            vs = slice(h * GLA_DV, (h + 1) * GLA_DV)
            v = [v_ref[bi, rows, vs] for bi in seqs]
            scores = [lax.dot_general(pre[bi][0][:, ks], pre[bi][1][:, ks], (((1,), (1,)), ((), ())),
                                      preferred_element_type=F32) for bi in seqs]
            state = [state_ref[bi, h] for bi in seqs]
            o = [jnp.dot(jnp.where(keep, scores[bi], 0.0).astype(BF16), v[bi], preferred_element_type=F32)
                 + jnp.dot(pre[bi][0][:, ks], state[bi].astype(BF16), preferred_element_type=F32) for bi in seqs]
            for bi in seqs:
                kend_t = pre[bi][2][:, ks].T.astype(BF16)
                dec_t = pre[bi][3][:, ks].T
                dec = jnp.concatenate([dec_t, dec_t], axis=1)
                state_ref[bi, h] = dec * state[bi] + jnp.dot(kend_t, v[bi], preferred_element_type=F32)
            for bi in seqs:
                ob = o[bi]
                if reverse:
                    ob = ob + of_ref[bi, rows, vs]
                    ms = jnp.mean(ob * ob, axis=-1, keepdims=True)
                    ob = ob * lax.rsqrt(ms + RMS_EPS) * ng_ref[:, vs]
                    g = gr_ref[bi, rows, vs].astype(F32)
                    ob = ob * (g * jax.nn.sigmoid(g))
                o_ref[bi, rows, vs] = ob.astype(o_ref.dtype)


GLA_SEQS_PER_STEP = 2


def _gla(proj, wa, ba, batch, seq, *, reverse, o_fwd=None, norm_g=None, ts=512):
    ts = min(ts, seq)
    nblk = seq // ts
    nb = GLA_SEQS_PER_STEP if batch % GLA_SEQS_PER_STEP == 0 else 1
    proj3 = proj.reshape(batch, seq, N_PROJ)

    def blk(i):
        return (nblk - 1 - i) if reverse else i

    def cols(width, col):
        return pl.BlockSpec((nb, ts, width), lambda b, i: (b, blk(i), col // width))

    in_specs = [cols(GLA_QK_W, COL_GQ), cols(GLA_QK_W, COL_GK), cols(GLA_V_W, COL_GV), cols(LANES, COL_SMALL),
                pl.BlockSpec((LANES, GLA_QK_W), lambda b, i: (0, 0)),
                pl.BlockSpec((1, GLA_QK_W), lambda b, i: (0, 0))]
    args = [proj3, proj3, proj3, proj3, wa, ba]
    if reverse:
        in_specs += [cols(GLA_V_W, 0), cols(GLA_V_W, COL_GR), pl.BlockSpec((1, GLA_V_W), lambda b, i: (0, 0))]
        args += [o_fwd.reshape(batch, seq, GLA_V_W), proj3, norm_g]
    out = pl.pallas_call(
        functools.partial(_gla_kernel, reverse=reverse, n_chunks=ts // GLA_CHUNK),
        out_shape=jax.ShapeDtypeStruct((batch, seq, GLA_V_W), BF16 if reverse else F32),
        grid=(batch // nb, nblk),
        in_specs=in_specs,
        out_specs=cols(GLA_V_W, 0),
        scratch_shapes=[pltpu.VMEM((nb, GLA_HEADS, GLA_DK, GLA_DV), F32)],
        compiler_params=_cparams(("parallel", "arbitrary")),
        name="gla_bwd" if reverse else "gla_fwd",
    )(*args)
    return out.reshape(batch * seq, GLA_V_W)


MLA_COL_CHUNK = 512
ROPE_PACK = LANES // (MLA_ROPE // 2)


def _mla_prep_kernel(cq_ref, ckv_ref, sm_ref, sm2_ref, pos_ref, inv_ref, gq_ref, gkv_ref,
                     wq_ref, wqr_ref, wk_ref, wv_ref, q_ref, k_ref, v_ref):
    cq = cq_ref[...].astype(F32)
    msq = jnp.sum(cq * cq, axis=-1, keepdims=True) * (1.0 / MLA_Q_RANK)
    cqn = (cq * lax.rsqrt(msq + RMS_EPS) * gq_ref[...]).astype(BF16)
    ckv = ckv_ref[...].astype(F32)
    mskv = jnp.mean(ckv * ckv, axis=-1, keepdims=True)
    ckvn = (ckv * lax.rsqrt(mskv + RMS_EPS) * gkv_ref[...]).astype(BF16)

    ts = cq_ref.shape[0]
    ang = pos_ref[...] * inv_ref[...]
    tok = lax.broadcasted_iota(jnp.int32, (ts, ts // ROPE_PACK), 0)
    grp = lax.broadcasted_iota(jnp.int32, (ts, ts // ROPE_PACK), 1)
    to_rows = (tok // ROPE_PACK == grp).astype(BF16)
    src = lax.broadcasted_iota(jnp.int32, (LANES, LANES), 0)
    dst = lax.broadcasted_iota(jnp.int32, (LANES, LANES), 1)
    half = MLA_ROPE // 2
    to_lanes = ((dst >= KR_LANE) & (dst < KR_LANE + MLA_ROPE) & ((dst - KR_LANE) % half == src % half)).astype(BF16)
    lane = lax.broadcasted_iota(jnp.int32, (ts, LANES), 1)
    own = lane // half == lax.broadcasted_iota(jnp.int32, (ts, LANES), 0) % ROPE_PACK

    def spread(packed):
        out = 0.0
        for part in _split_bf16(packed):
            rows = jnp.dot(to_rows, part, preferred_element_type=F32)
            rows = jnp.where(own, rows, 0.0).astype(BF16)
            out = out + jnp.dot(rows, to_lanes, preferred_element_type=F32)
        return out

    cos = jnp.where(lane < KR_LANE, 1.0, spread(jnp.cos(ang)))
    sin = spread(jnp.sin(ang))
    kr = jnp.where(lane >= KR_LANE, sm_ref[...].astype(F32), 0.0)
    kr = kr * cos + sm2_ref[...].astype(F32) * sin

    heads_per_chunk = MLA_COL_CHUNK // HEAD_PAD
    cos_t = jnp.concatenate([cos] * heads_per_chunk, axis=1)
    sin_t = jnp.concatenate([sin] * heads_per_chunk, axis=1)
    kr_t = jnp.concatenate([kr] * heads_per_chunk, axis=1)
    qscale = (MLA_QK ** -0.5) * float(np.log2(np.e))
    for c in range(MLA_HEADS * HEAD_PAD // MLA_COL_CHUNK):
        sl = slice(c * MLA_COL_CHUNK, (c + 1) * MLA_COL_CHUNK)
        q = jnp.dot(cqn, wq_ref[:, sl], preferred_element_type=F32)
        qr = jnp.dot(cqn, wqr_ref[:, sl], preferred_element_type=F32)
        q_ref[:, sl] = ((q * cos_t + qr * sin_t) * qscale).astype(BF16)
        k = jnp.dot(ckvn, wk_ref[:, sl], preferred_element_type=F32)
        k_ref[:, sl] = (k + kr_t).astype(BF16)
    v_ref[...] = lax.dot_general(wv_ref[...], ckvn, (((1,), (1,)), ((), ())),
                                 preferred_element_type=F32).astype(BF16)


def _mla_prep(proj, pos, inv_lane, gq, gkv, wq, wqr, wk, wv, ts=512):
    t = proj.shape[0]
    hp = MLA_HEADS * HEAD_PAD
    vw = MLA_HEADS * MLA_V
    const = lambda i: (0, 0)
    return pl.pallas_call(
        _mla_prep_kernel,
        out_shape=(jax.ShapeDtypeStruct((t, hp), BF16), jax.ShapeDtypeStruct((t, hp), BF16),
                   jax.ShapeDtypeStruct((vw, t), BF16)),
        grid=(t // ts,),
        in_specs=[
            pl.BlockSpec((ts, CQ_PAD), lambda i: (i, COL_CQ // CQ_PAD)),
            pl.BlockSpec((ts, LANES), lambda i: (i, COL_CKV // LANES)),
            pl.BlockSpec((ts, LANES), lambda i: (i, COL_SMALL // LANES)),
            pl.BlockSpec((ts, LANES), lambda i: (i, COL_SMALL2 // LANES)),
            pl.BlockSpec((ts // ROPE_PACK, LANES), lambda i: (i, 0)),
            pl.BlockSpec((1, LANES), const),
            pl.BlockSpec((1, CQ_PAD), const),
            pl.BlockSpec((1, MLA_KV_RANK), const),
            pl.BlockSpec((CQ_PAD, hp), const),
            pl.BlockSpec((CQ_PAD, hp), const),
            pl.BlockSpec((MLA_KV_RANK, hp), const),
            pl.BlockSpec((vw, MLA_KV_RANK), const),
        ],
        out_specs=(pl.BlockSpec((ts, hp), lambda i: (i, 0)), pl.BlockSpec((ts, hp), lambda i: (i, 0)),
                   pl.BlockSpec((vw, ts), lambda i: (0, i))),
        compiler_params=_cparams(("parallel",)),
        name="mla_prep",
    )(proj, proj, proj, proj, pos, inv_lane, gq, gkv, wq, wqr, wk, wv)


SUBLANES = 8


BOUND_SLACK = 1.01
SHIFT_LIMIT = 48.0


def _attn_kernel(q_ref, k_ref, vt_ref, o_ref, s_scr, bound_scr, safe_scr, *, tk):
    seq = k_ref.shape[0]
    tq = s_scr.shape[-1]
    nk = seq // tk
    nq = seq // tq
    nt = (((1,), (1,)), ((), ()))
    head = lambda hh: slice(hh * HEAD_PAD, (hh + 1) * HEAD_PAD)
    ones8 = jnp.ones((SUBLANES, HEAD_PAD), BF16)
    ones_sq = jnp.ones((HEAD_PAD, HEAD_PAD), BF16)

    def max_key_norm_sq(hh):
        k = k_ref[:, head(hh)].astype(F32)
        row_sums = jnp.dot((k * k).astype(BF16), ones_sq, preferred_element_type=F32)
        return jnp.max(row_sums, axis=0, keepdims=True)[:, 0:1]

    for hh in range(2):
        qf = q_ref[:, head(hh)].astype(F32)
        q_sq = lax.dot_general(ones8, (qf * qf).astype(BF16), nt, preferred_element_type=F32)
        b8 = jnp.sqrt(q_sq * max_key_norm_sq(hh)) * BOUND_SLACK
        for j in range(nq):
            bound_scr[hh, j] = b8[:, j * tq:(j + 1) * tq]
    for j in range(nq):
        worst = jnp.max(jnp.maximum(bound_scr[0, j], bound_scr[1, j]))
        safe_scr[j] = (worst < SHIFT_LIMIT).astype(jnp.int32)

    def scores(hh, q, c):
        return lax.dot_general(k_ref[c * tk:(c + 1) * tk, head(hh)], q, nt, preferred_element_type=F32)

    def weigh(hh, c, p, l8, acc):
        l8 = l8 + jnp.sum(p.reshape(tk // SUBLANES, SUBLANES, tq), axis=0)
        vt = vt_ref[hh * MLA_V:(hh + 1) * MLA_V, c * tk:(c + 1) * tk]
        return l8, acc + jnp.dot(vt, p.astype(BF16), preferred_element_type=F32)

    def q_tile(i, carry):
        rows = pl.ds(pl.multiple_of(i * tq, tq), tq)
        qs = [q_ref[rows, head(hh)] for hh in range(2)]
        bound = [bound_scr[hh, i, 0:1, :] for hh in range(2)]
        safe = safe_scr[i] != 0
        zeros = lambda n: jnp.zeros((n, tq), F32)

        @pl.when(safe)
        def _():
            l8, acc = [zeros(SUBLANES)] * 2, [zeros(MLA_V)] * 2
            st_next = [scores(hh, qs[hh], 0) for hh in range(2)]
            for c in range(nk):
                st = st_next
                if c + 1 < nk:
                    st_next = [scores(hh, qs[hh], c + 1) for hh in range(2)]
                for hh in range(2):
                    l8[hh], acc[hh] = weigh(hh, c, jnp.exp2(st[hh] - bound[hh]), l8[hh], acc[hh])
            out = [acc[hh] / jnp.sum(l8[hh], axis=0, keepdims=True) for hh in range(2)]
            o_ref[rows, :] = jnp.concatenate(out, axis=0).T.astype(BF16)

        @pl.when(jnp.logical_not(safe))
        def _():
            out = []
            for hh in range(2):
                m8 = jnp.full((SUBLANES, tq), NEG_BIG, F32)
                for c in range(nk):
                    st = scores(hh, qs[hh], c)
                    s_scr[c] = st
                    m8 = jnp.maximum(m8, jnp.max(st.reshape(tk // SUBLANES, SUBLANES, tq), axis=0))
                m = jnp.max(m8, axis=0, keepdims=True)
                l8, acc = zeros(SUBLANES), zeros(MLA_V)
                for c in range(nk):
                    l8, acc = weigh(hh, c, jnp.exp2(s_scr[c] - m), l8, acc)
                out.append(acc / jnp.sum(l8, axis=0, keepdims=True))
            o_ref[rows, :] = jnp.concatenate(out, axis=0).T.astype(BF16)

        return carry

    lax.fori_loop(0, nq, q_tile, 0)


def _attention(q2, k2, vt, batch, seq, tq=512, tk=512):
    tq = min(tq, seq)
    tk = min(tk, seq)
    t = batch * seq
    pairs = MLA_HEADS // 2
    return pl.pallas_call(
        functools.partial(_attn_kernel, tk=tk),
        scratch_shapes=[pltpu.VMEM((seq // tk, tk, tq), F32), pltpu.VMEM((2, seq // tq, SUBLANES, tq), F32),
                        pltpu.SMEM((seq // tq,), jnp.int32)],
        out_shape=jax.ShapeDtypeStruct((t, MLA_HEADS * MLA_V), BF16),
        grid=(batch, pairs),
        in_specs=[
            pl.BlockSpec((seq, 2 * HEAD_PAD), lambda b, p: (b, p)),
            pl.BlockSpec((seq, 2 * HEAD_PAD), lambda b, p: (b, p)),
            pl.BlockSpec((2 * MLA_V, seq), lambda b, p: (p, b)),
        ],
        out_specs=pl.BlockSpec((seq, 2 * MLA_V), lambda b, p: (b, p)),
        compiler_params=_cparams(("parallel", "parallel")),
        name="mla_attention",
    )(q2, k2, vt)


TOKEN_TILE_ROWS = D_MODEL // LANES


def _rows_to_token_tiles(ref, y):
    n = y.shape[0]
    for s in range(TOKEN_TILE_ROWS):
        ref[pl.ds(s, n, stride=TOKEN_TILE_ROWS), :] = y[:, s * LANES:(s + 1) * LANES]


def _token_tiles_to_rows(ref, n):
    return jnp.concatenate([ref[pl.ds(s, n, stride=TOKEN_TILE_ROWS), :] for s in range(TOKEN_TILE_ROWS)], axis=1)


def _merge_out_kernel(oa_ref, ob_ref, ga_ref, gb_ref, h_ref, w_ref, g_ref, b_ref, ot_ref):
    merged = (jax.nn.sigmoid(ga_ref[...].astype(F32)) * oa_ref[...].astype(F32)
              + jax.nn.sigmoid(gb_ref[...].astype(F32)) * ob_ref[...].astype(F32))
    mix = jnp.dot(merged.astype(BF16), w_ref[...], preferred_element_type=F32)
    y = _ln_rows(DEEPNORM_ALPHA * h_ref[...] + mix, g_ref[...], b_ref[...])
    _rows_to_token_tiles(ot_ref, y)


def _merge_out(o_gla, o_mla, proj, h, w_out, g, b, tm=512):
    t, d = h.shape
    row = lambda i: (i, 0)
    const = lambda i: (0, 0)
    return pl.pallas_call(
        _merge_out_kernel,
        out_shape=jax.ShapeDtypeStruct((t * TOKEN_TILE_ROWS, LANES), F32),
        grid=(t // tm,),
        in_specs=[
            pl.BlockSpec((tm, d), row),
            pl.BlockSpec((tm, d), row),
            pl.BlockSpec((tm, d), lambda i: (i, COL_GA // D_MODEL)),
            pl.BlockSpec((tm, d), lambda i: (i, COL_GB // D_MODEL)),
            pl.BlockSpec((tm, d), row),
            pl.BlockSpec((d, d), const),
            pl.BlockSpec((1, d), const),
            pl.BlockSpec((1, d), const),
        ],
        out_specs=pl.BlockSpec((tm * TOKEN_TILE_ROWS, LANES), row),
        compiler_params=_cparams(("parallel",)),
        name="merge_out_ln",
    )(o_gla, o_mla, proj, proj, h, w_out, g, b)


GRP_LANE = N_EXPERTS
META_W0, META_W1, META_E0, META_E1, META_R0, META_R1 = range(6)


def _router_kernel(x_ref, w_ref, b_ref, meta_ref, cnt_ref, carry_ref):
    @pl.when(pl.program_id(0) == 0)
    def _():
        carry_ref[...] = jnp.zeros_like(carry_ref)

    tm = x_ref.shape[0] // TOKEN_TILE_ROWS
    x_hi, x_lo = _split_bf16(_token_tiles_to_rows(x_ref, tm))
    w_hi, w_lo = _split_bf16(w_ref[...])
    logits = (jnp.dot(x_hi, w_hi, preferred_element_type=F32)
              + jnp.dot(x_lo, w_hi, preferred_element_type=F32)
              + jnp.dot(x_hi, w_lo, preferred_element_type=F32)) + b_ref[...]
    lane = lax.broadcasted_iota(jnp.int32, logits.shape, 1)
    is_grp = (lane >= GRP_LANE) & (lane < GRP_LANE + N_GROUPS)
    gl = jnp.where(is_grp, logits, NEG_BIG)
    gmax = jnp.max(gl, axis=1, keepdims=True)
    g_lane = jnp.min(jnp.where(gl == gmax, lane, 4 * LANES), axis=1, keepdims=True)
    g_w = 1.0 / jnp.sum(jnp.where(is_grp, jnp.exp(gl - gmax), 0.0), axis=1, keepdims=True)
    lo_lane = (g_lane - GRP_LANE) * EXPERTS_PER_GROUP
    in_grp = (lane >= lo_lane) & (lane < lo_lane + EXPERTS_PER_GROUP)
    el = jnp.where(in_grp, logits, NEG_BIG)
    v1 = jnp.max(el, axis=1, keepdims=True)
    i1 = jnp.min(jnp.where(el == v1, lane, 4 * LANES), axis=1, keepdims=True)
    el2 = jnp.where(lane == i1, NEG_BIG, el)
    v2 = jnp.max(el2, axis=1, keepdims=True)
    i2 = jnp.min(jnp.where(el2 == v2, lane, 4 * LANES), axis=1, keepdims=True)
    e2 = jnp.exp(v2 - v1)
    w1 = g_w / (1.0 + e2)
    w2 = g_w * e2 / (1.0 + e2)

    onehot = jnp.where(lane == i1, 1.0, jnp.where(lane == i2, 1.0, 0.0))
    r = lax.broadcasted_iota(jnp.int32, (tm, tm), 0)
    c = lax.broadcasted_iota(jnp.int32, (tm, tm), 1)
    earlier = (c < r).astype(BF16)
    base = carry_ref[...] + jnp.dot(earlier, onehot.astype(BF16), preferred_element_type=F32)
    rank1 = jnp.sum(jnp.where(lane == i1, base, 0.0), axis=1, keepdims=True)
    rank2 = jnp.sum(jnp.where(lane == i2, base, 0.0), axis=1, keepdims=True)
    meta = jnp.zeros(logits.shape, F32)
    for ln, val in ((META_W0, w1), (META_W1, w2), (META_E0, i1.astype(F32)), (META_E1, i2.astype(F32)),
                    (META_R0, rank1), (META_R1, rank2)):
        meta = jnp.where(lane == ln, val, meta)
    meta_ref[...] = meta
    carry_ref[...] += jnp.sum(onehot, axis=0, keepdims=True)
    cnt_ref[...] = carry_ref[...]


def _router(h_tiles, w_r, b_r, tm=512):
    t, d = h_tiles.shape[0] // TOKEN_TILE_ROWS, D_MODEL
    return pl.pallas_call(
        _router_kernel,
        out_shape=(jax.ShapeDtypeStruct((t, LANES), F32), jax.ShapeDtypeStruct((1, LANES), F32)),
        grid=(t // tm,),
        in_specs=[pl.BlockSpec((tm * TOKEN_TILE_ROWS, LANES), lambda i: (i, 0)),
                  pl.BlockSpec((d, LANES), lambda i: (0, 0)),
                  pl.BlockSpec((1, LANES), lambda i: (0, 0))],
        out_specs=(pl.BlockSpec((tm, LANES), lambda i: (i, 0)), pl.BlockSpec((1, LANES), lambda i: (0, 0))),
        scratch_shapes=[pltpu.VMEM((1, LANES), F32)],
        compiler_params=_cparams(("arbitrary",)),
        name="router",
    )(h_tiles, w_r, b_r)


EXPERT_ROW_TILE = 512
MOE_TOKENS = 512
TOP_K = 2
DMA_ISSUE_UNROLL = 16


def _tile_at(ref, first_row):
    return ref.at[pl.ds(pl.multiple_of(first_row, TOKEN_TILE_ROWS), TOKEN_TILE_ROWS)]


def _tile(ref, token):
    return _tile_at(ref, token * TOKEN_TILE_ROWS)


def _slots_kernel(meta_ref, offs_ref, d_ref):
    meta = meta_ref[...]
    offs = offs_ref[...]
    lane = lax.broadcasted_iota(jnp.int32, meta.shape, 1)

    def slot(e_lane, r_lane):
        e = meta[:, e_lane:e_lane + 1].astype(jnp.int32)
        return jnp.sum(jnp.where(lane == e, offs, 0.0), axis=1, keepdims=True) + meta[:, r_lane:r_lane + 1]

    both = jnp.where(lane == 0, slot(META_E0, META_R0), jnp.where(lane == 1, slot(META_E1, META_R1), 0.0))
    d_ref[0] = (both.T[:SUBLANES, :] * float(TOKEN_TILE_ROWS)).astype(jnp.int32)


def _slots(meta, offs, tt):
    t = meta.shape[0]
    return pl.pallas_call(
        _slots_kernel,
        out_shape=jax.ShapeDtypeStruct((t // tt, SUBLANES, tt), jnp.int32),
        grid=(t // tt,),
        in_specs=[pl.BlockSpec((tt, LANES), lambda i: (i, 0)), pl.BlockSpec((1, LANES), lambda i: (0, 0))],
        out_specs=pl.BlockSpec((1, SUBLANES, tt), lambda i: (i, 0, 0)),
        compiler_params=_cparams(("arbitrary",)),
        name="moe_slots",
    )(meta, offs)


def _zero_tile_kernel(lt_ref, xs_ref):
    del lt_ref
    xs_ref[...] = jnp.zeros_like(xs_ref)


def _zero_last_tiles(last_tile, n_rows):
    blk = EXPERT_ROW_TILE * TOKEN_TILE_ROWS
    grid_spec = pltpu.PrefetchScalarGridSpec(
        num_scalar_prefetch=1, grid=(N_EXPERTS,), in_specs=[],
        out_specs=pl.BlockSpec((blk, LANES), lambda e, lt: (lt[e], 0)))
    return pl.pallas_call(
        _zero_tile_kernel,
        out_shape=jax.ShapeDtypeStruct((n_rows * TOKEN_TILE_ROWS, LANES), F32),
        grid_spec=grid_spec,
        compiler_params=_cparams(("arbitrary",)),
        name="moe_zero_tiles",
    )(last_tile)


def _dispatch_kernel(d_ref, ht_ref, xs_init_hbm, xs_hbm, sem):
    del xs_init_hbm
    tt = d_ref.shape[-1]

    def issue(j, carry):
        src = _tile(ht_ref, j)
        for k in range(TOP_K):
            pltpu.make_async_copy(src, _tile_at(xs_hbm, d_ref[0, k, j]), sem).start(priority=k)
        return carry

    lax.fori_loop(0, tt, issue, 0, unroll=DMA_ISSUE_UNROLL)
    for k in range(TOP_K):
        pltpu.make_async_copy(ht_ref, xs_hbm.at[pl.ds(0, tt * TOKEN_TILE_ROWS)], sem).wait()


def _dispatch(h_tiles, slots, xs_init):
    tt = slots.shape[-1]
    t = h_tiles.shape[0] // TOKEN_TILE_ROWS
    return pl.pallas_call(
        _dispatch_kernel,
        out_shape=jax.ShapeDtypeStruct(xs_init.shape, F32),
        grid=(t // tt,),
        in_specs=[pl.BlockSpec((1, SUBLANES, tt), lambda i: (i, 0, 0), memory_space=pltpu.SMEM),
                  pl.BlockSpec((tt * TOKEN_TILE_ROWS, LANES), lambda i: (i, 0)),
                  pl.BlockSpec(memory_space=pl.ANY)],
        out_specs=pl.BlockSpec(memory_space=pl.ANY),
        scratch_shapes=[pltpu.SemaphoreType.DMA(())],
        input_output_aliases={2: 0},
        compiler_params=_cparams(("arbitrary",)),
        name="moe_dispatch",
    )(slots, h_tiles, xs_init)


def _expert_kernel(te_ref, nu_ref, xs_ref, wgu_ref, wd_ref, ys_ref):
    del te_ref
    i = pl.program_id(0)
    tm = EXPERT_ROW_TILE

    @pl.when(i < nu_ref[0])
    def _():
        x = _token_tiles_to_rows(xs_ref, tm).astype(BF16)
        gu = jnp.dot(x, wgu_ref[0], preferred_element_type=F32)
        gate = gu[:, :D_EXPERT]
        hid = gate * jax.nn.sigmoid(gate) * gu[:, D_EXPERT:]
        _rows_to_token_tiles(ys_ref, jnp.dot(hid.astype(BF16), wd_ref[0], preferred_element_type=F32))

    @pl.when(i >= nu_ref[0])
    def _():
        ys_ref[...] = jnp.zeros_like(ys_ref)


def _experts(xs, tile_expert, n_used, wgu, wd):
    d = D_MODEL
    tm = EXPERT_ROW_TILE
    blk = pl.BlockSpec((tm * TOKEN_TILE_ROWS, LANES), lambda i, te, nu: (i, 0))
    blk_in = pl.BlockSpec((tm * TOKEN_TILE_ROWS, LANES), lambda i, te, nu: (jnp.minimum(i, nu[0] - 1), 0))
    grid_spec = pltpu.PrefetchScalarGridSpec(
        num_scalar_prefetch=2,
        grid=(xs.shape[0] // (tm * TOKEN_TILE_ROWS),),
        in_specs=[blk_in,
                  pl.BlockSpec((1, d, 2 * D_EXPERT), lambda i, te, nu: (te[i], 0, 0)),
                  pl.BlockSpec((1, D_EXPERT, d), lambda i, te, nu: (te[i], 0, 0))],
        out_specs=blk,
    )
    return pl.pallas_call(
        _expert_kernel,
        out_shape=jax.ShapeDtypeStruct(xs.shape, F32),
        grid_spec=grid_spec,
        compiler_params=_cparams(("arbitrary",)),
        name="moe_experts",
    )(tile_expert, n_used, xs, wgu, wd)


def _combine_kernel(dc_ref, dn_ref, ys_hbm, meta_ref, ht_ref, g_ref, b_ref, o_ref, obf_ref, buf, sem):
    i = pl.program_id(0)
    n = pl.num_programs(0)
    tt = dc_ref.shape[-1]
    slot = lax.rem(i, 2)

    def gather(d_ref, s):
        def issue(j, carry):
            for k in range(TOP_K):
                pltpu.make_async_copy(_tile_at(ys_hbm, d_ref[0, k, j]), _tile(buf.at[s, k], j),
                                      sem.at[s]).start(priority=k)
            return carry
        lax.fori_loop(0, tt, issue, 0, unroll=DMA_ISSUE_UNROLL)

    @pl.when(i == 0)
    def _():
        gather(dc_ref, 0)

    @pl.when(i + 1 < n)
    def _():
        gather(dn_ref, 1 - slot)

    for k in range(TOP_K):
        pltpu.make_async_copy(ys_hbm.at[pl.ds(0, tt * TOKEN_TILE_ROWS)], buf.at[slot, k], sem.at[slot]).wait()
    meta = meta_ref[...]
    ffn = (meta[:, META_W0:META_W0 + 1] * _token_tiles_to_rows(buf.at[slot, 0], tt)
           + meta[:, META_W1:META_W1 + 1] * _token_tiles_to_rows(buf.at[slot, 1], tt))
    y = _ln_rows(DEEPNORM_ALPHA * _token_tiles_to_rows(ht_ref, tt) + ffn, g_ref[...], b_ref[...])
    o_ref[...] = y
    obf_ref[...] = y.astype(BF16)


def _combine(ys, slots, meta, h_tiles, g, b):
    t, d = h_tiles.shape[0] // TOKEN_TILE_ROWS, D_MODEL
    tt = slots.shape[-1]
    n = t // tt
    cur = pl.BlockSpec((1, SUBLANES, tt), lambda i: (i, 0, 0), memory_space=pltpu.SMEM)
    nxt = pl.BlockSpec((1, SUBLANES, tt), lambda i: (jnp.minimum(i + 1, n - 1), 0, 0), memory_space=pltpu.SMEM)
    row = lambda i: (i, 0)
    const = lambda i: (0, 0)
    return pl.pallas_call(
        _combine_kernel,
        out_shape=(jax.ShapeDtypeStruct((t, d), F32), jax.ShapeDtypeStruct((t, d), BF16)),
        grid=(n,),
        in_specs=[cur, nxt, pl.BlockSpec(memory_space=pl.ANY),
                  pl.BlockSpec((tt, LANES), row), pl.BlockSpec((tt * TOKEN_TILE_ROWS, LANES), row),
                  pl.BlockSpec((1, d), const), pl.BlockSpec((1, d), const)],
        out_specs=(pl.BlockSpec((tt, d), row), pl.BlockSpec((tt, d), row)),
        scratch_shapes=[pltpu.VMEM((2, TOP_K, tt * TOKEN_TILE_ROWS, LANES), F32), pltpu.SemaphoreType.DMA((2,))],
        compiler_params=_cparams(("arbitrary",)),
        name="moe_combine_ln",
    )(slots, slots, ys, meta, h_tiles, g, b)


def _segment_tables(counts, t):
    tm = EXPERT_ROW_TILE
    n_rows = TOP_K * t + N_EXPERTS * tm
    cnt = counts[0, :N_EXPERTS].astype(jnp.int32)
    tiles = (cnt + tm - 1) // tm
    ends = jnp.cumsum(tiles)
    offs = jnp.zeros((1, LANES), F32).at[0, :N_EXPERTS].set(((ends - tiles) * tm).astype(F32))
    tile_ids = jnp.arange(n_rows // tm, dtype=jnp.int32)
    tile_expert = jnp.minimum(jnp.sum((tile_ids[:, None] >= ends[None, :]).astype(jnp.int32), axis=1),
                              N_EXPERTS - 1)
    last_tile = jnp.maximum(ends - 1, 0).astype(jnp.int32)
    return n_rows, offs, tile_expert, ends[-1:].astype(jnp.int32), last_tile


def _sparse_moe(h_tiles, meta, counts, wgu, wd, g, b):
    t = h_tiles.shape[0] // TOKEN_TILE_ROWS
    n_rows, offs, tile_expert, n_used, last_tile = _segment_tables(counts, t)
    slots = _slots(meta, offs, min(MOE_TOKENS, t))
    xs = _dispatch(h_tiles, slots, _zero_last_tiles(last_tile, n_rows))
    ys = _experts(xs, tile_expert, n_used, wgu, wd)
    return _combine(ys, slots, meta, h_tiles, g, b)


def _rotate_half_cols(w):
    half = w.shape[-1] // 2
    return jnp.concatenate([-w[..., half:], w[..., :half]], axis=-1)


def _pack_input_proj(w_in, b_in):
    d = w_in.shape[0]
    offs = np.cumsum((0, GLA_QK_W, GLA_QK_W, GLA_V_W, GLA_V_W, GLA_GATE_RANK, GLA_GATE_RANK,
                      MLA_Q_RANK, MLA_KV_RANK, MLA_ROPE, D_MODEL, D_MODEL))

    def seg(i):
        return w_in[:, offs[i]:offs[i + 1]], b_in[offs[i]:offs[i + 1]]

    w = jnp.zeros((d, N_PROJ), F32)
    b = jnp.zeros((N_PROJ,), F32)

    def put(w, b, col, ws, bs):
        return w.at[:, col:col + ws.shape[1]].set(ws), b.at[col:col + ws.shape[1]].set(bs)

    for i, col in ((0, COL_GQ), (1, COL_GK), (2, COL_GV), (3, COL_GR), (9, COL_GA), (10, COL_GB),
                   (6, COL_CQ), (7, COL_CKV), (4, COL_SMALL), (5, COL_SMALL + GLA_GATE_RANK)):
        w, b = put(w, b, col, *seg(i))
    wkr, bkr = seg(8)
    w, b = put(w, b, COL_SMALL + KR_LANE, wkr, bkr)
    w, b = put(w, b, COL_SMALL2 + KR_LANE, _rotate_half_cols(wkr), _rotate_half_cols(bkr))
    return w.astype(BF16), b.reshape(1, N_PROJ)


def _pack_decay(wa2, ba, lane0):
    w = jnp.zeros((LANES, GLA_QK_W), F32).at[lane0:lane0 + GLA_GATE_RANK].set(wa2)
    return w.astype(BF16), ba.reshape(1, GLA_QK_W)


def _pack_mla(w_uq, w_ukv, q_norm_g, kv_norm_g):
    wq = w_uq.reshape(MLA_Q_RANK, MLA_HEADS, MLA_QK)
    rope = wq[..., MLA_NOPE:]
    zq = jnp.zeros((MLA_Q_RANK, MLA_HEADS, HEAD_PAD - MLA_QK), F32)
    wq_p = jnp.concatenate([wq, zq], axis=-1)
    wqr_p = jnp.concatenate([jnp.zeros_like(wq[..., :MLA_NOPE]), _rotate_half_cols(rope), zq], axis=-1)

    def pad_rows(w):
        w = w.reshape(MLA_Q_RANK, MLA_HEADS * HEAD_PAD)
        return jnp.pad(w, ((0, CQ_PAD - MLA_Q_RANK), (0, 0))).astype(BF16)

    wkv = w_ukv.reshape(MLA_KV_RANK, MLA_HEADS, MLA_NOPE + MLA_V)
    wk_p = jnp.concatenate([wkv[..., :MLA_NOPE],
                            jnp.zeros((MLA_KV_RANK, MLA_HEADS, HEAD_PAD - MLA_NOPE), F32)], axis=-1)
    wk_p = wk_p.reshape(MLA_KV_RANK, MLA_HEADS * HEAD_PAD).astype(BF16)
    wv_p = wkv[..., MLA_NOPE:].reshape(MLA_KV_RANK, MLA_HEADS * MLA_V).T.astype(BF16)
    gq = jnp.pad(q_norm_g, (0, CQ_PAD - MLA_Q_RANK)).reshape(1, CQ_PAD)
    return pad_rows(wq_p), pad_rows(wqr_p), wk_p, wv_p, gq, kv_norm_g.reshape(1, MLA_KV_RANK)


def _rope_lane_table():
    inv = ROPE_BASE ** (-jnp.arange(0, MLA_ROPE, 2, dtype=F32) / MLA_ROPE)
    return jnp.tile(inv, ROPE_PACK).reshape(1, LANES)


def _pack_router(w_grp, b_grp, w_exp, b_exp):
    d = w_grp.shape[0]
    w = jnp.zeros((d, LANES), F32).at[:, :N_EXPERTS].set(w_exp).at[:, GRP_LANE:GRP_LANE + N_GROUPS].set(w_grp)
    b = jnp.zeros((LANES,), F32).at[:N_EXPERTS].set(b_exp).at[GRP_LANE:GRP_LANE + N_GROUPS].set(b_grp)
    return w, b.reshape(1, LANES)


def kernel(x, positions, ln_emb_g, ln_emb_b, w_in, b_in, gla_wa2_f, gla_ba_f, gla_wa2_b, gla_ba_b, gla_norm_g, mla_q_norm_g, mla_w_uq, mla_kv_norm_g, mla_w_ukv, w_out, ln1_g, ln1_b, w_grp, b_grp, w_exp, b_exp, w_gate, w_up, w_down, ln2_g, ln2_b):
    batch, seq, d = x.shape
    t = batch * seq
    pos = jnp.repeat(positions.reshape(t // ROPE_PACK, ROPE_PACK).astype(F32), MLA_ROPE // 2, axis=1)
    inv_lane = _rope_lane_table()
    for l in range(DEPTH):
        w_p, b_p = _pack_input_proj(w_in[l], b_in[l])
        if l == 0:
            h, proj = _input_proj(x.reshape(t, d), w_p, b_p, ln=(ln_emb_g, ln_emb_b))
        else:
            proj = _input_proj(hb, w_p, b_p)
        wa_f, ba_f = _pack_decay(gla_wa2_f[l], gla_ba_f[l], 0)
        wa_b, ba_b = _pack_decay(gla_wa2_b[l], gla_ba_b[l], GLA_GATE_RANK)
        o_f = _gla(proj, wa_f, ba_f, batch, seq, reverse=False)
        o_gla = _gla(proj, wa_b, ba_b, batch, seq, reverse=True, o_fwd=o_f,
                     norm_g=gla_norm_g[l].reshape(1, GLA_V_W))
        wq, wqr, wk, wv, gq, gkv = _pack_mla(mla_w_uq[l], mla_w_ukv[l], mla_q_norm_g[l], mla_kv_norm_g[l])
        q2, k2, v = _mla_prep(proj, pos, inv_lane, gq, gkv, wq, wqr, wk, wv)
        o_mla = _attention(q2, k2, v, batch, seq)
        h_tiles = _merge_out(o_gla, o_mla, proj, h, w_out[l].astype(BF16),
                           ln1_g[l].reshape(1, d), ln1_b[l].reshape(1, d))
        w_r, b_r = _pack_router(w_grp[l], b_grp[l], w_exp[l], b_exp[l])
        meta, counts = _router(h_tiles, w_r, b_r)
        wgu = jnp.concatenate([w_gate[l], w_up[l]], axis=-1).astype(BF16)
        h, hb = _sparse_moe(h_tiles, meta, counts, wgu, w_down[l].astype(BF16),
                            ln2_g[l].reshape(1, d), ln2_b[l].reshape(1, d))
    return h.reshape(batch, seq, d)
```

```python
import functools

import numpy as np
import jax
import jax.numpy as jnp
from jax import lax
from jax.experimental import pallas as pl
from jax.experimental.pallas import tpu as pltpu

F32 = jnp.float32
BF16 = jnp.bfloat16

D_MODEL = 1024
DEPTH = 2
GLA_HEADS = 4
GLA_DK = 128
GLA_DV = 256
GLA_GATE_RANK = 16
GLA_TAU = 16.0
MLA_HEADS = 16
MLA_NOPE = 64
MLA_ROPE = 32
MLA_V = 64
MLA_QK = MLA_NOPE + MLA_ROPE
MLA_Q_RANK = 384
MLA_KV_RANK = 128
ROPE_BASE = 10000.0
N_GROUPS = 8
EXPERTS_PER_GROUP = 4
N_EXPERTS = 32
D_EXPERT = 256
GLA_QK_W = GLA_HEADS * GLA_DK
GLA_V_W = GLA_HEADS * GLA_DV
DEEPNORM_ALPHA = (2.0 * DEPTH) ** 0.25
LN_EPS = 1e-5
RMS_EPS = 1e-6

LANES = 128
VMEM_LIMIT_BYTES = 56 * 1024 * 1024

COL_GQ = 0
COL_GK = 512
COL_GV = 1024
COL_GR = 2048
COL_GA = 3072
COL_GB = 4096
COL_CQ = 5120
CQ_PAD = 512
COL_SMALL = 5632
COL_SMALL2 = 5760
COL_CKV = 5888
N_PROJ = 6144
HEAD_PAD = 128
KR_LANE = 64

GLA_CHUNK = 128
NEG_BIG = -1e30


def _cparams(sem):
    return pltpu.CompilerParams(dimension_semantics=sem, vmem_limit_bytes=VMEM_LIMIT_BYTES)


def _ln_rows(x, g, b):
    mu = jnp.mean(x, axis=-1, keepdims=True)
    xc = x - mu
    var = jnp.mean(xc * xc, axis=-1, keepdims=True)
    return xc * lax.rsqrt(var + LN_EPS) * g + b


PROJ_COL_CHUNK = 512


def _proj_kernel(*refs, embed_ln):
    if embed_ln:
        x_ref, g_ref, beta_ref, w_ref, b_ref, h_ref, o_ref = refs
        h = _ln_rows(x_ref[...], g_ref[...], beta_ref[...])
        h_ref[...] = h
        x = h.astype(BF16)
    else:
        x_ref, w_ref, b_ref, o_ref = refs
        x = x_ref[...]
    for c in range(N_PROJ // PROJ_COL_CHUNK):
        sl = slice(c * PROJ_COL_CHUNK, (c + 1) * PROJ_COL_CHUNK)
        acc = jnp.dot(x, w_ref[:, sl], preferred_element_type=F32) + b_ref[:, sl]
        o_ref[:, sl] = acc.astype(o_ref.dtype)


def _input_proj(x, w, b, ln=None, tm=512):
    t, d = x.shape
    row = lambda i: (i, 0)
    const = lambda i: (0, 0)
    vec = pl.BlockSpec((1, d), const)
    in_specs = [pl.BlockSpec((tm, d), row)] + ([vec, vec] if ln else []) + [
        pl.BlockSpec((d, N_PROJ), const), pl.BlockSpec((1, N_PROJ), const)]
    proj_shape = jax.ShapeDtypeStruct((t, N_PROJ), BF16)
    proj_spec = pl.BlockSpec((tm, N_PROJ), row)
    args = (x,) + (tuple(v.reshape(1, d) for v in ln) if ln else ()) + (w, b)
    return pl.pallas_call(
        functools.partial(_proj_kernel, embed_ln=ln is not None),
        out_shape=(jax.ShapeDtypeStruct((t, d), F32), proj_shape) if ln else proj_shape,
        grid=(t // tm,),
        in_specs=in_specs,
        out_specs=(pl.BlockSpec((tm, d), row), proj_spec) if ln else proj_spec,
        compiler_params=_cparams(("parallel",)),
        name="ln_input_proj" if ln else "input_proj",
    )(*args)


def _log_sigmoid(x):
    return jnp.minimum(x, 0.0) - jnp.log(1.0 + jnp.exp(-jnp.abs(x)))


def _split_bf16(x):
    hi = x.astype(BF16)
    lo = (x - hi.astype(F32)).astype(BF16)
    return hi, lo


def _gla_kernel(*refs, reverse, n_chunks):
    if reverse:
        (q_ref, k_ref, v_ref, z_ref, wa_ref, ba_ref, of_ref, gr_ref, ng_ref, o_ref, state_ref) = refs
    else:
        (q_ref, k_ref, v_ref, z_ref, wa_ref, ba_ref, o_ref, state_ref) = refs
    c_len = GLA_CHUNK

    @pl.when(pl.program_id(1) == 0)
    def _():
        state_ref[...] = jnp.zeros_like(state_ref)

    row = lax.broadcasted_iota(jnp.int32, (c_len, c_len), 0)
    col = lax.broadcasted_iota(jnp.int32, (c_len, c_len), 1)
    if reverse:
        tri = (col >= row).astype(BF16)
        keep = col > row
        last = 0
    else:
        tri = (col <= row).astype(BF16)
        keep = col <= row
        last = c_len - 1

    seqs = range(q_ref.shape[0])

    def decays(bi, rows):
        z = z_ref[bi, rows, :]
        la = _log_sigmoid(jnp.dot(z, wa_ref[...], preferred_element_type=F32) + ba_ref[...]) * (1.0 / GLA_TAU)
        la_hi, la_lo = _split_bf16(la)
        b_all = (jnp.dot(tri, la_hi, preferred_element_type=F32)
                 + jnp.dot(tri, la_lo, preferred_element_type=F32))
        b_last = b_all[last:last + 1, :]
        q = q_ref[bi, rows, :].astype(F32)
        k = k_ref[bi, rows, :].astype(F32)
        qd_all = (q * (jnp.exp(b_all) * (GLA_DK ** -0.5))).astype(BF16)
        kinv_all = (k * jnp.exp(-b_all)).astype(BF16)
        kend_all = k * jnp.exp(b_last - b_all)
        dec_all = jnp.broadcast_to(jnp.exp(b_last), (c_len, GLA_QK_W))
        return qd_all, kinv_all, kend_all, dec_all

    def chunk_rows(cc):
        c = (n_chunks - 1 - cc) if reverse else cc
        return slice(c * c_len, (c + 1) * c_len)

    pre_next = [decays(bi, chunk_rows(0)) for bi in seqs]
    for cc in range(n_chunks):
        rows = chunk_rows(cc)
        pre = pre_next
        if cc + 1 < n_chunks:
            pre_next = [decays(bi, chunk_rows(cc + 1)) for bi in seqs]
        for h in range(GLA_HEADS):
            ks = slice(h * GLA_DK, (h + 1) * GLA_DK)
            vs = slice(h * GLA_DV, (h + 1) * GLA_DV)
            v = [v_ref[bi, rows, vs] for bi in seqs]
            scores = [lax.dot_general(pre[bi][0][:, ks], pre[bi][1][:, ks], (((1,), (1,)), ((), ())),
                                      preferred_element_type=F32) for bi in seqs]
            state = [state_ref[bi, h] for bi in seqs]
            o = [jnp.dot(jnp.where(keep, scores[bi], 0.0).astype(BF16), v[bi], preferred_element_type=F32)
                 + jnp.dot(pre[bi][0][:, ks], state[bi].astype(BF16), preferred_element_type=F32) for bi in seqs]
            for bi in seqs:
                kend_t = pre[bi][2][:, ks].T.astype(BF16)
                dec_t = pre[bi][3][:, ks].T
                dec = jnp.concatenate([dec_t, dec_t], axis=1)
                state_ref[bi, h] = dec * state[bi] + jnp.dot(kend_t, v[bi], preferred_element_type=F32)
            for bi in seqs:
                ob = o[bi]
                if reverse:
                    ob = ob + of_ref[bi, rows, vs]
                    ms = jnp.mean(ob * ob, axis=-1, keepdims=True)
                    ob = ob * lax.rsqrt(ms + RMS_EPS) * ng_ref[:, vs]
                    g = gr_ref[bi, rows, vs].astype(F32)
                    ob = ob * (g * jax.nn.sigmoid(g))
                o_ref[bi, rows, vs] = ob.astype(o_ref.dtype)


GLA_SEQS_PER_STEP = 2


def _gla(proj, wa, ba, batch, seq, *, reverse, o_fwd=None, norm_g=None, ts=512):
    ts = min(ts, seq)
    nblk = seq // ts
    nb = GLA_SEQS_PER_STEP if batch % GLA_SEQS_PER_STEP == 0 else 1
    proj3 = proj.reshape(batch, seq, N_PROJ)

    def blk(i):
        return (nblk - 1 - i) if reverse else i

    def cols(width, col):
        return pl.BlockSpec((nb, ts, width), lambda b, i: (b, blk(i), col // width))

    in_specs = [cols(GLA_QK_W, COL_GQ), cols(GLA_QK_W, COL_GK), cols(GLA_V_W, COL_GV), cols(LANES, COL_SMALL),
                pl.BlockSpec((LANES, GLA_QK_W), lambda b, i: (0, 0)),
                pl.BlockSpec((1, GLA_QK_W), lambda b, i: (0, 0))]
    args = [proj3, proj3, proj3, proj3, wa, ba]
    if reverse:
        in_specs += [cols(GLA_V_W, 0), cols(GLA_V_W, COL_GR), pl.BlockSpec((1, GLA_V_W), lambda b, i: (0, 0))]
        args += [o_fwd.reshape(batch, seq, GLA_V_W), proj3, norm_g]
    out = pl.pallas_call(
        functools.partial(_gla_kernel, reverse=reverse, n_chunks=ts // GLA_CHUNK),
        out_shape=jax.ShapeDtypeStruct((batch, seq, GLA_V_W), BF16 if reverse else F32),
        grid=(batch // nb, nblk),
        in_specs=in_specs,
        out_specs=cols(GLA_V_W, 0),
        scratch_shapes=[pltpu.VMEM((nb, GLA_HEADS, GLA_DK, GLA_DV), F32)],
        compiler_params=_cparams(("parallel", "arbitrary")),
        name="gla_bwd" if reverse else "gla_fwd",
    )(*args)
    return out.reshape(batch * seq, GLA_V_W)


MLA_COL_CHUNK = 512
ROPE_PACK = LANES // (MLA_ROPE // 2)


def _mla_prep_kernel(cq_ref, ckv_ref, sm_ref, sm2_ref, pos_ref, inv_ref, gq_ref, gkv_ref,
                     wq_ref, wk_ref, wv_ref, q_ref, k_ref, v_ref):
    cq = cq_ref[...].astype(F32)
    msq = jnp.sum(cq * cq, axis=-1, keepdims=True) * (1.0 / MLA_Q_RANK)
    cqn = (cq * lax.rsqrt(msq + RMS_EPS) * gq_ref[...]).astype(BF16)
    ckv = ckv_ref[...].astype(F32)
    mskv = jnp.mean(ckv * ckv, axis=-1, keepdims=True)
    ckvn = (ckv * lax.rsqrt(mskv + RMS_EPS) * gkv_ref[...]).astype(BF16)

    ts = cq_ref.shape[0]
    ang = pos_ref[...] * inv_ref[...]
    tok = lax.broadcasted_iota(jnp.int32, (ts, ts // ROPE_PACK), 0)
    grp = lax.broadcasted_iota(jnp.int32, (ts, ts // ROPE_PACK), 1)
    to_rows = (tok // ROPE_PACK == grp).astype(BF16)
    src = lax.broadcasted_iota(jnp.int32, (LANES, LANES), 0)
    dst = lax.broadcasted_iota(jnp.int32, (LANES, LANES), 1)
    half = MLA_ROPE // 2
    to_lanes = ((dst >= KR_LANE) & (dst < KR_LANE + MLA_ROPE) & ((dst - KR_LANE) % half == src % half)).astype(BF16)
    lane = lax.broadcasted_iota(jnp.int32, (ts, LANES), 1)
    own = lane // half == lax.broadcasted_iota(jnp.int32, (ts, LANES), 0) % ROPE_PACK

    def spread(packed):
        out = 0.0
        for part in _split_bf16(packed):
            rows = jnp.dot(to_rows, part, preferred_element_type=F32)
            rows = jnp.where(own, rows, 0.0).astype(BF16)
            out = out + jnp.dot(rows, to_lanes, preferred_element_type=F32)
        return out

    cos = jnp.where(lane < KR_LANE, 1.0, spread(jnp.cos(ang)))
    sin = spread(jnp.sin(ang))
    kr = jnp.where(lane >= KR_LANE, sm_ref[...].astype(F32), 0.0)
    kr = kr * cos + sm2_ref[...].astype(F32) * sin

    heads_per_chunk = MLA_COL_CHUNK // HEAD_PAD
    cos_t = jnp.concatenate([cos] * heads_per_chunk, axis=1)
    sin_t = jnp.concatenate([sin] * heads_per_chunk, axis=1)
    kr_t = jnp.concatenate([kr] * heads_per_chunk, axis=1)
    qscale = (MLA_QK ** -0.5) * float(np.log2(np.e))
    chunk_lane = lax.broadcasted_iota(jnp.int32, (ts, MLA_COL_CHUNK), 1) % HEAD_PAD
    first_half = chunk_lane < KR_LANE + half
    for c in range(MLA_HEADS * HEAD_PAD // MLA_COL_CHUNK):
        sl = slice(c * MLA_COL_CHUNK, (c + 1) * MLA_COL_CHUNK)
        q = jnp.dot(cqn, wq_ref[:, sl], preferred_element_type=F32)
        qr = jnp.where(first_half, -pltpu.roll(q, MLA_COL_CHUNK - half, axis=1), pltpu.roll(q, half, axis=1))
        q_ref[:, sl] = ((q * cos_t + qr * sin_t) * qscale).astype(BF16)
        k = jnp.dot(ckvn, wk_ref[:, sl], preferred_element_type=F32)
        k_ref[:, sl] = (k + kr_t).astype(BF16)
    v_ref[...] = lax.dot_general(wv_ref[...], ckvn, (((1,), (1,)), ((), ())),
                                 preferred_element_type=F32).astype(BF16)


def _mla_prep(proj, pos, inv_lane, gq, gkv, wq, wk, wv, ts=512):
    t = proj.shape[0]
    hp = MLA_HEADS * HEAD_PAD
    vw = MLA_HEADS * MLA_V
    const = lambda i: (0, 0)
    return pl.pallas_call(
        _mla_prep_kernel,
        out_shape=(jax.ShapeDtypeStruct((t, hp), BF16), jax.ShapeDtypeStruct((t, hp), BF16),
                   jax.ShapeDtypeStruct((vw, t), BF16)),
        grid=(t // ts,),
        in_specs=[
            pl.BlockSpec((ts, CQ_PAD), lambda i: (i, COL_CQ // CQ_PAD)),
            pl.BlockSpec((ts, LANES), lambda i: (i, COL_CKV // LANES)),
            pl.BlockSpec((ts, LANES), lambda i: (i, COL_SMALL // LANES)),
            pl.BlockSpec((ts, LANES), lambda i: (i, COL_SMALL2 // LANES)),
            pl.BlockSpec((ts // ROPE_PACK, LANES), lambda i: (i, 0)),
            pl.BlockSpec((1, LANES), const),
            pl.BlockSpec((1, CQ_PAD), const),
            pl.BlockSpec((1, MLA_KV_RANK), const),
            pl.BlockSpec((CQ_PAD, hp), const),
            pl.BlockSpec((MLA_KV_RANK, hp), const),
            pl.BlockSpec((vw, MLA_KV_RANK), const),
        ],
        out_specs=(pl.BlockSpec((ts, hp), lambda i: (i, 0)), pl.BlockSpec((ts, hp), lambda i: (i, 0)),
                   pl.BlockSpec((vw, ts), lambda i: (0, i))),
        compiler_params=_cparams(("parallel",)),
        name="mla_prep",
    )(proj, proj, proj, proj, pos, inv_lane, gq, gkv, wq, wk, wv)


SUBLANES = 8


BOUND_SLACK = 1.01
SHIFT_LIMIT = 48.0


def _attn_kernel(q_ref, k_ref, vt_ref, o_ref, s_scr, bound_scr, safe_scr, *, tk):
    seq = k_ref.shape[0]
    tq = s_scr.shape[-1]
    nk = seq // tk
    nq = seq // tq
    nt = (((1,), (1,)), ((), ()))
    head = lambda hh: slice(hh * HEAD_PAD, (hh + 1) * HEAD_PAD)
    ones8 = jnp.ones((SUBLANES, HEAD_PAD), BF16)
    ones_sq = jnp.ones((HEAD_PAD, HEAD_PAD), BF16)

    def max_key_norm_sq(hh):
        k = k_ref[:, head(hh)].astype(F32)
        row_sums = jnp.dot((k * k).astype(BF16), ones_sq, preferred_element_type=F32)
        return jnp.max(row_sums, axis=0, keepdims=True)[:, 0:1]

    for hh in range(2):
        qf = q_ref[:, head(hh)].astype(F32)
        q_sq = lax.dot_general(ones8, (qf * qf).astype(BF16), nt, preferred_element_type=F32)
        b8 = jnp.sqrt(q_sq * max_key_norm_sq(hh)) * BOUND_SLACK
        for j in range(nq):
            bound_scr[hh, j] = b8[:, j * tq:(j + 1) * tq]
    for j in range(nq):
        worst = jnp.max(jnp.maximum(bound_scr[0, j], bound_scr[1, j]))
        safe_scr[j] = (worst < SHIFT_LIMIT).astype(jnp.int32)

    def scores(hh, q, c):
        return lax.dot_general(k_ref[c * tk:(c + 1) * tk, head(hh)], q, nt, preferred_element_type=F32)

    def weigh(hh, c, p, l8, acc):
        l8 = l8 + jnp.sum(p.reshape(tk // SUBLANES, SUBLANES, tq), axis=0)
        vt = vt_ref[hh * MLA_V:(hh + 1) * MLA_V, c * tk:(c + 1) * tk]
        return l8, acc + jnp.dot(vt, p.astype(BF16), preferred_element_type=F32)

    def q_tile(i, carry):
        rows = pl.ds(pl.multiple_of(i * tq, tq), tq)
        qs = [q_ref[rows, head(hh)] for hh in range(2)]
        bound = [bound_scr[hh, i, 0:1, :] for hh in range(2)]
        safe = safe_scr[i] != 0
        zeros = lambda n: jnp.zeros((n, tq), F32)

        @pl.when(safe)
        def _():
            l8, acc = [zeros(SUBLANES)] * 2, [zeros(MLA_V)] * 2
            order = [(c, hh) for c in range(nk) for hh in range(2)]
            st_next = scores(0, qs[0], 0)
            for n, (c, hh) in enumerate(order):
                st = st_next
                if n + 1 < len(order):
                    c2, h2 = order[n + 1]
                    st_next = scores(h2, qs[h2], c2)
                l8[hh], acc[hh] = weigh(hh, c, jnp.exp2(st - bound[hh]), l8[hh], acc[hh])
            out = [acc[hh] / jnp.sum(l8[hh], axis=0, keepdims=True) for hh in range(2)]
            o_ref[rows, :] = jnp.concatenate(out, axis=0).T.astype(BF16)

        @pl.when(jnp.logical_not(safe))
        def _():
            out = []
            for hh in range(2):
                m8 = jnp.full((SUBLANES, tq), NEG_BIG, F32)
                for c in range(nk):
                    st = scores(hh, qs[hh], c)
                    s_scr[c] = st
                    m8 = jnp.maximum(m8, jnp.max(st.reshape(tk // SUBLANES, SUBLANES, tq), axis=0))
                m = jnp.max(m8, axis=0, keepdims=True)
                l8, acc = zeros(SUBLANES), zeros(MLA_V)
                for c in range(nk):
                    l8, acc = weigh(hh, c, jnp.exp2(s_scr[c] - m), l8, acc)
                out.append(acc / jnp.sum(l8, axis=0, keepdims=True))
            o_ref[rows, :] = jnp.concatenate(out, axis=0).T.astype(BF16)

        return carry

    lax.fori_loop(0, nq, q_tile, 0)


def _attention(q2, k2, vt, batch, seq, tq=512, tk=512):
    tq = min(tq, seq)
    tk = min(tk, seq)
    t = batch * seq
    pairs = MLA_HEADS // 2
    return pl.pallas_call(
        functools.partial(_attn_kernel, tk=tk),
        scratch_shapes=[pltpu.VMEM((seq // tk, tk, tq), F32), pltpu.VMEM((2, seq // tq, SUBLANES, tq), F32),
                        pltpu.SMEM((seq // tq,), jnp.int32)],
        out_shape=jax.ShapeDtypeStruct((t, MLA_HEADS * MLA_V), BF16),
        grid=(batch, pairs),
        in_specs=[
            pl.BlockSpec((seq, 2 * HEAD_PAD), lambda b, p: (b, p)),
            pl.BlockSpec((seq, 2 * HEAD_PAD), lambda b, p: (b, p)),
            pl.BlockSpec((2 * MLA_V, seq), lambda b, p: (p, b)),
        ],
        out_specs=pl.BlockSpec((seq, 2 * MLA_V), lambda b, p: (b, p)),
        compiler_params=_cparams(("parallel", "parallel")),
        name="mla_attention",
    )(q2, k2, vt)


TOKEN_TILE_ROWS = D_MODEL // LANES


def _rows_to_token_tiles(ref, y):
    n = y.shape[0]
    for s in range(TOKEN_TILE_ROWS):
        ref[pl.ds(s, n, stride=TOKEN_TILE_ROWS), :] = y[:, s * LANES:(s + 1) * LANES]


def _token_tiles_to_rows(ref, n):
    return jnp.concatenate([ref[pl.ds(s, n, stride=TOKEN_TILE_ROWS), :] for s in range(TOKEN_TILE_ROWS)], axis=1)


def _merge_out_kernel(oa_ref, ob_ref, ga_ref, gb_ref, h_ref, w_ref, g_ref, b_ref, ot_ref):
    merged = (jax.nn.sigmoid(ga_ref[...].astype(F32)) * oa_ref[...].astype(F32)
              + jax.nn.sigmoid(gb_ref[...].astype(F32)) * ob_ref[...].astype(F32))
    mix = jnp.dot(merged.astype(BF16), w_ref[...], preferred_element_type=F32)
    y = _ln_rows(DEEPNORM_ALPHA * h_ref[...] + mix, g_ref[...], b_ref[...])
    _rows_to_token_tiles(ot_ref, y)


def _merge_out(o_gla, o_mla, proj, h, w_out, g, b, tm=512):
    t, d = h.shape
    row = lambda i: (i, 0)
    const = lambda i: (0, 0)
    return pl.pallas_call(
        _merge_out_kernel,
        out_shape=jax.ShapeDtypeStruct((t * TOKEN_TILE_ROWS, LANES), F32),
        grid=(t // tm,),
        in_specs=[
            pl.BlockSpec((tm, d), row),
            pl.BlockSpec((tm, d), row),
            pl.BlockSpec((tm, d), lambda i: (i, COL_GA // D_MODEL)),
            pl.BlockSpec((tm, d), lambda i: (i, COL_GB // D_MODEL)),
            pl.BlockSpec((tm, d), row),
            pl.BlockSpec((d, d), const),
            pl.BlockSpec((1, d), const),
            pl.BlockSpec((1, d), const),
        ],
        out_specs=pl.BlockSpec((tm * TOKEN_TILE_ROWS, LANES), row),
        compiler_params=_cparams(("parallel",)),
        name="merge_out_ln",
    )(o_gla, o_mla, proj, proj, h, w_out, g, b)


GRP_LANE = N_EXPERTS
META_W0, META_W1, META_E0, META_E1, META_R0, META_R1 = range(6)


def _router_kernel(x_ref, w_ref, b_ref, meta_ref, cnt_ref, carry_ref):
    @pl.when(pl.program_id(0) == 0)
    def _():
        carry_ref[...] = jnp.zeros_like(carry_ref)

    tm = x_ref.shape[0] // TOKEN_TILE_ROWS
    x_hi, x_lo = _split_bf16(_token_tiles_to_rows(x_ref, tm))
    w_hi, w_lo = _split_bf16(w_ref[...])
    logits = (jnp.dot(x_hi, w_hi, preferred_element_type=F32)
              + jnp.dot(x_lo, w_hi, preferred_element_type=F32)
              + jnp.dot(x_hi, w_lo, preferred_element_type=F32)) + b_ref[...]
    lane = lax.broadcasted_iota(jnp.int32, logits.shape, 1)
    is_grp = (lane >= GRP_LANE) & (lane < GRP_LANE + N_GROUPS)
    gl = jnp.where(is_grp, logits, NEG_BIG)
    gmax = jnp.max(gl, axis=1, keepdims=True)
    g_lane = jnp.min(jnp.where(gl == gmax, lane, 4 * LANES), axis=1, keepdims=True)
    g_w = 1.0 / jnp.sum(jnp.where(is_grp, jnp.exp(gl - gmax), 0.0), axis=1, keepdims=True)
    lo_lane = (g_lane - GRP_LANE) * EXPERTS_PER_GROUP
    in_grp = (lane >= lo_lane) & (lane < lo_lane + EXPERTS_PER_GROUP)
    el = jnp.where(in_grp, logits, NEG_BIG)
    v1 = jnp.max(el, axis=1, keepdims=True)
    i1 = jnp.min(jnp.where(el == v1, lane, 4 * LANES), axis=1, keepdims=True)
    el2 = jnp.where(lane == i1, NEG_BIG, el)
    v2 = jnp.max(el2, axis=1, keepdims=True)
    i2 = jnp.min(jnp.where(el2 == v2, lane, 4 * LANES), axis=1, keepdims=True)
    e2 = jnp.exp(v2 - v1)
    w1 = g_w / (1.0 + e2)
    w2 = g_w * e2 / (1.0 + e2)

    onehot = jnp.where(lane == i1, 1.0, jnp.where(lane == i2, 1.0, 0.0))
    r = lax.broadcasted_iota(jnp.int32, (tm, tm), 0)
    c = lax.broadcasted_iota(jnp.int32, (tm, tm), 1)
    earlier = (c < r).astype(BF16)
    base = carry_ref[...] + jnp.dot(earlier, onehot.astype(BF16), preferred_element_type=F32)
    rank1 = jnp.sum(jnp.where(lane == i1, base, 0.0), axis=1, keepdims=True)
    rank2 = jnp.sum(jnp.where(lane == i2, base, 0.0), axis=1, keepdims=True)
    meta = jnp.zeros(logits.shape, F32)
    for ln, val in ((META_W0, w1), (META_W1, w2), (META_E0, i1.astype(F32)), (META_E1, i2.astype(F32)),
                    (META_R0, rank1), (META_R1, rank2)):
        meta = jnp.where(lane == ln, val, meta)
    meta_ref[...] = meta
    carry_ref[...] += jnp.sum(onehot, axis=0, keepdims=True)
    cnt_ref[...] = carry_ref[...]


def _router(h_tiles, w_r, b_r, tm=512):
    t, d = h_tiles.shape[0] // TOKEN_TILE_ROWS, D_MODEL
    return pl.pallas_call(
        _router_kernel,
        out_shape=(jax.ShapeDtypeStruct((t, LANES), F32), jax.ShapeDtypeStruct((1, LANES), F32)),
        grid=(t // tm,),
        in_specs=[pl.BlockSpec((tm * TOKEN_TILE_ROWS, LANES), lambda i: (i, 0)),
                  pl.BlockSpec((d, LANES), lambda i: (0, 0)),
                  pl.BlockSpec((1, LANES), lambda i: (0, 0))],
        out_specs=(pl.BlockSpec((tm, LANES), lambda i: (i, 0)), pl.BlockSpec((1, LANES), lambda i: (0, 0))),
        scratch_shapes=[pltpu.VMEM((1, LANES), F32)],
        compiler_params=_cparams(("arbitrary",)),
        name="router",
    )(h_tiles, w_r, b_r)


EXPERT_ROW_TILE = 512
MOE_TOKENS = 512
TOP_K = 2
DMA_ISSUE_UNROLL = 16


def _tile_at(ref, first_row):
    return ref.at[pl.ds(pl.multiple_of(first_row, TOKEN_TILE_ROWS), TOKEN_TILE_ROWS)]


def _tile(ref, token):
    return _tile_at(ref, token * TOKEN_TILE_ROWS)


def _slots_kernel(meta_ref, offs_ref, d_ref):
    meta = meta_ref[...]
    offs = offs_ref[...]
    lane = lax.broadcasted_iota(jnp.int32, meta.shape, 1)

    def slot(e_lane, r_lane):
        e = meta[:, e_lane:e_lane + 1].astype(jnp.int32)
        return jnp.sum(jnp.where(lane == e, offs, 0.0), axis=1, keepdims=True) + meta[:, r_lane:r_lane + 1]

    both = jnp.where(lane == 0, slot(META_E0, META_R0), jnp.where(lane == 1, slot(META_E1, META_R1), 0.0))
    d_ref[0] = (both.T[:SUBLANES, :] * float(TOKEN_TILE_ROWS)).astype(jnp.int32)


def _slots(meta, offs, tt):
    t = meta.shape[0]
    return pl.pallas_call(
        _slots_kernel,
        out_shape=jax.ShapeDtypeStruct((t // tt, SUBLANES, tt), jnp.int32),
        grid=(t // tt,),
        in_specs=[pl.BlockSpec((tt, LANES), lambda i: (i, 0)), pl.BlockSpec((1, LANES), lambda i: (0, 0))],
        out_specs=pl.BlockSpec((1, SUBLANES, tt), lambda i: (i, 0, 0)),
        compiler_params=_cparams(("arbitrary",)),
        name="moe_slots",
    )(meta, offs)


def _zero_tile_kernel(lt_ref, xs_ref):
    del lt_ref
    xs_ref[...] = jnp.zeros_like(xs_ref)


def _zero_last_tiles(last_tile, n_rows):
    blk = EXPERT_ROW_TILE * TOKEN_TILE_ROWS
    grid_spec = pltpu.PrefetchScalarGridSpec(
        num_scalar_prefetch=1, grid=(N_EXPERTS,), in_specs=[],
        out_specs=pl.BlockSpec((blk, LANES), lambda e, lt: (lt[e], 0)))
    return pl.pallas_call(
        _zero_tile_kernel,
        out_shape=jax.ShapeDtypeStruct((n_rows * TOKEN_TILE_ROWS, LANES), F32),
        grid_spec=grid_spec,
        compiler_params=_cparams(("arbitrary",)),
        name="moe_zero_tiles",
    )(last_tile)


def _dispatch_kernel(d_ref, ht_ref, xs_init_hbm, xs_hbm, sem):
    del xs_init_hbm
    tt = d_ref.shape[-1]

    def issue(j, carry):
        src = _tile(ht_ref, j)
        for k in range(TOP_K):
            pltpu.make_async_copy(src, _tile_at(xs_hbm, d_ref[0, k, j]), sem).start(priority=k)
        return carry

    lax.fori_loop(0, tt, issue, 0, unroll=DMA_ISSUE_UNROLL)
    for k in range(TOP_K):
        pltpu.make_async_copy(ht_ref, xs_hbm.at[pl.ds(0, tt * TOKEN_TILE_ROWS)], sem).wait()


def _dispatch(h_tiles, slots, xs_init):
    tt = slots.shape[-1]
    t = h_tiles.shape[0] // TOKEN_TILE_ROWS
    return pl.pallas_call(
        _dispatch_kernel,
        out_shape=jax.ShapeDtypeStruct(xs_init.shape, F32),
        grid=(t // tt,),
        in_specs=[pl.BlockSpec((1, SUBLANES, tt), lambda i: (i, 0, 0), memory_space=pltpu.SMEM),
                  pl.BlockSpec((tt * TOKEN_TILE_ROWS, LANES), lambda i: (i, 0)),
                  pl.BlockSpec(memory_space=pl.ANY)],
        out_specs=pl.BlockSpec(memory_space=pl.ANY),
        scratch_shapes=[pltpu.SemaphoreType.DMA(())],
        input_output_aliases={2: 0},
        compiler_params=_cparams(("arbitrary",)),
        name="moe_dispatch",
    )(slots, h_tiles, xs_init)


def _expert_kernel(te_ref, nu_ref, xs_ref, wgu_ref, wd_ref, ys_ref):
    del te_ref
    i = pl.program_id(0)
    tm = EXPERT_ROW_TILE

    @pl.when(i < nu_ref[0])
    def _():
        x = _token_tiles_to_rows(xs_ref, tm).astype(BF16)
        gu = jnp.dot(x, wgu_ref[0], preferred_element_type=F32)
        gate = gu[:, :D_EXPERT]
        hid = gate * jax.nn.sigmoid(gate) * gu[:, D_EXPERT:]
        _rows_to_token_tiles(ys_ref, jnp.dot(hid.astype(BF16), wd_ref[0], preferred_element_type=F32))

    @pl.when(i >= nu_ref[0])
    def _():
        ys_ref[...] = jnp.zeros_like(ys_ref)


def _experts(xs, tile_expert, n_used, wgu, wd):
    d = D_MODEL
    tm = EXPERT_ROW_TILE
    blk = pl.BlockSpec((tm * TOKEN_TILE_ROWS, LANES), lambda i, te, nu: (i, 0))
    blk_in = pl.BlockSpec((tm * TOKEN_TILE_ROWS, LANES), lambda i, te, nu: (jnp.minimum(i, nu[0] - 1), 0))
    grid_spec = pltpu.PrefetchScalarGridSpec(
        num_scalar_prefetch=2,
        grid=(xs.shape[0] // (tm * TOKEN_TILE_ROWS),),
        in_specs=[blk_in,
                  pl.BlockSpec((1, d, 2 * D_EXPERT), lambda i, te, nu: (te[i], 0, 0)),
                  pl.BlockSpec((1, D_EXPERT, d), lambda i, te, nu: (te[i], 0, 0))],
        out_specs=blk,
    )
    return pl.pallas_call(
        _expert_kernel,
        out_shape=jax.ShapeDtypeStruct(xs.shape, F32),
        grid_spec=grid_spec,
        compiler_params=_cparams(("arbitrary",)),
        name="moe_experts",
    )(tile_expert, n_used, xs, wgu, wd)


def _combine_kernel(dc_ref, dn_ref, ys_hbm, meta_ref, ht_ref, g_ref, b_ref, o_ref, obf_ref, buf, sem):
    i = pl.program_id(0)
    n = pl.num_programs(0)
    tt = dc_ref.shape[-1]
    slot = lax.rem(i, 2)

    def gather(d_ref, s):
        def issue(j, carry):
            for k in range(TOP_K):
                pltpu.make_async_copy(_tile_at(ys_hbm, d_ref[0, k, j]), _tile(buf.at[s, k], j),
                                      sem.at[s]).start(priority=k)
            return carry
        lax.fori_loop(0, tt, issue, 0, unroll=DMA_ISSUE_UNROLL)

    @pl.when(i == 0)
    def _():
        gather(dc_ref, 0)

    @pl.when(i + 1 < n)
    def _():
        gather(dn_ref, 1 - slot)

    for k in range(TOP_K):
        pltpu.make_async_copy(ys_hbm.at[pl.ds(0, tt * TOKEN_TILE_ROWS)], buf.at[slot, k], sem.at[slot]).wait()
    meta = meta_ref[...]
    ffn = (meta[:, META_W0:META_W0 + 1] * _token_tiles_to_rows(buf.at[slot, 0], tt)
           + meta[:, META_W1:META_W1 + 1] * _token_tiles_to_rows(buf.at[slot, 1], tt))
    y = _ln_rows(DEEPNORM_ALPHA * _token_tiles_to_rows(ht_ref, tt) + ffn, g_ref[...], b_ref[...])
    o_ref[...] = y
    obf_ref[...] = y.astype(BF16)


def _combine(ys, slots, meta, h_tiles, g, b):
    t, d = h_tiles.shape[0] // TOKEN_TILE_ROWS, D_MODEL
    tt = slots.shape[-1]
    n = t // tt
    cur = pl.BlockSpec((1, SUBLANES, tt), lambda i: (i, 0, 0), memory_space=pltpu.SMEM)
    nxt = pl.BlockSpec((1, SUBLANES, tt), lambda i: (jnp.minimum(i + 1, n - 1), 0, 0), memory_space=pltpu.SMEM)
    row = lambda i: (i, 0)
    const = lambda i: (0, 0)
    return pl.pallas_call(
        _combine_kernel,
        out_shape=(jax.ShapeDtypeStruct((t, d), F32), jax.ShapeDtypeStruct((t, d), BF16)),
        grid=(n,),
        in_specs=[cur, nxt, pl.BlockSpec(memory_space=pl.ANY),
                  pl.BlockSpec((tt, LANES), row), pl.BlockSpec((tt * TOKEN_TILE_ROWS, LANES), row),
                  pl.BlockSpec((1, d), const), pl.BlockSpec((1, d), const)],
        out_specs=(pl.BlockSpec((tt, d), row), pl.BlockSpec((tt, d), row)),
        scratch_shapes=[pltpu.VMEM((2, TOP_K, tt * TOKEN_TILE_ROWS, LANES), F32), pltpu.SemaphoreType.DMA((2,))],
        compiler_params=_cparams(("arbitrary",)),
        name="moe_combine_ln",
    )(slots, slots, ys, meta, h_tiles, g, b)


def _segment_tables(counts, t):
    tm = EXPERT_ROW_TILE
    n_rows = TOP_K * t + N_EXPERTS * tm
    cnt = counts[0, :N_EXPERTS].astype(jnp.int32)
    tiles = (cnt + tm - 1) // tm
    ends = jnp.cumsum(tiles)
    offs = jnp.zeros((1, LANES), F32).at[0, :N_EXPERTS].set(((ends - tiles) * tm).astype(F32))
    tile_ids = jnp.arange(n_rows // tm, dtype=jnp.int32)
    tile_expert = jnp.minimum(jnp.sum((tile_ids[:, None] >= ends[None, :]).astype(jnp.int32), axis=1),
                              N_EXPERTS - 1)
    last_tile = jnp.maximum(ends - 1, 0).astype(jnp.int32)
    return n_rows, offs, tile_expert, ends[-1:].astype(jnp.int32), last_tile


def _sparse_moe(h_tiles, meta, counts, wgu, wd, g, b):
    t = h_tiles.shape[0] // TOKEN_TILE_ROWS
    n_rows, offs, tile_expert, n_used, last_tile = _segment_tables(counts, t)
    slots = _slots(meta, offs, min(MOE_TOKENS, t))
    xs = _dispatch(h_tiles, slots, _zero_last_tiles(last_tile, n_rows))
    ys = _experts(xs, tile_expert, n_used, wgu, wd)
    return _combine(ys, slots, meta, h_tiles, g, b)


def _rotate_half_cols(w):
    half = w.shape[-1] // 2
    return jnp.concatenate([-w[..., half:], w[..., :half]], axis=-1)


def _pack_input_proj(w_in, b_in):
    d = w_in.shape[0]
    offs = np.cumsum((0, GLA_QK_W, GLA_QK_W, GLA_V_W, GLA_V_W, GLA_GATE_RANK, GLA_GATE_RANK,
                      MLA_Q_RANK, MLA_KV_RANK, MLA_ROPE, D_MODEL, D_MODEL))

    def seg(i):
        return w_in[:, offs[i]:offs[i + 1]], b_in[offs[i]:offs[i + 1]]

    w = jnp.zeros((d, N_PROJ), F32)
    b = jnp.zeros((N_PROJ,), F32)

    def put(w, b, col, ws, bs):
        return w.at[:, col:col + ws.shape[1]].set(ws), b.at[col:col + ws.shape[1]].set(bs)

    for i, col in ((0, COL_GQ), (1, COL_GK), (2, COL_GV), (3, COL_GR), (9, COL_GA), (10, COL_GB),
                   (6, COL_CQ), (7, COL_CKV), (4, COL_SMALL), (5, COL_SMALL + GLA_GATE_RANK)):
        w, b = put(w, b, col, *seg(i))
    wkr, bkr = seg(8)
    w, b = put(w, b, COL_SMALL + KR_LANE, wkr, bkr)
    w, b = put(w, b, COL_SMALL2 + KR_LANE, _rotate_half_cols(wkr), _rotate_half_cols(bkr))
    return w.astype(BF16), b.reshape(1, N_PROJ)


def _pack_decay(wa2, ba, lane0):
    w = jnp.zeros((LANES, GLA_QK_W), F32).at[lane0:lane0 + GLA_GATE_RANK].set(wa2)
    return w.astype(BF16), ba.reshape(1, GLA_QK_W)


def _pack_mla(w_uq, w_ukv, q_norm_g, kv_norm_g):
    wq = w_uq.reshape(MLA_Q_RANK, MLA_HEADS, MLA_QK)
    zq = jnp.zeros((MLA_Q_RANK, MLA_HEADS, HEAD_PAD - MLA_QK), F32)
    wq_p = jnp.concatenate([wq, zq], axis=-1)

    def pad_rows(w):
        w = w.reshape(MLA_Q_RANK, MLA_HEADS * HEAD_PAD)
        return jnp.pad(w, ((0, CQ_PAD - MLA_Q_RANK), (0, 0))).astype(BF16)

    wkv = w_ukv.reshape(MLA_KV_RANK, MLA_HEADS, MLA_NOPE + MLA_V)
    wk_p = jnp.concatenate([wkv[..., :MLA_NOPE],
                            jnp.zeros((MLA_KV_RANK, MLA_HEADS, HEAD_PAD - MLA_NOPE), F32)], axis=-1)
    wk_p = wk_p.reshape(MLA_KV_RANK, MLA_HEADS * HEAD_PAD).astype(BF16)
    wv_p = wkv[..., MLA_NOPE:].reshape(MLA_KV_RANK, MLA_HEADS * MLA_V).T.astype(BF16)
    gq = jnp.pad(q_norm_g, (0, CQ_PAD - MLA_Q_RANK)).reshape(1, CQ_PAD)
    return pad_rows(wq_p), wk_p, wv_p, gq, kv_norm_g.reshape(1, MLA_KV_RANK)


def _rope_lane_table():
    inv = ROPE_BASE ** (-jnp.arange(0, MLA_ROPE, 2, dtype=F32) / MLA_ROPE)
    return jnp.tile(inv, ROPE_PACK).reshape(1, LANES)


def _pack_router(w_grp, b_grp, w_exp, b_exp):
    d = w_grp.shape[0]
    w = jnp.zeros((d, LANES), F32).at[:, :N_EXPERTS].set(w_exp).at[:, GRP_LANE:GRP_LANE + N_GROUPS].set(w_grp)
    b = jnp.zeros((LANES,), F32).at[:N_EXPERTS].set(b_exp).at[GRP_LANE:GRP_LANE + N_GROUPS].set(b_grp)
    return w, b.reshape(1, LANES)


def kernel(x, positions, ln_emb_g, ln_emb_b, w_in, b_in, gla_wa2_f, gla_ba_f, gla_wa2_b, gla_ba_b, gla_norm_g, mla_q_norm_g, mla_w_uq, mla_kv_norm_g, mla_w_ukv, w_out, ln1_g, ln1_b, w_grp, b_grp, w_exp, b_exp, w_gate, w_up, w_down, ln2_g, ln2_b):
    batch, seq, d = x.shape
    t = batch * seq
    pos = jnp.repeat(positions.reshape(t // ROPE_PACK, ROPE_PACK).astype(F32), MLA_ROPE // 2, axis=1)
    inv_lane = _rope_lane_table()
    for l in range(DEPTH):
        w_p, b_p = _pack_input_proj(w_in[l], b_in[l])
        if l == 0:
            h, proj = _input_proj(x.reshape(t, d), w_p, b_p, ln=(ln_emb_g, ln_emb_b))
        else:
            proj = _input_proj(hb, w_p, b_p)
        wa_f, ba_f = _pack_decay(gla_wa2_f[l], gla_ba_f[l], 0)
        wa_b, ba_b = _pack_decay(gla_wa2_b[l], gla_ba_b[l], GLA_GATE_RANK)
        o_f = _gla(proj, wa_f, ba_f, batch, seq, reverse=False)
        o_gla = _gla(proj, wa_b, ba_b, batch, seq, reverse=True, o_fwd=o_f,
                     norm_g=gla_norm_g[l].reshape(1, GLA_V_W))
        wq, wk, wv, gq, gkv = _pack_mla(mla_w_uq[l], mla_w_ukv[l], mla_q_norm_g[l], mla_kv_norm_g[l])
        q2, k2, v = _mla_prep(proj, pos, inv_lane, gq, gkv, wq, wk, wv)
        o_mla = _attention(q2, k2, v, batch, seq)
        h_tiles = _merge_out(o_gla, o_mla, proj, h, w_out[l].astype(BF16),
                           ln1_g[l].reshape(1, d), ln1_b[l].reshape(1, d))
        w_r, b_r = _pack_router(w_grp[l], b_grp[l], w_exp[l], b_exp[l])
        meta, counts = _router(h_tiles, w_r, b_r)
        wgu = jnp.concatenate([w_gate[l], w_up[l]], axis=-1).astype(BF16)
        h, hb = _sparse_moe(h_tiles, meta, counts, wgu, w_down[l].astype(BF16),
                            ln2_g[l].reshape(1, d), ln2_b[l].reshape(1, d))
    return h.reshape(batch, seq, d)
```

```python
import functools

import numpy as np
import jax
import jax.numpy as jnp
from jax import lax
from jax.experimental import pallas as pl
from jax.experimental.pallas import tpu as pltpu

F32 = jnp.float32
BF16 = jnp.bfloat16

D_MODEL = 1024
DEPTH = 2
GLA_HEADS = 4
GLA_DK = 128
GLA_DV = 256
GLA_GATE_RANK = 16
GLA_TAU = 16.0
MLA_HEADS = 16
MLA_NOPE = 64
MLA_ROPE = 32
MLA_V = 64
MLA_QK = MLA_NOPE + MLA_ROPE
MLA_Q_RANK = 384
MLA_KV_RANK = 128
ROPE_BASE = 10000.0
N_GROUPS = 8
EXPERTS_PER_GROUP = 4
N_EXPERTS = 32
D_EXPERT = 256
GLA_QK_W = GLA_HEADS * GLA_DK
GLA_V_W = GLA_HEADS * GLA_DV
DEEPNORM_ALPHA = (2.0 * DEPTH) ** 0.25
LN_EPS = 1e-5
RMS_EPS = 1e-6

LANES = 128
VMEM_LIMIT_BYTES = 56 * 1024 * 1024

COL_GQ = 0
COL_GK = 512
COL_GV = 1024
COL_GR = 2048
COL_GA = 3072
COL_GB = 4096
COL_CQ = 5120
CQ_PAD = 512
COL_SMALL = 5632
COL_SMALL2 = 5760
COL_CKV = 5888
N_PROJ = 6144
HEAD_PAD = 128
KR_LANE = 64

GLA_CHUNK = 128
NEG_BIG = -1e30


def _cparams(sem):
    return pltpu.CompilerParams(dimension_semantics=sem, vmem_limit_bytes=VMEM_LIMIT_BYTES)


def _ln_rows(x, g, b):
    mu = jnp.mean(x, axis=-1, keepdims=True)
    xc = x - mu
    var = jnp.mean(xc * xc, axis=-1, keepdims=True)
    return xc * lax.rsqrt(var + LN_EPS) * g + b


PROJ_COL_CHUNK = 512


def _proj_kernel(*refs, embed_ln):
    if embed_ln:
        x_ref, g_ref, beta_ref, w_ref, b_ref, h_ref, o_ref = refs
        h = _ln_rows(x_ref[...], g_ref[...], beta_ref[...])
        h_ref[...] = h
        x = h.astype(BF16)
    else:
        x_ref, w_ref, b_ref, o_ref = refs
        x = x_ref[...]
    for c in range(N_PROJ // PROJ_COL_CHUNK):
        sl = slice(c * PROJ_COL_CHUNK, (c + 1) * PROJ_COL_CHUNK)
        acc = jnp.dot(x, w_ref[:, sl], preferred_element_type=F32) + b_ref[:, sl]
        o_ref[:, sl] = acc.astype(o_ref.dtype)


def _input_proj(x, w, b, ln=None, tm=512):
    t, d = x.shape
    row = lambda i: (i, 0)
    const = lambda i: (0, 0)
    vec = pl.BlockSpec((1, d), const)
    in_specs = [pl.BlockSpec((tm, d), row)] + ([vec, vec] if ln else []) + [
        pl.BlockSpec((d, N_PROJ), const), pl.BlockSpec((1, N_PROJ), const)]
    proj_shape = jax.ShapeDtypeStruct((t, N_PROJ), BF16)
    proj_spec = pl.BlockSpec((tm, N_PROJ), row)
    args = (x,) + (tuple(v.reshape(1, d) for v in ln) if ln else ()) + (w, b)
    return pl.pallas_call(
        functools.partial(_proj_kernel, embed_ln=ln is not None),
        out_shape=(jax.ShapeDtypeStruct((t, d), F32), proj_shape) if ln else proj_shape,
        grid=(t // tm,),
        in_specs=in_specs,
        out_specs=(pl.BlockSpec((tm, d), row), proj_spec) if ln else proj_spec,
        compiler_params=_cparams(("parallel",)),
        name="ln_input_proj" if ln else "input_proj",
    )(*args)


def _log_sigmoid(x):
    return jnp.minimum(x, 0.0) - jnp.log(1.0 + jnp.exp(-jnp.abs(x)))


def _split_bf16(x):
    hi = x.astype(BF16)
    lo = (x - hi.astype(F32)).astype(BF16)
    return hi, lo


def _gla_kernel(*refs, reverse, n_chunks):
    if reverse:
        (q_ref, k_ref, v_ref, z_ref, wa_ref, ba_ref, of_ref, gr_ref, ng_ref, o_ref, state_ref) = refs
    else:
        (q_ref, k_ref, v_ref, z_ref, wa_ref, ba_ref, o_ref, state_ref) = refs
    c_len = GLA_CHUNK

    @pl.when(pl.program_id(1) == 0)
    def _():
        state_ref[...] = jnp.zeros_like(state_ref)

    row = lax.broadcasted_iota(jnp.int32, (c_len, c_len), 0)
    col = lax.broadcasted_iota(jnp.int32, (c_len, c_len), 1)
    if reverse:
        tri = (col >= row).astype(BF16)
        keep = col > row
        last = 0
    else:
        tri = (col <= row).astype(BF16)
        keep = col <= row
        last = c_len - 1

    seqs = range(q_ref.shape[0])

    def decays(bi, rows):
        z = z_ref[bi, rows, :]
        la = _log_sigmoid(jnp.dot(z, wa_ref[...], preferred_element_type=F32) + ba_ref[...]) * (1.0 / GLA_TAU)
        la_hi, la_lo = _split_bf16(la)
        b_all = (jnp.dot(tri, la_hi, preferred_element_type=F32)
                 + jnp.dot(tri, la_lo, preferred_element_type=F32))
        b_last = b_all[last:last + 1, :]
        q = q_ref[bi, rows, :].astype(F32)
        k = k_ref[bi, rows, :].astype(F32)
        qd_all = (q * (jnp.exp(b_all) * (GLA_DK ** -0.5))).astype(BF16)
        kinv_all = (k * jnp.exp(-b_all)).astype(BF16)
        kend_all = k * jnp.exp(b_last - b_all)
        dec_all = jnp.broadcast_to(jnp.exp(b_last), (c_len, GLA_QK_W))
        return qd_all, kinv_all, kend_all, dec_all

    def chunk_rows(cc):
        c = (n_chunks - 1 - cc) if reverse else cc
        return slice(c * c_len, (c + 1) * c_len)

    pre_next = [decays(bi, chunk_rows(0)) for bi in seqs]
    for cc in range(n_chunks):
        rows = chunk_rows(cc)
        pre = pre_next
        if cc + 1 < n_chunks:
            pre_next = [decays(bi, chunk_rows(cc + 1)) for bi in seqs]
        for h in range(GLA_HEADS):
            ks = slice(h * GLA_DK, (h + 1) * GLA_DK)
            vs = slice(h * GLA_DV, (h + 1) * GLA_DV)
            v = [v_ref[bi, rows, vs] for bi in seqs]
            scores = [lax.dot_general(pre[bi][0][:, ks], pre[bi][1][:, ks], (((1,), (1,)), ((), ())),
                                      preferred_element_type=F32) for bi in seqs]
            state = [state_ref[bi, h] for bi in seqs]
            o = [jnp.dot(jnp.where(keep, scores[bi], 0.0).astype(BF16), v[bi], preferred_element_type=F32)
                 + jnp.dot(pre[bi][0][:, ks], state[bi].astype(BF16), preferred_element_type=F32) for bi in seqs]
            for bi in seqs:
                kend_t = pre[bi][2][:, ks].T.astype(BF16)
                dec_t = pre[bi][3][:, ks].T
                dec = jnp.concatenate([dec_t, dec_t], axis=1)
                state_ref[bi, h] = dec * state[bi] + jnp.dot(kend_t, v[bi], preferred_element_type=F32)
            for bi in seqs:
                ob = o[bi]
                if reverse:
                    ob = ob + of_ref[bi, rows, vs]
                    ms = jnp.mean(ob * ob, axis=-1, keepdims=True)
                    ob = ob * lax.rsqrt(ms + RMS_EPS) * ng_ref[:, vs]
                    g = gr_ref[bi, rows, vs].astype(F32)
                    ob = ob * (g * jax.nn.sigmoid(g))
                o_ref[bi, rows, vs] = ob.astype(o_ref.dtype)


GLA_SEQS_PER_STEP = 2


def _gla(proj, wa, ba, batch, seq, *, reverse, o_fwd=None, norm_g=None, ts=512):
    ts = min(ts, seq)
    nblk = seq // ts
    nb = GLA_SEQS_PER_STEP if batch % GLA_SEQS_PER_STEP == 0 else 1
    proj3 = proj.reshape(batch, seq, N_PROJ)

    def blk(i):
        return (nblk - 1 - i) if reverse else i

    def cols(width, col):
        return pl.BlockSpec((nb, ts, width), lambda b, i: (b, blk(i), col // width))

    in_specs = [cols(GLA_QK_W, COL_GQ), cols(GLA_QK_W, COL_GK), cols(GLA_V_W, COL_GV), cols(LANES, COL_SMALL),
                pl.BlockSpec((LANES, GLA_QK_W), lambda b, i: (0, 0)),
                pl.BlockSpec((1, GLA_QK_W), lambda b, i: (0, 0))]
    args = [proj3, proj3, proj3, proj3, wa, ba]
    if reverse:
        in_specs += [cols(GLA_V_W, 0), cols(GLA_V_W, COL_GR), pl.BlockSpec((1, GLA_V_W), lambda b, i: (0, 0))]
        args += [o_fwd.reshape(batch, seq, GLA_V_W), proj3, norm_g]
    out = pl.pallas_call(
        functools.partial(_gla_kernel, reverse=reverse, n_chunks=ts // GLA_CHUNK),
        out_shape=jax.ShapeDtypeStruct((batch, seq, GLA_V_W), BF16 if reverse else F32),
        grid=(batch // nb, nblk),
        in_specs=in_specs,
        out_specs=cols(GLA_V_W, 0),
        scratch_shapes=[pltpu.VMEM((nb, GLA_HEADS, GLA_DK, GLA_DV), F32)],
        compiler_params=_cparams(("parallel", "arbitrary")),
        name="gla_bwd" if reverse else "gla_fwd",
    )(*args)
    return out.reshape(batch * seq, GLA_V_W)


MLA_COL_CHUNK = 512
ROPE_PACK = LANES // (MLA_ROPE // 2)


def _mla_prep_kernel(cq_ref, ckv_ref, sm_ref, sm2_ref, pos_ref, inv_ref, gq_ref, gkv_ref,
                     wq_ref, wk_ref, wv_ref, q_ref, k_ref, v_ref):
    cq = cq_ref[...].astype(F32)
    msq = jnp.sum(cq * cq, axis=-1, keepdims=True) * (1.0 / MLA_Q_RANK)
    cqn = (cq * lax.rsqrt(msq + RMS_EPS) * gq_ref[...]).astype(BF16)
    ckv = ckv_ref[...].astype(F32)
    mskv = jnp.mean(ckv * ckv, axis=-1, keepdims=True)
    ckvn = (ckv * lax.rsqrt(mskv + RMS_EPS) * gkv_ref[...]).astype(BF16)

    ts = cq_ref.shape[0]
    ang = pos_ref[...] * inv_ref[...]
    tok = lax.broadcasted_iota(jnp.int32, (ts, ts // ROPE_PACK), 0)
    grp = lax.broadcasted_iota(jnp.int32, (ts, ts // ROPE_PACK), 1)
    to_rows = (tok // ROPE_PACK == grp).astype(BF16)
    src = lax.broadcasted_iota(jnp.int32, (LANES, LANES), 0)
    dst = lax.broadcasted_iota(jnp.int32, (LANES, LANES), 1)
    half = MLA_ROPE // 2
    to_lanes = ((dst >= KR_LANE) & (dst < KR_LANE + MLA_ROPE) & ((dst - KR_LANE) % half == src % half)).astype(BF16)
    lane = lax.broadcasted_iota(jnp.int32, (ts, LANES), 1)
    own = lane // half == lax.broadcasted_iota(jnp.int32, (ts, LANES), 0) % ROPE_PACK

    def spread(packed):
        out = 0.0
        for part in _split_bf16(packed):
            rows = jnp.dot(to_rows, part, preferred_element_type=F32)
            rows = jnp.where(own, rows, 0.0).astype(BF16)
            out = out + jnp.dot(rows, to_lanes, preferred_element_type=F32)
        return out

    cos = jnp.where(lane < KR_LANE, 1.0, spread(jnp.cos(ang)))
    sin = spread(jnp.sin(ang))
    kr = jnp.where(lane >= KR_LANE, sm_ref[...].astype(F32), 0.0)
    kr = kr * cos + sm2_ref[...].astype(F32) * sin

    heads_per_chunk = MLA_COL_CHUNK // HEAD_PAD
    cos_t = jnp.concatenate([cos] * heads_per_chunk, axis=1)
    sin_t = jnp.concatenate([sin] * heads_per_chunk, axis=1)
    kr_t = jnp.concatenate([kr] * heads_per_chunk, axis=1)
    qscale = (MLA_QK ** -0.5) * float(np.log2(np.e))
    chunk_lane = lax.broadcasted_iota(jnp.int32, (ts, MLA_COL_CHUNK), 1) % HEAD_PAD
    first_half = chunk_lane < KR_LANE + half
    for c in range(MLA_HEADS * HEAD_PAD // MLA_COL_CHUNK):
        sl = slice(c * MLA_COL_CHUNK, (c + 1) * MLA_COL_CHUNK)
        q = jnp.dot(cqn, wq_ref[:, sl], preferred_element_type=F32)
        qr = jnp.where(first_half, -pltpu.roll(q, MLA_COL_CHUNK - half, axis=1), pltpu.roll(q, half, axis=1))
        q_ref[:, sl] = ((q * cos_t + qr * sin_t) * qscale).astype(BF16)
        k = jnp.dot(ckvn, wk_ref[:, sl], preferred_element_type=F32)
        k_ref[:, sl] = (k + kr_t).astype(BF16)
    v_ref[...] = lax.dot_general(wv_ref[...], ckvn, (((1,), (1,)), ((), ())),
                                 preferred_element_type=F32).astype(BF16)


def _mla_prep(proj, pos, inv_lane, gq, gkv, wq, wk, wv, ts=512):
    t = proj.shape[0]
    hp = MLA_HEADS * HEAD_PAD
    vw = MLA_HEADS * MLA_V
    const = lambda i: (0, 0)
    return pl.pallas_call(
        _mla_prep_kernel,
        out_shape=(jax.ShapeDtypeStruct((t, hp), BF16), jax.ShapeDtypeStruct((t, hp), BF16),
                   jax.ShapeDtypeStruct((vw, t), BF16)),
        grid=(t // ts,),
        in_specs=[
            pl.BlockSpec((ts, CQ_PAD), lambda i: (i, COL_CQ // CQ_PAD)),
            pl.BlockSpec((ts, LANES), lambda i: (i, COL_CKV // LANES)),
            pl.BlockSpec((ts, LANES), lambda i: (i, COL_SMALL // LANES)),
            pl.BlockSpec((ts, LANES), lambda i: (i, COL_SMALL2 // LANES)),
            pl.BlockSpec((ts // ROPE_PACK, LANES), lambda i: (i, 0)),
            pl.BlockSpec((1, LANES), const),
            pl.BlockSpec((1, CQ_PAD), const),
            pl.BlockSpec((1, MLA_KV_RANK), const),
            pl.BlockSpec((CQ_PAD, hp), const),
            pl.BlockSpec((MLA_KV_RANK, hp), const),
            pl.BlockSpec((vw, MLA_KV_RANK), const),
        ],
        out_specs=(pl.BlockSpec((ts, hp), lambda i: (i, 0)), pl.BlockSpec((ts, hp), lambda i: (i, 0)),
                   pl.BlockSpec((vw, ts), lambda i: (0, i))),
        compiler_params=_cparams(("parallel",)),
        name="mla_prep",
    )(proj, proj, proj, proj, pos, inv_lane, gq, gkv, wq, wk, wv)


SUBLANES = 8


BOUND_SLACK = 1.01
SHIFT_LIMIT = 48.0


def _attn_kernel(q_ref, k_ref, vt_ref, o_ref, s_scr, bound_scr, safe_scr, *, tk):
    seq = k_ref.shape[0]
    tq = s_scr.shape[-1]
    nk = seq // tk
    nq = seq // tq
    nt = (((1,), (1,)), ((), ()))
    head = lambda hh: slice(hh * HEAD_PAD, (hh + 1) * HEAD_PAD)
    ones8 = jnp.ones((SUBLANES, HEAD_PAD), BF16)
    ones_sq = jnp.ones((HEAD_PAD, HEAD_PAD), BF16)

    def max_key_norm_sq(hh):
        k = k_ref[:, head(hh)].astype(F32)
        row_sums = jnp.dot((k * k).astype(BF16), ones_sq, preferred_element_type=F32)
        return jnp.max(row_sums, axis=0, keepdims=True)[:, 0:1]

    for hh in range(2):
        qf = q_ref[:, head(hh)].astype(F32)
        q_sq = lax.dot_general(ones8, (qf * qf).astype(BF16), nt, preferred_element_type=F32)
        b8 = jnp.sqrt(q_sq * max_key_norm_sq(hh)) * BOUND_SLACK
        for j in range(nq):
            bound_scr[hh, j] = b8[:, j * tq:(j + 1) * tq]
    for j in range(nq):
        worst = jnp.max(jnp.maximum(bound_scr[0, j], bound_scr[1, j]))
        safe_scr[j] = (worst < SHIFT_LIMIT).astype(jnp.int32)

    def scores(hh, q, c):
        return lax.dot_general(k_ref[c * tk:(c + 1) * tk, head(hh)], q, nt, preferred_element_type=F32)

    def weigh(hh, c, p, l8, acc):
        l8 = l8 + jnp.sum(p.reshape(tk // SUBLANES, SUBLANES, tq), axis=0)
        vt = vt_ref[hh * MLA_V:(hh + 1) * MLA_V, c * tk:(c + 1) * tk]
        return l8, acc + jnp.dot(vt, p.astype(BF16), preferred_element_type=F32)

    def q_tile(i, carry):
        rows = pl.ds(pl.multiple_of(i * tq, tq), tq)
        qs = [q_ref[rows, head(hh)] for hh in range(2)]
        bound = [bound_scr[hh, i, 0:1, :] for hh in range(2)]
        safe = safe_scr[i] != 0
        zeros = lambda n: jnp.zeros((n, tq), F32)

        @pl.when(safe)
        def _():
            l8, acc = [zeros(SUBLANES)] * 2, [zeros(MLA_V)] * 2
            st_next = [scores(hh, qs[hh], 0) for hh in range(2)]
            for c in range(nk):
                st = st_next
                if c + 1 < nk:
                    st_next = [scores(hh, qs[hh], c + 1) for hh in range(2)]
                for hh in range(2):
                    l8[hh], acc[hh] = weigh(hh, c, jnp.exp2(st[hh] - bound[hh]), l8[hh], acc[hh])
            out = [acc[hh] / jnp.sum(l8[hh], axis=0, keepdims=True) for hh in range(2)]
            o_ref[rows, :] = jnp.concatenate(out, axis=0).T.astype(BF16)

        @pl.when(jnp.logical_not(safe))
        def _():
            out = []
            for hh in range(2):
                m8 = jnp.full((SUBLANES, tq), NEG_BIG, F32)
                for c in range(nk):
                    st = scores(hh, qs[hh], c)
                    s_scr[c] = st
                    m8 = jnp.maximum(m8, jnp.max(st.reshape(tk // SUBLANES, SUBLANES, tq), axis=0))
                m = jnp.max(m8, axis=0, keepdims=True)
                l8, acc = zeros(SUBLANES), zeros(MLA_V)
                for c in range(nk):
                    l8, acc = weigh(hh, c, jnp.exp2(s_scr[c] - m), l8, acc)
                out.append(acc / jnp.sum(l8, axis=0, keepdims=True))
            o_ref[rows, :] = jnp.concatenate(out, axis=0).T.astype(BF16)

        return carry

    lax.fori_loop(0, nq, q_tile, 0)


def _attention(q2, k2, vt, batch, seq, tq=512, tk=512):
    tq = min(tq, seq)
    tk = min(tk, seq)
    t = batch * seq
    pairs = MLA_HEADS // 2
    return pl.pallas_call(
        functools.partial(_attn_kernel, tk=tk),
        scratch_shapes=[pltpu.VMEM((seq // tk, tk, tq), F32), pltpu.VMEM((2, seq // tq, SUBLANES, tq), F32),
                        pltpu.SMEM((seq // tq,), jnp.int32)],
        out_shape=jax.ShapeDtypeStruct((t, MLA_HEADS * MLA_V), BF16),
        grid=(batch, pairs),
        in_specs=[
            pl.BlockSpec((seq, 2 * HEAD_PAD), lambda b, p: (b, p)),
            pl.BlockSpec((seq, 2 * HEAD_PAD), lambda b, p: (b, p)),
            pl.BlockSpec((2 * MLA_V, seq), lambda b, p: (p, b)),
        ],
        out_specs=pl.BlockSpec((seq, 2 * MLA_V), lambda b, p: (b, p)),
        compiler_params=_cparams(("parallel", "parallel")),
        name="mla_attention",
    )(q2, k2, vt)


TOKEN_TILE_ROWS = D_MODEL // LANES


def _rows_to_token_tiles(ref, y):
    n = y.shape[0]
    for s in range(TOKEN_TILE_ROWS):
        ref[pl.ds(s, n, stride=TOKEN_TILE_ROWS), :] = y[:, s * LANES:(s + 1) * LANES]


def _token_tiles_to_rows(ref, n):
    return jnp.concatenate([ref[pl.ds(s, n, stride=TOKEN_TILE_ROWS), :] for s in range(TOKEN_TILE_ROWS)], axis=1)


def _merge_out_kernel(oa_ref, ob_ref, ga_ref, gb_ref, h_ref, w_ref, g_ref, b_ref, ot_ref):
    merged = (jax.nn.sigmoid(ga_ref[...].astype(F32)) * oa_ref[...].astype(F32)
              + jax.nn.sigmoid(gb_ref[...].astype(F32)) * ob_ref[...].astype(F32))
    mix = jnp.dot(merged.astype(BF16), w_ref[...], preferred_element_type=F32)
    y = _ln_rows(DEEPNORM_ALPHA * h_ref[...] + mix, g_ref[...], b_ref[...])
    _rows_to_token_tiles(ot_ref, y)


def _merge_out(o_gla, o_mla, proj, h, w_out, g, b, tm=512):
    t, d = h.shape
    row = lambda i: (i, 0)
    const = lambda i: (0, 0)
    return pl.pallas_call(
        _merge_out_kernel,
        out_shape=jax.ShapeDtypeStruct((t * TOKEN_TILE_ROWS, LANES), F32),
        grid=(t // tm,),
        in_specs=[
            pl.BlockSpec((tm, d), row),
            pl.BlockSpec((tm, d), row),
            pl.BlockSpec((tm, d), lambda i: (i, COL_GA // D_MODEL)),
            pl.BlockSpec((tm, d), lambda i: (i, COL_GB // D_MODEL)),
            pl.BlockSpec((tm, d), row),
            pl.BlockSpec((d, d), const),
            pl.BlockSpec((1, d), const),
            pl.BlockSpec((1, d), const),
        ],
        out_specs=pl.BlockSpec((tm * TOKEN_TILE_ROWS, LANES), row),
        compiler_params=_cparams(("parallel",)),
        name="merge_out_ln",
    )(o_gla, o_mla, proj, proj, h, w_out, g, b)


GRP_LANE = N_EXPERTS
META_W0, META_W1, META_E0, META_E1, META_R0, META_R1 = range(6)


def _router_kernel(x_ref, w_ref, b_ref, meta_ref, cnt_ref, carry_ref):
    @pl.when(pl.program_id(0) == 0)
    def _():
        carry_ref[...] = jnp.zeros_like(carry_ref)

    tm = x_ref.shape[0] // TOKEN_TILE_ROWS
    x_hi, x_lo = _split_bf16(_token_tiles_to_rows(x_ref, tm))
    w_hi, w_lo = _split_bf16(w_ref[...])
    logits = (jnp.dot(x_hi, w_hi, preferred_element_type=F32)
              + jnp.dot(x_lo, w_hi, preferred_element_type=F32)
              + jnp.dot(x_hi, w_lo, preferred_element_type=F32)) + b_ref[...]
    lane = lax.broadcasted_iota(jnp.int32, logits.shape, 1)
    is_grp = (lane >= GRP_LANE) & (lane < GRP_LANE + N_GROUPS)
    gl = jnp.where(is_grp, logits, NEG_BIG)
    gmax = jnp.max(gl, axis=1, keepdims=True)
    g_lane = jnp.min(jnp.where(gl == gmax, lane, 4 * LANES), axis=1, keepdims=True)
    g_w = 1.0 / jnp.sum(jnp.where(is_grp, jnp.exp(gl - gmax), 0.0), axis=1, keepdims=True)
    lo_lane = (g_lane - GRP_LANE) * EXPERTS_PER_GROUP
    in_grp = (lane >= lo_lane) & (lane < lo_lane + EXPERTS_PER_GROUP)
    el = jnp.where(in_grp, logits, NEG_BIG)
    v1 = jnp.max(el, axis=1, keepdims=True)
    i1 = jnp.min(jnp.where(el == v1, lane, 4 * LANES), axis=1, keepdims=True)
    el2 = jnp.where(lane == i1, NEG_BIG, el)
    v2 = jnp.max(el2, axis=1, keepdims=True)
    i2 = jnp.min(jnp.where(el2 == v2, lane, 4 * LANES), axis=1, keepdims=True)
    e2 = jnp.exp(v2 - v1)
    w1 = g_w / (1.0 + e2)
    w2 = g_w * e2 / (1.0 + e2)

    onehot = jnp.where(lane == i1, 1.0, jnp.where(lane == i2, 1.0, 0.0))
    r = lax.broadcasted_iota(jnp.int32, (tm, tm), 0)
    c = lax.broadcasted_iota(jnp.int32, (tm, tm), 1)
    earlier = (c < r).astype(BF16)
    base = carry_ref[...] + jnp.dot(earlier, onehot.astype(BF16), preferred_element_type=F32)
    rank1 = jnp.sum(jnp.where(lane == i1, base, 0.0), axis=1, keepdims=True)
    rank2 = jnp.sum(jnp.where(lane == i2, base, 0.0), axis=1, keepdims=True)
    meta = jnp.zeros(logits.shape, F32)
    for ln, val in ((META_W0, w1), (META_W1, w2), (META_E0, i1.astype(F32)), (META_E1, i2.astype(F32)),
                    (META_R0, rank1), (META_R1, rank2)):
        meta = jnp.where(lane == ln, val, meta)
    meta_ref[...] = meta
    carry_ref[...] += jnp.sum(onehot, axis=0, keepdims=True)
    cnt_ref[...] = carry_ref[...]


def _router(h_tiles, w_r, b_r, tm=512):
    t, d = h_tiles.shape[0] // TOKEN_TILE_ROWS, D_MODEL
    return pl.pallas_call(
        _router_kernel,
        out_shape=(jax.ShapeDtypeStruct((t, LANES), F32), jax.ShapeDtypeStruct((1, LANES), F32)),
        grid=(t // tm,),
        in_specs=[pl.BlockSpec((tm * TOKEN_TILE_ROWS, LANES), lambda i: (i, 0)),
                  pl.BlockSpec((d, LANES), lambda i: (0, 0)),
                  pl.BlockSpec((1, LANES), lambda i: (0, 0))],
        out_specs=(pl.BlockSpec((tm, LANES), lambda i: (i, 0)), pl.BlockSpec((1, LANES), lambda i: (0, 0))),
        scratch_shapes=[pltpu.VMEM((1, LANES), F32)],
        compiler_params=_cparams(("arbitrary",)),
        name="router",
    )(h_tiles, w_r, b_r)


EXPERT_ROW_TILE = 512
MOE_TOKENS = 512
TOP_K = 2
DMA_ISSUE_UNROLL = 16


def _tile_at(ref, first_row):
    return ref.at[pl.ds(pl.multiple_of(first_row, TOKEN_TILE_ROWS), TOKEN_TILE_ROWS)]


def _tile(ref, token):
    return _tile_at(ref, token * TOKEN_TILE_ROWS)


def _slots_kernel(meta_ref, offs_ref, d_ref):
    meta = meta_ref[...]
    offs = offs_ref[...]
    lane = lax.broadcasted_iota(jnp.int32, meta.shape, 1)

    def slot(e_lane, r_lane):
        e = meta[:, e_lane:e_lane + 1].astype(jnp.int32)
        return jnp.sum(jnp.where(lane == e, offs, 0.0), axis=1, keepdims=True) + meta[:, r_lane:r_lane + 1]

    both = jnp.where(lane == 0, slot(META_E0, META_R0), jnp.where(lane == 1, slot(META_E1, META_R1), 0.0))
    d_ref[0] = (both.T[:SUBLANES, :] * float(TOKEN_TILE_ROWS)).astype(jnp.int32)


def _slots(meta, offs, tt):
    t = meta.shape[0]
    return pl.pallas_call(
        _slots_kernel,
        out_shape=jax.ShapeDtypeStruct((t // tt, SUBLANES, tt), jnp.int32),
        grid=(t // tt,),
        in_specs=[pl.BlockSpec((tt, LANES), lambda i: (i, 0)), pl.BlockSpec((1, LANES), lambda i: (0, 0))],
        out_specs=pl.BlockSpec((1, SUBLANES, tt), lambda i: (i, 0, 0)),
        compiler_params=_cparams(("arbitrary",)),
        name="moe_slots",
    )(meta, offs)


def _zero_tile_kernel(lt_ref, xs_ref):
    del lt_ref
    xs_ref[...] = jnp.zeros_like(xs_ref)


def _zero_last_tiles(last_tile, n_rows):
    blk = EXPERT_ROW_TILE * TOKEN_TILE_ROWS
    grid_spec = pltpu.PrefetchScalarGridSpec(
        num_scalar_prefetch=1, grid=(N_EXPERTS,), in_specs=[],
        out_specs=pl.BlockSpec((blk, LANES), lambda e, lt: (lt[e], 0)))
    return pl.pallas_call(
        _zero_tile_kernel,
        out_shape=jax.ShapeDtypeStruct((n_rows * TOKEN_TILE_ROWS, LANES), F32),
        grid_spec=grid_spec,
        compiler_params=_cparams(("arbitrary",)),
        name="moe_zero_tiles",
    )(last_tile)


def _dispatch_kernel(d_ref, ht_ref, xs_init_hbm, xs_hbm, sem):
    del xs_init_hbm
    tt = d_ref.shape[-1]

    def issue(j, carry):
        src = _tile(ht_ref, j)
        for k in range(TOP_K):
            pltpu.make_async_copy(src, _tile_at(xs_hbm, d_ref[0, k, j]), sem).start(priority=k)
        return carry

    lax.fori_loop(0, tt, issue, 0, unroll=DMA_ISSUE_UNROLL)
    for k in range(TOP_K):
        pltpu.make_async_copy(ht_ref, xs_hbm.at[pl.ds(0, tt * TOKEN_TILE_ROWS)], sem).wait()


def _dispatch(h_tiles, slots, xs_init):
    tt = slots.shape[-1]
    t = h_tiles.shape[0] // TOKEN_TILE_ROWS
    return pl.pallas_call(
        _dispatch_kernel,
        out_shape=jax.ShapeDtypeStruct(xs_init.shape, F32),
        grid=(t // tt,),
        in_specs=[pl.BlockSpec((1, SUBLANES, tt), lambda i: (i, 0, 0), memory_space=pltpu.SMEM),
                  pl.BlockSpec((tt * TOKEN_TILE_ROWS, LANES), lambda i: (i, 0)),
                  pl.BlockSpec(memory_space=pl.ANY)],
        out_specs=pl.BlockSpec(memory_space=pl.ANY),
        scratch_shapes=[pltpu.SemaphoreType.DMA(())],
        input_output_aliases={2: 0},
        compiler_params=_cparams(("arbitrary",)),
        name="moe_dispatch",
    )(slots, h_tiles, xs_init)


def _expert_kernel(te_ref, nu_ref, xs_ref, wgu_ref, wd_ref, ys_ref):
    del te_ref
    i = pl.program_id(0)
    tm = EXPERT_ROW_TILE

    @pl.when(i < nu_ref[0])
    def _():
        x = _token_tiles_to_rows(xs_ref, tm).astype(BF16)
        gu = jnp.dot(x, wgu_ref[0], preferred_element_type=F32)
        gate = gu[:, :D_EXPERT]
        hid = gate * jax.nn.sigmoid(gate) * gu[:, D_EXPERT:]
        _rows_to_token_tiles(ys_ref, jnp.dot(hid.astype(BF16), wd_ref[0], preferred_element_type=F32))

    @pl.when(i >= nu_ref[0])
    def _():
        ys_ref[...] = jnp.zeros_like(ys_ref)


def _experts(xs, tile_expert, n_used, wgu, wd):
    d = D_MODEL
    tm = EXPERT_ROW_TILE
    blk = pl.BlockSpec((tm * TOKEN_TILE_ROWS, LANES), lambda i, te, nu: (i, 0))
    blk_in = pl.BlockSpec((tm * TOKEN_TILE_ROWS, LANES), lambda i, te, nu: (jnp.minimum(i, nu[0] - 1), 0))
    grid_spec = pltpu.PrefetchScalarGridSpec(
        num_scalar_prefetch=2,
        grid=(xs.shape[0] // (tm * TOKEN_TILE_ROWS),),
        in_specs=[blk_in,
                  pl.BlockSpec((1, d, 2 * D_EXPERT), lambda i, te, nu: (te[i], 0, 0)),
                  pl.BlockSpec((1, D_EXPERT, d), lambda i, te, nu: (te[i], 0, 0))],
        out_specs=blk,
    )
    return pl.pallas_call(
        _expert_kernel,
        out_shape=jax.ShapeDtypeStruct(xs.shape, F32),
        grid_spec=grid_spec,
        compiler_params=_cparams(("arbitrary",)),
        name="moe_experts",
    )(tile_expert, n_used, xs, wgu, wd)


def _combine_kernel(dc_ref, dn_ref, ys_hbm, meta_ref, ht_ref, g_ref, b_ref, o_ref, obf_ref, buf, sem):
    i = pl.program_id(0)
    n = pl.num_programs(0)
    tt = dc_ref.shape[-1]
    slot = lax.rem(i, 2)

    def gather(d_ref, s):
        def issue(j, carry):
            for k in range(TOP_K):
                pltpu.make_async_copy(_tile_at(ys_hbm, d_ref[0, k, j]), _tile(buf.at[s, k], j),
                                      sem.at[s]).start(priority=k)
            return carry
        lax.fori_loop(0, tt, issue, 0, unroll=DMA_ISSUE_UNROLL)

    @pl.when(i == 0)
    def _():
        gather(dc_ref, 0)

    @pl.when(i + 1 < n)
    def _():
        gather(dn_ref, 1 - slot)

    for k in range(TOP_K):
        pltpu.make_async_copy(ys_hbm.at[pl.ds(0, tt * TOKEN_TILE_ROWS)], buf.at[slot, k], sem.at[slot]).wait()
    meta = meta_ref[...]
    ffn = (meta[:, META_W0:META_W0 + 1] * _token_tiles_to_rows(buf.at[slot, 0], tt)
           + meta[:, META_W1:META_W1 + 1] * _token_tiles_to_rows(buf.at[slot, 1], tt))
    y = _ln_rows(DEEPNORM_ALPHA * _token_tiles_to_rows(ht_ref, tt) + ffn, g_ref[...], b_ref[...])
    o_ref[...] = y
    obf_ref[...] = y.astype(BF16)


def _combine(ys, slots, meta, h_tiles, g, b):
    t, d = h_tiles.shape[0] // TOKEN_TILE_ROWS, D_MODEL
    tt = slots.shape[-1]
    n = t // tt
    cur = pl.BlockSpec((1, SUBLANES, tt), lambda i: (i, 0, 0), memory_space=pltpu.SMEM)
    nxt = pl.BlockSpec((1, SUBLANES, tt), lambda i: (jnp.minimum(i + 1, n - 1), 0, 0), memory_space=pltpu.SMEM)
    row = lambda i: (i, 0)
    const = lambda i: (0, 0)
    return pl.pallas_call(
        _combine_kernel,
        out_shape=(jax.ShapeDtypeStruct((t, d), F32), jax.ShapeDtypeStruct((t, d), BF16)),
        grid=(n,),
        in_specs=[cur, nxt, pl.BlockSpec(memory_space=pl.ANY),
                  pl.BlockSpec((tt, LANES), row), pl.BlockSpec((tt * TOKEN_TILE_ROWS, LANES), row),
                  pl.BlockSpec((1, d), const), pl.BlockSpec((1, d), const)],
        out_specs=(pl.BlockSpec((tt, d), row), pl.BlockSpec((tt, d), row)),
        scratch_shapes=[pltpu.VMEM((2, TOP_K, tt * TOKEN_TILE_ROWS, LANES), F32), pltpu.SemaphoreType.DMA((2,))],
        compiler_params=_cparams(("arbitrary",)),
        name="moe_combine_ln",
    )(slots, slots, ys, meta, h_tiles, g, b)


def _segment_tables(counts, t):
    tm = EXPERT_ROW_TILE
    n_rows = TOP_K * t + N_EXPERTS * tm
    cnt = counts[0, :N_EXPERTS].astype(jnp.int32)
    tiles = (cnt + tm - 1) // tm
    ends = jnp.cumsum(tiles)
    offs = jnp.zeros((1, LANES), F32).at[0, :N_EXPERTS].set(((ends - tiles) * tm).astype(F32))
    tile_ids = jnp.arange(n_rows // tm, dtype=jnp.int32)
    tile_expert = jnp.minimum(jnp.sum((tile_ids[:, None] >= ends[None, :]).astype(jnp.int32), axis=1),
                              N_EXPERTS - 1)
    last_tile = jnp.maximum(ends - 1, 0).astype(jnp.int32)
    return n_rows, offs, tile_expert, ends[-1:].astype(jnp.int32), last_tile


def _sparse_moe(h_tiles, meta, counts, wgu, wd, g, b):
    t = h_tiles.shape[0] // TOKEN_TILE_ROWS
    n_rows, offs, tile_expert, n_used, last_tile = _segment_tables(counts, t)
    slots = _slots(meta, offs, min(MOE_TOKENS, t))
    xs = _dispatch(h_tiles, slots, _zero_last_tiles(last_tile, n_rows))
    ys = _experts(xs, tile_expert, n_used, wgu, wd)
    return _combine(ys, slots, meta, h_tiles, g, b)


def _rotate_half_cols(w):
    half = w.shape[-1] // 2
    return jnp.concatenate([-w[..., half:], w[..., :half]], axis=-1)


def _pack_input_proj(w_in, b_in):
    d = w_in.shape[0]
    offs = np.cumsum((0, GLA_QK_W, GLA_QK_W, GLA_V_W, GLA_V_W, GLA_GATE_RANK, GLA_GATE_RANK,
                      MLA_Q_RANK, MLA_KV_RANK, MLA_ROPE, D_MODEL, D_MODEL))

    def seg(i):
        return w_in[:, offs[i]:offs[i + 1]], b_in[offs[i]:offs[i + 1]]

    w = jnp.zeros((d, N_PROJ), F32)
    b = jnp.zeros((N_PROJ,), F32)

    def put(w, b, col, ws, bs):
        return w.at[:, col:col + ws.shape[1]].set(ws), b.at[col:col + ws.shape[1]].set(bs)

    for i, col in ((0, COL_GQ), (1, COL_GK), (2, COL_GV), (3, COL_GR), (9, COL_GA), (10, COL_GB),
                   (6, COL_CQ), (7, COL_CKV), (4, COL_SMALL), (5, COL_SMALL + GLA_GATE_RANK)):
        w, b = put(w, b, col, *seg(i))
    wkr, bkr = seg(8)
    w, b = put(w, b, COL_SMALL + KR_LANE, wkr, bkr)
    w, b = put(w, b, COL_SMALL2 + KR_LANE, _rotate_half_cols(wkr), _rotate_half_cols(bkr))
    return w.astype(BF16), b.reshape(1, N_PROJ)


def _pack_decay(wa2, ba, lane0):
    w = jnp.zeros((LANES, GLA_QK_W), F32).at[lane0:lane0 + GLA_GATE_RANK].set(wa2)
    return w.astype(BF16), ba.reshape(1, GLA_QK_W)


def _pack_mla(w_uq, w_ukv, q_norm_g, kv_norm_g):
    wq = w_uq.reshape(MLA_Q_RANK, MLA_HEADS, MLA_QK)
    zq = jnp.zeros((MLA_Q_RANK, MLA_HEADS, HEAD_PAD - MLA_QK), F32)
    wq_p = jnp.concatenate([wq, zq], axis=-1)

    def pad_rows(w):
        w = w.reshape(MLA_Q_RANK, MLA_HEADS * HEAD_PAD)
        return jnp.pad(w, ((0, CQ_PAD - MLA_Q_RANK), (0, 0))).astype(BF16)

    wkv = w_ukv.reshape(MLA_KV_RANK, MLA_HEADS, MLA_NOPE + MLA_V)
    wk_p = jnp.concatenate([wkv[..., :MLA_NOPE],
                            jnp.zeros((MLA_KV_RANK, MLA_HEADS, HEAD_PAD - MLA_NOPE), F32)], axis=-1)
    wk_p = wk_p.reshape(MLA_KV_RANK, MLA_HEADS * HEAD_PAD).astype(BF16)
    wv_p = wkv[..., MLA_NOPE:].reshape(MLA_KV_RANK, MLA_HEADS * MLA_V).T.astype(BF16)
    gq = jnp.pad(q_norm_g, (0, CQ_PAD - MLA_Q_RANK)).reshape(1, CQ_PAD)
    return pad_rows(wq_p), wk_p, wv_p, gq, kv_norm_g.reshape(1, MLA_KV_RANK)


def _rope_lane_table():
    inv = ROPE_BASE ** (-jnp.arange(0, MLA_ROPE, 2, dtype=F32) / MLA_ROPE)
    return jnp.tile(inv, ROPE_PACK).reshape(1, LANES)


def _pack_router(w_grp, b_grp, w_exp, b_exp):
    d = w_grp.shape[0]
    w = jnp.zeros((d, LANES), F32).at[:, :N_EXPERTS].set(w_exp).at[:, GRP_LANE:GRP_LANE + N_GROUPS].set(w_grp)
    b = jnp.zeros((LANES,), F32).at[:N_EXPERTS].set(b_exp).at[GRP_LANE:GRP_LANE + N_GROUPS].set(b_grp)
    return w, b.reshape(1, LANES)


def kernel(x, positions, ln_emb_g, ln_emb_b, w_in, b_in, gla_wa2_f, gla_ba_f, gla_wa2_b, gla_ba_b, gla_norm_g, mla_q_norm_g, mla_w_uq, mla_kv_norm_g, mla_w_ukv, w_out, ln1_g, ln1_b, w_grp, b_grp, w_exp, b_exp, w_gate, w_up, w_down, ln2_g, ln2_b):
    batch, seq, d = x.shape
    t = batch * seq
    pos = jnp.repeat(positions.reshape(t // ROPE_PACK, ROPE_PACK).astype(F32), MLA_ROPE // 2, axis=1)
    inv_lane = _rope_lane_table()
    for l in range(DEPTH):
        w_p, b_p = _pack_input_proj(w_in[l], b_in[l])
        if l == 0:
            h, proj = _input_proj(x.reshape(t, d), w_p, b_p, ln=(ln_emb_g, ln_emb_b))
        else:
            proj = _input_proj(hb, w_p, b_p)
        wa_f, ba_f = _pack_decay(gla_wa2_f[l], gla_ba_f[l], 0)
        wa_b, ba_b = _pack_decay(gla_wa2_b[l], gla_ba_b[l], GLA_GATE_RANK)
        o_f = _gla(proj, wa_f, ba_f, batch, seq, reverse=False)
        o_gla = _gla(proj, wa_b, ba_b, batch, seq, reverse=True, o_fwd=o_f,
                     norm_g=gla_norm_g[l].reshape(1, GLA_V_W))
        wq, wk, wv, gq, gkv = _pack_mla(mla_w_uq[l], mla_w_ukv[l], mla_q_norm_g[l], mla_kv_norm_g[l])
        q2, k2, v = _mla_prep(proj, pos, inv_lane, gq, gkv, wq, wk, wv)
        o_mla = _attention(q2, k2, v, batch, seq)
        h_tiles = _merge_out(o_gla, o_mla, proj, h, w_out[l].astype(BF16),
                           ln1_g[l].reshape(1, d), ln1_b[l].reshape(1, d))
        w_r, b_r = _pack_router(w_grp[l], b_grp[l], w_exp[l], b_exp[l])
        meta, counts = _router(h_tiles, w_r, b_r)
        wgu = jnp.concatenate([w_gate[l], w_up[l]], axis=-1).astype(BF16)
        h, hb = _sparse_moe(h_tiles, meta, counts, wgu, w_down[l].astype(BF16),
                            ln2_g[l].reshape(1, d), ln2_b[l].reshape(1, d))
    return h.reshape(batch, seq, d)
```

```python
import functools

import numpy as np
import jax
import jax.numpy as jnp
from jax import lax
from jax.experimental import pallas as pl
from jax.experimental.pallas import tpu as pltpu

F32 = jnp.float32
BF16 = jnp.bfloat16

D_MODEL = 1024
DEPTH = 2
GLA_HEADS = 4
GLA_DK = 128
GLA_DV = 256
GLA_GATE_RANK = 16
GLA_TAU = 16.0
MLA_HEADS = 16
MLA_NOPE = 64
MLA_ROPE = 32
MLA_V = 64
MLA_QK = MLA_NOPE + MLA_ROPE
MLA_Q_RANK = 384
MLA_KV_RANK = 128
ROPE_BASE = 10000.0
N_GROUPS = 8
EXPERTS_PER_GROUP = 4
N_EXPERTS = 32
D_EXPERT = 256
GLA_QK_W = GLA_HEADS * GLA_DK
GLA_V_W = GLA_HEADS * GLA_DV
DEEPNORM_ALPHA = (2.0 * DEPTH) ** 0.25
LN_EPS = 1e-5
RMS_EPS = 1e-6

LANES = 128
VMEM_LIMIT_BYTES = 56 * 1024 * 1024

COL_GQ = 0
COL_GK = 512
COL_GV = 1024
COL_GR = 2048
COL_GA = 3072
COL_GB = 4096
COL_CQ = 5120
CQ_PAD = 512
COL_SMALL = 5632
COL_SMALL2 = 5760
COL_CKV = 5888
N_PROJ = 6144
HEAD_PAD = 128
KR_LANE = 64

GLA_CHUNK = 128
NEG_BIG = -1e30


def _cparams(sem):
    return pltpu.CompilerParams(dimension_semantics=sem, vmem_limit_bytes=VMEM_LIMIT_BYTES)


def _ln_rows(x, g, b):
    mu = jnp.mean(x, axis=-1, keepdims=True)
    xc = x - mu
    var = jnp.mean(xc * xc, axis=-1, keepdims=True)
    return xc * lax.rsqrt(var + LN_EPS) * g + b


PROJ_COL_CHUNK = 512


def _proj_kernel(*refs, embed_ln):
    if embed_ln:
        x_ref, g_ref, beta_ref, w_ref, b_ref, h_ref, o_ref = refs
        h = _ln_rows(x_ref[...], g_ref[...], beta_ref[...])
        h_ref[...] = h
        x = h.astype(BF16)
    else:
        x_ref, w_ref, b_ref, o_ref = refs
        x = x_ref[...]
    for c in range(N_PROJ // PROJ_COL_CHUNK):
        sl = slice(c * PROJ_COL_CHUNK, (c + 1) * PROJ_COL_CHUNK)
        acc = jnp.dot(x, w_ref[:, sl], preferred_element_type=F32) + b_ref[:, sl]
        o_ref[:, sl] = acc.astype(o_ref.dtype)


def _input_proj(x, w, b, ln=None, tm=512):
    t, d = x.shape
    row = lambda i: (i, 0)
    const = lambda i: (0, 0)
    vec = pl.BlockSpec((1, d), const)
    in_specs = [pl.BlockSpec((tm, d), row)] + ([vec, vec] if ln else []) + [
        pl.BlockSpec((d, N_PROJ), const), pl.BlockSpec((1, N_PROJ), const)]
    proj_shape = jax.ShapeDtypeStruct((t, N_PROJ), BF16)
    proj_spec = pl.BlockSpec((tm, N_PROJ), row)
    args = (x,) + (tuple(v.reshape(1, d) for v in ln) if ln else ()) + (w, b)
    return pl.pallas_call(
        functools.partial(_proj_kernel, embed_ln=ln is not None),
        out_shape=(jax.ShapeDtypeStruct((t, d), F32), proj_shape) if ln else proj_shape,
        grid=(t // tm,),
        in_specs=in_specs,
        out_specs=(pl.BlockSpec((tm, d), row), proj_spec) if ln else proj_spec,
        compiler_params=_cparams(("parallel",)),
        name="ln_input_proj" if ln else "input_proj",
    )(*args)


def _log_sigmoid(x):
    return jnp.minimum(x, 0.0) - jnp.log(1.0 + jnp.exp(-jnp.abs(x)))


def _split_bf16(x):
    hi = x.astype(BF16)
    lo = (x - hi.astype(F32)).astype(BF16)
    return hi, lo


def _gla_kernel(*refs, reverse, n_chunks):
    if reverse:
        (q_ref, k_ref, v_ref, z_ref, wa_ref, ba_ref, of_ref, gr_ref, ng_ref, o_ref, state_ref) = refs
    else:
        (q_ref, k_ref, v_ref, z_ref, wa_ref, ba_ref, o_ref, state_ref) = refs
    c_len = GLA_CHUNK

    @pl.when(pl.program_id(1) == 0)
    def _():
        state_ref[...] = jnp.zeros_like(state_ref)

    row = lax.broadcasted_iota(jnp.int32, (c_len, c_len), 0)
    col = lax.broadcasted_iota(jnp.int32, (c_len, c_len), 1)
    if reverse:
        tri = (col >= row).astype(BF16)
        keep = col > row
        last = 0
    else:
        tri = (col <= row).astype(BF16)
        keep = col <= row
        last = c_len - 1

    seqs = range(q_ref.shape[0])

    def decays(bi, rows):
        z = z_ref[bi, rows, :]
        la = _log_sigmoid(jnp.dot(z, wa_ref[...], preferred_element_type=F32) + ba_ref[...]) * (1.0 / GLA_TAU)
        la_hi, la_lo = _split_bf16(la)
        b_all = (jnp.dot(tri, la_hi, preferred_element_type=F32)
                 + jnp.dot(tri, la_lo, preferred_element_type=F32))
        b_last = b_all[last:last + 1, :]
        q = q_ref[bi, rows, :].astype(F32)
        k = k_ref[bi, rows, :].astype(F32)
        qd_all = (q * (jnp.exp(b_all) * (GLA_DK ** -0.5))).astype(BF16)
        kinv_all = (k * jnp.exp(-b_all)).astype(BF16)
        kend_all = k * jnp.exp(b_last - b_all)
        dec_all = jnp.broadcast_to(jnp.exp(b_last), (c_len, GLA_QK_W))
        return qd_all, kinv_all, kend_all, dec_all

    def chunk_rows(cc):
        c = (n_chunks - 1 - cc) if reverse else cc
        return slice(c * c_len, (c + 1) * c_len)

    pre_next = [decays(bi, chunk_rows(0)) for bi in seqs]
    for cc in range(n_chunks):
        rows = chunk_rows(cc)
        pre = pre_next
        if cc + 1 < n_chunks:
            pre_next = [decays(bi, chunk_rows(cc + 1)) for bi in seqs]
        for h in range(GLA_HEADS):
            ks = slice(h * GLA_DK, (h + 1) * GLA_DK)
            vs = slice(h * GLA_DV, (h + 1) * GLA_DV)
            v = [v_ref[bi, rows, vs] for bi in seqs]
            scores = [lax.dot_general(pre[bi][0][:, ks], pre[bi][1][:, ks], (((1,), (1,)), ((), ())),
                                      preferred_element_type=F32) for bi in seqs]
            state = [state_ref[bi, h] for bi in seqs]
            o = [jnp.dot(jnp.where(keep, scores[bi], 0.0).astype(BF16), v[bi], preferred_element_type=F32)
                 + jnp.dot(pre[bi][0][:, ks], state[bi].astype(BF16), preferred_element_type=F32) for bi in seqs]
            for bi in seqs:
                kend_t = pre[bi][2][:, ks].T.astype(BF16)
                dec_t = pre[bi][3][:, ks].T
                dec = jnp.concatenate([dec_t, dec_t], axis=1)
                state_ref[bi, h] = dec * state[bi] + jnp.dot(kend_t, v[bi], preferred_element_type=F32)
            for bi in seqs:
                ob = o[bi]
                if reverse:
                    ob = ob + of_ref[bi, rows, vs]
                    ms = jnp.mean(ob * ob, axis=-1, keepdims=True)
                    ob = ob * lax.rsqrt(ms + RMS_EPS) * ng_ref[:, vs]
                    g = gr_ref[bi, rows, vs].astype(F32)
                    ob = ob * (g * jax.nn.sigmoid(g))
                o_ref[bi, rows, vs] = ob.astype(o_ref.dtype)


GLA_SEQS_PER_STEP = 2


def _gla(proj, wa, ba, batch, seq, *, reverse, o_fwd=None, norm_g=None, ts=512):
    ts = min(ts, seq)
    nblk = seq // ts
    nb = GLA_SEQS_PER_STEP if batch % GLA_SEQS_PER_STEP == 0 else 1
    proj3 = proj.reshape(batch, seq, N_PROJ)

    def blk(i):
        return (nblk - 1 - i) if reverse else i

    def cols(width, col):
        return pl.BlockSpec((nb, ts, width), lambda b, i: (b, blk(i), col // width))

    in_specs = [cols(GLA_QK_W, COL_GQ), cols(GLA_QK_W, COL_GK), cols(GLA_V_W, COL_GV), cols(LANES, COL_SMALL),
                pl.BlockSpec((LANES, GLA_QK_W), lambda b, i: (0, 0)),
                pl.BlockSpec((1, GLA_QK_W), lambda b, i: (0, 0))]
    args = [proj3, proj3, proj3, proj3, wa, ba]
    if reverse:
        in_specs += [cols(GLA_V_W, 0), cols(GLA_V_W, COL_GR), pl.BlockSpec((1, GLA_V_W), lambda b, i: (0, 0))]
        args += [o_fwd.reshape(batch, seq, GLA_V_W), proj3, norm_g]
    out = pl.pallas_call(
        functools.partial(_gla_kernel, reverse=reverse, n_chunks=ts // GLA_CHUNK),
        out_shape=jax.ShapeDtypeStruct((batch, seq, GLA_V_W), BF16 if reverse else F32),
        grid=(batch // nb, nblk),
        in_specs=in_specs,
        out_specs=cols(GLA_V_W, 0),
        scratch_shapes=[pltpu.VMEM((nb, GLA_HEADS, GLA_DK, GLA_DV), F32)],
        compiler_params=_cparams(("parallel", "arbitrary")),
        name="gla_bwd" if reverse else "gla_fwd",
    )(*args)
    return out.reshape(batch * seq, GLA_V_W)


MLA_COL_CHUNK = 512
ROPE_PACK = LANES // (MLA_ROPE // 2)


def _mla_prep_kernel(cq_ref, ckv_ref, sm_ref, sm2_ref, pos_ref, inv_ref, gq_ref, gkv_ref,
                     wq_ref, wk_ref, wv_ref, q_ref, k_ref, v_ref):
    cq = cq_ref[...].astype(F32)
    msq = jnp.sum(cq * cq, axis=-1, keepdims=True) * (1.0 / MLA_Q_RANK)
    cqn = (cq * lax.rsqrt(msq + RMS_EPS) * gq_ref[...]).astype(BF16)
    ckv = ckv_ref[...].astype(F32)
    mskv = jnp.mean(ckv * ckv, axis=-1, keepdims=True)
    ckvn = (ckv * lax.rsqrt(mskv + RMS_EPS) * gkv_ref[...]).astype(BF16)

    ts = cq_ref.shape[0]
    ang = pos_ref[...] * inv_ref[...]
    tok = lax.broadcasted_iota(jnp.int32, (ts, ts // ROPE_PACK), 0)
    grp = lax.broadcasted_iota(jnp.int32, (ts, ts // ROPE_PACK), 1)
    to_rows = (tok // ROPE_PACK == grp).astype(BF16)
    src = lax.broadcasted_iota(jnp.int32, (LANES, LANES), 0)
    dst = lax.broadcasted_iota(jnp.int32, (LANES, LANES), 1)
    half = MLA_ROPE // 2
    to_lanes = ((dst >= KR_LANE) & (dst < KR_LANE + MLA_ROPE) & ((dst - KR_LANE) % half == src % half)).astype(BF16)
    lane = lax.broadcasted_iota(jnp.int32, (ts, LANES), 1)
    own = lane // half == lax.broadcasted_iota(jnp.int32, (ts, LANES), 0) % ROPE_PACK

    def spread(packed):
        out = 0.0
        for part in _split_bf16(packed):
            rows = jnp.dot(to_rows, part, preferred_element_type=F32)
            rows = jnp.where(own, rows, 0.0).astype(BF16)
            out = out + jnp.dot(rows, to_lanes, preferred_element_type=F32)
        return out

    cos = jnp.where(lane < KR_LANE, 1.0, spread(jnp.cos(ang)))
    sin = spread(jnp.sin(ang))
    kr = jnp.where(lane >= KR_LANE, sm_ref[...].astype(F32), 0.0)
    kr = kr * cos + sm2_ref[...].astype(F32) * sin

    heads_per_chunk = MLA_COL_CHUNK // HEAD_PAD
    cos_t = jnp.concatenate([cos] * heads_per_chunk, axis=1)
    sin_t = jnp.concatenate([sin] * heads_per_chunk, axis=1)
    kr_t = jnp.concatenate([kr] * heads_per_chunk, axis=1)
    qscale = (MLA_QK ** -0.5) * float(np.log2(np.e))
    chunk_lane = lax.broadcasted_iota(jnp.int32, (ts, MLA_COL_CHUNK), 1) % HEAD_PAD
    first_half = chunk_lane < KR_LANE + half
    for c in range(MLA_HEADS * HEAD_PAD // MLA_COL_CHUNK):
        sl = slice(c * MLA_COL_CHUNK, (c + 1) * MLA_COL_CHUNK)
        q = jnp.dot(cqn, wq_ref[:, sl], preferred_element_type=F32)
        qr = jnp.where(first_half, -pltpu.roll(q, MLA_COL_CHUNK - half, axis=1), pltpu.roll(q, half, axis=1))
        q_ref[:, sl] = ((q * cos_t + qr * sin_t) * qscale).astype(BF16)
        k = jnp.dot(ckvn, wk_ref[:, sl], preferred_element_type=F32)
        k_ref[:, sl] = (k + kr_t).astype(BF16)
    v_ref[...] = lax.dot_general(wv_ref[...], ckvn, (((1,), (1,)), ((), ())),
                                 preferred_element_type=F32).astype(BF16)


def _mla_prep(proj, pos, inv_lane, gq, gkv, wq, wk, wv, ts=512):
    t = proj.shape[0]
    hp = MLA_HEADS * HEAD_PAD
    vw = MLA_HEADS * MLA_V
    const = lambda i: (0, 0)
    return pl.pallas_call(
        _mla_prep_kernel,
        out_shape=(jax.ShapeDtypeStruct((t, hp), BF16), jax.ShapeDtypeStruct((t, hp), BF16),
                   jax.ShapeDtypeStruct((vw, t), BF16)),
        grid=(t // ts,),
        in_specs=[
            pl.BlockSpec((ts, CQ_PAD), lambda i: (i, COL_CQ // CQ_PAD)),
            pl.BlockSpec((ts, LANES), lambda i: (i, COL_CKV // LANES)),
            pl.BlockSpec((ts, LANES), lambda i: (i, COL_SMALL // LANES)),
            pl.BlockSpec((ts, LANES), lambda i: (i, COL_SMALL2 // LANES)),
            pl.BlockSpec((ts // ROPE_PACK, LANES), lambda i: (i, 0)),
            pl.BlockSpec((1, LANES), const),
            pl.BlockSpec((1, CQ_PAD), const),
            pl.BlockSpec((1, MLA_KV_RANK), const),
            pl.BlockSpec((CQ_PAD, hp), const),
            pl.BlockSpec((MLA_KV_RANK, hp), const),
            pl.BlockSpec((vw, MLA_KV_RANK), const),
        ],
        out_specs=(pl.BlockSpec((ts, hp), lambda i: (i, 0)), pl.BlockSpec((ts, hp), lambda i: (i, 0)),
                   pl.BlockSpec((vw, ts), lambda i: (0, i))),
        compiler_params=_cparams(("parallel",)),
        name="mla_prep",
    )(proj, proj, proj, proj, pos, inv_lane, gq, gkv, wq, wk, wv)


SUBLANES = 8


BOUND_SLACK = 1.01
SHIFT_LIMIT = 48.0
SCORE_CHUNKS_AHEAD = 2


def _attn_kernel(q_ref, k_ref, vt_ref, o_ref, s_scr, bound_scr, safe_scr, *, tk):
    seq = k_ref.shape[0]
    tq = s_scr.shape[-1]
    nk = seq // tk
    nq = seq // tq
    nt = (((1,), (1,)), ((), ()))
    head = lambda hh: slice(hh * HEAD_PAD, (hh + 1) * HEAD_PAD)
    ones8 = jnp.ones((SUBLANES, HEAD_PAD), BF16)
    ones_sq = jnp.ones((HEAD_PAD, HEAD_PAD), BF16)

    def max_key_norm_sq(hh):
        k = k_ref[:, head(hh)].astype(F32)
        row_sums = jnp.dot((k * k).astype(BF16), ones_sq, preferred_element_type=F32)
        return jnp.max(row_sums, axis=0, keepdims=True)[:, 0:1]

    for hh in range(2):
        qf = q_ref[:, head(hh)].astype(F32)
        q_sq = lax.dot_general(ones8, (qf * qf).astype(BF16), nt, preferred_element_type=F32)
        b8 = jnp.sqrt(q_sq * max_key_norm_sq(hh)) * BOUND_SLACK
        for j in range(nq):
            bound_scr[hh, j] = b8[:, j * tq:(j + 1) * tq]
    for j in range(nq):
        worst = jnp.max(jnp.maximum(bound_scr[0, j], bound_scr[1, j]))
        safe_scr[j] = (worst < SHIFT_LIMIT).astype(jnp.int32)

    def scores(hh, q, c):
        return lax.dot_general(k_ref[c * tk:(c + 1) * tk, head(hh)], q, nt, preferred_element_type=F32)

    def weigh(hh, c, p, l8, acc):
        l8 = l8 + jnp.sum(p.reshape(tk // SUBLANES, SUBLANES, tq), axis=0)
        vt = vt_ref[hh * MLA_V:(hh + 1) * MLA_V, c * tk:(c + 1) * tk]
        return l8, acc + jnp.dot(vt, p.astype(BF16), preferred_element_type=F32)

    def q_tile(i, carry):
        rows = pl.ds(pl.multiple_of(i * tq, tq), tq)
        qs = [q_ref[rows, head(hh)] for hh in range(2)]
        bound = [bound_scr[hh, i, 0:1, :] for hh in range(2)]
        safe = safe_scr[i] != 0
        zeros = lambda n: jnp.zeros((n, tq), F32)

        @pl.when(safe)
        def _():
            l8, acc = [zeros(SUBLANES)] * 2, [zeros(MLA_V)] * 2
            ahead = min(SCORE_CHUNKS_AHEAD, nk)
            pending = [[scores(hh, qs[hh], c) for hh in range(2)] for c in range(ahead)]
            for c in range(nk):
                st = pending.pop(0)
                if c + ahead < nk:
                    pending.append([scores(hh, qs[hh], c + ahead) for hh in range(2)])
                for hh in range(2):
                    l8[hh], acc[hh] = weigh(hh, c, jnp.exp2(st[hh] - bound[hh]), l8[hh], acc[hh])
            out = [acc[hh] / jnp.sum(l8[hh], axis=0, keepdims=True) for hh in range(2)]
            o_ref[rows, :] = jnp.concatenate(out, axis=0).T.astype(BF16)

        @pl.when(jnp.logical_not(safe))
        def _():
            out = []
            for hh in range(2):
                m8 = jnp.full((SUBLANES, tq), NEG_BIG, F32)
                for c in range(nk):
                    st = scores(hh, qs[hh], c)
                    s_scr[c] = st
                    m8 = jnp.maximum(m8, jnp.max(st.reshape(tk // SUBLANES, SUBLANES, tq), axis=0))
                m = jnp.max(m8, axis=0, keepdims=True)
                l8, acc = zeros(SUBLANES), zeros(MLA_V)
                for c in range(nk):
                    l8, acc = weigh(hh, c, jnp.exp2(s_scr[c] - m), l8, acc)
                out.append(acc / jnp.sum(l8, axis=0, keepdims=True))
            o_ref[rows, :] = jnp.concatenate(out, axis=0).T.astype(BF16)

        return carry

    lax.fori_loop(0, nq, q_tile, 0)


def _attention(q2, k2, vt, batch, seq, tq=512, tk=512):
    tq = min(tq, seq)
    tk = min(tk, seq)
    t = batch * seq
    pairs = MLA_HEADS // 2
    return pl.pallas_call(
        functools.partial(_attn_kernel, tk=tk),
        scratch_shapes=[pltpu.VMEM((seq // tk, tk, tq), F32), pltpu.VMEM((2, seq // tq, SUBLANES, tq), F32),
                        pltpu.SMEM((seq // tq,), jnp.int32)],
        out_shape=jax.ShapeDtypeStruct((t, MLA_HEADS * MLA_V), BF16),
        grid=(batch, pairs),
        in_specs=[
            pl.BlockSpec((seq, 2 * HEAD_PAD), lambda b, p: (b, p)),
            pl.BlockSpec((seq, 2 * HEAD_PAD), lambda b, p: (b, p)),
            pl.BlockSpec((2 * MLA_V, seq), lambda b, p: (p, b)),
        ],
        out_specs=pl.BlockSpec((seq, 2 * MLA_V), lambda b, p: (b, p)),
        compiler_params=_cparams(("parallel", "parallel")),
        name="mla_attention",
    )(q2, k2, vt)


TOKEN_TILE_ROWS = D_MODEL // LANES


def _rows_to_token_tiles(ref, y):
    n = y.shape[0]
    for s in range(TOKEN_TILE_ROWS):
        ref[pl.ds(s, n, stride=TOKEN_TILE_ROWS), :] = y[:, s * LANES:(s + 1) * LANES]


def _token_tiles_to_rows(ref, n):
    return jnp.concatenate([ref[pl.ds(s, n, stride=TOKEN_TILE_ROWS), :] for s in range(TOKEN_TILE_ROWS)], axis=1)


def _merge_out_kernel(oa_ref, ob_ref, ga_ref, gb_ref, h_ref, w_ref, g_ref, b_ref, ot_ref):
    merged = (jax.nn.sigmoid(ga_ref[...].astype(F32)) * oa_ref[...].astype(F32)
              + jax.nn.sigmoid(gb_ref[...].astype(F32)) * ob_ref[...].astype(F32))
    mix = jnp.dot(merged.astype(BF16), w_ref[...], preferred_element_type=F32)
    y = _ln_rows(DEEPNORM_ALPHA * h_ref[...] + mix, g_ref[...], b_ref[...])
    _rows_to_token_tiles(ot_ref, y)


def _merge_out(o_gla, o_mla, proj, h, w_out, g, b, tm=512):
    t, d = h.shape
    row = lambda i: (i, 0)
    const = lambda i: (0, 0)
    return pl.pallas_call(
        _merge_out_kernel,
        out_shape=jax.ShapeDtypeStruct((t * TOKEN_TILE_ROWS, LANES), F32),
        grid=(t // tm,),
        in_specs=[
            pl.BlockSpec((tm, d), row),
            pl.BlockSpec((tm, d), row),
            pl.BlockSpec((tm, d), lambda i: (i, COL_GA // D_MODEL)),
            pl.BlockSpec((tm, d), lambda i: (i, COL_GB // D_MODEL)),
            pl.BlockSpec((tm, d), row),
            pl.BlockSpec((d, d), const),
            pl.BlockSpec((1, d), const),
            pl.BlockSpec((1, d), const),
        ],
        out_specs=pl.BlockSpec((tm * TOKEN_TILE_ROWS, LANES), row),
        compiler_params=_cparams(("parallel",)),
        name="merge_out_ln",
    )(o_gla, o_mla, proj, proj, h, w_out, g, b)


GRP_LANE = N_EXPERTS
META_W0, META_W1, META_E0, META_E1, META_R0, META_R1 = range(6)


def _router_kernel(x_ref, w_ref, b_ref, meta_ref, cnt_ref, carry_ref):
    @pl.when(pl.program_id(0) == 0)
    def _():
        carry_ref[...] = jnp.zeros_like(carry_ref)

    tm = x_ref.shape[0] // TOKEN_TILE_ROWS
    x_hi, x_lo = _split_bf16(_token_tiles_to_rows(x_ref, tm))
    w_hi, w_lo = _split_bf16(w_ref[...])
    logits = (jnp.dot(x_hi, w_hi, preferred_element_type=F32)
              + jnp.dot(x_lo, w_hi, preferred_element_type=F32)
              + jnp.dot(x_hi, w_lo, preferred_element_type=F32)) + b_ref[...]
    lane = lax.broadcasted_iota(jnp.int32, logits.shape, 1)
    is_grp = (lane >= GRP_LANE) & (lane < GRP_LANE + N_GROUPS)
    gl = jnp.where(is_grp, logits, NEG_BIG)
    gmax = jnp.max(gl, axis=1, keepdims=True)
    g_lane = jnp.min(jnp.where(gl == gmax, lane, 4 * LANES), axis=1, keepdims=True)
    g_w = 1.0 / jnp.sum(jnp.where(is_grp, jnp.exp(gl - gmax), 0.0), axis=1, keepdims=True)
    lo_lane = (g_lane - GRP_LANE) * EXPERTS_PER_GROUP
    in_grp = (lane >= lo_lane) & (lane < lo_lane + EXPERTS_PER_GROUP)
    el = jnp.where(in_grp, logits, NEG_BIG)
    v1 = jnp.max(el, axis=1, keepdims=True)
    i1 = jnp.min(jnp.where(el == v1, lane, 4 * LANES), axis=1, keepdims=True)
    el2 = jnp.where(lane == i1, NEG_BIG, el)
    v2 = jnp.max(el2, axis=1, keepdims=True)
    i2 = jnp.min(jnp.where(el2 == v2, lane, 4 * LANES), axis=1, keepdims=True)
    e2 = jnp.exp(v2 - v1)
    w1 = g_w / (1.0 + e2)
    w2 = g_w * e2 / (1.0 + e2)

    onehot = jnp.where(lane == i1, 1.0, jnp.where(lane == i2, 1.0, 0.0))
    r = lax.broadcasted_iota(jnp.int32, (tm, tm), 0)
    c = lax.broadcasted_iota(jnp.int32, (tm, tm), 1)
    earlier = (c < r).astype(BF16)
    base = carry_ref[...] + jnp.dot(earlier, onehot.astype(BF16), preferred_element_type=F32)
    rank1 = jnp.sum(jnp.where(lane == i1, base, 0.0), axis=1, keepdims=True)
    rank2 = jnp.sum(jnp.where(lane == i2, base, 0.0), axis=1, keepdims=True)
    meta = jnp.zeros(logits.shape, F32)
    for ln, val in ((META_W0, w1), (META_W1, w2), (META_E0, i1.astype(F32)), (META_E1, i2.astype(F32)),
                    (META_R0, rank1), (META_R1, rank2)):
        meta = jnp.where(lane == ln, val, meta)
    meta_ref[...] = meta
    carry_ref[...] += jnp.sum(onehot, axis=0, keepdims=True)
    cnt_ref[...] = carry_ref[...]


def _router(h_tiles, w_r, b_r, tm=512):
    t, d = h_tiles.shape[0] // TOKEN_TILE_ROWS, D_MODEL
    return pl.pallas_call(
        _router_kernel,
        out_shape=(jax.ShapeDtypeStruct((t, LANES), F32), jax.ShapeDtypeStruct((1, LANES), F32)),
        grid=(t // tm,),
        in_specs=[pl.BlockSpec((tm * TOKEN_TILE_ROWS, LANES), lambda i: (i, 0)),
                  pl.BlockSpec((d, LANES), lambda i: (0, 0)),
                  pl.BlockSpec((1, LANES), lambda i: (0, 0))],
        out_specs=(pl.BlockSpec((tm, LANES), lambda i: (i, 0)), pl.BlockSpec((1, LANES), lambda i: (0, 0))),
        scratch_shapes=[pltpu.VMEM((1, LANES), F32)],
        compiler_params=_cparams(("arbitrary",)),
        name="router",
    )(h_tiles, w_r, b_r)


EXPERT_ROW_TILE = 512
MOE_TOKENS = 512
TOP_K = 2
DMA_ISSUE_UNROLL = 16


def _tile_at(ref, first_row):
    return ref.at[pl.ds(pl.multiple_of(first_row, TOKEN_TILE_ROWS), TOKEN_TILE_ROWS)]


def _tile(ref, token):
    return _tile_at(ref, token * TOKEN_TILE_ROWS)


def _slots_kernel(meta_ref, offs_ref, d_ref):
    meta = meta_ref[...]
    offs = offs_ref[...]
    lane = lax.broadcasted_iota(jnp.int32, meta.shape, 1)

    def slot(e_lane, r_lane):
        e = meta[:, e_lane:e_lane + 1].astype(jnp.int32)
        return jnp.sum(jnp.where(lane == e, offs, 0.0), axis=1, keepdims=True) + meta[:, r_lane:r_lane + 1]

    both = jnp.where(lane == 0, slot(META_E0, META_R0), jnp.where(lane == 1, slot(META_E1, META_R1), 0.0))
    d_ref[0] = (both.T[:SUBLANES, :] * float(TOKEN_TILE_ROWS)).astype(jnp.int32)


def _slots(meta, offs, tt):
    t = meta.shape[0]
    return pl.pallas_call(
        _slots_kernel,
        out_shape=jax.ShapeDtypeStruct((t // tt, SUBLANES, tt), jnp.int32),
        grid=(t // tt,),
        in_specs=[pl.BlockSpec((tt, LANES), lambda i: (i, 0)), pl.BlockSpec((1, LANES), lambda i: (0, 0))],
        out_specs=pl.BlockSpec((1, SUBLANES, tt), lambda i: (i, 0, 0)),
        compiler_params=_cparams(("arbitrary",)),
        name="moe_slots",
    )(meta, offs)


def _zero_tile_kernel(lt_ref, xs_ref):
    del lt_ref
    xs_ref[...] = jnp.zeros_like(xs_ref)


def _zero_last_tiles(last_tile, n_rows):
    blk = EXPERT_ROW_TILE * TOKEN_TILE_ROWS
    grid_spec = pltpu.PrefetchScalarGridSpec(
        num_scalar_prefetch=1, grid=(N_EXPERTS,), in_specs=[],
        out_specs=pl.BlockSpec((blk, LANES), lambda e, lt: (lt[e], 0)))
    return pl.pallas_call(
        _zero_tile_kernel,
        out_shape=jax.ShapeDtypeStruct((n_rows * TOKEN_TILE_ROWS, LANES), F32),
        grid_spec=grid_spec,
        compiler_params=_cparams(("arbitrary",)),
        name="moe_zero_tiles",
    )(last_tile)


def _dispatch_kernel(d_ref, ht_ref, xs_init_hbm, xs_hbm, sem):
    del xs_init_hbm
    tt = d_ref.shape[-1]

    def issue(j, carry):
        src = _tile(ht_ref, j)
        for k in range(TOP_K):
            pltpu.make_async_copy(src, _tile_at(xs_hbm, d_ref[0, k, j]), sem).start(priority=k)
        return carry

    lax.fori_loop(0, tt, issue, 0, unroll=DMA_ISSUE_UNROLL)
    for k in range(TOP_K):
        pltpu.make_async_copy(ht_ref, xs_hbm.at[pl.ds(0, tt * TOKEN_TILE_ROWS)], sem).wait()


def _dispatch(h_tiles, slots, xs_init):
    tt = slots.shape[-1]
    t = h_tiles.shape[0] // TOKEN_TILE_ROWS
    return pl.pallas_call(
        _dispatch_kernel,
        out_shape=jax.ShapeDtypeStruct(xs_init.shape, F32),
        grid=(t // tt,),
        in_specs=[pl.BlockSpec((1, SUBLANES, tt), lambda i: (i, 0, 0), memory_space=pltpu.SMEM),
                  pl.BlockSpec((tt * TOKEN_TILE_ROWS, LANES), lambda i: (i, 0)),
                  pl.BlockSpec(memory_space=pl.ANY)],
        out_specs=pl.BlockSpec(memory_space=pl.ANY),
        scratch_shapes=[pltpu.SemaphoreType.DMA(())],
        input_output_aliases={2: 0},
        compiler_params=_cparams(("arbitrary",)),
        name="moe_dispatch",
    )(slots, h_tiles, xs_init)


def _expert_kernel(te_ref, nu_ref, xs_ref, wgu_ref, wd_ref, ys_ref):
    del te_ref
    i = pl.program_id(0)
    tm = EXPERT_ROW_TILE

    @pl.when(i < nu_ref[0])
    def _():
        x = _token_tiles_to_rows(xs_ref, tm).astype(BF16)
        gu = jnp.dot(x, wgu_ref[0], preferred_element_type=F32)
        gate = gu[:, :D_EXPERT]
        hid = gate * jax.nn.sigmoid(gate) * gu[:, D_EXPERT:]
        _rows_to_token_tiles(ys_ref, jnp.dot(hid.astype(BF16), wd_ref[0], preferred_element_type=F32))

    @pl.when(i >= nu_ref[0])
    def _():
        ys_ref[...] = jnp.zeros_like(ys_ref)


def _experts(xs, tile_expert, n_used, wgu, wd):
    d = D_MODEL
    tm = EXPERT_ROW_TILE
    blk = pl.BlockSpec((tm * TOKEN_TILE_ROWS, LANES), lambda i, te, nu: (i, 0))
    blk_in = pl.BlockSpec((tm * TOKEN_TILE_ROWS, LANES), lambda i, te, nu: (jnp.minimum(i, nu[0] - 1), 0))
    grid_spec = pltpu.PrefetchScalarGridSpec(
        num_scalar_prefetch=2,
        grid=(xs.shape[0] // (tm * TOKEN_TILE_ROWS),),
        in_specs=[blk_in,
                  pl.BlockSpec((1, d, 2 * D_EXPERT), lambda i, te, nu: (te[i], 0, 0)),
                  pl.BlockSpec((1, D_EXPERT, d), lambda i, te, nu: (te[i], 0, 0))],
        out_specs=blk,
    )
    return pl.pallas_call(
        _expert_kernel,
        out_shape=jax.ShapeDtypeStruct(xs.shape, F32),
        grid_spec=grid_spec,
        compiler_params=_cparams(("arbitrary",)),
        name="moe_experts",
    )(tile_expert, n_used, xs, wgu, wd)


def _combine_kernel(dc_ref, dn_ref, ys_hbm, meta_ref, ht_ref, g_ref, b_ref, o_ref, obf_ref, buf, sem):
    i = pl.program_id(0)
    n = pl.num_programs(0)
    tt = dc_ref.shape[-1]
    slot = lax.rem(i, 2)

    def gather(d_ref, s):
        def issue(j, carry):
            for k in range(TOP_K):
                pltpu.make_async_copy(_tile_at(ys_hbm, d_ref[0, k, j]), _tile(buf.at[s, k], j),
                                      sem.at[s]).start(priority=k)
            return carry
        lax.fori_loop(0, tt, issue, 0, unroll=DMA_ISSUE_UNROLL)

    @pl.when(i == 0)
    def _():
        gather(dc_ref, 0)

    @pl.when(i + 1 < n)
    def _():
        gather(dn_ref, 1 - slot)

    for k in range(TOP_K):
        pltpu.make_async_copy(ys_hbm.at[pl.ds(0, tt * TOKEN_TILE_ROWS)], buf.at[slot, k], sem.at[slot]).wait()
    meta = meta_ref[...]
    ffn = (meta[:, META_W0:META_W0 + 1] * _token_tiles_to_rows(buf.at[slot, 0], tt)
           + meta[:, META_W1:META_W1 + 1] * _token_tiles_to_rows(buf.at[slot, 1], tt))
    y = _ln_rows(DEEPNORM_ALPHA * _token_tiles_to_rows(ht_ref, tt) + ffn, g_ref[...], b_ref[...])
    o_ref[...] = y
    obf_ref[...] = y.astype(BF16)


def _combine(ys, slots, meta, h_tiles, g, b):
    t, d = h_tiles.shape[0] // TOKEN_TILE_ROWS, D_MODEL
    tt = slots.shape[-1]
    n = t // tt
    cur = pl.BlockSpec((1, SUBLANES, tt), lambda i: (i, 0, 0), memory_space=pltpu.SMEM)
    nxt = pl.BlockSpec((1, SUBLANES, tt), lambda i: (jnp.minimum(i + 1, n - 1), 0, 0), memory_space=pltpu.SMEM)
    row = lambda i: (i, 0)
    const = lambda i: (0, 0)
    return pl.pallas_call(
        _combine_kernel,
        out_shape=(jax.ShapeDtypeStruct((t, d), F32), jax.ShapeDtypeStruct((t, d), BF16)),
        grid=(n,),
        in_specs=[cur, nxt, pl.BlockSpec(memory_space=pl.ANY),
                  pl.BlockSpec((tt, LANES), row), pl.BlockSpec((tt * TOKEN_TILE_ROWS, LANES), row),
                  pl.BlockSpec((1, d), const), pl.BlockSpec((1, d), const)],
        out_specs=(pl.BlockSpec((tt, d), row), pl.BlockSpec((tt, d), row)),
        scratch_shapes=[pltpu.VMEM((2, TOP_K, tt * TOKEN_TILE_ROWS, LANES), F32), pltpu.SemaphoreType.DMA((2,))],
        compiler_params=_cparams(("arbitrary",)),
        name="moe_combine_ln",
    )(slots, slots, ys, meta, h_tiles, g, b)


def _segment_tables(counts, t):
    tm = EXPERT_ROW_TILE
    n_rows = TOP_K * t + N_EXPERTS * tm
    cnt = counts[0, :N_EXPERTS].astype(jnp.int32)
    tiles = (cnt + tm - 1) // tm
    ends = jnp.cumsum(tiles)
    offs = jnp.zeros((1, LANES), F32).at[0, :N_EXPERTS].set(((ends - tiles) * tm).astype(F32))
    tile_ids = jnp.arange(n_rows // tm, dtype=jnp.int32)
    tile_expert = jnp.minimum(jnp.sum((tile_ids[:, None] >= ends[None, :]).astype(jnp.int32), axis=1),
                              N_EXPERTS - 1)
    last_tile = jnp.maximum(ends - 1, 0).astype(jnp.int32)
    return n_rows, offs, tile_expert, ends[-1:].astype(jnp.int32), last_tile


def _sparse_moe(h_tiles, meta, counts, wgu, wd, g, b):
    t = h_tiles.shape[0] // TOKEN_TILE_ROWS
    n_rows, offs, tile_expert, n_used, last_tile = _segment_tables(counts, t)
    slots = _slots(meta, offs, min(MOE_TOKENS, t))
    xs = _dispatch(h_tiles, slots, _zero_last_tiles(last_tile, n_rows))
    ys = _experts(xs, tile_expert, n_used, wgu, wd)
    return _combine(ys, slots, meta, h_tiles, g, b)


def _rotate_half_cols(w):
    half = w.shape[-1] // 2
    return jnp.concatenate([-w[..., half:], w[..., :half]], axis=-1)


def _pack_input_proj(w_in, b_in):
    d = w_in.shape[0]
    offs = np.cumsum((0, GLA_QK_W, GLA_QK_W, GLA_V_W, GLA_V_W, GLA_GATE_RANK, GLA_GATE_RANK,
                      MLA_Q_RANK, MLA_KV_RANK, MLA_ROPE, D_MODEL, D_MODEL))

    def seg(i):
        return w_in[:, offs[i]:offs[i + 1]], b_in[offs[i]:offs[i + 1]]

    w = jnp.zeros((d, N_PROJ), F32)
    b = jnp.zeros((N_PROJ,), F32)

    def put(w, b, col, ws, bs):
        return w.at[:, col:col + ws.shape[1]].set(ws), b.at[col:col + ws.shape[1]].set(bs)

    for i, col in ((0, COL_GQ), (1, COL_GK), (2, COL_GV), (3, COL_GR), (9, COL_GA), (10, COL_GB),
                   (6, COL_CQ), (7, COL_CKV), (4, COL_SMALL), (5, COL_SMALL + GLA_GATE_RANK)):
        w, b = put(w, b, col, *seg(i))
    wkr, bkr = seg(8)
    w, b = put(w, b, COL_SMALL + KR_LANE, wkr, bkr)
    w, b = put(w, b, COL_SMALL2 + KR_LANE, _rotate_half_cols(wkr), _rotate_half_cols(bkr))
    return w.astype(BF16), b.reshape(1, N_PROJ)


def _pack_decay(wa2, ba, lane0):
    w = jnp.zeros((LANES, GLA_QK_W), F32).at[lane0:lane0 + GLA_GATE_RANK].set(wa2)
    return w.astype(BF16), ba.reshape(1, GLA_QK_W)


def _pack_mla(w_uq, w_ukv, q_norm_g, kv_norm_g):
    wq = w_uq.reshape(MLA_Q_RANK, MLA_HEADS, MLA_QK)
    zq = jnp.zeros((MLA_Q_RANK, MLA_HEADS, HEAD_PAD - MLA_QK), F32)
    wq_p = jnp.concatenate([wq, zq], axis=-1)

    def pad_rows(w):
        w = w.reshape(MLA_Q_RANK, MLA_HEADS * HEAD_PAD)
        return jnp.pad(w, ((0, CQ_PAD - MLA_Q_RANK), (0, 0))).astype(BF16)

    wkv = w_ukv.reshape(MLA_KV_RANK, MLA_HEADS, MLA_NOPE + MLA_V)
    wk_p = jnp.concatenate([wkv[..., :MLA_NOPE],
                            jnp.zeros((MLA_KV_RANK, MLA_HEADS, HEAD_PAD - MLA_NOPE), F32)], axis=-1)
    wk_p = wk_p.reshape(MLA_KV_RANK, MLA_HEADS * HEAD_PAD).astype(BF16)
    wv_p = wkv[..., MLA_NOPE:].reshape(MLA_KV_RANK, MLA_HEADS * MLA_V).T.astype(BF16)
    gq = jnp.pad(q_norm_g, (0, CQ_PAD - MLA_Q_RANK)).reshape(1, CQ_PAD)
    return pad_rows(wq_p), wk_p, wv_p, gq, kv_norm_g.reshape(1, MLA_KV_RANK)


def _rope_lane_table():
    inv = ROPE_BASE ** (-jnp.arange(0, MLA_ROPE, 2, dtype=F32) / MLA_ROPE)
    return jnp.tile(inv, ROPE_PACK).reshape(1, LANES)


def _pack_router(w_grp, b_grp, w_exp, b_exp):
    d = w_grp.shape[0]
    w = jnp.zeros((d, LANES), F32).at[:, :N_EXPERTS].set(w_exp).at[:, GRP_LANE:GRP_LANE + N_GROUPS].set(w_grp)
    b = jnp.zeros((LANES,), F32).at[:N_EXPERTS].set(b_exp).at[GRP_LANE:GRP_LANE + N_GROUPS].set(b_grp)
    return w, b.reshape(1, LANES)


def kernel(x, positions, ln_emb_g, ln_emb_b, w_in, b_in, gla_wa2_f, gla_ba_f, gla_wa2_b, gla_ba_b, gla_norm_g, mla_q_norm_g, mla_w_uq, mla_kv_norm_g, mla_w_ukv, w_out, ln1_g, ln1_b, w_grp, b_grp, w_exp, b_exp, w_gate, w_up, w_down, ln2_g, ln2_b):
    batch, seq, d = x.shape
    t = batch * seq
    pos = jnp.repeat(positions.reshape(t // ROPE_PACK, ROPE_PACK).astype(F32), MLA_ROPE // 2, axis=1)
    inv_lane = _rope_lane_table()
    for l in range(DEPTH):
        w_p, b_p = _pack_input_proj(w_in[l], b_in[l])
        if l == 0:
            h, proj = _input_proj(x.reshape(t, d), w_p, b_p, ln=(ln_emb_g, ln_emb_b))
        else:
            proj = _input_proj(hb, w_p, b_p)
        wa_f, ba_f = _pack_decay(gla_wa2_f[l], gla_ba_f[l], 0)
        wa_b, ba_b = _pack_decay(gla_wa2_b[l], gla_ba_b[l], GLA_GATE_RANK)
        o_f = _gla(proj, wa_f, ba_f, batch, seq, reverse=False)
        o_gla = _gla(proj, wa_b, ba_b, batch, seq, reverse=True, o_fwd=o_f,
                     norm_g=gla_norm_g[l].reshape(1, GLA_V_W))
        wq, wk, wv, gq, gkv = _pack_mla(mla_w_uq[l], mla_w_ukv[l], mla_q_norm_g[l], mla_kv_norm_g[l])
        q2, k2, v = _mla_prep(proj, pos, inv_lane, gq, gkv, wq, wk, wv)
        o_mla = _attention(q2, k2, v, batch, seq)
        h_tiles = _merge_out(o_gla, o_mla, proj, h, w_out[l].astype(BF16),
                           ln1_g[l].reshape(1, d), ln1_b[l].reshape(1, d))
        w_r, b_r = _pack_router(w_grp[l], b_grp[l], w_exp[l], b_exp[l])
        meta, counts = _router(h_tiles, w_r, b_r)
        wgu = jnp.concatenate([w_gate[l], w_up[l]], axis=-1).astype(BF16)
        h, hb = _sparse_moe(h_tiles, meta, counts, wgu, w_down[l].astype(BF16),
                            ln2_g[l].reshape(1, d), ln2_b[l].reshape(1, d))
    return h.reshape(batch, seq, d)
```

```python
import functools

import numpy as np
import jax
import jax.numpy as jnp
from jax import lax
from jax.experimental import pallas as pl
from jax.experimental.pallas import tpu as pltpu

F32 = jnp.float32
BF16 = jnp.bfloat16

D_MODEL = 1024
DEPTH = 2
GLA_HEADS = 4
GLA_DK = 128
GLA_DV = 256
GLA_GATE_RANK = 16
GLA_TAU = 16.0
MLA_HEADS = 16
MLA_NOPE = 64
MLA_ROPE = 32
MLA_V = 64
MLA_QK = MLA_NOPE + MLA_ROPE
MLA_Q_RANK = 384
MLA_KV_RANK = 128
ROPE_BASE = 10000.0
N_GROUPS = 8
EXPERTS_PER_GROUP = 4
N_EXPERTS = 32
D_EXPERT = 256
GLA_QK_W = GLA_HEADS * GLA_DK
GLA_V_W = GLA_HEADS * GLA_DV
DEEPNORM_ALPHA = (2.0 * DEPTH) ** 0.25
LN_EPS = 1e-5
RMS_EPS = 1e-6

LANES = 128
VMEM_LIMIT_BYTES = 56 * 1024 * 1024

COL_GQ = 0
COL_GK = 512
COL_GV = 1024
COL_GR = 2048
COL_GA = 3072
COL_GB = 4096
COL_CQ = 5120
CQ_PAD = 512
COL_SMALL = 5632
COL_SMALL2 = 5760
COL_CKV = 5888
N_PROJ = 6144
HEAD_PAD = 128
KR_LANE = 64

GLA_CHUNK = 128
NEG_BIG = -1e30


def _cparams(sem):
    return pltpu.CompilerParams(dimension_semantics=sem, vmem_limit_bytes=VMEM_LIMIT_BYTES)


def _ln_rows(x, g, b):
    mu = jnp.mean(x, axis=-1, keepdims=True)
    xc = x - mu
    var = jnp.mean(xc * xc, axis=-1, keepdims=True)
    return xc * lax.rsqrt(var + LN_EPS) * g + b


PROJ_COL_CHUNK = 512


def _proj_kernel(*refs, embed_ln):
    if embed_ln:
        x_ref, g_ref, beta_ref, w_ref, b_ref, h_ref, o_ref = refs
        h = _ln_rows(x_ref[...], g_ref[...], beta_ref[...])
        h_ref[...] = h
        x = h.astype(BF16)
    else:
        x_ref, w_ref, b_ref, o_ref = refs
        x = x_ref[...]
    for c in range(N_PROJ // PROJ_COL_CHUNK):
        sl = slice(c * PROJ_COL_CHUNK, (c + 1) * PROJ_COL_CHUNK)
        acc = jnp.dot(x, w_ref[:, sl], preferred_element_type=F32) + b_ref[:, sl]
        o_ref[:, sl] = acc.astype(o_ref.dtype)


def _input_proj(x, w, b, ln=None, tm=512):
    t, d = x.shape
    row = lambda i: (i, 0)
    const = lambda i: (0, 0)
    vec = pl.BlockSpec((1, d), const)
    in_specs = [pl.BlockSpec((tm, d), row)] + ([vec, vec] if ln else []) + [
        pl.BlockSpec((d, N_PROJ), const), pl.BlockSpec((1, N_PROJ), const)]
    proj_shape = jax.ShapeDtypeStruct((t, N_PROJ), BF16)
    proj_spec = pl.BlockSpec((tm, N_PROJ), row)
    args = (x,) + (tuple(v.reshape(1, d) for v in ln) if ln else ()) + (w, b)
    return pl.pallas_call(
        functools.partial(_proj_kernel, embed_ln=ln is not None),
        out_shape=(jax.ShapeDtypeStruct((t, d), F32), proj_shape) if ln else proj_shape,
        grid=(t // tm,),
        in_specs=in_specs,
        out_specs=(pl.BlockSpec((tm, d), row), proj_spec) if ln else proj_spec,
        compiler_params=_cparams(("parallel",)),
        name="ln_input_proj" if ln else "input_proj",
    )(*args)


def _log_sigmoid(x):
    return jnp.minimum(x, 0.0) - jnp.log(1.0 + jnp.exp(-jnp.abs(x)))


def _split_bf16(x):
    hi = x.astype(BF16)
    lo = (x - hi.astype(F32)).astype(BF16)
    return hi, lo


def _gla_kernel(*refs, reverse, n_chunks):
    if reverse:
        (q_ref, k_ref, v_ref, z_ref, wa_ref, ba_ref, of_ref, gr_ref, ng_ref, o_ref, state_ref) = refs
    else:
        (q_ref, k_ref, v_ref, z_ref, wa_ref, ba_ref, o_ref, state_ref) = refs
    c_len = GLA_CHUNK

    @pl.when(pl.program_id(1) == 0)
    def _():
        state_ref[...] = jnp.zeros_like(state_ref)

    row = lax.broadcasted_iota(jnp.int32, (c_len, c_len), 0)
    col = lax.broadcasted_iota(jnp.int32, (c_len, c_len), 1)
    if reverse:
        tri = (col >= row).astype(BF16)
        keep = col > row
        last = 0
    else:
        tri = (col <= row).astype(BF16)
        keep = col <= row
        last = c_len - 1

    seqs = range(q_ref.shape[0])

    def decays(bi, rows):
        z = z_ref[bi, rows, :]
        la = _log_sigmoid(jnp.dot(z, wa_ref[...], preferred_element_type=F32) + ba_ref[...]) * (1.0 / GLA_TAU)
        la_hi, la_lo = _split_bf16(la)
        b_all = (jnp.dot(tri, la_hi, preferred_element_type=F32)
                 + jnp.dot(tri, la_lo, preferred_element_type=F32))
        b_last = b_all[last:last + 1, :]
        q = q_ref[bi, rows, :].astype(F32)
        k = k_ref[bi, rows, :].astype(F32)
        qd_all = (q * (jnp.exp(b_all) * (GLA_DK ** -0.5))).astype(BF16)
        kinv_all = (k * jnp.exp(-b_all)).astype(BF16)
        kend_all = k * jnp.exp(b_last - b_all)
        dec_all = jnp.broadcast_to(jnp.exp(b_last), (c_len, GLA_QK_W))
        return qd_all, kinv_all, kend_all, dec_all

    def chunk_rows(cc):
        c = (n_chunks - 1 - cc) if reverse else cc
        return slice(c * c_len, (c + 1) * c_len)

    pre_next = [decays(bi, chunk_rows(0)) for bi in seqs]
    for cc in range(n_chunks):
        rows = chunk_rows(cc)
        pre = pre_next
        if cc + 1 < n_chunks:
            pre_next = [decays(bi, chunk_rows(cc + 1)) for bi in seqs]
        for h in range(GLA_HEADS):
            ks = slice(h * GLA_DK, (h + 1) * GLA_DK)
            vs = slice(h * GLA_DV, (h + 1) * GLA_DV)
            v = [v_ref[bi, rows, vs] for bi in seqs]
            scores = [lax.dot_general(pre[bi][0][:, ks], pre[bi][1][:, ks], (((1,), (1,)), ((), ())),
                                      preferred_element_type=F32) for bi in seqs]
            state = [state_ref[bi, h] for bi in seqs]
            o = [jnp.dot(jnp.where(keep, scores[bi], 0.0).astype(BF16), v[bi], preferred_element_type=F32)
                 + jnp.dot(pre[bi][0][:, ks], state[bi].astype(BF16), preferred_element_type=F32) for bi in seqs]
            for bi in seqs:
                kend_t = pre[bi][2][:, ks].T.astype(BF16)
                dec_t = pre[bi][3][:, ks].T
                dec = jnp.concatenate([dec_t, dec_t], axis=1)
                state_ref[bi, h] = dec * state[bi] + jnp.dot(kend_t, v[bi], preferred_element_type=F32)
            for bi in seqs:
                ob = o[bi]
                if reverse:
                    ob = ob + of_ref[bi, rows, vs]
                    ms = jnp.mean(ob * ob, axis=-1, keepdims=True)
                    ob = ob * lax.rsqrt(ms + RMS_EPS) * ng_ref[:, vs]
                    g = gr_ref[bi, rows, vs].astype(F32)
                    ob = ob * (g * jax.nn.sigmoid(g))
                o_ref[bi, rows, vs] = ob.astype(o_ref.dtype)


GLA_SEQS_PER_STEP = 2


def _gla(proj, wa, ba, batch, seq, *, reverse, o_fwd=None, norm_g=None, ts=512):
    ts = min(ts, seq)
    nblk = seq // ts
    nb = GLA_SEQS_PER_STEP if batch % GLA_SEQS_PER_STEP == 0 else 1
    proj3 = proj.reshape(batch, seq, N_PROJ)

    def blk(i):
        return (nblk - 1 - i) if reverse else i

    def cols(width, col):
        return pl.BlockSpec((nb, ts, width), lambda b, i: (b, blk(i), col // width))

    in_specs = [cols(GLA_QK_W, COL_GQ), cols(GLA_QK_W, COL_GK), cols(GLA_V_W, COL_GV), cols(LANES, COL_SMALL),
                pl.BlockSpec((LANES, GLA_QK_W), lambda b, i: (0, 0)),
                pl.BlockSpec((1, GLA_QK_W), lambda b, i: (0, 0))]
    args = [proj3, proj3, proj3, proj3, wa, ba]
    if reverse:
        in_specs += [cols(GLA_V_W, 0), cols(GLA_V_W, COL_GR), pl.BlockSpec((1, GLA_V_W), lambda b, i: (0, 0))]
        args += [o_fwd.reshape(batch, seq, GLA_V_W), proj3, norm_g]
    out = pl.pallas_call(
        functools.partial(_gla_kernel, reverse=reverse, n_chunks=ts // GLA_CHUNK),
        out_shape=jax.ShapeDtypeStruct((batch, seq, GLA_V_W), BF16 if reverse else F32),
        grid=(batch // nb, nblk),
        in_specs=in_specs,
        out_specs=cols(GLA_V_W, 0),
        scratch_shapes=[pltpu.VMEM((nb, GLA_HEADS, GLA_DK, GLA_DV), F32)],
        compiler_params=_cparams(("parallel", "arbitrary")),
        name="gla_bwd" if reverse else "gla_fwd",
    )(*args)
    return out.reshape(batch * seq, GLA_V_W)


MLA_COL_CHUNK = 512
ROPE_PACK = LANES // (MLA_ROPE // 2)


def _mla_prep_kernel(cq_ref, ckv_ref, sm_ref, sm2_ref, pos_ref, inv_ref, gq_ref, gkv_ref,
                     wq_ref, wk_ref, wv_ref, q_ref, k_ref, v_ref):
    cq = cq_ref[...].astype(F32)
    msq = jnp.sum(cq * cq, axis=-1, keepdims=True) * (1.0 / MLA_Q_RANK)
    cqn = (cq * lax.rsqrt(msq + RMS_EPS) * gq_ref[...]).astype(BF16)
    ckv = ckv_ref[...].astype(F32)
    mskv = jnp.mean(ckv * ckv, axis=-1, keepdims=True)
    ckvn = (ckv * lax.rsqrt(mskv + RMS_EPS) * gkv_ref[...]).astype(BF16)

    ts = cq_ref.shape[0]
    ang = pos_ref[...] * inv_ref[...]
    tok = lax.broadcasted_iota(jnp.int32, (ts, ts // ROPE_PACK), 0)
    grp = lax.broadcasted_iota(jnp.int32, (ts, ts // ROPE_PACK), 1)
    to_rows = (tok // ROPE_PACK == grp).astype(BF16)
    src = lax.broadcasted_iota(jnp.int32, (LANES, LANES), 0)
    dst = lax.broadcasted_iota(jnp.int32, (LANES, LANES), 1)
    half = MLA_ROPE // 2
    to_lanes = ((dst >= KR_LANE) & (dst < KR_LANE + MLA_ROPE) & ((dst - KR_LANE) % half == src % half)).astype(BF16)
    lane = lax.broadcasted_iota(jnp.int32, (ts, LANES), 1)
    own = lane // half == lax.broadcasted_iota(jnp.int32, (ts, LANES), 0) % ROPE_PACK

    def spread(packed):
        out = 0.0
        for part in _split_bf16(packed):
            rows = jnp.dot(to_rows, part, preferred_element_type=F32)
            rows = jnp.where(own, rows, 0.0).astype(BF16)
            out = out + jnp.dot(rows, to_lanes, preferred_element_type=F32)
        return out

    cos = jnp.where(lane < KR_LANE, 1.0, spread(jnp.cos(ang)))
    sin = spread(jnp.sin(ang))
    kr = jnp.where(lane >= KR_LANE, sm_ref[...].astype(F32), 0.0)
    kr = kr * cos + sm2_ref[...].astype(F32) * sin

    heads_per_chunk = MLA_COL_CHUNK // HEAD_PAD
    cos_t = jnp.concatenate([cos] * heads_per_chunk, axis=1)
    sin_t = jnp.concatenate([sin] * heads_per_chunk, axis=1)
    kr_t = jnp.concatenate([kr] * heads_per_chunk, axis=1)
    qscale = (MLA_QK ** -0.5) * float(np.log2(np.e))
    chunk_lane = lax.broadcasted_iota(jnp.int32, (ts, MLA_COL_CHUNK), 1) % HEAD_PAD
    first_half = chunk_lane < KR_LANE + half
    for c in range(MLA_HEADS * HEAD_PAD // MLA_COL_CHUNK):
        sl = slice(c * MLA_COL_CHUNK, (c + 1) * MLA_COL_CHUNK)
        q = jnp.dot(cqn, wq_ref[:, sl], preferred_element_type=F32)
        qr = jnp.where(first_half, -pltpu.roll(q, MLA_COL_CHUNK - half, axis=1), pltpu.roll(q, half, axis=1))
        q_ref[:, sl] = ((q * cos_t + qr * sin_t) * qscale).astype(BF16)
        k = jnp.dot(ckvn, wk_ref[:, sl], preferred_element_type=F32)
        k_ref[:, sl] = (k + kr_t).astype(BF16)
    v_ref[...] = lax.dot_general(wv_ref[...], ckvn, (((1,), (1,)), ((), ())),
                                 preferred_element_type=F32).astype(BF16)


def _mla_prep(proj, pos, inv_lane, gq, gkv, wq, wk, wv, ts=512):
    t = proj.shape[0]
    hp = MLA_HEADS * HEAD_PAD
    vw = MLA_HEADS * MLA_V
    const = lambda i: (0, 0)
    return pl.pallas_call(
        _mla_prep_kernel,
        out_shape=(jax.ShapeDtypeStruct((t, hp), BF16), jax.ShapeDtypeStruct((t, hp), BF16),
                   jax.ShapeDtypeStruct((vw, t), BF16)),
        grid=(t // ts,),
        in_specs=[
            pl.BlockSpec((ts, CQ_PAD), lambda i: (i, COL_CQ // CQ_PAD)),
            pl.BlockSpec((ts, LANES), lambda i: (i, COL_CKV // LANES)),
            pl.BlockSpec((ts, LANES), lambda i: (i, COL_SMALL // LANES)),
            pl.BlockSpec((ts, LANES), lambda i: (i, COL_SMALL2 // LANES)),
            pl.BlockSpec((ts // ROPE_PACK, LANES), lambda i: (i, 0)),
            pl.BlockSpec((1, LANES), const),
            pl.BlockSpec((1, CQ_PAD), const),
            pl.BlockSpec((1, MLA_KV_RANK), const),
            pl.BlockSpec((CQ_PAD, hp), const),
            pl.BlockSpec((MLA_KV_RANK, hp), const),
            pl.BlockSpec((vw, MLA_KV_RANK), const),
        ],
        out_specs=(pl.BlockSpec((ts, hp), lambda i: (i, 0)), pl.BlockSpec((ts, hp), lambda i: (i, 0)),
                   pl.BlockSpec((vw, ts), lambda i: (0, i))),
        compiler_params=_cparams(("parallel",)),
        name="mla_prep",
    )(proj, proj, proj, proj, pos, inv_lane, gq, gkv, wq, wk, wv)


SUBLANES = 8


BOUND_SLACK = 1.01
SHIFT_LIMIT = 48.0
SCORE_CHUNKS_AHEAD = 3


def _attn_kernel(q_ref, k_ref, vt_ref, o_ref, s_scr, bound_scr, safe_scr, *, tk):
    seq = k_ref.shape[0]
    tq = s_scr.shape[-1]
    nk = seq // tk
    nq = seq // tq
    nt = (((1,), (1,)), ((), ()))
    head = lambda hh: slice(hh * HEAD_PAD, (hh + 1) * HEAD_PAD)
    ones8 = jnp.ones((SUBLANES, HEAD_PAD), BF16)

    def max_key_norm_sq(hh):
        k = k_ref[:, head(hh)].astype(F32)
        k_sq = lax.dot_general(ones8, (k * k).astype(BF16), nt, preferred_element_type=F32)
        return jnp.max(k_sq[0:1], axis=1, keepdims=True)

    for hh in range(2):
        qf = q_ref[:, head(hh)].astype(F32)
        q_sq = lax.dot_general(ones8, (qf * qf).astype(BF16), nt, preferred_element_type=F32)
        b8 = jnp.sqrt(q_sq * max_key_norm_sq(hh)) * BOUND_SLACK
        for j in range(nq):
            bound_scr[hh, j] = b8[:, j * tq:(j + 1) * tq]
    for j in range(nq):
        worst = jnp.max(jnp.maximum(bound_scr[0, j], bound_scr[1, j]))
        safe_scr[j] = (worst < SHIFT_LIMIT).astype(jnp.int32)

    def scores(hh, q, c):
        return lax.dot_general(k_ref[c * tk:(c + 1) * tk, head(hh)], q, nt, preferred_element_type=F32)

    def weigh(hh, c, p, l8, acc):
        l8 = l8 + jnp.sum(p.reshape(tk // SUBLANES, SUBLANES, tq), axis=0)
        vt = vt_ref[hh * MLA_V:(hh + 1) * MLA_V, c * tk:(c + 1) * tk]
        return l8, acc + jnp.dot(vt, p.astype(BF16), preferred_element_type=F32)

    def q_tile(i, carry):
        rows = pl.ds(pl.multiple_of(i * tq, tq), tq)
        qs = [q_ref[rows, head(hh)] for hh in range(2)]
        bound = [bound_scr[hh, i, 0:1, :] for hh in range(2)]
        safe = safe_scr[i] != 0
        zeros = lambda n: jnp.zeros((n, tq), F32)

        @pl.when(safe)
        def _():
            l8, acc = [zeros(SUBLANES)] * 2, [zeros(MLA_V)] * 2
            ahead = min(SCORE_CHUNKS_AHEAD, nk)
            pending = [[scores(hh, qs[hh], c) for hh in range(2)] for c in range(ahead)]
            for c in range(nk):
                st = pending.pop(0)
                if c + ahead < nk:
                    pending.append([scores(hh, qs[hh], c + ahead) for hh in range(2)])
                for hh in range(2):
                    l8[hh], acc[hh] = weigh(hh, c, jnp.exp2(st[hh] - bound[hh]), l8[hh], acc[hh])
            out = [acc[hh] / jnp.sum(l8[hh], axis=0, keepdims=True) for hh in range(2)]
            o_ref[rows, :] = jnp.concatenate(out, axis=0).T.astype(BF16)

        @pl.when(jnp.logical_not(safe))
        def _():
            out = []
            for hh in range(2):
                m8 = jnp.full((SUBLANES, tq), NEG_BIG, F32)
                for c in range(nk):
                    st = scores(hh, qs[hh], c)
                    s_scr[c] = st
                    m8 = jnp.maximum(m8, jnp.max(st.reshape(tk // SUBLANES, SUBLANES, tq), axis=0))
                m = jnp.max(m8, axis=0, keepdims=True)
                l8, acc = zeros(SUBLANES), zeros(MLA_V)
                for c in range(nk):
                    l8, acc = weigh(hh, c, jnp.exp2(s_scr[c] - m), l8, acc)
                out.append(acc / jnp.sum(l8, axis=0, keepdims=True))
            o_ref[rows, :] = jnp.concatenate(out, axis=0).T.astype(BF16)

        return carry

    lax.fori_loop(0, nq, q_tile, 0)


def _attention(q2, k2, vt, batch, seq, tq=512, tk=512):
    tq = min(tq, seq)
    tk = min(tk, seq)
    t = batch * seq
    pairs = MLA_HEADS // 2
    return pl.pallas_call(
        functools.partial(_attn_kernel, tk=tk),
        scratch_shapes=[pltpu.VMEM((seq // tk, tk, tq), F32), pltpu.VMEM((2, seq // tq, SUBLANES, tq), F32),
                        pltpu.SMEM((seq // tq,), jnp.int32)],
        out_shape=jax.ShapeDtypeStruct((t, MLA_HEADS * MLA_V), BF16),
        grid=(batch, pairs),
        in_specs=[
            pl.BlockSpec((seq, 2 * HEAD_PAD), lambda b, p: (b, p)),
            pl.BlockSpec((seq, 2 * HEAD_PAD), lambda b, p: (b, p)),
            pl.BlockSpec((2 * MLA_V, seq), lambda b, p: (p, b)),
        ],
        out_specs=pl.BlockSpec((seq, 2 * MLA_V), lambda b, p: (b, p)),
        compiler_params=_cparams(("parallel", "parallel")),
        name="mla_attention",
    )(q2, k2, vt)


TOKEN_TILE_ROWS = D_MODEL // LANES


def _rows_to_token_tiles(ref, y):
    n = y.shape[0]
    for s in range(TOKEN_TILE_ROWS):
        ref[pl.ds(s, n, stride=TOKEN_TILE_ROWS), :] = y[:, s * LANES:(s + 1) * LANES]


def _token_tiles_to_rows(ref, n):
    return jnp.concatenate([ref[pl.ds(s, n, stride=TOKEN_TILE_ROWS), :] for s in range(TOKEN_TILE_ROWS)], axis=1)


def _merge_out_kernel(oa_ref, ob_ref, ga_ref, gb_ref, h_ref, w_ref, g_ref, b_ref, ot_ref):
    merged = (jax.nn.sigmoid(ga_ref[...].astype(F32)) * oa_ref[...].astype(F32)
              + jax.nn.sigmoid(gb_ref[...].astype(F32)) * ob_ref[...].astype(F32))
    mix = jnp.dot(merged.astype(BF16), w_ref[...], preferred_element_type=F32)
    y = _ln_rows(DEEPNORM_ALPHA * h_ref[...] + mix, g_ref[...], b_ref[...])
    _rows_to_token_tiles(ot_ref, y)


def _merge_out(o_gla, o_mla, proj, h, w_out, g, b, tm=512):
    t, d = h.shape
    row = lambda i: (i, 0)
    const = lambda i: (0, 0)
    return pl.pallas_call(
        _merge_out_kernel,
        out_shape=jax.ShapeDtypeStruct((t * TOKEN_TILE_ROWS, LANES), F32),
        grid=(t // tm,),
        in_specs=[
            pl.BlockSpec((tm, d), row),
            pl.BlockSpec((tm, d), row),
            pl.BlockSpec((tm, d), lambda i: (i, COL_GA // D_MODEL)),
            pl.BlockSpec((tm, d), lambda i: (i, COL_GB // D_MODEL)),
            pl.BlockSpec((tm, d), row),
            pl.BlockSpec((d, d), const),
            pl.BlockSpec((1, d), const),
            pl.BlockSpec((1, d), const),
        ],
        out_specs=pl.BlockSpec((tm * TOKEN_TILE_ROWS, LANES), row),
        compiler_params=_cparams(("parallel",)),
        name="merge_out_ln",
    )(o_gla, o_mla, proj, proj, h, w_out, g, b)


GRP_LANE = N_EXPERTS
META_W0, META_W1, META_E0, META_E1, META_R0, META_R1 = range(6)


def _router_kernel(x_ref, w_ref, b_ref, meta_ref, cnt_ref, carry_ref):
    @pl.when(pl.program_id(0) == 0)
    def _():
        carry_ref[...] = jnp.zeros_like(carry_ref)

    tm = x_ref.shape[0] // TOKEN_TILE_ROWS
    x_hi, x_lo = _split_bf16(_token_tiles_to_rows(x_ref, tm))
    w_hi, w_lo = _split_bf16(w_ref[...])
    logits = (jnp.dot(x_hi, w_hi, preferred_element_type=F32)
              + jnp.dot(x_lo, w_hi, preferred_element_type=F32)
              + jnp.dot(x_hi, w_lo, preferred_element_type=F32)) + b_ref[...]
    lane = lax.broadcasted_iota(jnp.int32, logits.shape, 1)
    is_grp = (lane >= GRP_LANE) & (lane < GRP_LANE + N_GROUPS)
    gl = jnp.where(is_grp, logits, NEG_BIG)
    gmax = jnp.max(gl, axis=1, keepdims=True)
    g_lane = jnp.min(jnp.where(gl == gmax, lane, 4 * LANES), axis=1, keepdims=True)
    g_w = 1.0 / jnp.sum(jnp.where(is_grp, jnp.exp(gl - gmax), 0.0), axis=1, keepdims=True)
    lo_lane = (g_lane - GRP_LANE) * EXPERTS_PER_GROUP
    in_grp = (lane >= lo_lane) & (lane < lo_lane + EXPERTS_PER_GROUP)
    el = jnp.where(in_grp, logits, NEG_BIG)
    v1 = jnp.max(el, axis=1, keepdims=True)
    i1 = jnp.min(jnp.where(el == v1, lane, 4 * LANES), axis=1, keepdims=True)
    el2 = jnp.where(lane == i1, NEG_BIG, el)
    v2 = jnp.max(el2, axis=1, keepdims=True)
    i2 = jnp.min(jnp.where(el2 == v2, lane, 4 * LANES), axis=1, keepdims=True)
    e2 = jnp.exp(v2 - v1)
    w1 = g_w / (1.0 + e2)
    w2 = g_w * e2 / (1.0 + e2)

    onehot = jnp.where(lane == i1, 1.0, jnp.where(lane == i2, 1.0, 0.0))
    r = lax.broadcasted_iota(jnp.int32, (tm, tm), 0)
    c = lax.broadcasted_iota(jnp.int32, (tm, tm), 1)
    earlier = (c < r).astype(BF16)
    base = carry_ref[...] + jnp.dot(earlier, onehot.astype(BF16), preferred_element_type=F32)
    rank1 = jnp.sum(jnp.where(lane == i1, base, 0.0), axis=1, keepdims=True)
    rank2 = jnp.sum(jnp.where(lane == i2, base, 0.0), axis=1, keepdims=True)
    meta = jnp.zeros(logits.shape, F32)
    for ln, val in ((META_W0, w1), (META_W1, w2), (META_E0, i1.astype(F32)), (META_E1, i2.astype(F32)),
                    (META_R0, rank1), (META_R1, rank2)):
        meta = jnp.where(lane == ln, val, meta)
    meta_ref[...] = meta
    carry_ref[...] += jnp.sum(onehot, axis=0, keepdims=True)
    cnt_ref[...] = carry_ref[...]


def _router(h_tiles, w_r, b_r, tm=512):
    t, d = h_tiles.shape[0] // TOKEN_TILE_ROWS, D_MODEL
    return pl.pallas_call(
        _router_kernel,
        out_shape=(jax.ShapeDtypeStruct((t, LANES), F32), jax.ShapeDtypeStruct((1, LANES), F32)),
        grid=(t // tm,),
        in_specs=[pl.BlockSpec((tm * TOKEN_TILE_ROWS, LANES), lambda i: (i, 0)),
                  pl.BlockSpec((d, LANES), lambda i: (0, 0)),
                  pl.BlockSpec((1, LANES), lambda i: (0, 0))],
        out_specs=(pl.BlockSpec((tm, LANES), lambda i: (i, 0)), pl.BlockSpec((1, LANES), lambda i: (0, 0))),
        scratch_shapes=[pltpu.VMEM((1, LANES), F32)],
        compiler_params=_cparams(("arbitrary",)),
        name="router",
    )(h_tiles, w_r, b_r)


EXPERT_ROW_TILE = 512
MOE_TOKENS = 512
TOP_K = 2
DMA_ISSUE_UNROLL = 16


def _tile_at(ref, first_row):
    return ref.at[pl.ds(pl.multiple_of(first_row, TOKEN_TILE_ROWS), TOKEN_TILE_ROWS)]


def _tile(ref, token):
    return _tile_at(ref, token * TOKEN_TILE_ROWS)


def _slots_kernel(meta_ref, offs_ref, d_ref):
    meta = meta_ref[...]
    offs = offs_ref[...]
    lane = lax.broadcasted_iota(jnp.int32, meta.shape, 1)

    def slot(e_lane, r_lane):
        e = meta[:, e_lane:e_lane + 1].astype(jnp.int32)
        return jnp.sum(jnp.where(lane == e, offs, 0.0), axis=1, keepdims=True) + meta[:, r_lane:r_lane + 1]

    both = jnp.where(lane == 0, slot(META_E0, META_R0), jnp.where(lane == 1, slot(META_E1, META_R1), 0.0))
    d_ref[0] = (both.T[:SUBLANES, :] * float(TOKEN_TILE_ROWS)).astype(jnp.int32)


def _slots(meta, offs, tt):
    t = meta.shape[0]
    return pl.pallas_call(
        _slots_kernel,
        out_shape=jax.ShapeDtypeStruct((t // tt, SUBLANES, tt), jnp.int32),
        grid=(t // tt,),
        in_specs=[pl.BlockSpec((tt, LANES), lambda i: (i, 0)), pl.BlockSpec((1, LANES), lambda i: (0, 0))],
        out_specs=pl.BlockSpec((1, SUBLANES, tt), lambda i: (i, 0, 0)),
        compiler_params=_cparams(("arbitrary",)),
        name="moe_slots",
    )(meta, offs)


def _zero_tile_kernel(lt_ref, xs_ref):
    del lt_ref
    xs_ref[...] = jnp.zeros_like(xs_ref)


def _zero_last_tiles(last_tile, n_rows):
    blk = EXPERT_ROW_TILE * TOKEN_TILE_ROWS
    grid_spec = pltpu.PrefetchScalarGridSpec(
        num_scalar_prefetch=1, grid=(N_EXPERTS,), in_specs=[],
        out_specs=pl.BlockSpec((blk, LANES), lambda e, lt: (lt[e], 0)))
    return pl.pallas_call(
        _zero_tile_kernel,
        out_shape=jax.ShapeDtypeStruct((n_rows * TOKEN_TILE_ROWS, LANES), F32),
        grid_spec=grid_spec,
        compiler_params=_cparams(("arbitrary",)),
        name="moe_zero_tiles",
    )(last_tile)


def _dispatch_kernel(d_ref, ht_ref, xs_init_hbm, xs_hbm, sem):
    del xs_init_hbm
    tt = d_ref.shape[-1]

    def issue(j, carry):
        src = _tile(ht_ref, j)
        for k in range(TOP_K):
            pltpu.make_async_copy(src, _tile_at(xs_hbm, d_ref[0, k, j]), sem).start(priority=k)
        return carry

    lax.fori_loop(0, tt, issue, 0, unroll=DMA_ISSUE_UNROLL)
    for k in range(TOP_K):
        pltpu.make_async_copy(ht_ref, xs_hbm.at[pl.ds(0, tt * TOKEN_TILE_ROWS)], sem).wait()


def _dispatch(h_tiles, slots, xs_init):
    tt = slots.shape[-1]
    t = h_tiles.shape[0] // TOKEN_TILE_ROWS
    return pl.pallas_call(
        _dispatch_kernel,
        out_shape=jax.ShapeDtypeStruct(xs_init.shape, F32),
        grid=(t // tt,),
        in_specs=[pl.BlockSpec((1, SUBLANES, tt), lambda i: (i, 0, 0), memory_space=pltpu.SMEM),
                  pl.BlockSpec((tt * TOKEN_TILE_ROWS, LANES), lambda i: (i, 0)),
                  pl.BlockSpec(memory_space=pl.ANY)],
        out_specs=pl.BlockSpec(memory_space=pl.ANY),
        scratch_shapes=[pltpu.SemaphoreType.DMA(())],
        input_output_aliases={2: 0},
        compiler_params=_cparams(("arbitrary",)),
        name="moe_dispatch",
    )(slots, h_tiles, xs_init)


def _expert_kernel(te_ref, nu_ref, xs_ref, wgu_ref, wd_ref, ys_ref):
    del te_ref
    i = pl.program_id(0)
    tm = EXPERT_ROW_TILE

    @pl.when(i < nu_ref[0])
    def _():
        x = _token_tiles_to_rows(xs_ref, tm).astype(BF16)
        gu = jnp.dot(x, wgu_ref[0], preferred_element_type=F32)
        gate = gu[:, :D_EXPERT]
        hid = gate * jax.nn.sigmoid(gate) * gu[:, D_EXPERT:]
        _rows_to_token_tiles(ys_ref, jnp.dot(hid.astype(BF16), wd_ref[0], preferred_element_type=F32))

    @pl.when(i >= nu_ref[0])
    def _():
        ys_ref[...] = jnp.zeros_like(ys_ref)


def _experts(xs, tile_expert, n_used, wgu, wd):
    d = D_MODEL
    tm = EXPERT_ROW_TILE
    blk = pl.BlockSpec((tm * TOKEN_TILE_ROWS, LANES), lambda i, te, nu: (i, 0))
    blk_in = pl.BlockSpec((tm * TOKEN_TILE_ROWS, LANES), lambda i, te, nu: (jnp.minimum(i, nu[0] - 1), 0))
    grid_spec = pltpu.PrefetchScalarGridSpec(
        num_scalar_prefetch=2,
        grid=(xs.shape[0] // (tm * TOKEN_TILE_ROWS),),
        in_specs=[blk_in,
                  pl.BlockSpec((1, d, 2 * D_EXPERT), lambda i, te, nu: (te[i], 0, 0)),
                  pl.BlockSpec((1, D_EXPERT, d), lambda i, te, nu: (te[i], 0, 0))],
        out_specs=blk,
    )
    return pl.pallas_call(
        _expert_kernel,
        out_shape=jax.ShapeDtypeStruct(xs.shape, F32),
        grid_spec=grid_spec,
        compiler_params=_cparams(("arbitrary",)),
        name="moe_experts",
    )(tile_expert, n_used, xs, wgu, wd)


def _combine_kernel(dc_ref, dn_ref, ys_hbm, meta_ref, ht_ref, g_ref, b_ref, o_ref, obf_ref, buf, sem):
    i = pl.program_id(0)
    n = pl.num_programs(0)
    tt = dc_ref.shape[-1]
    slot = lax.rem(i, 2)

    def gather(d_ref, s):
        def issue(j, carry):
            for k in range(TOP_K):
                pltpu.make_async_copy(_tile_at(ys_hbm, d_ref[0, k, j]), _tile(buf.at[s, k], j),
                                      sem.at[s]).start(priority=k)
            return carry
        lax.fori_loop(0, tt, issue, 0, unroll=DMA_ISSUE_UNROLL)

    @pl.when(i == 0)
    def _():
        gather(dc_ref, 0)

    @pl.when(i + 1 < n)
    def _():
        gather(dn_ref, 1 - slot)

    for k in range(TOP_K):
        pltpu.make_async_copy(ys_hbm.at[pl.ds(0, tt * TOKEN_TILE_ROWS)], buf.at[slot, k], sem.at[slot]).wait()
    meta = meta_ref[...]
    ffn = (meta[:, META_W0:META_W0 + 1] * _token_tiles_to_rows(buf.at[slot, 0], tt)
           + meta[:, META_W1:META_W1 + 1] * _token_tiles_to_rows(buf.at[slot, 1], tt))
    y = _ln_rows(DEEPNORM_ALPHA * _token_tiles_to_rows(ht_ref, tt) + ffn, g_ref[...], b_ref[...])
    o_ref[...] = y
    obf_ref[...] = y.astype(BF16)


def _combine(ys, slots, meta, h_tiles, g, b):
    t, d = h_tiles.shape[0] // TOKEN_TILE_ROWS, D_MODEL
    tt = slots.shape[-1]
    n = t // tt
    cur = pl.BlockSpec((1, SUBLANES, tt), lambda i: (i, 0, 0), memory_space=pltpu.SMEM)
    nxt = pl.BlockSpec((1, SUBLANES, tt), lambda i: (jnp.minimum(i + 1, n - 1), 0, 0), memory_space=pltpu.SMEM)
    row = lambda i: (i, 0)
    const = lambda i: (0, 0)
    return pl.pallas_call(
        _combine_kernel,
        out_shape=(jax.ShapeDtypeStruct((t, d), F32), jax.ShapeDtypeStruct((t, d), BF16)),
        grid=(n,),
        in_specs=[cur, nxt, pl.BlockSpec(memory_space=pl.ANY),
                  pl.BlockSpec((tt, LANES), row), pl.BlockSpec((tt * TOKEN_TILE_ROWS, LANES), row),
                  pl.BlockSpec((1, d), const), pl.BlockSpec((1, d), const)],
        out_specs=(pl.BlockSpec((tt, d), row), pl.BlockSpec((tt, d), row)),
        scratch_shapes=[pltpu.VMEM((2, TOP_K, tt * TOKEN_TILE_ROWS, LANES), F32), pltpu.SemaphoreType.DMA((2,))],
        compiler_params=_cparams(("arbitrary",)),
        name="moe_combine_ln",
    )(slots, slots, ys, meta, h_tiles, g, b)


def _segment_tables(counts, t):
    tm = EXPERT_ROW_TILE
    n_rows = TOP_K * t + N_EXPERTS * tm
    cnt = counts[0, :N_EXPERTS].astype(jnp.int32)
    tiles = (cnt + tm - 1) // tm
    ends = jnp.cumsum(tiles)
    offs = jnp.zeros((1, LANES), F32).at[0, :N_EXPERTS].set(((ends - tiles) * tm).astype(F32))
    tile_ids = jnp.arange(n_rows // tm, dtype=jnp.int32)
    tile_expert = jnp.minimum(jnp.sum((tile_ids[:, None] >= ends[None, :]).astype(jnp.int32), axis=1),
                              N_EXPERTS - 1)
    last_tile = jnp.maximum(ends - 1, 0).astype(jnp.int32)
    return n_rows, offs, tile_expert, ends[-1:].astype(jnp.int32), last_tile


def _sparse_moe(h_tiles, meta, counts, wgu, wd, g, b):
    t = h_tiles.shape[0] // TOKEN_TILE_ROWS
    n_rows, offs, tile_expert, n_used, last_tile = _segment_tables(counts, t)
    slots = _slots(meta, offs, min(MOE_TOKENS, t))
    xs = _dispatch(h_tiles, slots, _zero_last_tiles(last_tile, n_rows))
    ys = _experts(xs, tile_expert, n_used, wgu, wd)
    return _combine(ys, slots, meta, h_tiles, g, b)


def _rotate_half_cols(w):
    half = w.shape[-1] // 2
    return jnp.concatenate([-w[..., half:], w[..., :half]], axis=-1)


def _pack_input_proj(w_in, b_in):
    d = w_in.shape[0]
    offs = np.cumsum((0, GLA_QK_W, GLA_QK_W, GLA_V_W, GLA_V_W, GLA_GATE_RANK, GLA_GATE_RANK,
                      MLA_Q_RANK, MLA_KV_RANK, MLA_ROPE, D_MODEL, D_MODEL))

    def seg(i):
        return w_in[:, offs[i]:offs[i + 1]], b_in[offs[i]:offs[i + 1]]

    w = jnp.zeros((d, N_PROJ), F32)
    b = jnp.zeros((N_PROJ,), F32)

    def put(w, b, col, ws, bs):
        return w.at[:, col:col + ws.shape[1]].set(ws), b.at[col:col + ws.shape[1]].set(bs)

    for i, col in ((0, COL_GQ), (1, COL_GK), (2, COL_GV), (3, COL_GR), (9, COL_GA), (10, COL_GB),
                   (6, COL_CQ), (7, COL_CKV), (4, COL_SMALL), (5, COL_SMALL + GLA_GATE_RANK)):
        w, b = put(w, b, col, *seg(i))
    wkr, bkr = seg(8)
    w, b = put(w, b, COL_SMALL + KR_LANE, wkr, bkr)
    w, b = put(w, b, COL_SMALL2 + KR_LANE, _rotate_half_cols(wkr), _rotate_half_cols(bkr))
    return w.astype(BF16), b.reshape(1, N_PROJ)


def _pack_decay(wa2, ba, lane0):
    w = jnp.zeros((LANES, GLA_QK_W), F32).at[lane0:lane0 + GLA_GATE_RANK].set(wa2)
    return w.astype(BF16), ba.reshape(1, GLA_QK_W)


def _pack_mla(w_uq, w_ukv, q_norm_g, kv_norm_g):
    wq = w_uq.reshape(MLA_Q_RANK, MLA_HEADS, MLA_QK)
    zq = jnp.zeros((MLA_Q_RANK, MLA_HEADS, HEAD_PAD - MLA_QK), F32)
    wq_p = jnp.concatenate([wq, zq], axis=-1)

    def pad_rows(w):
        w = w.reshape(MLA_Q_RANK, MLA_HEADS * HEAD_PAD)
        return jnp.pad(w, ((0, CQ_PAD - MLA_Q_RANK), (0, 0))).astype(BF16)

    wkv = w_ukv.reshape(MLA_KV_RANK, MLA_HEADS, MLA_NOPE + MLA_V)
    wk_p = jnp.concatenate([wkv[..., :MLA_NOPE],
                            jnp.zeros((MLA_KV_RANK, MLA_HEADS, HEAD_PAD - MLA_NOPE), F32)], axis=-1)
    wk_p = wk_p.reshape(MLA_KV_RANK, MLA_HEADS * HEAD_PAD).astype(BF16)
    wv_p = wkv[..., MLA_NOPE:].reshape(MLA_KV_RANK, MLA_HEADS * MLA_V).T.astype(BF16)
    gq = jnp.pad(q_norm_g, (0, CQ_PAD - MLA_Q_RANK)).reshape(1, CQ_PAD)
    return pad_rows(wq_p), wk_p, wv_p, gq, kv_norm_g.reshape(1, MLA_KV_RANK)


def _rope_lane_table():
    inv = ROPE_BASE ** (-jnp.arange(0, MLA_ROPE, 2, dtype=F32) / MLA_ROPE)
    return jnp.tile(inv, ROPE_PACK).reshape(1, LANES)


def _pack_router(w_grp, b_grp, w_exp, b_exp):
    d = w_grp.shape[0]
    w = jnp.zeros((d, LANES), F32).at[:, :N_EXPERTS].set(w_exp).at[:, GRP_LANE:GRP_LANE + N_GROUPS].set(w_grp)
    b = jnp.zeros((LANES,), F32).at[:N_EXPERTS].set(b_exp).at[GRP_LANE:GRP_LANE + N_GROUPS].set(b_grp)
    return w, b.reshape(1, LANES)


def kernel(x, positions, ln_emb_g, ln_emb_b, w_in, b_in, gla_wa2_f, gla_ba_f, gla_wa2_b, gla_ba_b, gla_norm_g, mla_q_norm_g, mla_w_uq, mla_kv_norm_g, mla_w_ukv, w_out, ln1_g, ln1_b, w_grp, b_grp, w_exp, b_exp, w_gate, w_up, w_down, ln2_g, ln2_b):
    batch, seq, d = x.shape
    t = batch * seq
    pos = jnp.repeat(positions.reshape(t // ROPE_PACK, ROPE_PACK).astype(F32), MLA_ROPE // 2, axis=1)
    inv_lane = _rope_lane_table()
    for l in range(DEPTH):
        w_p, b_p = _pack_input_proj(w_in[l], b_in[l])
        if l == 0:
            h, proj = _input_proj(x.reshape(t, d), w_p, b_p, ln=(ln_emb_g, ln_emb_b))
        else:
            proj = _input_proj(hb, w_p, b_p)
        wa_f, ba_f = _pack_decay(gla_wa2_f[l], gla_ba_f[l], 0)
        wa_b, ba_b = _pack_decay(gla_wa2_b[l], gla_ba_b[l], GLA_GATE_RANK)
        o_f = _gla(proj, wa_f, ba_f, batch, seq, reverse=False)
        o_gla = _gla(proj, wa_b, ba_b, batch, seq, reverse=True, o_fwd=o_f,
                     norm_g=gla_norm_g[l].reshape(1, GLA_V_W))
        wq, wk, wv, gq, gkv = _pack_mla(mla_w_uq[l], mla_w_ukv[l], mla_q_norm_g[l], mla_kv_norm_g[l])
        q2, k2, v = _mla_prep(proj, pos, inv_lane, gq, gkv, wq, wk, wv)
        o_mla = _attention(q2, k2, v, batch, seq)
        h_tiles = _merge_out(o_gla, o_mla, proj, h, w_out[l].astype(BF16),
                           ln1_g[l].reshape(1, d), ln1_b[l].reshape(1, d))
        w_r, b_r = _pack_router(w_grp[l], b_grp[l], w_exp[l], b_exp[l])
        meta, counts = _router(h_tiles, w_r, b_r)
        wgu = jnp.concatenate([w_gate[l], w_up[l]], axis=-1).astype(BF16)
        h, hb = _sparse_moe(h_tiles, meta, counts, wgu, w_down[l].astype(BF16),
                            ln2_g[l].reshape(1, d), ln2_b[l].reshape(1, d))
    return h.reshape(batch, seq, d)
```

```python
import functools

import numpy as np
import jax
import jax.numpy as jnp
from jax import lax
from jax.experimental import pallas as pl
from jax.experimental.pallas import tpu as pltpu

F32 = jnp.float32
BF16 = jnp.bfloat16

D_MODEL = 1024
DEPTH = 2
GLA_HEADS = 4
GLA_DK = 128
GLA_DV = 256
GLA_GATE_RANK = 16
GLA_TAU = 16.0
MLA_HEADS = 16
MLA_NOPE = 64
MLA_ROPE = 32
MLA_V = 64
MLA_QK = MLA_NOPE + MLA_ROPE
MLA_Q_RANK = 384
MLA_KV_RANK = 128
ROPE_BASE = 10000.0
N_GROUPS = 8
EXPERTS_PER_GROUP = 4
N_EXPERTS = 32
D_EXPERT = 256
GLA_QK_W = GLA_HEADS * GLA_DK
GLA_V_W = GLA_HEADS * GLA_DV
DEEPNORM_ALPHA = (2.0 * DEPTH) ** 0.25
LN_EPS = 1e-5
RMS_EPS = 1e-6

LANES = 128
VMEM_LIMIT_BYTES = 56 * 1024 * 1024

COL_GQ = 0
COL_GK = 512
COL_GV = 1024
COL_GR = 2048
COL_GA = 3072
COL_GB = 4096
COL_CQ = 5120
CQ_PAD = 512
COL_SMALL = 5632
COL_SMALL2 = 5760
COL_CKV = 5888
N_PROJ = 6144
HEAD_PAD = 128
KR_LANE = 64

GLA_CHUNK = 128
NEG_BIG = -1e30


def _cparams(sem):
    return pltpu.CompilerParams(dimension_semantics=sem, vmem_limit_bytes=VMEM_LIMIT_BYTES)


def _ln_rows(x, g, b):
    mu = jnp.mean(x, axis=-1, keepdims=True)
    xc = x - mu
    var = jnp.mean(xc * xc, axis=-1, keepdims=True)
    return xc * lax.rsqrt(var + LN_EPS) * g + b


PROJ_COL_CHUNK = 512


def _proj_kernel(*refs, embed_ln):
    if embed_ln:
        x_ref, g_ref, beta_ref, w_ref, b_ref, h_ref, o_ref = refs
        h = _ln_rows(x_ref[...], g_ref[...], beta_ref[...])
        h_ref[...] = h
        x = h.astype(BF16)
    else:
        x_ref, w_ref, b_ref, o_ref = refs
        x = x_ref[...]
    for c in range(N_PROJ // PROJ_COL_CHUNK):
        sl = slice(c * PROJ_COL_CHUNK, (c + 1) * PROJ_COL_CHUNK)
        acc = jnp.dot(x, w_ref[:, sl], preferred_element_type=F32) + b_ref[:, sl]
        o_ref[:, sl] = acc.astype(o_ref.dtype)


def _input_proj(x, w, b, ln=None, tm=512):
    t, d = x.shape
    row = lambda i: (i, 0)
    const = lambda i: (0, 0)
    vec = pl.BlockSpec((1, d), const)
    in_specs = [pl.BlockSpec((tm, d), row)] + ([vec, vec] if ln else []) + [
        pl.BlockSpec((d, N_PROJ), const), pl.BlockSpec((1, N_PROJ), const)]
    proj_shape = jax.ShapeDtypeStruct((t, N_PROJ), BF16)
    proj_spec = pl.BlockSpec((tm, N_PROJ), row)
    args = (x,) + (tuple(v.reshape(1, d) for v in ln) if ln else ()) + (w, b)
    return pl.pallas_call(
        functools.partial(_proj_kernel, embed_ln=ln is not None),
        out_shape=(jax.ShapeDtypeStruct((t, d), F32), proj_shape) if ln else proj_shape,
        grid=(t // tm,),
        in_specs=in_specs,
        out_specs=(pl.BlockSpec((tm, d), row), proj_spec) if ln else proj_spec,
        compiler_params=_cparams(("parallel",)),
        name="ln_input_proj" if ln else "input_proj",
    )(*args)


def _log_sigmoid(x):
    return jnp.minimum(x, 0.0) - jnp.log(1.0 + jnp.exp(-jnp.abs(x)))


def _split_bf16(x):
    hi = x.astype(BF16)
    lo = (x - hi.astype(F32)).astype(BF16)
    return hi, lo


def _gla_kernel(*refs, reverse, n_chunks):
    if reverse:
        (q_ref, k_ref, v_ref, z_ref, wa_ref, ba_ref, of_ref, gr_ref, ng_ref, o_ref, state_ref) = refs
    else:
        (q_ref, k_ref, v_ref, z_ref, wa_ref, ba_ref, o_ref, state_ref) = refs
    c_len = GLA_CHUNK

    @pl.when(pl.program_id(1) == 0)
    def _():
        state_ref[...] = jnp.zeros_like(state_ref)

    row = lax.broadcasted_iota(jnp.int32, (c_len, c_len), 0)
    col = lax.broadcasted_iota(jnp.int32, (c_len, c_len), 1)
    if reverse:
        tri = (col >= row).astype(BF16)
        keep = col > row
        last = 0
    else:
        tri = (col <= row).astype(BF16)
        keep = col <= row
        last = c_len - 1

    seqs = range(q_ref.shape[0])

    def decays(bi, rows):
        z = z_ref[bi, rows, :]
        la = _log_sigmoid(jnp.dot(z, wa_ref[...], preferred_element_type=F32) + ba_ref[...]) * (1.0 / GLA_TAU)
        la_hi, la_lo = _split_bf16(la)
        b_all = (jnp.dot(tri, la_hi, preferred_element_type=F32)
                 + jnp.dot(tri, la_lo, preferred_element_type=F32))
        b_last = b_all[last:last + 1, :]
        q = q_ref[bi, rows, :].astype(F32)
        k = k_ref[bi, rows, :].astype(F32)
        qd_all = (q * (jnp.exp(b_all) * (GLA_DK ** -0.5))).astype(BF16)
        kinv_all = (k * jnp.exp(-b_all)).astype(BF16)
        kend_all = k * jnp.exp(b_last - b_all)
        dec_all = jnp.broadcast_to(jnp.exp(b_last), (c_len, GLA_QK_W))
        return qd_all, kinv_all, kend_all, dec_all

    def chunk_rows(cc):
        c = (n_chunks - 1 - cc) if reverse else cc
        return slice(c * c_len, (c + 1) * c_len)

    pre_next = [decays(bi, chunk_rows(0)) for bi in seqs]
    for cc in range(n_chunks):
        rows = chunk_rows(cc)
        pre = pre_next
        if cc + 1 < n_chunks:
            pre_next = [decays(bi, chunk_rows(cc + 1)) for bi in seqs]
        for h in range(GLA_HEADS):
            ks = slice(h * GLA_DK, (h + 1) * GLA_DK)
            vs = slice(h * GLA_DV, (h + 1) * GLA_DV)
            v = [v_ref[bi, rows, vs] for bi in seqs]
            scores = [lax.dot_general(pre[bi][0][:, ks], pre[bi][1][:, ks], (((1,), (1,)), ((), ())),
                                      preferred_element_type=F32) for bi in seqs]
            state = [state_ref[bi, h] for bi in seqs]
            o = [jnp.dot(jnp.where(keep, scores[bi], 0.0).astype(BF16), v[bi], preferred_element_type=F32)
                 + jnp.dot(pre[bi][0][:, ks], state[bi].astype(BF16), preferred_element_type=F32) for bi in seqs]
            for bi in seqs:
                kend_t = pre[bi][2][:, ks].T.astype(BF16)
                dec_t = pre[bi][3][:, ks].T
                dec = jnp.concatenate([dec_t, dec_t], axis=1)
                state_ref[bi, h] = dec * state[bi] + jnp.dot(kend_t, v[bi], preferred_element_type=F32)
            for bi in seqs:
                ob = o[bi]
                if reverse:
                    ob = ob + of_ref[bi, rows, vs]
                    ms = jnp.mean(ob * ob, axis=-1, keepdims=True)
                    ob = ob * lax.rsqrt(ms + RMS_EPS) * ng_ref[:, vs]
                    g = gr_ref[bi, rows, vs].astype(F32)
                    ob = ob * (g * jax.nn.sigmoid(g))
                o_ref[bi, rows, vs] = ob.astype(o_ref.dtype)


GLA_SEQS_PER_STEP = 2


def _gla(proj, wa, ba, batch, seq, *, reverse, o_fwd=None, norm_g=None, ts=512):
    ts = min(ts, seq)
    nblk = seq // ts
    nb = GLA_SEQS_PER_STEP if batch % GLA_SEQS_PER_STEP == 0 else 1
    proj3 = proj.reshape(batch, seq, N_PROJ)

    def blk(i):
        return (nblk - 1 - i) if reverse else i

    def cols(width, col):
        return pl.BlockSpec((nb, ts, width), lambda b, i: (b, blk(i), col // width))

    in_specs = [cols(GLA_QK_W, COL_GQ), cols(GLA_QK_W, COL_GK), cols(GLA_V_W, COL_GV), cols(LANES, COL_SMALL),
                pl.BlockSpec((LANES, GLA_QK_W), lambda b, i: (0, 0)),
                pl.BlockSpec((1, GLA_QK_W), lambda b, i: (0, 0))]
    args = [proj3, proj3, proj3, proj3, wa, ba]
    if reverse:
        in_specs += [cols(GLA_V_W, 0), cols(GLA_V_W, COL_GR), pl.BlockSpec((1, GLA_V_W), lambda b, i: (0, 0))]
        args += [o_fwd.reshape(batch, seq, GLA_V_W), proj3, norm_g]
    out = pl.pallas_call(
        functools.partial(_gla_kernel, reverse=reverse, n_chunks=ts // GLA_CHUNK),
        out_shape=jax.ShapeDtypeStruct((batch, seq, GLA_V_W), BF16 if reverse else F32),
        grid=(batch // nb, nblk),
        in_specs=in_specs,
        out_specs=cols(GLA_V_W, 0),
        scratch_shapes=[pltpu.VMEM((nb, GLA_HEADS, GLA_DK, GLA_DV), F32)],
        compiler_params=_cparams(("parallel", "arbitrary")),
        name="gla_bwd" if reverse else "gla_fwd",
    )(*args)
    return out.reshape(batch * seq, GLA_V_W)


MLA_COL_CHUNK = 512
ROPE_PACK = LANES // (MLA_ROPE // 2)


def _mla_prep_kernel(cq_ref, ckv_ref, sm_ref, sm2_ref, pos_ref, inv_ref, gq_ref, gkv_ref,
                     wq_ref, wk_ref, wv_ref, q_ref, k_ref, v_ref):
    cq = cq_ref[...].astype(F32)
    msq = jnp.sum(cq * cq, axis=-1, keepdims=True) * (1.0 / MLA_Q_RANK)
    cqn = (cq * lax.rsqrt(msq + RMS_EPS) * gq_ref[...]).astype(BF16)
    ckv = ckv_ref[...].astype(F32)
    mskv = jnp.mean(ckv * ckv, axis=-1, keepdims=True)
    ckvn = (ckv * lax.rsqrt(mskv + RMS_EPS) * gkv_ref[...]).astype(BF16)

    ts = cq_ref.shape[0]
    ang = pos_ref[...] * inv_ref[...]
    tok = lax.broadcasted_iota(jnp.int32, (ts, ts // ROPE_PACK), 0)
    grp = lax.broadcasted_iota(jnp.int32, (ts, ts // ROPE_PACK), 1)
    to_rows = (tok // ROPE_PACK == grp).astype(BF16)
    src = lax.broadcasted_iota(jnp.int32, (LANES, LANES), 0)
    dst = lax.broadcasted_iota(jnp.int32, (LANES, LANES), 1)
    half = MLA_ROPE // 2
    to_lanes = ((dst >= KR_LANE) & (dst < KR_LANE + MLA_ROPE) & ((dst - KR_LANE) % half == src % half)).astype(BF16)
    lane = lax.broadcasted_iota(jnp.int32, (ts, LANES), 1)
    own = lane // half == lax.broadcasted_iota(jnp.int32, (ts, LANES), 0) % ROPE_PACK

    def spread(packed):
        out = 0.0
        for part in _split_bf16(packed):
            rows = jnp.dot(to_rows, part, preferred_element_type=F32)
            rows = jnp.where(own, rows, 0.0).astype(BF16)
            out = out + jnp.dot(rows, to_lanes, preferred_element_type=F32)
        return out

    cos = jnp.where(lane < KR_LANE, 1.0, spread(jnp.cos(ang)))
    sin = spread(jnp.sin(ang))
    kr = jnp.where(lane >= KR_LANE, sm_ref[...].astype(F32), 0.0)
    kr = kr * cos + sm2_ref[...].astype(F32) * sin

    heads_per_chunk = MLA_COL_CHUNK // HEAD_PAD
    cos_t = jnp.concatenate([cos] * heads_per_chunk, axis=1)
    sin_t = jnp.concatenate([sin] * heads_per_chunk, axis=1)
    kr_t = jnp.concatenate([kr] * heads_per_chunk, axis=1)
    qscale = (MLA_QK ** -0.5) * float(np.log2(np.e))
    chunk_lane = lax.broadcasted_iota(jnp.int32, (ts, MLA_COL_CHUNK), 1) % HEAD_PAD
    first_half = chunk_lane < KR_LANE + half
    for c in range(MLA_HEADS * HEAD_PAD // MLA_COL_CHUNK):
        sl = slice(c * MLA_COL_CHUNK, (c + 1) * MLA_COL_CHUNK)
        q = jnp.dot(cqn, wq_ref[:, sl], preferred_element_type=F32)
        qr = jnp.where(first_half, -pltpu.roll(q, MLA_COL_CHUNK - half, axis=1), pltpu.roll(q, half, axis=1))
        q_ref[:, sl] = ((q * cos_t + qr * sin_t) * qscale).astype(BF16)
        k = jnp.dot(ckvn, wk_ref[:, sl], preferred_element_type=F32)
        k_ref[:, sl] = (k + kr_t).astype(BF16)
    v_ref[...] = lax.dot_general(wv_ref[...], ckvn, (((1,), (1,)), ((), ())),
                                 preferred_element_type=F32).astype(BF16)


def _mla_prep(proj, pos, inv_lane, gq, gkv, wq, wk, wv, ts=512):
    t = proj.shape[0]
    hp = MLA_HEADS * HEAD_PAD
    vw = MLA_HEADS * MLA_V
    const = lambda i: (0, 0)
    return pl.pallas_call(
        _mla_prep_kernel,
        out_shape=(jax.ShapeDtypeStruct((t, hp), BF16), jax.ShapeDtypeStruct((t, hp), BF16),
                   jax.ShapeDtypeStruct((vw, t), BF16)),
        grid=(t // ts,),
        in_specs=[
            pl.BlockSpec((ts, CQ_PAD), lambda i: (i, COL_CQ // CQ_PAD)),
            pl.BlockSpec((ts, LANES), lambda i: (i, COL_CKV // LANES)),
            pl.BlockSpec((ts, LANES), lambda i: (i, COL_SMALL // LANES)),
            pl.BlockSpec((ts, LANES), lambda i: (i, COL_SMALL2 // LANES)),
            pl.BlockSpec((ts // ROPE_PACK, LANES), lambda i: (i, 0)),
            pl.BlockSpec((1, LANES), const),
            pl.BlockSpec((1, CQ_PAD), const),
            pl.BlockSpec((1, MLA_KV_RANK), const),
            pl.BlockSpec((CQ_PAD, hp), const),
            pl.BlockSpec((MLA_KV_RANK, hp), const),
            pl.BlockSpec((vw, MLA_KV_RANK), const),
        ],
        out_specs=(pl.BlockSpec((ts, hp), lambda i: (i, 0)), pl.BlockSpec((ts, hp), lambda i: (i, 0)),
                   pl.BlockSpec((vw, ts), lambda i: (0, i))),
        compiler_params=_cparams(("parallel",)),
        name="mla_prep",
    )(proj, proj, proj, proj, pos, inv_lane, gq, gkv, wq, wk, wv)


SUBLANES = 8


BOUND_SLACK = 1.01
SHIFT_LIMIT = 48.0
SCORE_CHUNKS_AHEAD = 4


def _attn_kernel(q_ref, k_ref, vt_ref, o_ref, s_scr, bound_scr, safe_scr, *, tk):
    seq = k_ref.shape[0]
    tq = s_scr.shape[-1]
    nk = seq // tk
    nq = seq // tq
    nt = (((1,), (1,)), ((), ()))
    head = lambda hh: slice(hh * HEAD_PAD, (hh + 1) * HEAD_PAD)
    ones8 = jnp.ones((SUBLANES, HEAD_PAD), BF16)

    def max_key_norm_sq(hh):
        k = k_ref[:, head(hh)].astype(F32)
        k_sq = lax.dot_general(ones8, (k * k).astype(BF16), nt, preferred_element_type=F32)
        return jnp.max(k_sq[0:1], axis=1, keepdims=True)

    for hh in range(2):
        qf = q_ref[:, head(hh)].astype(F32)
        q_sq = lax.dot_general(ones8, (qf * qf).astype(BF16), nt, preferred_element_type=F32)
        b8 = jnp.sqrt(q_sq * max_key_norm_sq(hh)) * BOUND_SLACK
        for j in range(nq):
            bound_scr[hh, j] = b8[:, j * tq:(j + 1) * tq]
    for j in range(nq):
        worst = jnp.max(jnp.maximum(bound_scr[0, j], bound_scr[1, j]))
        safe_scr[j] = (worst < SHIFT_LIMIT).astype(jnp.int32)

    def scores(hh, q, c):
        return lax.dot_general(k_ref[c * tk:(c + 1) * tk, head(hh)], q, nt, preferred_element_type=F32)

    def weigh(hh, c, p, l8, acc):
        l8 = l8 + jnp.sum(p.reshape(tk // SUBLANES, SUBLANES, tq), axis=0)
        vt = vt_ref[hh * MLA_V:(hh + 1) * MLA_V, c * tk:(c + 1) * tk]
        return l8, acc + jnp.dot(vt, p.astype(BF16), preferred_element_type=F32)

    def q_tile(i, carry):
        rows = pl.ds(pl.multiple_of(i * tq, tq), tq)
        qs = [q_ref[rows, head(hh)] for hh in range(2)]
        bound = [bound_scr[hh, i, 0:1, :] for hh in range(2)]
        safe = safe_scr[i] != 0
        zeros = lambda n: jnp.zeros((n, tq), F32)

        @pl.when(safe)
        def _():
            l8, acc = [zeros(SUBLANES)] * 2, [zeros(MLA_V)] * 2
            ahead = min(SCORE_CHUNKS_AHEAD, nk)
            pending = [[scores(hh, qs[hh], c) for hh in range(2)] for c in range(ahead)]
            for c in range(nk):
                st = pending.pop(0)
                if c + ahead < nk:
                    pending.append([scores(hh, qs[hh], c + ahead) for hh in range(2)])
                for hh in range(2):
                    l8[hh], acc[hh] = weigh(hh, c, jnp.exp2(st[hh] - bound[hh]), l8[hh], acc[hh])
            out = [acc[hh] / jnp.sum(l8[hh], axis=0, keepdims=True) for hh in range(2)]
            o_ref[rows, :] = jnp.concatenate(out, axis=0).T.astype(BF16)

        @pl.when(jnp.logical_not(safe))
        def _():
            out = []
            for hh in range(2):
                m8 = jnp.full((SUBLANES, tq), NEG_BIG, F32)
                for c in range(nk):
                    st = scores(hh, qs[hh], c)
                    s_scr[c] = st
                    m8 = jnp.maximum(m8, jnp.max(st.reshape(tk // SUBLANES, SUBLANES, tq), axis=0))
                m = jnp.max(m8, axis=0, keepdims=True)
                l8, acc = zeros(SUBLANES), zeros(MLA_V)
                for c in range(nk):
                    l8, acc = weigh(hh, c, jnp.exp2(s_scr[c] - m), l8, acc)
                out.append(acc / jnp.sum(l8, axis=0, keepdims=True))
            o_ref[rows, :] = jnp.concatenate(out, axis=0).T.astype(BF16)

        return carry

    lax.fori_loop(0, nq, q_tile, 0)


def _attention(q2, k2, vt, batch, seq, tq=512, tk=512):
    tq = min(tq, seq)
    tk = min(tk, seq)
    t = batch * seq
    pairs = MLA_HEADS // 2
    return pl.pallas_call(
        functools.partial(_attn_kernel, tk=tk),
        scratch_shapes=[pltpu.VMEM((seq // tk, tk, tq), F32), pltpu.VMEM((2, seq // tq, SUBLANES, tq), F32),
                        pltpu.SMEM((seq // tq,), jnp.int32)],
        out_shape=jax.ShapeDtypeStruct((t, MLA_HEADS * MLA_V), BF16),
        grid=(batch, pairs),
        in_specs=[
            pl.BlockSpec((seq, 2 * HEAD_PAD), lambda b, p: (b, p)),
            pl.BlockSpec((seq, 2 * HEAD_PAD), lambda b, p: (b, p)),
            pl.BlockSpec((2 * MLA_V, seq), lambda b, p: (p, b)),
        ],
        out_specs=pl.BlockSpec((seq, 2 * MLA_V), lambda b, p: (b, p)),
        compiler_params=_cparams(("parallel", "parallel")),
        name="mla_attention",
    )(q2, k2, vt)


TOKEN_TILE_ROWS = D_MODEL // LANES


def _rows_to_token_tiles(ref, y):
    n = y.shape[0]
    for s in range(TOKEN_TILE_ROWS):
        ref[pl.ds(s, n, stride=TOKEN_TILE_ROWS), :] = y[:, s * LANES:(s + 1) * LANES]


def _token_tiles_to_rows(ref, n):
    return jnp.concatenate([ref[pl.ds(s, n, stride=TOKEN_TILE_ROWS), :] for s in range(TOKEN_TILE_ROWS)], axis=1)


def _merge_out_kernel(oa_ref, ob_ref, ga_ref, gb_ref, h_ref, w_ref, g_ref, b_ref, ot_ref):
    merged = (jax.nn.sigmoid(ga_ref[...].astype(F32)) * oa_ref[...].astype(F32)
              + jax.nn.sigmoid(gb_ref[...].astype(F32)) * ob_ref[...].astype(F32))
    mix = jnp.dot(merged.astype(BF16), w_ref[...], preferred_element_type=F32)
    y = _ln_rows(DEEPNORM_ALPHA * h_ref[...] + mix, g_ref[...], b_ref[...])
    _rows_to_token_tiles(ot_ref, y)


def _merge_out(o_gla, o_mla, proj, h, w_out, g, b, tm=512):
    t, d = h.shape
    row = lambda i: (i, 0)
    const = lambda i: (0, 0)
    return pl.pallas_call(
        _merge_out_kernel,
        out_shape=jax.ShapeDtypeStruct((t * TOKEN_TILE_ROWS, LANES), F32),
        grid=(t // tm,),
        in_specs=[
            pl.BlockSpec((tm, d), row),
            pl.BlockSpec((tm, d), row),
            pl.BlockSpec((tm, d), lambda i: (i, COL_GA // D_MODEL)),
            pl.BlockSpec((tm, d), lambda i: (i, COL_GB // D_MODEL)),
            pl.BlockSpec((tm, d), row),
            pl.BlockSpec((d, d), const),
            pl.BlockSpec((1, d), const),
            pl.BlockSpec((1, d), const),
        ],
        out_specs=pl.BlockSpec((tm * TOKEN_TILE_ROWS, LANES), row),
        compiler_params=_cparams(("parallel",)),
        name="merge_out_ln",
    )(o_gla, o_mla, proj, proj, h, w_out, g, b)


GRP_LANE = N_EXPERTS
META_W0, META_W1, META_E0, META_E1, META_R0, META_R1 = range(6)


def _router_kernel(x_ref, w_ref, b_ref, meta_ref, cnt_ref, carry_ref):
    @pl.when(pl.program_id(0) == 0)
    def _():
        carry_ref[...] = jnp.zeros_like(carry_ref)

    tm = x_ref.shape[0] // TOKEN_TILE_ROWS
    x_hi, x_lo = _split_bf16(_token_tiles_to_rows(x_ref, tm))
    w_hi, w_lo = _split_bf16(w_ref[...])
    logits = (jnp.dot(x_hi, w_hi, preferred_element_type=F32)
              + jnp.dot(x_lo, w_hi, preferred_element_type=F32)
              + jnp.dot(x_hi, w_lo, preferred_element_type=F32)) + b_ref[...]
    lane = lax.broadcasted_iota(jnp.int32, logits.shape, 1)
    is_grp = (lane >= GRP_LANE) & (lane < GRP_LANE + N_GROUPS)
    gl = jnp.where(is_grp, logits, NEG_BIG)
    gmax = jnp.max(gl, axis=1, keepdims=True)
    g_lane = jnp.min(jnp.where(gl == gmax, lane, 4 * LANES), axis=1, keepdims=True)
    g_w = 1.0 / jnp.sum(jnp.where(is_grp, jnp.exp(gl - gmax), 0.0), axis=1, keepdims=True)
    lo_lane = (g_lane - GRP_LANE) * EXPERTS_PER_GROUP
    in_grp = (lane >= lo_lane) & (lane < lo_lane + EXPERTS_PER_GROUP)
    el = jnp.where(in_grp, logits, NEG_BIG)
    v1 = jnp.max(el, axis=1, keepdims=True)
    i1 = jnp.min(jnp.where(el == v1, lane, 4 * LANES), axis=1, keepdims=True)
    el2 = jnp.where(lane == i1, NEG_BIG, el)
    v2 = jnp.max(el2, axis=1, keepdims=True)
    i2 = jnp.min(jnp.where(el2 == v2, lane, 4 * LANES), axis=1, keepdims=True)
    e2 = jnp.exp(v2 - v1)
    w1 = g_w / (1.0 + e2)
    w2 = g_w * e2 / (1.0 + e2)

    onehot = jnp.where(lane == i1, 1.0, jnp.where(lane == i2, 1.0, 0.0))
    r = lax.broadcasted_iota(jnp.int32, (tm, tm), 0)
    c = lax.broadcasted_iota(jnp.int32, (tm, tm), 1)
    earlier = (c < r).astype(BF16)
    base = carry_ref[...] + jnp.dot(earlier, onehot.astype(BF16), preferred_element_type=F32)
    rank1 = jnp.sum(jnp.where(lane == i1, base, 0.0), axis=1, keepdims=True)
    rank2 = jnp.sum(jnp.where(lane == i2, base, 0.0), axis=1, keepdims=True)
    meta = jnp.zeros(logits.shape, F32)
    for ln, val in ((META_W0, w1), (META_W1, w2), (META_E0, i1.astype(F32)), (META_E1, i2.astype(F32)),
                    (META_R0, rank1), (META_R1, rank2)):
        meta = jnp.where(lane == ln, val, meta)
    meta_ref[...] = meta
    carry_ref[...] += jnp.sum(onehot, axis=0, keepdims=True)
    cnt_ref[...] = carry_ref[...]


def _router(h_tiles, w_r, b_r, tm=512):
    t, d = h_tiles.shape[0] // TOKEN_TILE_ROWS, D_MODEL
    return pl.pallas_call(
        _router_kernel,
        out_shape=(jax.ShapeDtypeStruct((t, LANES), F32), jax.ShapeDtypeStruct((1, LANES), F32)),
        grid=(t // tm,),
        in_specs=[pl.BlockSpec((tm * TOKEN_TILE_ROWS, LANES), lambda i: (i, 0)),
                  pl.BlockSpec((d, LANES), lambda i: (0, 0)),
                  pl.BlockSpec((1, LANES), lambda i: (0, 0))],
        out_specs=(pl.BlockSpec((tm, LANES), lambda i: (i, 0)), pl.BlockSpec((1, LANES), lambda i: (0, 0))),
        scratch_shapes=[pltpu.VMEM((1, LANES), F32)],
        compiler_params=_cparams(("arbitrary",)),
        name="router",
    )(h_tiles, w_r, b_r)


EXPERT_ROW_TILE = 512
MOE_TOKENS = 512
TOP_K = 2
DMA_ISSUE_UNROLL = 16


def _tile_at(ref, first_row):
    return ref.at[pl.ds(pl.multiple_of(first_row, TOKEN_TILE_ROWS), TOKEN_TILE_ROWS)]


def _tile(ref, token):
    return _tile_at(ref, token * TOKEN_TILE_ROWS)


def _slots_kernel(meta_ref, offs_ref, d_ref):
    meta = meta_ref[...]
    offs = offs_ref[...]
    lane = lax.broadcasted_iota(jnp.int32, meta.shape, 1)

    def slot(e_lane, r_lane):
        e = meta[:, e_lane:e_lane + 1].astype(jnp.int32)
        return jnp.sum(jnp.where(lane == e, offs, 0.0), axis=1, keepdims=True) + meta[:, r_lane:r_lane + 1]

    both = jnp.where(lane == 0, slot(META_E0, META_R0), jnp.where(lane == 1, slot(META_E1, META_R1), 0.0))
    d_ref[0] = (both.T[:SUBLANES, :] * float(TOKEN_TILE_ROWS)).astype(jnp.int32)


def _slots(meta, offs, tt):
    t = meta.shape[0]
    return pl.pallas_call(
        _slots_kernel,
        out_shape=jax.ShapeDtypeStruct((t // tt, SUBLANES, tt), jnp.int32),
        grid=(t // tt,),
        in_specs=[pl.BlockSpec((tt, LANES), lambda i: (i, 0)), pl.BlockSpec((1, LANES), lambda i: (0, 0))],
        out_specs=pl.BlockSpec((1, SUBLANES, tt), lambda i: (i, 0, 0)),
        compiler_params=_cparams(("arbitrary",)),
        name="moe_slots",
    )(meta, offs)


def _zero_tile_kernel(lt_ref, xs_ref):
    del lt_ref
    xs_ref[...] = jnp.zeros_like(xs_ref)


def _zero_last_tiles(last_tile, n_rows):
    blk = EXPERT_ROW_TILE * TOKEN_TILE_ROWS
    grid_spec = pltpu.PrefetchScalarGridSpec(
        num_scalar_prefetch=1, grid=(N_EXPERTS,), in_specs=[],
        out_specs=pl.BlockSpec((blk, LANES), lambda e, lt: (lt[e], 0)))
    return pl.pallas_call(
        _zero_tile_kernel,
        out_shape=jax.ShapeDtypeStruct((n_rows * TOKEN_TILE_ROWS, LANES), F32),
        grid_spec=grid_spec,
        compiler_params=_cparams(("arbitrary",)),
        name="moe_zero_tiles",
    )(last_tile)


def _dispatch_kernel(d_ref, ht_ref, xs_init_hbm, xs_hbm, sem):
    del xs_init_hbm
    tt = d_ref.shape[-1]

    def issue(j, carry):
        src = _tile(ht_ref, j)
        for k in range(TOP_K):
            pltpu.make_async_copy(src, _tile_at(xs_hbm, d_ref[0, k, j]), sem).start(priority=k)
        return carry

    lax.fori_loop(0, tt, issue, 0, unroll=DMA_ISSUE_UNROLL)
    for k in range(TOP_K):
        pltpu.make_async_copy(ht_ref, xs_hbm.at[pl.ds(0, tt * TOKEN_TILE_ROWS)], sem).wait()


def _dispatch(h_tiles, slots, xs_init):
    tt = slots.shape[-1]
    t = h_tiles.shape[0] // TOKEN_TILE_ROWS
    return pl.pallas_call(
        _dispatch_kernel,
        out_shape=jax.ShapeDtypeStruct(xs_init.shape, F32),
        grid=(t // tt,),
        in_specs=[pl.BlockSpec((1, SUBLANES, tt), lambda i: (i, 0, 0), memory_space=pltpu.SMEM),
                  pl.BlockSpec((tt * TOKEN_TILE_ROWS, LANES), lambda i: (i, 0)),
                  pl.BlockSpec(memory_space=pl.ANY)],
        out_specs=pl.BlockSpec(memory_space=pl.ANY),
        scratch_shapes=[pltpu.SemaphoreType.DMA(())],
        input_output_aliases={2: 0},
        compiler_params=_cparams(("arbitrary",)),
        name="moe_dispatch",
    )(slots, h_tiles, xs_init)


def _expert_kernel(te_ref, nu_ref, xs_ref, wgu_ref, wd_ref, ys_ref):
    del te_ref
    i = pl.program_id(0)
    tm = EXPERT_ROW_TILE

    @pl.when(i < nu_ref[0])
    def _():
        x = _token_tiles_to_rows(xs_ref, tm).astype(BF16)
        gu = jnp.dot(x, wgu_ref[0], preferred_element_type=F32)
        gate = gu[:, :D_EXPERT]
        hid = gate * jax.nn.sigmoid(gate) * gu[:, D_EXPERT:]
        _rows_to_token_tiles(ys_ref, jnp.dot(hid.astype(BF16), wd_ref[0], preferred_element_type=F32))

    @pl.when(i >= nu_ref[0])
    def _():
        ys_ref[...] = jnp.zeros_like(ys_ref)


def _experts(xs, tile_expert, n_used, wgu, wd):
    d = D_MODEL
    tm = EXPERT_ROW_TILE
    blk = pl.BlockSpec((tm * TOKEN_TILE_ROWS, LANES), lambda i, te, nu: (i, 0))
    blk_in = pl.BlockSpec((tm * TOKEN_TILE_ROWS, LANES), lambda i, te, nu: (jnp.minimum(i, nu[0] - 1), 0))
    grid_spec = pltpu.PrefetchScalarGridSpec(
        num_scalar_prefetch=2,
        grid=(xs.shape[0] // (tm * TOKEN_TILE_ROWS),),
        in_specs=[blk_in,
                  pl.BlockSpec((1, d, 2 * D_EXPERT), lambda i, te, nu: (te[i], 0, 0)),
                  pl.BlockSpec((1, D_EXPERT, d), lambda i, te, nu: (te[i], 0, 0))],
        out_specs=blk,
    )
    return pl.pallas_call(
        _expert_kernel,
        out_shape=jax.ShapeDtypeStruct(xs.shape, F32),
        grid_spec=grid_spec,
        compiler_params=_cparams(("arbitrary",)),
        name="moe_experts",
    )(tile_expert, n_used, xs, wgu, wd)


def _combine_kernel(dc_ref, dn_ref, ys_hbm, meta_ref, ht_ref, g_ref, b_ref, o_ref, obf_ref, buf, sem):
    i = pl.program_id(0)
    n = pl.num_programs(0)
    tt = dc_ref.shape[-1]
    slot = lax.rem(i, 2)

    def gather(d_ref, s):
        def issue(j, carry):
            for k in range(TOP_K):
                pltpu.make_async_copy(_tile_at(ys_hbm, d_ref[0, k, j]), _tile(buf.at[s, k], j),
                                      sem.at[s]).start(priority=k)
            return carry
        lax.fori_loop(0, tt, issue, 0, unroll=DMA_ISSUE_UNROLL)

    @pl.when(i == 0)
    def _():
        gather(dc_ref, 0)

    @pl.when(i + 1 < n)
    def _():
        gather(dn_ref, 1 - slot)

    for k in range(TOP_K):
        pltpu.make_async_copy(ys_hbm.at[pl.ds(0, tt * TOKEN_TILE_ROWS)], buf.at[slot, k], sem.at[slot]).wait()
    meta = meta_ref[...]
    ffn = (meta[:, META_W0:META_W0 + 1] * _token_tiles_to_rows(buf.at[slot, 0], tt)
           + meta[:, META_W1:META_W1 + 1] * _token_tiles_to_rows(buf.at[slot, 1], tt))
    y = _ln_rows(DEEPNORM_ALPHA * _token_tiles_to_rows(ht_ref, tt) + ffn, g_ref[...], b_ref[...])
    o_ref[...] = y
    obf_ref[...] = y.astype(BF16)


def _combine(ys, slots, meta, h_tiles, g, b):
    t, d = h_tiles.shape[0] // TOKEN_TILE_ROWS, D_MODEL
    tt = slots.shape[-1]
    n = t // tt
    cur = pl.BlockSpec((1, SUBLANES, tt), lambda i: (i, 0, 0), memory_space=pltpu.SMEM)
    nxt = pl.BlockSpec((1, SUBLANES, tt), lambda i: (jnp.minimum(i + 1, n - 1), 0, 0), memory_space=pltpu.SMEM)
    row = lambda i: (i, 0)
    const = lambda i: (0, 0)
    return pl.pallas_call(
        _combine_kernel,
        out_shape=(jax.ShapeDtypeStruct((t, d), F32), jax.ShapeDtypeStruct((t, d), BF16)),
        grid=(n,),
        in_specs=[cur, nxt, pl.BlockSpec(memory_space=pl.ANY),
                  pl.BlockSpec((tt, LANES), row), pl.BlockSpec((tt * TOKEN_TILE_ROWS, LANES), row),
                  pl.BlockSpec((1, d), const), pl.BlockSpec((1, d), const)],
        out_specs=(pl.BlockSpec((tt, d), row), pl.BlockSpec((tt, d), row)),
        scratch_shapes=[pltpu.VMEM((2, TOP_K, tt * TOKEN_TILE_ROWS, LANES), F32), pltpu.SemaphoreType.DMA((2,))],
        compiler_params=_cparams(("arbitrary",)),
        name="moe_combine_ln",
    )(slots, slots, ys, meta, h_tiles, g, b)


def _segment_tables(counts, t):
    tm = EXPERT_ROW_TILE
    n_rows = TOP_K * t + N_EXPERTS * tm
    cnt = counts[0, :N_EXPERTS].astype(jnp.int32)
    tiles = (cnt + tm - 1) // tm
    ends = jnp.cumsum(tiles)
    offs = jnp.zeros((1, LANES), F32).at[0, :N_EXPERTS].set(((ends - tiles) * tm).astype(F32))
    tile_ids = jnp.arange(n_rows // tm, dtype=jnp.int32)
    tile_expert = jnp.minimum(jnp.sum((tile_ids[:, None] >= ends[None, :]).astype(jnp.int32), axis=1),
                              N_EXPERTS - 1)
    last_tile = jnp.maximum(ends - 1, 0).astype(jnp.int32)
    return n_rows, offs, tile_expert, ends[-1:].astype(jnp.int32), last_tile


def _sparse_moe(h_tiles, meta, counts, wgu, wd, g, b):
    t = h_tiles.shape[0] // TOKEN_TILE_ROWS
    n_rows, offs, tile_expert, n_used, last_tile = _segment_tables(counts, t)
    slots = _slots(meta, offs, min(MOE_TOKENS, t))
    xs = _dispatch(h_tiles, slots, _zero_last_tiles(last_tile, n_rows))
    ys = _experts(xs, tile_expert, n_used, wgu, wd)
    return _combine(ys, slots, meta, h_tiles, g, b)


def _rotate_half_cols(w):
    half = w.shape[-1] // 2
    return jnp.concatenate([-w[..., half:], w[..., :half]], axis=-1)


def _pack_input_proj(w_in, b_in):
    d = w_in.shape[0]
    offs = np.cumsum((0, GLA_QK_W, GLA_QK_W, GLA_V_W, GLA_V_W, GLA_GATE_RANK, GLA_GATE_RANK,
                      MLA_Q_RANK, MLA_KV_RANK, MLA_ROPE, D_MODEL, D_MODEL))

    def seg(i):
        return w_in[:, offs[i]:offs[i + 1]], b_in[offs[i]:offs[i + 1]]

    w = jnp.zeros((d, N_PROJ), F32)
    b = jnp.zeros((N_PROJ,), F32)

    def put(w, b, col, ws, bs):
        return w.at[:, col:col + ws.shape[1]].set(ws), b.at[col:col + ws.shape[1]].set(bs)

    for i, col in ((0, COL_GQ), (1, COL_GK), (2, COL_GV), (3, COL_GR), (9, COL_GA), (10, COL_GB),
                   (6, COL_CQ), (7, COL_CKV), (4, COL_SMALL), (5, COL_SMALL + GLA_GATE_RANK)):
        w, b = put(w, b, col, *seg(i))
    wkr, bkr = seg(8)
    w, b = put(w, b, COL_SMALL + KR_LANE, wkr, bkr)
    w, b = put(w, b, COL_SMALL2 + KR_LANE, _rotate_half_cols(wkr), _rotate_half_cols(bkr))
    return w.astype(BF16), b.reshape(1, N_PROJ)


def _pack_decay(wa2, ba, lane0):
    w = jnp.zeros((LANES, GLA_QK_W), F32).at[lane0:lane0 + GLA_GATE_RANK].set(wa2)
    return w.astype(BF16), ba.reshape(1, GLA_QK_W)


def _pack_mla(w_uq, w_ukv, q_norm_g, kv_norm_g):
    wq = w_uq.reshape(MLA_Q_RANK, MLA_HEADS, MLA_QK)
    zq = jnp.zeros((MLA_Q_RANK, MLA_HEADS, HEAD_PAD - MLA_QK), F32)
    wq_p = jnp.concatenate([wq, zq], axis=-1)

    def pad_rows(w):
        w = w.reshape(MLA_Q_RANK, MLA_HEADS * HEAD_PAD)
        return jnp.pad(w, ((0, CQ_PAD - MLA_Q_RANK), (0, 0))).astype(BF16)

    wkv = w_ukv.reshape(MLA_KV_RANK, MLA_HEADS, MLA_NOPE + MLA_V)
    wk_p = jnp.concatenate([wkv[..., :MLA_NOPE],
                            jnp.zeros((MLA_KV_RANK, MLA_HEADS, HEAD_PAD - MLA_NOPE), F32)], axis=-1)
    wk_p = wk_p.reshape(MLA_KV_RANK, MLA_HEADS * HEAD_PAD).astype(BF16)
    wv_p = wkv[..., MLA_NOPE:].reshape(MLA_KV_RANK, MLA_HEADS * MLA_V).T.astype(BF16)
    gq = jnp.pad(q_norm_g, (0, CQ_PAD - MLA_Q_RANK)).reshape(1, CQ_PAD)
    return pad_rows(wq_p), wk_p, wv_p, gq, kv_norm_g.reshape(1, MLA_KV_RANK)


def _rope_lane_table():
    inv = ROPE_BASE ** (-jnp.arange(0, MLA_ROPE, 2, dtype=F32) / MLA_ROPE)
    return jnp.tile(inv, ROPE_PACK).reshape(1, LANES)


def _pack_router(w_grp, b_grp, w_exp, b_exp):
    d = w_grp.shape[0]
    w = jnp.zeros((d, LANES), F32).at[:, :N_EXPERTS].set(w_exp).at[:, GRP_LANE:GRP_LANE + N_GROUPS].set(w_grp)
    b = jnp.zeros((LANES,), F32).at[:N_EXPERTS].set(b_exp).at[GRP_LANE:GRP_LANE + N_GROUPS].set(b_grp)
    return w, b.reshape(1, LANES)


def kernel(x, positions, ln_emb_g, ln_emb_b, w_in, b_in, gla_wa2_f, gla_ba_f, gla_wa2_b, gla_ba_b, gla_norm_g, mla_q_norm_g, mla_w_uq, mla_kv_norm_g, mla_w_ukv, w_out, ln1_g, ln1_b, w_grp, b_grp, w_exp, b_exp, w_gate, w_up, w_down, ln2_g, ln2_b):
    batch, seq, d = x.shape
    t = batch * seq
    pos = jnp.repeat(positions.reshape(t // ROPE_PACK, ROPE_PACK).astype(F32), MLA_ROPE // 2, axis=1)
    inv_lane = _rope_lane_table()
    for l in range(DEPTH):
        w_p, b_p = _pack_input_proj(w_in[l], b_in[l])
        if l == 0:
            h, proj = _input_proj(x.reshape(t, d), w_p, b_p, ln=(ln_emb_g, ln_emb_b))
        else:
            proj = _input_proj(hb, w_p, b_p)
        wa_f, ba_f = _pack_decay(gla_wa2_f[l], gla_ba_f[l], 0)
        wa_b, ba_b = _pack_decay(gla_wa2_b[l], gla_ba_b[l], GLA_GATE_RANK)
        o_f = _gla(proj, wa_f, ba_f, batch, seq, reverse=False)
        o_gla = _gla(proj, wa_b, ba_b, batch, seq, reverse=True, o_fwd=o_f,
                     norm_g=gla_norm_g[l].reshape(1, GLA_V_W))
        wq, wk, wv, gq, gkv = _pack_mla(mla_w_uq[l], mla_w_ukv[l], mla_q_norm_g[l], mla_kv_norm_g[l])
        q2, k2, v = _mla_prep(proj, pos, inv_lane, gq, gkv, wq, wk, wv)
        o_mla = _attention(q2, k2, v, batch, seq)
        h_tiles = _merge_out(o_gla, o_mla, proj, h, w_out[l].astype(BF16),
                           ln1_g[l].reshape(1, d), ln1_b[l].reshape(1, d))
        w_r, b_r = _pack_router(w_grp[l], b_grp[l], w_exp[l], b_exp[l])
        meta, counts = _router(h_tiles, w_r, b_r)
        wgu = jnp.concatenate([w_gate[l], w_up[l]], axis=-1).astype(BF16)
        h, hb = _sparse_moe(h_tiles, meta, counts, wgu, w_down[l].astype(BF16),
                            ln2_g[l].reshape(1, d), ln2_b[l].reshape(1, d))
    return h.reshape(batch, seq, d)
```

```python
import functools

import numpy as np
import jax
import jax.numpy as jnp
from jax import lax
from jax.experimental import pallas as pl
from jax.experimental.pallas import tpu as pltpu

F32 = jnp.float32
BF16 = jnp.bfloat16

D_MODEL = 1024
DEPTH = 2
GLA_HEADS = 4
GLA_DK = 128
GLA_DV = 256
GLA_GATE_RANK = 16
GLA_TAU = 16.0
MLA_HEADS = 16
MLA_NOPE = 64
MLA_ROPE = 32
MLA_V = 64
MLA_QK = MLA_NOPE + MLA_ROPE
MLA_Q_RANK = 384
MLA_KV_RANK = 128
ROPE_BASE = 10000.0
N_GROUPS = 8
EXPERTS_PER_GROUP = 4
N_EXPERTS = 32
D_EXPERT = 256
GLA_QK_W = GLA_HEADS * GLA_DK
GLA_V_W = GLA_HEADS * GLA_DV
DEEPNORM_ALPHA = (2.0 * DEPTH) ** 0.25
LN_EPS = 1e-5
RMS_EPS = 1e-6

LANES = 128
VMEM_LIMIT_BYTES = 56 * 1024 * 1024

COL_GQ = 0
COL_GK = 512
COL_GV = 1024
COL_GR = 2048
COL_GA = 3072
COL_GB = 4096
COL_CQ = 5120
CQ_PAD = 512
COL_SMALL = 5632
COL_SMALL2 = 5760
COL_CKV = 5888
N_PROJ = 6144
HEAD_PAD = 128
KR_LANE = 64

GLA_CHUNK = 128
NEG_BIG = -1e30


def _cparams(sem):
    return pltpu.CompilerParams(dimension_semantics=sem, vmem_limit_bytes=VMEM_LIMIT_BYTES)


def _ln_rows(x, g, b):
    mu = jnp.mean(x, axis=-1, keepdims=True)
    xc = x - mu
    var = jnp.mean(xc * xc, axis=-1, keepdims=True)
    return xc * lax.rsqrt(var + LN_EPS) * g + b


PROJ_COL_CHUNK = 512


def _proj_kernel(*refs, embed_ln):
    if embed_ln:
        x_ref, g_ref, beta_ref, w_ref, b_ref, h_ref, o_ref = refs
        h = _ln_rows(x_ref[...], g_ref[...], beta_ref[...])
        h_ref[...] = h
        x = h.astype(BF16)
    else:
        x_ref, w_ref, b_ref, o_ref = refs
        x = x_ref[...]
    for c in range(N_PROJ // PROJ_COL_CHUNK):
        sl = slice(c * PROJ_COL_CHUNK, (c + 1) * PROJ_COL_CHUNK)
        acc = jnp.dot(x, w_ref[:, sl], preferred_element_type=F32) + b_ref[:, sl]
        o_ref[:, sl] = acc.astype(o_ref.dtype)


def _input_proj(x, w, b, ln=None, tm=512):
    t, d = x.shape
    row = lambda i: (i, 0)
    const = lambda i: (0, 0)
    vec = pl.BlockSpec((1, d), const)
    in_specs = [pl.BlockSpec((tm, d), row)] + ([vec, vec] if ln else []) + [
        pl.BlockSpec((d, N_PROJ), const), pl.BlockSpec((1, N_PROJ), const)]
    proj_shape = jax.ShapeDtypeStruct((t, N_PROJ), BF16)
    proj_spec = pl.BlockSpec((tm, N_PROJ), row)
    args = (x,) + (tuple(v.reshape(1, d) for v in ln) if ln else ()) + (w, b)
    return pl.pallas_call(
        functools.partial(_proj_kernel, embed_ln=ln is not None),
        out_shape=(jax.ShapeDtypeStruct((t, d), F32), proj_shape) if ln else proj_shape,
        grid=(t // tm,),
        in_specs=in_specs,
        out_specs=(pl.BlockSpec((tm, d), row), proj_spec) if ln else proj_spec,
        compiler_params=_cparams(("parallel",)),
        name="ln_input_proj" if ln else "input_proj",
    )(*args)


def _log_sigmoid(x):
    return jnp.minimum(x, 0.0) - jnp.log(1.0 + jnp.exp(-jnp.abs(x)))


def _split_bf16(x):
    hi = x.astype(BF16)
    lo = (x - hi.astype(F32)).astype(BF16)
    return hi, lo


def _gla_kernel(*refs, reverse, n_chunks):
    if reverse:
        (q_ref, k_ref, v_ref, z_ref, wa_ref, ba_ref, of_ref, gr_ref, ng_ref, o_ref, state_ref) = refs
    else:
        (q_ref, k_ref, v_ref, z_ref, wa_ref, ba_ref, o_ref, state_ref) = refs
    c_len = GLA_CHUNK

    @pl.when(pl.program_id(1) == 0)
    def _():
        state_ref[...] = jnp.zeros_like(state_ref)

    row = lax.broadcasted_iota(jnp.int32, (c_len, c_len), 0)
    col = lax.broadcasted_iota(jnp.int32, (c_len, c_len), 1)
    if reverse:
        tri = (col >= row).astype(BF16)
        keep = col > row
        last = 0
    else:
        tri = (col <= row).astype(BF16)
        keep = col <= row
        last = c_len - 1

    seqs = range(q_ref.shape[0])

    def decays(bi, rows):
        z = z_ref[bi, rows, :]
        la = _log_sigmoid(jnp.dot(z, wa_ref[...], preferred_element_type=F32) + ba_ref[...]) * (1.0 / GLA_TAU)
        la_hi, la_lo = _split_bf16(la)
        b_all = (jnp.dot(tri, la_hi, preferred_element_type=F32)
                 + jnp.dot(tri, la_lo, preferred_element_type=F32))
        b_last = b_all[last:last + 1, :]
        q = q_ref[bi, rows, :].astype(F32)
        k = k_ref[bi, rows, :].astype(F32)
        qd_all = (q * (jnp.exp(b_all) * (GLA_DK ** -0.5))).astype(BF16)
        kinv_all = (k * jnp.exp(-b_all)).astype(BF16)
        kend_all = k * jnp.exp(b_last - b_all)
        dec_all = jnp.broadcast_to(jnp.exp(b_last), (c_len, GLA_QK_W))
        return qd_all, kinv_all, kend_all, dec_all

    def chunk_rows(cc):
        c = (n_chunks - 1 - cc) if reverse else cc
        return slice(c * c_len, (c + 1) * c_len)

    pre_next = [decays(bi, chunk_rows(0)) for bi in seqs]
    for cc in range(n_chunks):
        rows = chunk_rows(cc)
        pre = pre_next
        if cc + 1 < n_chunks:
            pre_next = [decays(bi, chunk_rows(cc + 1)) for bi in seqs]
        for h in range(GLA_HEADS):
            ks = slice(h * GLA_DK, (h + 1) * GLA_DK)
            vs = slice(h * GLA_DV, (h + 1) * GLA_DV)
            v = [v_ref[bi, rows, vs] for bi in seqs]
            scores = [lax.dot_general(pre[bi][0][:, ks], pre[bi][1][:, ks], (((1,), (1,)), ((), ())),
                                      preferred_element_type=F32) for bi in seqs]
            state = [state_ref[bi, h] for bi in seqs]
            o = [jnp.dot(jnp.where(keep, scores[bi], 0.0).astype(BF16), v[bi], preferred_element_type=F32)
                 + jnp.dot(pre[bi][0][:, ks], state[bi].astype(BF16), preferred_element_type=F32) for bi in seqs]
            for bi in seqs:
                kend_t = pre[bi][2][:, ks].T.astype(BF16)
                dec_t = pre[bi][3][:, ks].T
                dec = jnp.concatenate([dec_t, dec_t], axis=1)
                state_ref[bi, h] = dec * state[bi] + jnp.dot(kend_t, v[bi], preferred_element_type=F32)
            for bi in seqs:
                ob = o[bi]
                if reverse:
                    ob = ob + of_ref[bi, rows, vs]
                    ms = jnp.mean(ob * ob, axis=-1, keepdims=True)
                    ob = ob * lax.rsqrt(ms + RMS_EPS) * ng_ref[:, vs]
                    g = gr_ref[bi, rows, vs].astype(F32)
                    ob = ob * (g * jax.nn.sigmoid(g))
                o_ref[bi, rows, vs] = ob.astype(o_ref.dtype)


GLA_SEQS_PER_STEP = 2


def _gla(proj, wa, ba, batch, seq, *, reverse, o_fwd=None, norm_g=None, ts=512):
    ts = min(ts, seq)
    nblk = seq // ts
    nb = GLA_SEQS_PER_STEP if batch % GLA_SEQS_PER_STEP == 0 else 1
    proj3 = proj.reshape(batch, seq, N_PROJ)

    def blk(i):
        return (nblk - 1 - i) if reverse else i

    def cols(width, col):
        return pl.BlockSpec((nb, ts, width), lambda b, i: (b, blk(i), col // width))

    in_specs = [cols(GLA_QK_W, COL_GQ), cols(GLA_QK_W, COL_GK), cols(GLA_V_W, COL_GV), cols(LANES, COL_SMALL),
                pl.BlockSpec((LANES, GLA_QK_W), lambda b, i: (0, 0)),
                pl.BlockSpec((1, GLA_QK_W), lambda b, i: (0, 0))]
    args = [proj3, proj3, proj3, proj3, wa, ba]
    if reverse:
        in_specs += [cols(GLA_V_W, 0), cols(GLA_V_W, COL_GR), pl.BlockSpec((1, GLA_V_W), lambda b, i: (0, 0))]
        args += [o_fwd.reshape(batch, seq, GLA_V_W), proj3, norm_g]
    out = pl.pallas_call(
        functools.partial(_gla_kernel, reverse=reverse, n_chunks=ts // GLA_CHUNK),
        out_shape=jax.ShapeDtypeStruct((batch, seq, GLA_V_W), BF16 if reverse else F32),
        grid=(batch // nb, nblk),
        in_specs=in_specs,
        out_specs=cols(GLA_V_W, 0),
        scratch_shapes=[pltpu.VMEM((nb, GLA_HEADS, GLA_DK, GLA_DV), F32)],
        compiler_params=_cparams(("parallel", "arbitrary")),
        name="gla_bwd" if reverse else "gla_fwd",
    )(*args)
    return out.reshape(batch * seq, GLA_V_W)


MLA_COL_CHUNK = 512
ROPE_PACK = LANES // (MLA_ROPE // 2)


def _mla_prep_kernel(cq_ref, ckv_ref, sm_ref, sm2_ref, pos_ref, inv_ref, gq_ref, gkv_ref,
                     wq_ref, wk_ref, wv_ref, q_ref, k_ref, v_ref):
    cq = cq_ref[...].astype(F32)
    msq = jnp.sum(cq * cq, axis=-1, keepdims=True) * (1.0 / MLA_Q_RANK)
    cqn = (cq * lax.rsqrt(msq + RMS_EPS) * gq_ref[...]).astype(BF16)
    ckv = ckv_ref[...].astype(F32)
    mskv = jnp.mean(ckv * ckv, axis=-1, keepdims=True)
    ckvn = (ckv * lax.rsqrt(mskv + RMS_EPS) * gkv_ref[...]).astype(BF16)

    ts = cq_ref.shape[0]
    ang = pos_ref[...] * inv_ref[...]
    tok = lax.broadcasted_iota(jnp.int32, (ts, ts // ROPE_PACK), 0)
    grp = lax.broadcasted_iota(jnp.int32, (ts, ts // ROPE_PACK), 1)
    to_rows = (tok // ROPE_PACK == grp).astype(BF16)
    src = lax.broadcasted_iota(jnp.int32, (LANES, LANES), 0)
    dst = lax.broadcasted_iota(jnp.int32, (LANES, LANES), 1)
    half = MLA_ROPE // 2
    to_lanes = ((dst >= KR_LANE) & (dst < KR_LANE + MLA_ROPE) & ((dst - KR_LANE) % half == src % half)).astype(BF16)
    lane = lax.broadcasted_iota(jnp.int32, (ts, LANES), 1)
    own = lane // half == lax.broadcasted_iota(jnp.int32, (ts, LANES), 0) % ROPE_PACK

    def spread(packed):
        out = 0.0
        for part in _split_bf16(packed):
            rows = jnp.dot(to_rows, part, preferred_element_type=F32)
            rows = jnp.where(own, rows, 0.0).astype(BF16)
            out = out + jnp.dot(rows, to_lanes, preferred_element_type=F32)
        return out

    cos = jnp.where(lane < KR_LANE, 1.0, spread(jnp.cos(ang)))
    sin = spread(jnp.sin(ang))
    kr = jnp.where(lane >= KR_LANE, sm_ref[...].astype(F32), 0.0)
    kr = kr * cos + sm2_ref[...].astype(F32) * sin

    heads_per_chunk = MLA_COL_CHUNK // HEAD_PAD
    cos_t = jnp.concatenate([cos] * heads_per_chunk, axis=1)
    sin_t = jnp.concatenate([sin] * heads_per_chunk, axis=1)
    kr_t = jnp.concatenate([kr] * heads_per_chunk, axis=1)
    qscale = (MLA_QK ** -0.5) * float(np.log2(np.e))
    chunk_lane = lax.broadcasted_iota(jnp.int32, (ts, MLA_COL_CHUNK), 1) % HEAD_PAD
    first_half = chunk_lane < KR_LANE + half
    for c in range(MLA_HEADS * HEAD_PAD // MLA_COL_CHUNK):
        sl = slice(c * MLA_COL_CHUNK, (c + 1) * MLA_COL_CHUNK)
        q = jnp.dot(cqn, wq_ref[:, sl], preferred_element_type=F32)
        qr = jnp.where(first_half, -pltpu.roll(q, MLA_COL_CHUNK - half, axis=1), pltpu.roll(q, half, axis=1))
        q_ref[:, sl] = ((q * cos_t + qr * sin_t) * qscale).astype(BF16)
        k = jnp.dot(ckvn, wk_ref[:, sl], preferred_element_type=F32)
        k_ref[:, sl] = (k + kr_t).astype(BF16)
    v_ref[...] = lax.dot_general(wv_ref[...], ckvn, (((1,), (1,)), ((), ())),
                                 preferred_element_type=F32).astype(BF16)


def _mla_prep(proj, pos, inv_lane, gq, gkv, wq, wk, wv, ts=512):
    t = proj.shape[0]
    hp = MLA_HEADS * HEAD_PAD
    vw = MLA_HEADS * MLA_V
    const = lambda i: (0, 0)
    return pl.pallas_call(
        _mla_prep_kernel,
        out_shape=(jax.ShapeDtypeStruct((t, hp), BF16), jax.ShapeDtypeStruct((t, hp), BF16),
                   jax.ShapeDtypeStruct((vw, t), BF16)),
        grid=(t // ts,),
        in_specs=[
            pl.BlockSpec((ts, CQ_PAD), lambda i: (i, COL_CQ // CQ_PAD)),
            pl.BlockSpec((ts, LANES), lambda i: (i, COL_CKV // LANES)),
            pl.BlockSpec((ts, LANES), lambda i: (i, COL_SMALL // LANES)),
            pl.BlockSpec((ts, LANES), lambda i: (i, COL_SMALL2 // LANES)),
            pl.BlockSpec((ts // ROPE_PACK, LANES), lambda i: (i, 0)),
            pl.BlockSpec((1, LANES), const),
            pl.BlockSpec((1, CQ_PAD), const),
            pl.BlockSpec((1, MLA_KV_RANK), const),
            pl.BlockSpec((CQ_PAD, hp), const),
            pl.BlockSpec((MLA_KV_RANK, hp), const),
            pl.BlockSpec((vw, MLA_KV_RANK), const),
        ],
        out_specs=(pl.BlockSpec((ts, hp), lambda i: (i, 0)), pl.BlockSpec((ts, hp), lambda i: (i, 0)),
                   pl.BlockSpec((vw, ts), lambda i: (0, i))),
        compiler_params=_cparams(("parallel",)),
        name="mla_prep",
    )(proj, proj, proj, proj, pos, inv_lane, gq, gkv, wq, wk, wv)


SUBLANES = 8


BOUND_SLACK = 1.01
SHIFT_LIMIT = 48.0
SCORE_CHUNKS_AHEAD = 8


def _attn_kernel(q_ref, k_ref, vt_ref, o_ref, s_scr, bound_scr, safe_scr, *, tk):
    seq = k_ref.shape[0]
    tq = s_scr.shape[-1]
    nk = seq // tk
    nq = seq // tq
    nt = (((1,), (1,)), ((), ()))
    head = lambda hh: slice(hh * HEAD_PAD, (hh + 1) * HEAD_PAD)
    ones8 = jnp.ones((SUBLANES, HEAD_PAD), BF16)

    def max_key_norm_sq(hh):
        k = k_ref[:, head(hh)].astype(F32)
        k_sq = lax.dot_general(ones8, (k * k).astype(BF16), nt, preferred_element_type=F32)
        return jnp.max(k_sq[0:1], axis=1, keepdims=True)

    for hh in range(2):
        qf = q_ref[:, head(hh)].astype(F32)
        q_sq = lax.dot_general(ones8, (qf * qf).astype(BF16), nt, preferred_element_type=F32)
        b8 = jnp.sqrt(q_sq * max_key_norm_sq(hh)) * BOUND_SLACK
        for j in range(nq):
            bound_scr[hh, j] = b8[:, j * tq:(j + 1) * tq]
    for j in range(nq):
        worst = jnp.max(jnp.maximum(bound_scr[0, j], bound_scr[1, j]))
        safe_scr[j] = (worst < SHIFT_LIMIT).astype(jnp.int32)

    def scores(hh, q, c):
        return lax.dot_general(k_ref[c * tk:(c + 1) * tk, head(hh)], q, nt, preferred_element_type=F32)

    def weigh(hh, c, p, l8, acc):
        l8 = l8 + jnp.sum(p.reshape(tk // SUBLANES, SUBLANES, tq), axis=0)
        vt = vt_ref[hh * MLA_V:(hh + 1) * MLA_V, c * tk:(c + 1) * tk]
        return l8, acc + jnp.dot(vt, p.astype(BF16), preferred_element_type=F32)

    def q_tile(i, carry):
        rows = pl.ds(pl.multiple_of(i * tq, tq), tq)
        qs = [q_ref[rows, head(hh)] for hh in range(2)]
        bound = [bound_scr[hh, i, 0:1, :] for hh in range(2)]
        safe = safe_scr[i] != 0
        zeros = lambda n: jnp.zeros((n, tq), F32)

        @pl.when(safe)
        def _():
            l8, acc = [zeros(SUBLANES)] * 2, [zeros(MLA_V)] * 2
            ahead = min(SCORE_CHUNKS_AHEAD, nk)
            pending = [[scores(hh, qs[hh], c) for hh in range(2)] for c in range(ahead)]
            for c in range(nk):
                st = pending.pop(0)
                if c + ahead < nk:
                    pending.append([scores(hh, qs[hh], c + ahead) for hh in range(2)])
                for hh in range(2):
                    l8[hh], acc[hh] = weigh(hh, c, jnp.exp2(st[hh] - bound[hh]), l8[hh], acc[hh])
            out = [acc[hh] / jnp.sum(l8[hh], axis=0, keepdims=True) for hh in range(2)]
            o_ref[rows, :] = jnp.concatenate(out, axis=0).T.astype(BF16)

        @pl.when(jnp.logical_not(safe))
        def _():
            out = []
            for hh in range(2):
                m8 = jnp.full((SUBLANES, tq), NEG_BIG, F32)
                for c in range(nk):
                    st = scores(hh, qs[hh], c)
                    s_scr[c] = st
                    m8 = jnp.maximum(m8, jnp.max(st.reshape(tk // SUBLANES, SUBLANES, tq), axis=0))
                m = jnp.max(m8, axis=0, keepdims=True)
                l8, acc = zeros(SUBLANES), zeros(MLA_V)
                for c in range(nk):
                    l8, acc = weigh(hh, c, jnp.exp2(s_scr[c] - m), l8, acc)
                out.append(acc / jnp.sum(l8, axis=0, keepdims=True))
            o_ref[rows, :] = jnp.concatenate(out, axis=0).T.astype(BF16)

        return carry

    lax.fori_loop(0, nq, q_tile, 0)


def _attention(q2, k2, vt, batch, seq, tq=512, tk=512):
    tq = min(tq, seq)
    tk = min(tk, seq)
    t = batch * seq
    pairs = MLA_HEADS // 2
    return pl.pallas_call(
        functools.partial(_attn_kernel, tk=tk),
        scratch_shapes=[pltpu.VMEM((seq // tk, tk, tq), F32), pltpu.VMEM((2, seq // tq, SUBLANES, tq), F32),
                        pltpu.SMEM((seq // tq,), jnp.int32)],
        out_shape=jax.ShapeDtypeStruct((t, MLA_HEADS * MLA_V), BF16),
        grid=(batch, pairs),
        in_specs=[
            pl.BlockSpec((seq, 2 * HEAD_PAD), lambda b, p: (b, p)),
            pl.BlockSpec((seq, 2 * HEAD_PAD), lambda b, p: (b, p)),
            pl.BlockSpec((2 * MLA_V, seq), lambda b, p: (p, b)),
        ],
        out_specs=pl.BlockSpec((seq, 2 * MLA_V), lambda b, p: (b, p)),
        compiler_params=_cparams(("parallel", "parallel")),
        name="mla_attention",
    )(q2, k2, vt)


TOKEN_TILE_ROWS = D_MODEL // LANES


def _rows_to_token_tiles(ref, y):
    n = y.shape[0]
    for s in range(TOKEN_TILE_ROWS):
        ref[pl.ds(s, n, stride=TOKEN_TILE_ROWS), :] = y[:, s * LANES:(s + 1) * LANES]


def _token_tiles_to_rows(ref, n):
    return jnp.concatenate([ref[pl.ds(s, n, stride=TOKEN_TILE_ROWS), :] for s in range(TOKEN_TILE_ROWS)], axis=1)


def _merge_out_kernel(oa_ref, ob_ref, ga_ref, gb_ref, h_ref, w_ref, g_ref, b_ref, ot_ref):
    merged = (jax.nn.sigmoid(ga_ref[...].astype(F32)) * oa_ref[...].astype(F32)
              + jax.nn.sigmoid(gb_ref[...].astype(F32)) * ob_ref[...].astype(F32))
    mix = jnp.dot(merged.astype(BF16), w_ref[...], preferred_element_type=F32)
    y = _ln_rows(DEEPNORM_ALPHA * h_ref[...] + mix, g_ref[...], b_ref[...])
    _rows_to_token_tiles(ot_ref, y)


def _merge_out(o_gla, o_mla, proj, h, w_out, g, b, tm=512):
    t, d = h.shape
    row = lambda i: (i, 0)
    const = lambda i: (0, 0)
    return pl.pallas_call(
        _merge_out_kernel,
        out_shape=jax.ShapeDtypeStruct((t * TOKEN_TILE_ROWS, LANES), F32),
        grid=(t // tm,),
        in_specs=[
            pl.BlockSpec((tm, d), row),
            pl.BlockSpec((tm, d), row),
            pl.BlockSpec((tm, d), lambda i: (i, COL_GA // D_MODEL)),
            pl.BlockSpec((tm, d), lambda i: (i, COL_GB // D_MODEL)),
            pl.BlockSpec((tm, d), row),
            pl.BlockSpec((d, d), const),
            pl.BlockSpec((1, d), const),
            pl.BlockSpec((1, d), const),
        ],
        out_specs=pl.BlockSpec((tm * TOKEN_TILE_ROWS, LANES), row),
        compiler_params=_cparams(("parallel",)),
        name="merge_out_ln",
    )(o_gla, o_mla, proj, proj, h, w_out, g, b)


GRP_LANE = N_EXPERTS
META_W0, META_W1, META_E0, META_E1, META_R0, META_R1 = range(6)


def _router_kernel(x_ref, w_ref, b_ref, meta_ref, cnt_ref, carry_ref):
    @pl.when(pl.program_id(0) == 0)
    def _():
        carry_ref[...] = jnp.zeros_like(carry_ref)

    tm = x_ref.shape[0] // TOKEN_TILE_ROWS
    x_hi, x_lo = _split_bf16(_token_tiles_to_rows(x_ref, tm))
    w_hi, w_lo = _split_bf16(w_ref[...])
    logits = (jnp.dot(x_hi, w_hi, preferred_element_type=F32)
              + jnp.dot(x_lo, w_hi, preferred_element_type=F32)
              + jnp.dot(x_hi, w_lo, preferred_element_type=F32)) + b_ref[...]
    lane = lax.broadcasted_iota(jnp.int32, logits.shape, 1)
    is_grp = (lane >= GRP_LANE) & (lane < GRP_LANE + N_GROUPS)
    gl = jnp.where(is_grp, logits, NEG_BIG)
    gmax = jnp.max(gl, axis=1, keepdims=True)
    g_lane = jnp.min(jnp.where(gl == gmax, lane, 4 * LANES), axis=1, keepdims=True)
    g_w = 1.0 / jnp.sum(jnp.where(is_grp, jnp.exp(gl - gmax), 0.0), axis=1, keepdims=True)
    lo_lane = (g_lane - GRP_LANE) * EXPERTS_PER_GROUP
    in_grp = (lane >= lo_lane) & (lane < lo_lane + EXPERTS_PER_GROUP)
    el = jnp.where(in_grp, logits, NEG_BIG)
    v1 = jnp.max(el, axis=1, keepdims=True)
    i1 = jnp.min(jnp.where(el == v1, lane, 4 * LANES), axis=1, keepdims=True)
    el2 = jnp.where(lane == i1, NEG_BIG, el)
    v2 = jnp.max(el2, axis=1, keepdims=True)
    i2 = jnp.min(jnp.where(el2 == v2, lane, 4 * LANES), axis=1, keepdims=True)
    e2 = jnp.exp(v2 - v1)
    w1 = g_w / (1.0 + e2)
    w2 = g_w * e2 / (1.0 + e2)

    onehot = jnp.where(lane == i1, 1.0, jnp.where(lane == i2, 1.0, 0.0))
    r = lax.broadcasted_iota(jnp.int32, (tm, tm), 0)
    c = lax.broadcasted_iota(jnp.int32, (tm, tm), 1)
    earlier = (c < r).astype(BF16)
    base = carry_ref[...] + jnp.dot(earlier, onehot.astype(BF16), preferred_element_type=F32)
    rank1 = jnp.sum(jnp.where(lane == i1, base, 0.0), axis=1, keepdims=True)
    rank2 = jnp.sum(jnp.where(lane == i2, base, 0.0), axis=1, keepdims=True)
    meta = jnp.zeros(logits.shape, F32)
    for ln, val in ((META_W0, w1), (META_W1, w2), (META_E0, i1.astype(F32)), (META_E1, i2.astype(F32)),
                    (META_R0, rank1), (META_R1, rank2)):
        meta = jnp.where(lane == ln, val, meta)
    meta_ref[...] = meta
    carry_ref[...] += jnp.sum(onehot, axis=0, keepdims=True)
    cnt_ref[...] = carry_ref[...]


def _router(h_tiles, w_r, b_r, tm=512):
    t, d = h_tiles.shape[0] // TOKEN_TILE_ROWS, D_MODEL
    return pl.pallas_call(
        _router_kernel,
        out_shape=(jax.ShapeDtypeStruct((t, LANES), F32), jax.ShapeDtypeStruct((1, LANES), F32)),
        grid=(t // tm,),
        in_specs=[pl.BlockSpec((tm * TOKEN_TILE_ROWS, LANES), lambda i: (i, 0)),
                  pl.BlockSpec((d, LANES), lambda i: (0, 0)),
                  pl.BlockSpec((1, LANES), lambda i: (0, 0))],
        out_specs=(pl.BlockSpec((tm, LANES), lambda i: (i, 0)), pl.BlockSpec((1, LANES), lambda i: (0, 0))),
        scratch_shapes=[pltpu.VMEM((1, LANES), F32)],
        compiler_params=_cparams(("arbitrary",)),
        name="router",
    )(h_tiles, w_r, b_r)


EXPERT_ROW_TILE = 512
MOE_TOKENS = 512
TOP_K = 2
DMA_ISSUE_UNROLL = 16


def _tile_at(ref, first_row):
    return ref.at[pl.ds(pl.multiple_of(first_row, TOKEN_TILE_ROWS), TOKEN_TILE_ROWS)]


def _tile(ref, token):
    return _tile_at(ref, token * TOKEN_TILE_ROWS)


def _slots_kernel(meta_ref, offs_ref, d_ref):
    meta = meta_ref[...]
    offs = offs_ref[...]
    lane = lax.broadcasted_iota(jnp.int32, meta.shape, 1)

    def slot(e_lane, r_lane):
        e = meta[:, e_lane:e_lane + 1].astype(jnp.int32)
        return jnp.sum(jnp.where(lane == e, offs, 0.0), axis=1, keepdims=True) + meta[:, r_lane:r_lane + 1]

    both = jnp.where(lane == 0, slot(META_E0, META_R0), jnp.where(lane == 1, slot(META_E1, META_R1), 0.0))
    d_ref[0] = (both.T[:SUBLANES, :] * float(TOKEN_TILE_ROWS)).astype(jnp.int32)


def _slots(meta, offs, tt):
    t = meta.shape[0]
    return pl.pallas_call(
        _slots_kernel,
        out_shape=jax.ShapeDtypeStruct((t // tt, SUBLANES, tt), jnp.int32),
        grid=(t // tt,),
        in_specs=[pl.BlockSpec((tt, LANES), lambda i: (i, 0)), pl.BlockSpec((1, LANES), lambda i: (0, 0))],
        out_specs=pl.BlockSpec((1, SUBLANES, tt), lambda i: (i, 0, 0)),
        compiler_params=_cparams(("arbitrary",)),
        name="moe_slots",
    )(meta, offs)


def _zero_tile_kernel(lt_ref, xs_ref):
    del lt_ref
    xs_ref[...] = jnp.zeros_like(xs_ref)


def _zero_last_tiles(last_tile, n_rows):
    blk = EXPERT_ROW_TILE * TOKEN_TILE_ROWS
    grid_spec = pltpu.PrefetchScalarGridSpec(
        num_scalar_prefetch=1, grid=(N_EXPERTS,), in_specs=[],
        out_specs=pl.BlockSpec((blk, LANES), lambda e, lt: (lt[e], 0)))
    return pl.pallas_call(
        _zero_tile_kernel,
        out_shape=jax.ShapeDtypeStruct((n_rows * TOKEN_TILE_ROWS, LANES), F32),
        grid_spec=grid_spec,
        compiler_params=_cparams(("arbitrary",)),
        name="moe_zero_tiles",
    )(last_tile)


def _dispatch_kernel(d_ref, ht_ref, xs_init_hbm, xs_hbm, sem):
    del xs_init_hbm
    tt = d_ref.shape[-1]

    def issue(j, carry):
        src = _tile(ht_ref, j)
        for k in range(TOP_K):
            pltpu.make_async_copy(src, _tile_at(xs_hbm, d_ref[0, k, j]), sem).start(priority=k)
        return carry

    lax.fori_loop(0, tt, issue, 0, unroll=DMA_ISSUE_UNROLL)
    for k in range(TOP_K):
        pltpu.make_async_copy(ht_ref, xs_hbm.at[pl.ds(0, tt * TOKEN_TILE_ROWS)], sem).wait()


def _dispatch(h_tiles, slots, xs_init):
    tt = slots.shape[-1]
    t = h_tiles.shape[0] // TOKEN_TILE_ROWS
    return pl.pallas_call(
        _dispatch_kernel,
        out_shape=jax.ShapeDtypeStruct(xs_init.shape, F32),
        grid=(t // tt,),
        in_specs=[pl.BlockSpec((1, SUBLANES, tt), lambda i: (i, 0, 0), memory_space=pltpu.SMEM),
                  pl.BlockSpec((tt * TOKEN_TILE_ROWS, LANES), lambda i: (i, 0)),
                  pl.BlockSpec(memory_space=pl.ANY)],
        out_specs=pl.BlockSpec(memory_space=pl.ANY),
        scratch_shapes=[pltpu.SemaphoreType.DMA(())],
        input_output_aliases={2: 0},
        compiler_params=_cparams(("arbitrary",)),
        name="moe_dispatch",
    )(slots, h_tiles, xs_init)


def _expert_kernel(te_ref, nu_ref, xs_ref, wgu_ref, wd_ref, ys_ref):
    del te_ref
    i = pl.program_id(0)
    tm = EXPERT_ROW_TILE

    @pl.when(i < nu_ref[0])
    def _():
        x = _token_tiles_to_rows(xs_ref, tm).astype(BF16)
        gu = jnp.dot(x, wgu_ref[0], preferred_element_type=F32)
        gate = gu[:, :D_EXPERT]
        hid = gate * jax.nn.sigmoid(gate) * gu[:, D_EXPERT:]
        _rows_to_token_tiles(ys_ref, jnp.dot(hid.astype(BF16), wd_ref[0], preferred_element_type=F32))

    @pl.when(i >= nu_ref[0])
    def _():
        ys_ref[...] = jnp.zeros_like(ys_ref)


def _experts(xs, tile_expert, n_used, wgu, wd):
    d = D_MODEL
    tm = EXPERT_ROW_TILE
    blk = pl.BlockSpec((tm * TOKEN_TILE_ROWS, LANES), lambda i, te, nu: (i, 0))
    blk_in = pl.BlockSpec((tm * TOKEN_TILE_ROWS, LANES), lambda i, te, nu: (jnp.minimum(i, nu[0] - 1), 0))
    grid_spec = pltpu.PrefetchScalarGridSpec(
        num_scalar_prefetch=2,
        grid=(xs.shape[0] // (tm * TOKEN_TILE_ROWS),),
        in_specs=[blk_in,
                  pl.BlockSpec((1, d, 2 * D_EXPERT), lambda i, te, nu: (te[i], 0, 0)),
                  pl.BlockSpec((1, D_EXPERT, d), lambda i, te, nu: (te[i], 0, 0))],
        out_specs=blk,
    )
    return pl.pallas_call(
        _expert_kernel,
        out_shape=jax.ShapeDtypeStruct(xs.shape, F32),
        grid_spec=grid_spec,
        compiler_params=_cparams(("arbitrary",)),
        name="moe_experts",
    )(tile_expert, n_used, xs, wgu, wd)


def _combine_kernel(dc_ref, dn_ref, ys_hbm, meta_ref, ht_ref, g_ref, b_ref, o_ref, obf_ref, buf, sem):
    i = pl.program_id(0)
    n = pl.num_programs(0)
    tt = dc_ref.shape[-1]
    slot = lax.rem(i, 2)

    def gather(d_ref, s):
        def issue(j, carry):
            for k in range(TOP_K):
                pltpu.make_async_copy(_tile_at(ys_hbm, d_ref[0, k, j]), _tile(buf.at[s, k], j),
                                      sem.at[s]).start(priority=k)
            return carry
        lax.fori_loop(0, tt, issue, 0, unroll=DMA_ISSUE_UNROLL)

    @pl.when(i == 0)
    def _():
        gather(dc_ref, 0)

    @pl.when(i + 1 < n)
    def _():
        gather(dn_ref, 1 - slot)

    for k in range(TOP_K):
        pltpu.make_async_copy(ys_hbm.at[pl.ds(0, tt * TOKEN_TILE_ROWS)], buf.at[slot, k], sem.at[slot]).wait()
    meta = meta_ref[...]
    ffn = (meta[:, META_W0:META_W0 + 1] * _token_tiles_to_rows(buf.at[slot, 0], tt)
           + meta[:, META_W1:META_W1 + 1] * _token_tiles_to_rows(buf.at[slot, 1], tt))
    y = _ln_rows(DEEPNORM_ALPHA * _token_tiles_to_rows(ht_ref, tt) + ffn, g_ref[...], b_ref[...])
    o_ref[...] = y
    obf_ref[...] = y.astype(BF16)


def _combine(ys, slots, meta, h_tiles, g, b):
    t, d = h_tiles.shape[0] // TOKEN_TILE_ROWS, D_MODEL
    tt = slots.shape[-1]
    n = t // tt
    cur = pl.BlockSpec((1, SUBLANES, tt), lambda i: (i, 0, 0), memory_space=pltpu.SMEM)
    nxt = pl.BlockSpec((1, SUBLANES, tt), lambda i: (jnp.minimum(i + 1, n - 1), 0, 0), memory_space=pltpu.SMEM)
    row = lambda i: (i, 0)
    const = lambda i: (0, 0)
    return pl.pallas_call(
        _combine_kernel,
        out_shape=(jax.ShapeDtypeStruct((t, d), F32), jax.ShapeDtypeStruct((t, d), BF16)),
        grid=(n,),
        in_specs=[cur, nxt, pl.BlockSpec(memory_space=pl.ANY),
                  pl.BlockSpec((tt, LANES), row), pl.BlockSpec((tt * TOKEN_TILE_ROWS, LANES), row),
                  pl.BlockSpec((1, d), const), pl.BlockSpec((1, d), const)],
        out_specs=(pl.BlockSpec((tt, d), row), pl.BlockSpec((tt, d), row)),
        scratch_shapes=[pltpu.VMEM((2, TOP_K, tt * TOKEN_TILE_ROWS, LANES), F32), pltpu.SemaphoreType.DMA((2,))],
        compiler_params=_cparams(("arbitrary",)),
        name="moe_combine_ln",
    )(slots, slots, ys, meta, h_tiles, g, b)


def _segment_tables(counts, t):
    tm = EXPERT_ROW_TILE
    n_rows = TOP_K * t + N_EXPERTS * tm
    cnt = counts[0, :N_EXPERTS].astype(jnp.int32)
    tiles = (cnt + tm - 1) // tm
    ends = jnp.cumsum(tiles)
    offs = jnp.zeros((1, LANES), F32).at[0, :N_EXPERTS].set(((ends - tiles) * tm).astype(F32))
    tile_ids = jnp.arange(n_rows // tm, dtype=jnp.int32)
    tile_expert = jnp.minimum(jnp.sum((tile_ids[:, None] >= ends[None, :]).astype(jnp.int32), axis=1),
                              N_EXPERTS - 1)
    last_tile = jnp.maximum(ends - 1, 0).astype(jnp.int32)
    return n_rows, offs, tile_expert, ends[-1:].astype(jnp.int32), last_tile


def _sparse_moe(h_tiles, meta, counts, wgu, wd, g, b):
    t = h_tiles.shape[0] // TOKEN_TILE_ROWS
    n_rows, offs, tile_expert, n_used, last_tile = _segment_tables(counts, t)
    slots = _slots(meta, offs, min(MOE_TOKENS, t))
    xs = _dispatch(h_tiles, slots, _zero_last_tiles(last_tile, n_rows))
    ys = _experts(xs, tile_expert, n_used, wgu, wd)
    return _combine(ys, slots, meta, h_tiles, g, b)


def _rotate_half_cols(w):
    half = w.shape[-1] // 2
    return jnp.concatenate([-w[..., half:], w[..., :half]], axis=-1)


def _pack_input_proj(w_in, b_in):
    d = w_in.shape[0]
    offs = np.cumsum((0, GLA_QK_W, GLA_QK_W, GLA_V_W, GLA_V_W, GLA_GATE_RANK, GLA_GATE_RANK,
                      MLA_Q_RANK, MLA_KV_RANK, MLA_ROPE, D_MODEL, D_MODEL))

    def seg(i):
        return w_in[:, offs[i]:offs[i + 1]], b_in[offs[i]:offs[i + 1]]

    w = jnp.zeros((d, N_PROJ), F32)
    b = jnp.zeros((N_PROJ,), F32)

    def put(w, b, col, ws, bs):
        return w.at[:, col:col + ws.shape[1]].set(ws), b.at[col:col + ws.shape[1]].set(bs)

    for i, col in ((0, COL_GQ), (1, COL_GK), (2, COL_GV), (3, COL_GR), (9, COL_GA), (10, COL_GB),
                   (6, COL_CQ), (7, COL_CKV), (4, COL_SMALL), (5, COL_SMALL + GLA_GATE_RANK)):
        w, b = put(w, b, col, *seg(i))
    wkr, bkr = seg(8)
    w, b = put(w, b, COL_SMALL + KR_LANE, wkr, bkr)
    w, b = put(w, b, COL_SMALL2 + KR_LANE, _rotate_half_cols(wkr), _rotate_half_cols(bkr))
    return w.astype(BF16), b.reshape(1, N_PROJ)


def _pack_decay(wa2, ba, lane0):
    w = jnp.zeros((LANES, GLA_QK_W), F32).at[lane0:lane0 + GLA_GATE_RANK].set(wa2)
    return w.astype(BF16), ba.reshape(1, GLA_QK_W)


def _pack_mla(w_uq, w_ukv, q_norm_g, kv_norm_g):
    wq = w_uq.reshape(MLA_Q_RANK, MLA_HEADS, MLA_QK)
    zq = jnp.zeros((MLA_Q_RANK, MLA_HEADS, HEAD_PAD - MLA_QK), F32)
    wq_p = jnp.concatenate([wq, zq], axis=-1)

    def pad_rows(w):
        w = w.reshape(MLA_Q_RANK, MLA_HEADS * HEAD_PAD)
        return jnp.pad(w, ((0, CQ_PAD - MLA_Q_RANK), (0, 0))).astype(BF16)

    wkv = w_ukv.reshape(MLA_KV_RANK, MLA_HEADS, MLA_NOPE + MLA_V)
    wk_p = jnp.concatenate([wkv[..., :MLA_NOPE],
                            jnp.zeros((MLA_KV_RANK, MLA_HEADS, HEAD_PAD - MLA_NOPE), F32)], axis=-1)
    wk_p = wk_p.reshape(MLA_KV_RANK, MLA_HEADS * HEAD_PAD).astype(BF16)
    wv_p = wkv[..., MLA_NOPE:].reshape(MLA_KV_RANK, MLA_HEADS * MLA_V).T.astype(BF16)
    gq = jnp.pad(q_norm_g, (0, CQ_PAD - MLA_Q_RANK)).reshape(1, CQ_PAD)
    return pad_rows(wq_p), wk_p, wv_p, gq, kv_norm_g.reshape(1, MLA_KV_RANK)


def _rope_lane_table():
    inv = ROPE_BASE ** (-jnp.arange(0, MLA_ROPE, 2, dtype=F32) / MLA_ROPE)
    return jnp.tile(inv, ROPE_PACK).reshape(1, LANES)


def _pack_router(w_grp, b_grp, w_exp, b_exp):
    d = w_grp.shape[0]
    w = jnp.zeros((d, LANES), F32).at[:, :N_EXPERTS].set(w_exp).at[:, GRP_LANE:GRP_LANE + N_GROUPS].set(w_grp)
    b = jnp.zeros((LANES,), F32).at[:N_EXPERTS].set(b_exp).at[GRP_LANE:GRP_LANE + N_GROUPS].set(b_grp)
    return w, b.reshape(1, LANES)


def kernel(x, positions, ln_emb_g, ln_emb_b, w_in, b_in, gla_wa2_f, gla_ba_f, gla_wa2_b, gla_ba_b, gla_norm_g, mla_q_norm_g, mla_w_uq, mla_kv_norm_g, mla_w_ukv, w_out, ln1_g, ln1_b, w_grp, b_grp, w_exp, b_exp, w_gate, w_up, w_down, ln2_g, ln2_b):
    batch, seq, d = x.shape
    t = batch * seq
    pos = jnp.repeat(positions.reshape(t // ROPE_PACK, ROPE_PACK).astype(F32), MLA_ROPE // 2, axis=1)
    inv_lane = _rope_lane_table()
    for l in range(DEPTH):
        w_p, b_p = _pack_input_proj(w_in[l], b_in[l])
        if l == 0:
            h, proj = _input_proj(x.reshape(t, d), w_p, b_p, ln=(ln_emb_g, ln_emb_b))
        else:
            proj = _input_proj(hb, w_p, b_p)
        wa_f, ba_f = _pack_decay(gla_wa2_f[l], gla_ba_f[l], 0)
        wa_b, ba_b = _pack_decay(gla_wa2_b[l], gla_ba_b[l], GLA_GATE_RANK)
        o_f = _gla(proj, wa_f, ba_f, batch, seq, reverse=False)
        o_gla = _gla(proj, wa_b, ba_b, batch, seq, reverse=True, o_fwd=o_f,
                     norm_g=gla_norm_g[l].reshape(1, GLA_V_W))
        wq, wk, wv, gq, gkv = _pack_mla(mla_w_uq[l], mla_w_ukv[l], mla_q_norm_g[l], mla_kv_norm_g[l])
        q2, k2, v = _mla_prep(proj, pos, inv_lane, gq, gkv, wq, wk, wv)
        o_mla = _attention(q2, k2, v, batch, seq)
        h_tiles = _merge_out(o_gla, o_mla, proj, h, w_out[l].astype(BF16),
                           ln1_g[l].reshape(1, d), ln1_b[l].reshape(1, d))
        w_r, b_r = _pack_router(w_grp[l], b_grp[l], w_exp[l], b_exp[l])
        meta, counts = _router(h_tiles, w_r, b_r)
        wgu = jnp.concatenate([w_gate[l], w_up[l]], axis=-1).astype(BF16)
        h, hb = _sparse_moe(h_tiles, meta, counts, wgu, w_down[l].astype(BF16),
                            ln2_g[l].reshape(1, d), ln2_b[l].reshape(1, d))
    return h.reshape(batch, seq, d)
```
